```python
import jax, jax.numpy as jnp
from jax import lax
import numpy as np

D_MODEL = 1024
BATCH = 8
SEQ = 4096
DEPTH = 1

HEAD_DIM = 64
MOBA_HEADS = D_MODEL // 128
NSA_HEADS = D_MODEL // 128
NSA_KV_GROUPS = 2
NSA_HPG = NSA_HEADS // NSA_KV_GROUPS
MOBA_BLOCK = 256
MOBA_TOPK = 3
CMP_LEN = 32
CMP_STRIDE = 16
CMP_HIDDEN = 128
SLC_BLOCK = 64
SLC_TOPN = 16
WINDOW = 512
GATHER_Q_CHUNK = 16
BAND_Q_CHUNK = 128
RMS_EPS = 1e-6
NEG_INF = -1e30
FORCED_SCORE = 1e9

MOBA_W = MOBA_HEADS * HEAD_DIM
NSA_W = NSA_HEADS * HEAD_DIM
NSA_KV_W = NSA_KV_GROUPS * HEAD_DIM
MIX_W = MOBA_W + NSA_W
IN_SPLITS = [MOBA_W, MOBA_W, MOBA_W, MOBA_W, NSA_W,
             NSA_KV_W, NSA_KV_W, NSA_KV_W, NSA_KV_W, NSA_KV_W, NSA_KV_W,
             3 * NSA_HEADS, NSA_W]
D_IN = sum(IN_SPLITS)

kernel_name = "hymba_moba_nsa_sandwich_alibi"


def _rmsnorm(x, g):
    xf = x.astype(jnp.float32)
    r = lax.rsqrt(jnp.mean(xf * xf, axis=-1, keepdims=True) + RMS_EPS)
    return (xf * r * g.astype(jnp.float32)).astype(x.dtype)


def _alibi_slopes(n):
    return jnp.asarray(2.0 ** (-8.0 * np.arange(1, n + 1) / n), dtype=jnp.float32)


def _masked_softmax(s, mask):
    s = jnp.where(mask, s.astype(jnp.float32), NEG_INF)
    m = jnp.max(s, axis=-1, keepdims=True)
    p = jnp.where(mask, jnp.exp(s - m), 0.0)
    return p / jnp.maximum(jnp.sum(p, axis=-1, keepdims=True), 1e-30)


def _moba(q, k, v, slopes):
    B, H, S, dh = q.shape
    nb = -(-S // MOBA_BLOCK)
    pad = nb * MOBA_BLOCK - S
    kb = jnp.pad(k, ((0, 0), (0, 0), (0, pad), (0, 0))).reshape(B, H, nb, MOBA_BLOCK, dh)
    vb = jnp.pad(v, ((0, 0), (0, 0), (0, pad), (0, 0))).reshape(B, H, nb, MOBA_BLOCK, dh)
    kmean = jnp.mean(kb, axis=3)
    pos = jnp.arange(S)
    own = pos // MOBA_BLOCK
    gate = jnp.einsum('bhsd,bhnd->bhsn', q, kmean).astype(jnp.float32)
    past = jnp.arange(nb)[None, :] < own[:, None]
    gate = jnp.where(past, gate, NEG_INF)
    _, top_idx = lax.top_k(gate, min(MOBA_TOPK, nb))
    top_valid = top_idx < own[:, None]
    own_b = jnp.broadcast_to(own[:, None], (B, H, S, 1)).astype(top_idx.dtype)
    idx = jnp.concatenate([top_idx, own_b], axis=-1)
    valid = jnp.concatenate([top_valid, jnp.ones((B, H, S, 1), bool)], axis=-1)
    scale = dh ** -0.5
    bi = jnp.arange(B)[:, None, None, None]
    hi = jnp.arange(H)[None, :, None, None]
    offs = jnp.arange(MOBA_BLOCK)
    n_chunks = S // GATHER_Q_CHUNK

    def chunk(q0):
        qc = lax.dynamic_slice_in_dim(q, q0, GATHER_Q_CHUNK, axis=2)
        ic = lax.dynamic_slice_in_dim(idx, q0, GATHER_Q_CHUNK, axis=2)
        vm = lax.dynamic_slice_in_dim(valid, q0, GATHER_Q_CHUNK, axis=2)
        kg = kb[bi, hi, ic]
        vg = vb[bi, hi, ic]
        s = jnp.einsum('bhqd,bhqjkd->bhqjk', qc, kg).astype(jnp.float32) * scale
        tq = q0 + jnp.arange(GATHER_Q_CHUNK)
        dist = tq[:, None, None] - (ic[..., None] * MOBA_BLOCK + offs)
        mask = vm[..., None] & (dist >= 0)
        s = s - slopes[None, :, None, None, None] * dist.astype(jnp.float32)
        shp = s.shape
        p = _masked_softmax(s.reshape(B, H, GATHER_Q_CHUNK, -1),
                            mask.reshape(B, H, GATHER_Q_CHUNK, -1)).reshape(shp)
        return jnp.einsum('bhqjk,bhqjkd->bhqd', p.astype(v.dtype), vg)

    outs = lax.map(chunk, jnp.arange(n_chunks, dtype=jnp.int32) * GATHER_Q_CHUNK)
    return outs.transpose(1, 2, 0, 3, 4).reshape(B, H, S, dh)


def _compress(x, pe, w1, w2):
    B, G, S, dh = x.shape
    n_cmp = (S - CMP_LEN) // CMP_STRIDE + 1
    cidx = jnp.arange(n_cmp)[:, None] * CMP_STRIDE + jnp.arange(CMP_LEN)[None, :]
    blocks = x[:, :, cidx] + pe
    flat = blocks.reshape(B, G, n_cmp, CMP_LEN * dh)
    return jax.nn.silu(flat @ w1) @ w2


def _nsa(q, k_cmp, v_cmp, k_slc, v_slc, k_win, v_win, gates, slopes,
         pos_k, pos_v, w_ck1, w_ck2, w_cv1, w_cv2):
    B, G, HPG, S, dh = q.shape
    scale = dh ** -0.5
    t = jnp.arange(S)
    sl = slopes.reshape(G, HPG)

    kc = _compress(k_cmp, pos_k, w_ck1, w_ck2)
    vc = _compress(v_cmp, pos_v, w_cv1, w_cv2)
    n_cmp = kc.shape[2]
    cstart = jnp.arange(n_cmp) * CMP_STRIDE
    cmask = (cstart + CMP_LEN - 1)[None, :] <= t[:, None]
    sc = jnp.einsum('bghtd,bgcd->bghtc', q, kc).astype(jnp.float32) * scale
    p_cmp = _masked_softmax(sc, cmask)
    o_cmp = jnp.einsum('bghtc,bgcd->bghtd', p_cmp.astype(vc.dtype), vc)

    n_slc = S // SLC_BLOCK
    jstart = jnp.arange(n_slc) * SLC_BLOCK
    overlap = ((cstart[:, None] < jstart[None, :] + SLC_BLOCK) &
               (cstart[:, None] + CMP_LEN > jstart[None, :])).astype(jnp.float32)
    imp = jnp.einsum('bghtc,cj->bgtj', p_cmp, overlap)
    own = t // SLC_BLOCK
    jb = jnp.arange(n_slc)
    forced = (jb[None, :] == 0) | (jb[None, :] == own[:, None]) | (jb[None, :] == own[:, None] - 1)
    causal_blk = jb[None, :] <= own[:, None]
    imp = jnp.where(forced, FORCED_SCORE, jnp.where(causal_blk, imp, NEG_INF))
    _, sidx = lax.top_k(imp, min(SLC_TOPN, n_slc))
    svalid = sidx <= own[:, None]

    ksb = k_slc.reshape(B, G, n_slc, SLC_BLOCK, dh)
    vsb = v_slc.reshape(B, G, n_slc, SLC_BLOCK, dh)
    bi = jnp.arange(B)[:, None, None, None]
    gi = jnp.arange(G)[None, :, None, None]
    offs = jnp.arange(SLC_BLOCK)

    def slc_chunk(q0):
        qc = lax.dynamic_slice_in_dim(q, q0, GATHER_Q_CHUNK, axis=3)
        ic = lax.dynamic_slice_in_dim(sidx, q0, GATHER_Q_CHUNK, axis=2)
        vm = lax.dynamic_slice_in_dim(svalid, q0, GATHER_Q_CHUNK, axis=2)
        kg = ksb[bi, gi, ic]
        vg = vsb[bi, gi, ic]
        s = jnp.einsum('bghqd,bgqnkd->bghqnk', qc, kg).astype(jnp.float32) * scale
        tq = q0 + jnp.arange(GATHER_Q_CHUNK)
        dist = tq[:, None, None] - (ic[..., None] * SLC_BLOCK + offs)
        mask = (vm[..., None] & (dist >= 0))[:, :, None]
        s = s - sl[None, :, :, None, None, None] * dist[:, :, None].astype(jnp.float32)
        shp = s.shape
        mask = jnp.broadcast_to(mask, shp)
        p = _masked_softmax(s.reshape(B, G, HPG, GATHER_Q_CHUNK, -1),
                            mask.reshape(B, G, HPG, GATHER_Q_CHUNK, -1)).reshape(shp)
        return jnp.einsum('bghqnk,bgqnkd->bghqd', p.astype(v_slc.dtype), vg)

    o_slc = lax.map(slc_chunk, jnp.arange(S // GATHER_Q_CHUNK, dtype=jnp.int32) * GATHER_Q_CHUNK)
    o_slc = o_slc.transpose(1, 2, 3, 0, 4, 5).reshape(B, G, HPG, S, dh)

    kwp = jnp.pad(k_win, ((0, 0), (0, 0), (WINDOW, 0), (0, 0)))
    vwp = jnp.pad(v_win, ((0, 0), (0, 0), (WINDOW, 0), (0, 0)))
    span = WINDOW + BAND_Q_CHUNK

    def win_chunk(q0):
        qc = lax.dynamic_slice_in_dim(q, q0, BAND_Q_CHUNK, axis=3)
        kw = lax.dynamic_slice_in_dim(kwp, q0, span, axis=2)
        vw = lax.dynamic_slice_in_dim(vwp, q0, span, axis=2)
        s = jnp.einsum('bghqd,bgkd->bghqk', qc, kw).astype(jnp.float32) * scale
        tq = q0 + jnp.arange(BAND_Q_CHUNK)
        sp = q0 - WINDOW + jnp.arange(span)
        dist = tq[:, None] - sp[None, :]
        mask = (dist >= 0) & (dist < WINDOW) & (sp[None, :] >= 0)
        s = s - sl[None, :, :, None, None] * dist.astype(jnp.float32)
        p = _masked_softmax(s, mask)
        return jnp.einsum('bghqk,bgkd->bghqd', p.astype(v_win.dtype), vw)

    o_win = lax.map(win_chunk, jnp.arange(S // BAND_Q_CHUNK, dtype=jnp.int32) * BAND_Q_CHUNK)
    o_win = o_win.transpose(1, 2, 3, 0, 4, 5).reshape(B, G, HPG, S, dh)

    return gates[..., 0:1] * o_cmp + gates[..., 1:2] * o_slc + gates[..., 2:3] * o_win


def setup_inputs(seed: int = 0) -> dict:
    key = jax.random.key(seed)
    ks = jax.random.split(key, 12)
    f32 = jnp.float32
    x = jax.random.normal(ks[0], (BATCH, SEQ, D_MODEL), f32)
    pre_norm_g = 1.0 + 0.05 * jax.random.normal(ks[1], (DEPTH, D_MODEL), f32)
    post_norm_g = 1.0 + 0.05 * jax.random.normal(ks[2], (DEPTH, D_MODEL), f32)
    w_in = jax.random.normal(ks[3], (DEPTH, D_MODEL, D_IN), f32) * D_MODEL ** -0.5
    cmp_pos_k = 0.1 * jax.random.normal(ks[4], (DEPTH, CMP_LEN, HEAD_DIM), f32)
    cmp_pos_v = 0.1 * jax.random.normal(ks[5], (DEPTH, CMP_LEN, HEAD_DIM), f32)
    w_cmp_k1 = jax.random.normal(ks[6], (DEPTH, CMP_LEN * HEAD_DIM, CMP_HIDDEN), f32) * (CMP_LEN * HEAD_DIM) ** -0.5
    w_cmp_k2 = jax.random.normal(ks[7], (DEPTH, CMP_HIDDEN, HEAD_DIM), f32) * CMP_HIDDEN ** -0.5
    w_cmp_v1 = jax.random.normal(ks[8], (DEPTH, CMP_LEN * HEAD_DIM, CMP_HIDDEN), f32) * (CMP_LEN * HEAD_DIM) ** -0.5
    w_cmp_v2 = jax.random.normal(ks[9], (DEPTH, CMP_HIDDEN, HEAD_DIM), f32) * CMP_HIDDEN ** -0.5
    w_out = jax.random.normal(ks[10], (DEPTH, MIX_W, D_MODEL), f32) * MIX_W ** -0.5
    return {"x": x, "pre_norm_g": pre_norm_g, "post_norm_g": post_norm_g, "w_in": w_in,
            "cmp_pos_k": cmp_pos_k, "cmp_pos_v": cmp_pos_v,
            "w_cmp_k1": w_cmp_k1, "w_cmp_k2": w_cmp_k2,
            "w_cmp_v1": w_cmp_v1, "w_cmp_v2": w_cmp_v2, "w_out": w_out}


def reference(x, pre_norm_g, post_norm_g, w_in, cmp_pos_k, cmp_pos_v,
              w_cmp_k1, w_cmp_k2, w_cmp_v1, w_cmp_v2, w_out):
    B, S, _ = x.shape
    split_points = np.cumsum(IN_SPLITS)[:-1].tolist()
    slopes_a = _alibi_slopes(MOBA_HEADS)
    slopes_b = _alibi_slopes(NSA_HEADS)
    for l in range(DEPTH):
        h = _rmsnorm(x, pre_norm_g[l])
        proj = h @ w_in[l]
        (qa, ka, va, za, qb, kcm, vcm, ksl, vsl, kwi, vwi, g_logit, zb) = jnp.split(proj, split_points, axis=-1)

        def heads_a(t):
            return t.reshape(B, S, MOBA_HEADS, HEAD_DIM).transpose(0, 2, 1, 3)
        oa = _moba(heads_a(qa), heads_a(ka), heads_a(va), slopes_a)
        oa = oa.transpose(0, 2, 1, 3).reshape(B, S, MOBA_W) * jax.nn.silu(za)

        def heads_kv(t):
            return t.reshape(B, S, NSA_KV_GROUPS, HEAD_DIM).transpose(0, 2, 1, 3)
        qn = qb.reshape(B, S, NSA_KV_GROUPS, NSA_HPG, HEAD_DIM).transpose(0, 2, 3, 1, 4)
        gates = jax.nn.sigmoid(g_logit.astype(jnp.float32)).astype(qb.dtype)
        gates = gates.reshape(B, S, NSA_KV_GROUPS, NSA_HPG, 3).transpose(0, 2, 3, 1, 4)
        ob = _nsa(qn, heads_kv(kcm), heads_kv(vcm), heads_kv(ksl), heads_kv(vsl),
                  heads_kv(kwi), heads_kv(vwi), gates, slopes_b,
                  cmp_pos_k[l], cmp_pos_v[l], w_cmp_k1[l], w_cmp_k2[l], w_cmp_v1[l], w_cmp_v2[l])
        ob = ob.transpose(0, 3, 1, 2, 4).reshape(B, S, NSA_W) * jax.nn.silu(zb)

        y = jnp.concatenate([oa, ob], axis=-1) @ w_out[l]
        x = x + _rmsnorm(y.astype(x.dtype), post_norm_g[l])
    return x
```

```python
import functools

import numpy as np
import jax
import jax.numpy as jnp
from jax import lax
from jax.experimental import pallas as pl
from jax.experimental.pallas import tpu as pltpu

D_MODEL = 1024
HEAD_DIM = 64
N_HEADS = 8
N_PAIRS = N_HEADS // 2
NSA_GROUPS = 2
NSA_HPG = N_HEADS // NSA_GROUPS
MOBA_BLOCK = 256
MOBA_TOPK = 3
CMP_LEN = 32
CMP_STRIDE = 16
CMP_HIDDEN = 128
SLC_BLOCK = 64
SLC_TOPN = 16
WINDOW = 512
RMS_EPS = 1e-6
NEG_INF = -1e30
FORCED_SCORE = 1e9
SCALE = HEAD_DIM ** -0.5

TILE = 256
LANES = 128
HALF = HEAD_DIM
MOBA_W = N_HEADS * HEAD_DIM
NSA_W = N_HEADS * HEAD_DIM
KV_W = NSA_GROUPS * HEAD_DIM
SLOPES = tuple(2.0 ** (-(i + 1)) for i in range(N_HEADS))

VMEM_LIMIT = 48 * 1024 * 1024

_OFF = dict(qa=0, ka=512, va=1024, za=1536, qb=2048, kcm=2560, vcm=2688,
            ksl=2816, vsl=2944, kwi=3072, vwi=3200, gate=3328, zb=3352)
_NAT = dict(ka=(0, 512), kcm=(512, 640), vcm=(640, 768), ksl=(768, 896),
            kwi=(896, 1024), za=(1024, 1536), gexp=(1536, 3072), zb=(3072, 3584))
N_NAT = 3584
_TR = dict(qa=(0, 512), va=(512, 1024), qb=(1024, 1536), vsl=(1536, 1664),
           vwi=(1664, 1792))
N_TR = 1792


def _pair_perm():
    idx = np.zeros(NSA_W, np.int32)
    for p in range(N_PAIRS):
        for g in range(NSA_GROUPS):
            for d in range(HEAD_DIM):
                idx[p * LANES + g * HALF + d] = (g * NSA_HPG + p) * HEAD_DIM + d
    return idx


def _column_maps():
    pp = _pair_perm()
    nat = np.zeros(N_NAT, np.int32)
    nat[0:512] = _OFF["ka"] + np.arange(512)
    nat[512:640] = _OFF["kcm"] + np.arange(128)
    nat[640:768] = _OFF["vcm"] + np.arange(128)
    nat[768:896] = _OFF["ksl"] + np.arange(128)
    nat[896:1024] = _OFF["kwi"] + np.arange(128)
    nat[1024:1536] = _OFF["za"] + np.arange(512)
    for r in range(3):
        nat[1536 + r * 512:1536 + (r + 1) * 512] = _OFF["gate"] + (pp // HEAD_DIM) * 3 + r
    nat[3072:3584] = _OFF["zb"] + pp
    tr = np.zeros(N_TR, np.int32)
    tr[0:512] = _OFF["qa"] + np.arange(512)
    tr[512:1024] = _OFF["va"] + np.arange(512)
    tr[1024:1536] = _OFF["qb"] + pp
    tr[1536:1664] = _OFF["vsl"] + np.arange(128)
    tr[1664:1792] = _OFF["vwi"] + np.arange(128)
    out_rows = np.concatenate([np.arange(512), 512 + pp]).astype(np.int32)
    return nat, tr, out_rows


def _sigmoid(x):
    return 1.0 / (1.0 + jnp.exp(-x))


def _silu(x):
    return x * _sigmoid(x)


def _dot(a, b):
    return jnp.dot(a, b, preferred_element_type=jnp.float32)


def _proj_kernel(x_ref, g_ref, wn_ref, wt_ref,
                 ka_ref, kmean_ref, kcm_ref, vcm_ref, ksl_ref, kwi_ref,
                 sza_ref, gexp_ref, szb_ref,
                 qaT_ref, vaT_ref, qbT_ref, vslT_ref, vwiT_ref):
    x = x_ref[0]
    r = lax.rsqrt(jnp.mean(x * x, axis=-1, keepdims=True) + RMS_EPS)
    h = (x * r * g_ref[...]).astype(jnp.bfloat16)

    def nat(name):
        a, b = _NAT[name]
        return _dot(h, wn_ref[:, a:b])

    ka = nat("ka")
    kmean_ref[0, 0] = jnp.mean(ka, axis=0, keepdims=True)
    for p in range(N_PAIRS):
        ka_ref[0, p] = ka[:, p * LANES:(p + 1) * LANES].astype(jnp.bfloat16)
    kcm_ref[0] = nat("kcm").astype(jnp.bfloat16)
    vcm_ref[0] = nat("vcm").astype(jnp.bfloat16)
    ksl_ref[0] = nat("ksl").astype(jnp.bfloat16)
    kwi_ref[0] = nat("kwi").astype(jnp.bfloat16)
    sza_ref[0] = _silu(nat("za"))
    gexp_ref[0] = _sigmoid(nat("gexp"))
    szb_ref[0] = _silu(nat("zb"))

    def tr(name):
        a, b = _TR[name]
        return lax.dot_general(wt_ref[a:b, :], h, (((1,), (1,)), ((), ())),
                               preferred_element_type=jnp.float32)

    qa = tr("qa") * SCALE
    va = tr("va")
    qb = tr("qb") * SCALE
    for p in range(N_PAIRS):
        sl = slice(p * LANES, (p + 1) * LANES)
        qaT_ref[0, p] = qa[sl].astype(jnp.bfloat16)
        vaT_ref[0, p] = va[sl].astype(jnp.bfloat16)
        qbT_ref[0, p] = qb[sl].astype(jnp.bfloat16)
    vslT_ref[0] = tr("vsl").astype(jnp.bfloat16)
    vwiT_ref[0] = tr("vwi").astype(jnp.bfloat16)


def _proj(x, g, wn, wt):
    B, S, _ = x.shape
    nq = S // TILE
    bf, f32 = jnp.bfloat16, jnp.float32
    slab_nat = lambda: pl.BlockSpec((1, N_PAIRS, TILE, LANES), lambda b, i: (b, 0, i, 0))
    slab_tr = lambda: pl.BlockSpec((1, N_PAIRS, LANES, TILE), lambda b, i: (b, 0, 0, i))
    rows = lambda w: pl.BlockSpec((1, TILE, w), lambda b, i: (b, i, 0))
    out_shape = [
        jax.ShapeDtypeStruct((B, N_PAIRS, S, LANES), bf),
        jax.ShapeDtypeStruct((B, nq, 1, MOBA_W), f32),
        jax.ShapeDtypeStruct((B, S, KV_W), bf),
        jax.ShapeDtypeStruct((B, S, KV_W), bf),
        jax.ShapeDtypeStruct((B, S, KV_W), bf),
        jax.ShapeDtypeStruct((B, S, KV_W), bf),
        jax.ShapeDtypeStruct((B, S, MOBA_W), f32),
        jax.ShapeDtypeStruct((B, S, 3 * NSA_W), f32),
        jax.ShapeDtypeStruct((B, S, NSA_W), f32),
        jax.ShapeDtypeStruct((B, N_PAIRS, LANES, S), bf),
        jax.ShapeDtypeStruct((B, N_PAIRS, LANES, S), bf),
        jax.ShapeDtypeStruct((B, N_PAIRS, LANES, S), bf),
        jax.ShapeDtypeStruct((B, KV_W, S), bf),
        jax.ShapeDtypeStruct((B, KV_W, S), bf),
    ]
    out_specs = [
        slab_nat(),
        pl.BlockSpec((1, 1, 1, MOBA_W), lambda b, i: (b, i, 0, 0)),
        rows(KV_W), rows(KV_W), rows(KV_W), rows(KV_W),
        rows(MOBA_W), rows(3 * NSA_W), rows(NSA_W),
        slab_tr(), slab_tr(), slab_tr(),
        pl.BlockSpec((1, KV_W, TILE), lambda b, i: (b, 0, i)),
        pl.BlockSpec((1, KV_W, TILE), lambda b, i: (b, 0, i)),
    ]
    return pl.pallas_call(
        _proj_kernel,
        grid=(B, nq),
        in_specs=[
            pl.BlockSpec((1, TILE, D_MODEL), lambda b, i: (b, i, 0)),
            pl.BlockSpec((1, D_MODEL), lambda b, i: (0, 0)),
            pl.BlockSpec((D_MODEL, N_NAT), lambda b, i: (0, 0)),
            pl.BlockSpec((N_TR, D_MODEL), lambda b, i: (0, 0)),
        ],
        out_specs=out_specs,
        out_shape=out_shape,
        compiler_params=pltpu.CompilerParams(
            dimension_semantics=("arbitrary", "arbitrary"),
            vmem_limit_bytes=VMEM_LIMIT),
        name="proj",
    )(x, g, wn, wt)


def _compress_kernel(n_cmp, xk_ref, xv_ref, wkt_ref, wkb_ref, wvt_ref, wvb_ref,
                     w2k_ref, w2v_ref, pek_ref, pev_ref, w1k_ref, w1v_ref,
                     kc_ref, vcT_ref):
    C = xk_ref.shape[1]
    row = lax.broadcasted_iota(jnp.int32, (C, KV_W), 0)

    def phi(x_ref, wt_ref, wb_ref, w2_ref, pe_ref, w1_ref):
        x = x_ref[0]
        top = _dot(x, wt_ref[...])
        bot = _dot(x, wb_ref[...])
        peb = _dot(jnp.broadcast_to(pe_ref[...], (8, pe_ref.shape[1])), w1_ref[...])[0:1]
        peb = jnp.concatenate([peb, peb], axis=1)
        hid = top + pltpu.roll(bot, C - 1, 0) + peb
        out = _dot(_silu(hid).astype(jnp.bfloat16), w2_ref[...])
        return jnp.where(row < n_cmp, out, 0.0)

    kc_ref[0] = phi(xk_ref, wkt_ref, wkb_ref, w2k_ref, pek_ref, w1k_ref).astype(jnp.bfloat16)
    vc = phi(xv_ref, wvt_ref, wvb_ref, w2v_ref, pev_ref, w1v_ref)
    vcT_ref[0] = vc.T.astype(jnp.bfloat16)


def _compress(xk, xv, wkt, wkb, wvt, wvb, w2k, w2v, pek, pev, w1k, w1v, n_cmp):
    B, C, F = xk.shape
    full = lambda a: pl.BlockSpec(a.shape, lambda b: (0,) * a.ndim)
    return pl.pallas_call(
        functools.partial(_compress_kernel, n_cmp),
        grid=(B,),
        in_specs=[pl.BlockSpec((1, C, F), lambda b: (b, 0, 0)),
                  pl.BlockSpec((1, C, F), lambda b: (b, 0, 0)),
                  full(wkt), full(wkb), full(wvt), full(wvb), full(w2k), full(w2v),
                  full(pek), full(pev), full(w1k), full(w1v)],
        out_specs=[pl.BlockSpec((1, C, KV_W), lambda b: (b, 0, 0)),
                   pl.BlockSpec((1, KV_W, C), lambda b: (b, 0, 0))],
        out_shape=[jax.ShapeDtypeStruct((B, C, KV_W), jnp.bfloat16),
                   jax.ShapeDtypeStruct((B, KV_W, C), jnp.bfloat16)],
        compiler_params=pltpu.CompilerParams(
            dimension_semantics=("arbitrary",), vmem_limit_bytes=VMEM_LIMIT),
        name="compress",
    )(xk, xv, wkt, wkb, wvt, wvb, w2k, w2v, pek, pev, w1k, w1v)


def _keep_half(qT, half):
    z = jnp.zeros((HALF, qT.shape[1]), qT.dtype)
    if half == 0:
        return jnp.concatenate([qT[0:HALF], z], axis=0)
    return jnp.concatenate([z, qT[HALF:2 * HALF]], axis=0)


def _rank_rows(vals, n_rows):
    R, T = vals.shape
    j = lax.broadcasted_iota(jnp.int32, (R, T), 0)
    rank = jnp.zeros((R, T), jnp.int32)
    for m in range(n_rows):
        vm = vals[m:m + 1, :]
        beats = (vm > vals) | ((vm == vals) & (j > m))
        rank = rank + beats.astype(jnp.int32)
    return rank


def _alibi_rows(n_rows, T, first, slope, i):
    r = lax.broadcasted_iota(jnp.int32, (n_rows, T), 0)
    trel = lax.broadcasted_iota(jnp.int32, (n_rows, T), 1).astype(jnp.float32)
    base = (i * TILE).astype(jnp.float32)
    out = jnp.where(r == first, -slope * trel, 0.0)
    out = jnp.where(r == first + 1, -slope * base, out)
    out = jnp.where((r == first + 2) | (r == first + 3), slope, out)
    return out


def _key_aug_table(nq, n_sel, blocks_per_tile):
    t = np.zeros((nq, TILE, LANES), np.float32)
    krel = np.arange(TILE)
    for j in range(nq):
        if n_sel:
            blk = j * blocks_per_tile + krel // (TILE // blocks_per_tile)
            t[j, krel, blk] = 1.0
        t[j, :, n_sel] = 1.0
        t[j, :, n_sel + 1] = 1.0
        t[j, :, n_sel + 2] = krel
        t[j, :, n_sel + 3] = TILE * j
    return jnp.asarray(t, jnp.bfloat16)


def _online_step(sT, vT_h, m_ref, l_ref, acc_ref, idx):
    m_prev = m_ref[idx]
    m_new = jnp.maximum(m_prev, jnp.max(sT, axis=0, keepdims=True))
    alpha = jnp.exp(m_prev - m_new)
    p = jnp.exp(sT - m_new)
    l_ref[idx] = alpha * l_ref[idx] + jnp.sum(p, axis=0, keepdims=True)
    acc_ref[idx] = alpha * acc_ref[idx] + _dot(vT_h, p.astype(jnp.bfloat16))
    m_ref[idx] = m_new


def _init_state(m_ref, l_ref, acc_ref):
    m_ref[...] = jnp.full(m_ref.shape, NEG_INF, jnp.float32)
    l_ref[...] = jnp.zeros(l_ref.shape, jnp.float32)
    acc_ref[...] = jnp.zeros(acc_ref.shape, jnp.float32)


def _lower_tri():
    k = lax.broadcasted_iota(jnp.int32, (TILE, TILE), 0)
    t = lax.broadcasted_iota(jnp.int32, (TILE, TILE), 1)
    return k <= t


def _finalize_pair(o_ref, l_ref, acc_ref, p):
    oT = jnp.concatenate([acc_ref[2 * p] / l_ref[2 * p],
                          acc_ref[2 * p + 1] / l_ref[2 * p + 1]], axis=0)
    o_ref[0, :, p * LANES:(p + 1) * LANES] = oT.T


def _tri_tables(nq):
    it, jt = [], []
    for i in range(nq):
        for j in range(i + 1):
            it.append(i)
            jt.append(j)
    return jnp.asarray(it, jnp.int32), jnp.asarray(jt, jnp.int32)


_STATE_SCRATCH = lambda: [
    pltpu.VMEM((N_HEADS, 2 * LANES, TILE), jnp.bfloat16),
    pltpu.VMEM((N_HEADS, 1, TILE), jnp.float32),
    pltpu.VMEM((N_HEADS, 1, TILE), jnp.float32),
    pltpu.VMEM((N_HEADS, HALF, TILE), jnp.float32),
]


def _moba_kernel(nq, it_ref, jt_ref, qT_ref, k_ref, vT_ref, kmean_ref, kaug_ref, o_ref,
                 qaug_ref, m_ref, l_ref, acc_ref):
    s = pl.program_id(1)
    i = it_ref[s]
    j = jt_ref[s]
    n_sel = 16

    @pl.when(j == 0)
    def _():
        _init_state(m_ref, l_ref, acc_ref)
        n_iota = lax.broadcasted_iota(jnp.int32, (nq, TILE), 0)
        for p in range(N_PAIRS):
            qT = qT_ref[0, p]
            km = kmean_ref[0, :, p * LANES:(p + 1) * LANES].astype(jnp.bfloat16)
            for hh in range(2):
                h = 2 * p + hh
                qm = _keep_half(qT, hh)
                gate = jnp.where(n_iota < i, _dot(km, qm), NEG_INF)
                rank = _rank_rows(gate, nq)
                sel = ((rank < MOBA_TOPK) & (n_iota < i)) | (n_iota == i)
                selneg = jnp.where(sel, 0.0, NEG_INF)
                if nq < n_sel:
                    selneg = jnp.concatenate(
                        [selneg, jnp.zeros((n_sel - nq, TILE), jnp.float32)], axis=0)
                ali = _alibi_rows(LANES - n_sel, TILE, 0, SLOPES[h], i)
                aug = jnp.concatenate([selneg, ali], axis=0).astype(jnp.bfloat16)
                qaug_ref[h] = jnp.concatenate([qm, aug], axis=0)

    def step(diag):
        kaug = kaug_ref[0]
        tri = _lower_tri() if diag else None
        for p in range(N_PAIRS):
            kfull = jnp.concatenate([k_ref[0, p], kaug], axis=1)
            vT = vT_ref[0, p]
            for hh in range(2):
                h = 2 * p + hh
                sT = _dot(kfull, qaug_ref[h])
                if diag:
                    sT = jnp.where(tri, sT, NEG_INF)
                _online_step(sT, vT[hh * HALF:(hh + 1) * HALF], m_ref, l_ref, acc_ref, h)
            if diag:
                _finalize_pair(o_ref, l_ref, acc_ref, p)

    @pl.when(j < i)
    def _():
        step(False)

    @pl.when(j == i)
    def _():
        step(True)


def _moba(qT, k, vT, kmean, kaug):
    B, _, _, S = qT.shape
    nq = S // TILE
    it, jt = _tri_tables(nq)
    return pl.pallas_call(
        functools.partial(_moba_kernel, nq),
        grid_spec=pltpu.PrefetchScalarGridSpec(
            num_scalar_prefetch=2,
            grid=(B, int(it.shape[0])),
            in_specs=[
                pl.BlockSpec((1, N_PAIRS, LANES, TILE), lambda b, s, it, jt: (b, 0, 0, it[s])),
                pl.BlockSpec((1, N_PAIRS, TILE, LANES), lambda b, s, it, jt: (b, 0, jt[s], 0)),
                pl.BlockSpec((1, N_PAIRS, LANES, TILE), lambda b, s, it, jt: (b, 0, 0, jt[s])),
                pl.BlockSpec((1, nq, MOBA_W), lambda b, s, it, jt: (b, 0, 0)),
                pl.BlockSpec((1, TILE, LANES), lambda b, s, it, jt: (jt[s], 0, 0)),
            ],
            out_specs=pl.BlockSpec((1, TILE, MOBA_W), lambda b, s, it, jt: (b, it[s], 0)),
            scratch_shapes=_STATE_SCRATCH(),
        ),
        out_shape=jax.ShapeDtypeStruct((B, S, MOBA_W), jnp.float32),
        compiler_params=pltpu.CompilerParams(
            dimension_semantics=("arbitrary", "arbitrary"), vmem_limit_bytes=VMEM_LIMIT),
        name="moba",
    )(it, jt, qT, k, vT, kmean, kaug)


def _cmp_kernel(n_cmp, n_slc, qT_ref, kc_ref, vcT_ref, ovT_ref, o_ref, sel_ref):
    i = pl.program_id(1)
    C = kc_ref.shape[1]
    kc = kc_ref[0]
    vcT = vcT_ref[0]
    c_iota = lax.broadcasted_iota(jnp.int32, (C, TILE), 0)
    t_abs = i * TILE + lax.broadcasted_iota(jnp.int32, (C, TILE), 1)
    cmask = (c_iota * CMP_STRIDE + (CMP_LEN - 1) <= t_abs) & (c_iota < n_cmp)

    psum = [jnp.zeros((C, TILE), jnp.float32) for _ in range(NSA_GROUPS)]
    for p in range(N_PAIRS):
        qT = qT_ref[0, p]
        halves = []
        for g in range(NSA_GROUPS):
            qm = _keep_half(qT, g)
            sT = jnp.where(cmask, _dot(kc, qm), NEG_INF)
            m = jnp.max(sT, axis=0, keepdims=True)
            e = jnp.where(cmask, jnp.exp(sT - m), 0.0)
            pn = e / jnp.maximum(jnp.sum(e, axis=0, keepdims=True), 1e-30)
            psum[g] = psum[g] + pn
            halves.append(_dot(vcT[g * HALF:(g + 1) * HALF], pn.astype(jnp.bfloat16)))
        o_ref[0, :, p * LANES:(p + 1) * LANES] = jnp.concatenate(halves, axis=0).T

    R = sel_ref.shape[2]
    j_iota = lax.broadcasted_iota(jnp.int32, (R, TILE), 0)
    t_q = i * TILE + lax.broadcasted_iota(jnp.int32, (R, TILE), 1)
    own = lax.shift_right_logical(t_q, int(np.log2(SLC_BLOCK)))
    forced = (j_iota == 0) | (j_iota == own) | (j_iota == own - 1)
    causal = j_iota <= own
    ovT = ovT_ref[...]
    for g in range(NSA_GROUPS):
        hi = psum[g].astype(jnp.bfloat16)
        lo = (psum[g] - hi.astype(jnp.float32)).astype(jnp.bfloat16)
        imp = (_dot(ovT, hi) + _dot(ovT, lo))[0:R]
        vals = jnp.where(forced, FORCED_SCORE, jnp.where(causal, imp, NEG_INF))
        rank = _rank_rows(vals, n_slc)
        sel = (rank < min(SLC_TOPN, n_slc)) & causal
        sel_ref[0, g] = jnp.where(sel, 0.0, NEG_INF).astype(jnp.bfloat16)


def _cmp(qbT, kc, vcT, ovT, n_cmp, n_slc):
    B, _, _, S = qbT.shape
    C = kc.shape[1]
    nq = S // TILE
    R = 64
    return pl.pallas_call(
        functools.partial(_cmp_kernel, n_cmp, n_slc),
        grid=(B, nq),
        in_specs=[
            pl.BlockSpec((1, N_PAIRS, LANES, TILE), lambda b, i: (b, 0, 0, i)),
            pl.BlockSpec((1, C, KV_W), lambda b, i: (b, 0, 0)),
            pl.BlockSpec((1, KV_W, C), lambda b, i: (b, 0, 0)),
            pl.BlockSpec(ovT.shape, lambda b, i: (0, 0)),
        ],
        out_specs=[pl.BlockSpec((1, TILE, NSA_W), lambda b, i: (b, i, 0)),
                   pl.BlockSpec((1, NSA_GROUPS, R, TILE), lambda b, i: (b, 0, 0, i))],
        out_shape=[jax.ShapeDtypeStruct((B, S, NSA_W), jnp.float32),
                   jax.ShapeDtypeStruct((B, NSA_GROUPS, R, S), jnp.bfloat16)],
        compiler_params=pltpu.CompilerParams(
            dimension_semantics=("arbitrary", "arbitrary"), vmem_limit_bytes=VMEM_LIMIT),
        name="cmp",
    )(qbT, kc, vcT, ovT)


def _slc_kernel(it_ref, jt_ref, qT_ref, k_ref, vT_ref, sel_ref, kaug_ref, o_ref,
                qaug_ref, m_ref, l_ref, acc_ref):
    s = pl.program_id(1)
    i = it_ref[s]
    j = jt_ref[s]
    n_sel = sel_ref.shape[2]

    @pl.when(j == 0)
    def _():
        _init_state(m_ref, l_ref, acc_ref)
        for p in range(N_PAIRS):
            qT = qT_ref[0, p]
            for g in range(NSA_GROUPS):
                h = 2 * p + g
                qm = _keep_half(qT, g)
                ali = _alibi_rows(LANES - n_sel, TILE, 0, SLOPES[g * NSA_HPG + p], i)
                qaug_ref[h] = jnp.concatenate(
                    [qm, sel_ref[0, g], ali.astype(jnp.bfloat16)], axis=0)

    def step(diag):
        kfull = jnp.concatenate([k_ref[0], kaug_ref[0]], axis=1)
        vT = vT_ref[0]
        tri = _lower_tri() if diag else None
        for p in range(N_PAIRS):
            for g in range(NSA_GROUPS):
                h = 2 * p + g
                sT = _dot(kfull, qaug_ref[h])
                if diag:
                    sT = jnp.where(tri, sT, NEG_INF)
                _online_step(sT, vT[g * HALF:(g + 1) * HALF], m_ref, l_ref, acc_ref, h)
            if diag:
                _finalize_pair(o_ref, l_ref, acc_ref, p)

    @pl.when(j < i)
    def _():
        step(False)

    @pl.when(j == i)
    def _():
        step(True)


def _slc(qbT, ksl, vslT, selT, kaug):
    B, _, _, S = qbT.shape
    nq = S // TILE
    R = selT.shape[2]
    it, jt = _tri_tables(nq)
    return pl.pallas_call(
        _slc_kernel,
        grid_spec=pltpu.PrefetchScalarGridSpec(
            num_scalar_prefetch=2,
            grid=(B, int(it.shape[0])),
            in_specs=[
                pl.BlockSpec((1, N_PAIRS, LANES, TILE), lambda b, s, it, jt: (b, 0, 0, it[s])),
                pl.BlockSpec((1, TILE, KV_W), lambda b, s, it, jt: (b, jt[s], 0)),
                pl.BlockSpec((1, KV_W, TILE), lambda b, s, it, jt: (b, 0, jt[s])),
                pl.BlockSpec((1, NSA_GROUPS, R, TILE), lambda b, s, it, jt: (b, 0, 0, it[s])),
                pl.BlockSpec((1, TILE, LANES), lambda b, s, it, jt: (jt[s], 0, 0)),
            ],
            out_specs=pl.BlockSpec((1, TILE, NSA_W), lambda b, s, it, jt: (b, it[s], 0)),
            scratch_shapes=_STATE_SCRATCH(),
        ),
        out_shape=jax.ShapeDtypeStruct((B, S, NSA_W), jnp.float32),
        compiler_params=pltpu.CompilerParams(
            dimension_semantics=("arbitrary", "arbitrary"), vmem_limit_bytes=VMEM_LIMIT),
        name="slc",
    )(it, jt, qbT, ksl, vslT, selT, kaug)


def _win_kernel(qT_ref, k2_ref, k1_ref, k0_ref, v2_ref, v1_ref, v0_ref, kaug_ref, o_ref,
                qaug_ref, m_ref, l_ref, acc_ref):
    i = pl.program_id(1)
    _init_state(m_ref, l_ref, acc_ref)
    for p in range(N_PAIRS):
        qT = qT_ref[0, p]
        for g in range(NSA_GROUPS):
            h = 2 * p + g
            qm = _keep_half(qT, g)
            ali = _alibi_rows(LANES, TILE, 0, SLOPES[g * NSA_HPG + p], i)
            qaug_ref[h] = jnp.concatenate([qm, ali.astype(jnp.bfloat16)], axis=0)

    def step(delta, k_ref, vT_ref):
        jabs = jnp.maximum(i - delta, 0)
        kfull = jnp.concatenate([k_ref[0], kaug_ref[jabs]], axis=1)
        vT = vT_ref[0]
        if delta == 0:
            keep = _lower_tri()
        elif delta == WINDOW // TILE:
            keep = jnp.logical_not(_lower_tri())
        else:
            keep = None
        for p in range(N_PAIRS):
            for g in range(NSA_GROUPS):
                h = 2 * p + g
                sT = _dot(kfull, qaug_ref[h])
                if keep is not None:
                    sT = jnp.where(keep, sT, NEG_INF)
                _online_step(sT, vT[g * HALF:(g + 1) * HALF], m_ref, l_ref, acc_ref, h)

    @pl.when(i >= 2)
    def _():
        step(2, k2_ref, v2_ref)

    @pl.when(i >= 1)
    def _():
        step(1, k1_ref, v1_ref)

    step(0, k0_ref, v0_ref)
    for p in range(N_PAIRS):
        _finalize_pair(o_ref, l_ref, acc_ref, p)


def _win(qbT, kwi, vwiT, kaug):
    B, _, _, S = qbT.shape
    nq = S // TILE
    kspec = lambda d: pl.BlockSpec((1, TILE, KV_W), lambda b, i: (b, jnp.maximum(i - d, 0), 0))
    vspec = lambda d: pl.BlockSpec((1, KV_W, TILE), lambda b, i: (b, 0, jnp.maximum(i - d, 0)))
    return pl.pallas_call(
        _win_kernel,
        grid=(B, nq),
        in_specs=[
            pl.BlockSpec((1, N_PAIRS, LANES, TILE), lambda b, i: (b, 0, 0, i)),
            kspec(2), kspec(1), kspec(0), vspec(2), vspec(1), vspec(0),
            pl.BlockSpec(kaug.shape, lambda b, i: (0, 0, 0)),
        ],
        out_specs=pl.BlockSpec((1, TILE, NSA_W), lambda b, i: (b, i, 0)),
        out_shape=jax.ShapeDtypeStruct((B, S, NSA_W), jnp.float32),
        scratch_shapes=_STATE_SCRATCH(),
        compiler_params=pltpu.CompilerParams(
            dimension_semantics=("arbitrary", "arbitrary"), vmem_limit_bytes=VMEM_LIMIT),
        name="win",
    )(qbT, kwi, kwi, kwi, vwiT, vwiT, vwiT, kaug)


def _out_kernel(x_ref, oa_ref, sza_ref, oc_ref, os_ref, ow_ref, gexp_ref, szb_ref,
                w_ref, g_ref, y_ref):
    mixa = oa_ref[0] * sza_ref[0]
    ge = gexp_ref[0]
    ob = (ge[:, 0:NSA_W] * oc_ref[0] + ge[:, NSA_W:2 * NSA_W] * os_ref[0]
          + ge[:, 2 * NSA_W:3 * NSA_W] * ow_ref[0])
    mix = jnp.concatenate([mixa, ob * szb_ref[0]], axis=1).astype(jnp.bfloat16)
    y = _dot(mix, w_ref[...])
    r = lax.rsqrt(jnp.mean(y * y, axis=-1, keepdims=True) + RMS_EPS)
    y_ref[0] = x_ref[0] + y * r * g_ref[...]


def _out(x, oa, sza, oc, os_, ow, gexp, szb, w, g):
    B, S, _ = x.shape
    rows = lambda wd: pl.BlockSpec((1, TILE, wd), lambda b, i: (b, i, 0))
    return pl.pallas_call(
        _out_kernel,
        grid=(B, S // TILE),
        in_specs=[rows(D_MODEL), rows(MOBA_W), rows(MOBA_W), rows(NSA_W), rows(NSA_W),
                  rows(NSA_W), rows(3 * NSA_W), rows(NSA_W),
                  pl.BlockSpec(w.shape, lambda b, i: (0, 0)),
                  pl.BlockSpec((1, D_MODEL), lambda b, i: (0, 0))],
        out_specs=rows(D_MODEL),
        out_shape=jax.ShapeDtypeStruct((B, S, D_MODEL), jnp.float32),
        compiler_params=pltpu.CompilerParams(
            dimension_semantics=("arbitrary", "arbitrary"), vmem_limit_bytes=VMEM_LIMIT),
        name="out",
    )(x, oa, sza, oc, os_, ow, gexp, szb, w, g)


def _compress_weights(w1, w2):
    half = (CMP_LEN // 2) * HEAD_DIM
    w1r = w1.reshape(2, CMP_LEN // 2, HEAD_DIM, CMP_HIDDEN)
    z = jnp.zeros_like(w1r)
    g0 = jnp.concatenate([w1r, z], axis=-1)
    g1 = jnp.concatenate([z, w1r], axis=-1)
    both = jnp.stack([g0, g1], axis=2)
    both = both.reshape(2, half * NSA_GROUPS, NSA_GROUPS * CMP_HIDDEN).astype(jnp.bfloat16)
    zz = jnp.zeros_like(w2)
    w2bd = jnp.concatenate([jnp.concatenate([w2, zz], axis=1),
                            jnp.concatenate([zz, w2], axis=1)], axis=0).astype(jnp.bfloat16)
    return both[0], both[1], w2bd


def _overlap_T(n_cmp_pad, n_slc, rows):
    c = np.arange(n_cmp_pad)[None, :] * CMP_STRIDE
    j = np.arange(rows)[:, None] * SLC_BLOCK
    ov = (c < j + SLC_BLOCK) & (c + CMP_LEN > j) & (np.arange(rows)[:, None] < n_slc)
    return jnp.asarray(ov.astype(np.float32), jnp.bfloat16)


def _layer(x, pre_g, post_g, w_in, pos_k, pos_v, w_k1, w_k2, w_v1, w_v2, w_out):
    B, S, _ = x.shape
    nq = S // TILE
    n_cmp = (S - CMP_LEN) // CMP_STRIDE + 1
    n_slc = S // SLC_BLOCK
    C = S // CMP_STRIDE
    nat_cols, tr_cols, out_rows = _column_maps()

    wn = jnp.take(w_in, nat_cols, axis=1).astype(jnp.bfloat16)
    wt = jnp.take(w_in, tr_cols, axis=1).T.astype(jnp.bfloat16)
    (ka, kmean, kcm, vcm, ksl, kwi, sza, gexp, szb,
     qaT, vaT, qbT, vslT, vwiT) = _proj(x, pre_g.reshape(1, D_MODEL), wn, wt)

    wkt, wkb, w2k = _compress_weights(w_k1, w_k2)
    wvt, wvb, w2v = _compress_weights(w_v1, w_v2)
    chunk = CMP_STRIDE * KV_W
    kc, vcT = _compress(
        kcm.reshape(B, C, chunk), vcm.reshape(B, C, chunk), wkt, wkb, wvt, wvb, w2k, w2v,
        pos_k.reshape(1, CMP_LEN * HEAD_DIM).astype(jnp.bfloat16),
        pos_v.reshape(1, CMP_LEN * HEAD_DIM).astype(jnp.bfloat16),
        w_k1.astype(jnp.bfloat16), w_v1.astype(jnp.bfloat16), n_cmp)

    oc, selT = _cmp(qbT, kc, vcT, _overlap_T(C, n_slc, LANES), n_cmp, n_slc)
    oa = _moba(qaT, ka, vaT, kmean.reshape(B, nq, MOBA_W), _key_aug_table(nq, 16, 1))
    osl = _slc(qbT, ksl, vslT, selT, _key_aug_table(nq, selT.shape[2], TILE // SLC_BLOCK))
    ow = _win(qbT, kwi, vwiT, _key_aug_table(nq, 0, 1))

    w_o = jnp.take(w_out, out_rows, axis=0).astype(jnp.bfloat16)
    return _out(x, oa, sza, oc, osl, ow, gexp, szb, w_o, post_g.reshape(1, D_MODEL))


def kernel(x, pre_norm_g, post_norm_g, w_in, cmp_pos_k, cmp_pos_v,
           w_cmp_k1, w_cmp_k2, w_cmp_v1, w_cmp_v2, w_out):
    for l in range(pre_norm_g.shape[0]):
        x = _layer(x, pre_norm_g[l], post_norm_g[l], w_in[l], cmp_pos_k[l], cmp_pos_v[l],
                   w_cmp_k1[l], w_cmp_k2[l], w_cmp_v1[l], w_cmp_v2[l], w_out[l])
    return x
```

```python
import functools

import numpy as np
import jax
import jax.numpy as jnp
from jax import lax
from jax.experimental import pallas as pl
from jax.experimental.pallas import tpu as pltpu

D_MODEL = 1024
HEAD_DIM = 64
N_HEADS = 8
N_PAIRS = N_HEADS // 2
NSA_GROUPS = 2
NSA_HPG = N_HEADS // NSA_GROUPS
MOBA_BLOCK = 256
MOBA_TOPK = 3
CMP_LEN = 32
CMP_STRIDE = 16
CMP_HIDDEN = 128
SLC_BLOCK = 64
SLC_TOPN = 16
WINDOW = 512
RMS_EPS = 1e-6
NEG_INF = -1e30
FORCED_SCORE = 1e9
SCALE = HEAD_DIM ** -0.5

TILE = 256
LANES = 128
HALF = HEAD_DIM
MOBA_W = N_HEADS * HEAD_DIM
NSA_W = N_HEADS * HEAD_DIM
KV_W = NSA_GROUPS * HEAD_DIM
SLOPES = tuple(2.0 ** (-(i + 1)) for i in range(N_HEADS))

VMEM_LIMIT = 48 * 1024 * 1024

_OFF = dict(qa=0, ka=512, va=1024, za=1536, qb=2048, kcm=2560, vcm=2688,
            ksl=2816, vsl=2944, kwi=3072, vwi=3200, gate=3328, zb=3352)
_NAT = dict(ka=(0, 512), kcm=(512, 640), vcm=(640, 768), ksl=(768, 896),
            kwi=(896, 1024), za=(1024, 1536), gexp=(1536, 3072), zb=(3072, 3584))
N_NAT = 3584
_TR = dict(qa=(0, 512), va=(512, 1024), qb=(1024, 1536), vsl=(1536, 1664),
           vwi=(1664, 1792))
N_TR = 1792


def _pair_perm():
    idx = np.zeros(NSA_W, np.int32)
    for p in range(N_PAIRS):
        for g in range(NSA_GROUPS):
            for d in range(HEAD_DIM):
                idx[p * LANES + g * HALF + d] = (g * NSA_HPG + p) * HEAD_DIM + d
    return idx


def _column_maps():
    pp = _pair_perm()
    nat = np.zeros(N_NAT, np.int32)
    nat[0:512] = _OFF["ka"] + np.arange(512)
    nat[512:640] = _OFF["kcm"] + np.arange(128)
    nat[640:768] = _OFF["vcm"] + np.arange(128)
    nat[768:896] = _OFF["ksl"] + np.arange(128)
    nat[896:1024] = _OFF["kwi"] + np.arange(128)
    nat[1024:1536] = _OFF["za"] + np.arange(512)
    for r in range(3):
        nat[1536 + r * 512:1536 + (r + 1) * 512] = _OFF["gate"] + (pp // HEAD_DIM) * 3 + r
    nat[3072:3584] = _OFF["zb"] + pp
    tr = np.zeros(N_TR, np.int32)
    tr[0:512] = _OFF["qa"] + np.arange(512)
    tr[512:1024] = _OFF["va"] + np.arange(512)
    tr[1024:1536] = _OFF["qb"] + pp
    tr[1536:1664] = _OFF["vsl"] + np.arange(128)
    tr[1664:1792] = _OFF["vwi"] + np.arange(128)
    out_rows = np.concatenate([np.arange(512), 512 + pp]).astype(np.int32)
    return nat, tr, out_rows


def _sigmoid(x):
    return 1.0 / (1.0 + jnp.exp(-x))


def _silu(x):
    return x * _sigmoid(x)


def _dot(a, b):
    return jnp.dot(a, b, preferred_element_type=jnp.float32)


def _proj_kernel(x_ref, g_ref, wn_ref, wt_ref,
                 ka_ref, kmean_ref, kcm_ref, vcm_ref, ksl_ref, kwi_ref,
                 sza_ref, gexp_ref, szb_ref,
                 qaT_ref, vaT_ref, qbT_ref, vslT_ref, vwiT_ref):
    x = x_ref[0]
    r = lax.rsqrt(jnp.mean(x * x, axis=-1, keepdims=True) + RMS_EPS)
    h = (x * r * g_ref[...]).astype(jnp.bfloat16)

    def nat(name):
        a, b = _NAT[name]
        return _dot(h, wn_ref[:, a:b])

    ka = nat("ka")
    kmean_ref[0, 0] = jnp.mean(ka, axis=0, keepdims=True)
    for p in range(N_PAIRS):
        ka_ref[0, p] = ka[:, p * LANES:(p + 1) * LANES].astype(jnp.bfloat16)
    kcm_ref[0] = nat("kcm").astype(jnp.bfloat16)
    vcm_ref[0] = nat("vcm").astype(jnp.bfloat16)
    ksl_ref[0] = nat("ksl").astype(jnp.bfloat16)
    kwi_ref[0] = nat("kwi").astype(jnp.bfloat16)
    sza_ref[0] = _silu(nat("za"))
    gexp_ref[0] = _sigmoid(nat("gexp"))
    szb_ref[0] = _silu(nat("zb"))

    def tr(name):
        a, b = _TR[name]
        return lax.dot_general(wt_ref[a:b, :], h, (((1,), (1,)), ((), ())),
                               preferred_element_type=jnp.float32)

    qa = tr("qa") * SCALE
    va = tr("va")
    qb = tr("qb") * SCALE
    for p in range(N_PAIRS):
        sl = slice(p * LANES, (p + 1) * LANES)
        qaT_ref[0, p] = qa[sl].astype(jnp.bfloat16)
        vaT_ref[0, p] = va[sl].astype(jnp.bfloat16)
        qbT_ref[0, p] = qb[sl].astype(jnp.bfloat16)
    vslT_ref[0] = tr("vsl").astype(jnp.bfloat16)
    vwiT_ref[0] = tr("vwi").astype(jnp.bfloat16)


def _proj(x, g, wn, wt):
    B, S, _ = x.shape
    nq = S // TILE
    bf, f32 = jnp.bfloat16, jnp.float32
    slab_nat = lambda: pl.BlockSpec((1, N_PAIRS, TILE, LANES), lambda b, i: (b, 0, i, 0))
    slab_tr = lambda: pl.BlockSpec((1, N_PAIRS, LANES, TILE), lambda b, i: (b, 0, 0, i))
    rows = lambda w: pl.BlockSpec((1, TILE, w), lambda b, i: (b, i, 0))
    out_shape = [
        jax.ShapeDtypeStruct((B, N_PAIRS, S, LANES), bf),
        jax.ShapeDtypeStruct((B, nq, 1, MOBA_W), f32),
        jax.ShapeDtypeStruct((B, S, KV_W), bf),
        jax.ShapeDtypeStruct((B, S, KV_W), bf),
        jax.ShapeDtypeStruct((B, S, KV_W), bf),
        jax.ShapeDtypeStruct((B, S, KV_W), bf),
        jax.ShapeDtypeStruct((B, S, MOBA_W), f32),
        jax.ShapeDtypeStruct((B, S, 3 * NSA_W), f32),
        jax.ShapeDtypeStruct((B, S, NSA_W), f32),
        jax.ShapeDtypeStruct((B, N_PAIRS, LANES, S), bf),
        jax.ShapeDtypeStruct((B, N_PAIRS, LANES, S), bf),
        jax.ShapeDtypeStruct((B, N_PAIRS, LANES, S), bf),
        jax.ShapeDtypeStruct((B, KV_W, S), bf),
        jax.ShapeDtypeStruct((B, KV_W, S), bf),
    ]
    out_specs = [
        slab_nat(),
        pl.BlockSpec((1, 1, 1, MOBA_W), lambda b, i: (b, i, 0, 0)),
        rows(KV_W), rows(KV_W), rows(KV_W), rows(KV_W),
        rows(MOBA_W), rows(3 * NSA_W), rows(NSA_W),
        slab_tr(), slab_tr(), slab_tr(),
        pl.BlockSpec((1, KV_W, TILE), lambda b, i: (b, 0, i)),
        pl.BlockSpec((1, KV_W, TILE), lambda b, i: (b, 0, i)),
    ]
    return pl.pallas_call(
        _proj_kernel,
        grid=(B, nq),
        in_specs=[
            pl.BlockSpec((1, TILE, D_MODEL), lambda b, i: (b, i, 0)),
            pl.BlockSpec((1, D_MODEL), lambda b, i: (0, 0)),
            pl.BlockSpec((D_MODEL, N_NAT), lambda b, i: (0, 0)),
            pl.BlockSpec((N_TR, D_MODEL), lambda b, i: (0, 0)),
        ],
        out_specs=out_specs,
        out_shape=out_shape,
        compiler_params=pltpu.CompilerParams(
            dimension_semantics=("arbitrary", "arbitrary"),
            vmem_limit_bytes=VMEM_LIMIT),
        name="proj",
    )(x, g, wn, wt)


def _compress_kernel(n_cmp, xk_ref, xv_ref, wkt_ref, wkb_ref, wvt_ref, wvb_ref,
                     w2k_ref, w2v_ref, pek_ref, pev_ref, w1k_ref, w1v_ref,
                     kc_ref, vcT_ref):
    C = xk_ref.shape[1]
    row = lax.broadcasted_iota(jnp.int32, (C, KV_W), 0)

    def phi(x_ref, wt_ref, wb_ref, w2_ref, pe_ref, w1_ref):
        x = x_ref[0]
        top = _dot(x, wt_ref[...])
        bot = _dot(x, wb_ref[...])
        peb = _dot(jnp.broadcast_to(pe_ref[...], (8, pe_ref.shape[1])), w1_ref[...])[0:1]
        peb = jnp.concatenate([peb, peb], axis=1)
        hid = top + pltpu.roll(bot, C - 1, 0) + peb
        out = _dot(_silu(hid).astype(jnp.bfloat16), w2_ref[...])
        return jnp.where(row < n_cmp, out, 0.0)

    kc_ref[0] = phi(xk_ref, wkt_ref, wkb_ref, w2k_ref, pek_ref, w1k_ref).astype(jnp.bfloat16)
    vc = phi(xv_ref, wvt_ref, wvb_ref, w2v_ref, pev_ref, w1v_ref)
    vcT_ref[0] = vc.T.astype(jnp.bfloat16)


def _compress(xk, xv, wkt, wkb, wvt, wvb, w2k, w2v, pek, pev, w1k, w1v, n_cmp):
    B, C, F = xk.shape
    full = lambda a: pl.BlockSpec(a.shape, lambda b: (0,) * a.ndim)
    return pl.pallas_call(
        functools.partial(_compress_kernel, n_cmp),
        grid=(B,),
        in_specs=[pl.BlockSpec((1, C, F), lambda b: (b, 0, 0)),
                  pl.BlockSpec((1, C, F), lambda b: (b, 0, 0)),
                  full(wkt), full(wkb), full(wvt), full(wvb), full(w2k), full(w2v),
                  full(pek), full(pev), full(w1k), full(w1v)],
        out_specs=[pl.BlockSpec((1, C, KV_W), lambda b: (b, 0, 0)),
                   pl.BlockSpec((1, KV_W, C), lambda b: (b, 0, 0))],
        out_shape=[jax.ShapeDtypeStruct((B, C, KV_W), jnp.bfloat16),
                   jax.ShapeDtypeStruct((B, KV_W, C), jnp.bfloat16)],
        compiler_params=pltpu.CompilerParams(
            dimension_semantics=("arbitrary",), vmem_limit_bytes=VMEM_LIMIT),
        name="compress",
    )(xk, xv, wkt, wkb, wvt, wvb, w2k, w2v, pek, pev, w1k, w1v)


def _keep_half(qT, half):
    z = jnp.zeros((HALF, qT.shape[1]), qT.dtype)
    if half == 0:
        return jnp.concatenate([qT[0:HALF], z], axis=0)
    return jnp.concatenate([z, qT[HALF:2 * HALF]], axis=0)


def _rank_rows(vals, n_rows):
    R, T = vals.shape
    j = lax.broadcasted_iota(jnp.int32, (R, T), 0)
    rank = jnp.zeros((R, T), jnp.int32)
    for m in range(n_rows):
        vm = vals[m:m + 1, :]
        beats = (vm > vals) | ((vm == vals) & (j > m))
        rank = rank + beats.astype(jnp.int32)
    return rank


def _alibi_rows(n_rows, T, first, slope, i):
    r = lax.broadcasted_iota(jnp.int32, (n_rows, T), 0)
    trel = lax.broadcasted_iota(jnp.int32, (n_rows, T), 1).astype(jnp.float32)
    base = (i * TILE).astype(jnp.float32)
    out = jnp.where(r == first, -slope * trel, 0.0)
    out = jnp.where(r == first + 1, -slope * base, out)
    out = jnp.where((r == first + 2) | (r == first + 3), slope, out)
    return out


def _key_aug_table(nq, n_sel, blocks_per_tile):
    t = np.zeros((nq, TILE, LANES), np.float32)
    krel = np.arange(TILE)
    for j in range(nq):
        if n_sel:
            blk = j * blocks_per_tile + krel // (TILE // blocks_per_tile)
            t[j, krel, blk] = 1.0
        t[j, :, n_sel] = 1.0
        t[j, :, n_sel + 1] = 1.0
        t[j, :, n_sel + 2] = krel
        t[j, :, n_sel + 3] = TILE * j
    return jnp.asarray(t, jnp.bfloat16)


def _online_step(sT, vT_h, m_ref, l_ref, acc_ref, idx):
    m_prev = m_ref[idx]
    m_new = jnp.maximum(m_prev, jnp.max(sT, axis=0, keepdims=True))
    alpha = jnp.exp(m_prev - m_new)
    p = jnp.exp(sT - m_new)
    l_ref[idx] = alpha * l_ref[idx] + jnp.sum(p, axis=0, keepdims=True)
    acc_ref[idx] = alpha * acc_ref[idx] + _dot(vT_h, p.astype(jnp.bfloat16))
    m_ref[idx] = m_new


def _attend_heads(score_fn, value_fn, keep, s_ref, m_ref, l_ref, acc_ref):
    for h in range(N_HEADS):
        s_ref[h] = score_fn(h)
    for h in range(N_HEADS):
        sT = s_ref[h]
        if keep is not None:
            sT = jnp.where(keep, sT, NEG_INF)
        _online_step(sT, value_fn(h), m_ref, l_ref, acc_ref, h)


def _init_state(m_ref, l_ref, acc_ref):
    m_ref[...] = jnp.full(m_ref.shape, NEG_INF, jnp.float32)
    l_ref[...] = jnp.zeros(l_ref.shape, jnp.float32)
    acc_ref[...] = jnp.zeros(acc_ref.shape, jnp.float32)


def _lower_tri():
    k = lax.broadcasted_iota(jnp.int32, (TILE, TILE), 0)
    t = lax.broadcasted_iota(jnp.int32, (TILE, TILE), 1)
    return k <= t


def _finalize_pair(o_ref, l_ref, acc_ref, p):
    oT = jnp.concatenate([acc_ref[2 * p] / l_ref[2 * p],
                          acc_ref[2 * p + 1] / l_ref[2 * p + 1]], axis=0)
    o_ref[0, :, p * LANES:(p + 1) * LANES] = oT.T


def _tri_tables(nq):
    it, jt = [], []
    for i in range(nq):
        for j in range(i + 1):
            it.append(i)
            jt.append(j)
    return jnp.asarray(it, jnp.int32), jnp.asarray(jt, jnp.int32)


_STATE_SCRATCH = lambda: [
    pltpu.VMEM((N_HEADS, 2 * LANES, TILE), jnp.bfloat16),
    pltpu.VMEM((N_HEADS, 1, TILE), jnp.float32),
    pltpu.VMEM((N_HEADS, 1, TILE), jnp.float32),
    pltpu.VMEM((N_HEADS, HALF, TILE), jnp.float32),
    pltpu.VMEM((N_HEADS, TILE, TILE), jnp.float32),
]


def _moba_kernel(nq, it_ref, jt_ref, qT_ref, k_ref, vT_ref, kmean_ref, kaug_ref, o_ref,
                 qaug_ref, m_ref, l_ref, acc_ref, s_ref):
    s = pl.program_id(1)
    i = it_ref[s]
    j = jt_ref[s]
    n_sel = 16

    @pl.when(j == 0)
    def _():
        _init_state(m_ref, l_ref, acc_ref)
        n_iota = lax.broadcasted_iota(jnp.int32, (nq, TILE), 0)
        for p in range(N_PAIRS):
            qT = qT_ref[0, p]
            km = kmean_ref[0, :, p * LANES:(p + 1) * LANES].astype(jnp.bfloat16)
            for hh in range(2):
                h = 2 * p + hh
                qm = _keep_half(qT, hh)
                gate = jnp.where(n_iota < i, _dot(km, qm), NEG_INF)
                rank = _rank_rows(gate, nq)
                sel = ((rank < MOBA_TOPK) & (n_iota < i)) | (n_iota == i)
                selneg = jnp.where(sel, 0.0, NEG_INF)
                if nq < n_sel:
                    selneg = jnp.concatenate(
                        [selneg, jnp.zeros((n_sel - nq, TILE), jnp.float32)], axis=0)
                ali = _alibi_rows(LANES - n_sel, TILE, 0, SLOPES[h], i)
                aug = jnp.concatenate([selneg, ali], axis=0).astype(jnp.bfloat16)
                qaug_ref[h] = jnp.concatenate([qm, aug], axis=0)

    def step(diag):
        kaug = kaug_ref[0]

        def score(h):
            return _dot(jnp.concatenate([k_ref[0, h // 2], kaug], axis=1), qaug_ref[h])

        def value(h):
            return vT_ref[0, h // 2, (h % 2) * HALF:(h % 2 + 1) * HALF]

        _attend_heads(score, value, _lower_tri() if diag else None,
                      s_ref, m_ref, l_ref, acc_ref)
        if diag:
            for p in range(N_PAIRS):
                _finalize_pair(o_ref, l_ref, acc_ref, p)

    @pl.when(j < i)
    def _():
        step(False)

    @pl.when(j == i)
    def _():
        step(True)


def _moba(qT, k, vT, kmean, kaug):
    B, _, _, S = qT.shape
    nq = S // TILE
    it, jt = _tri_tables(nq)
    return pl.pallas_call(
        functools.partial(_moba_kernel, nq),
        grid_spec=pltpu.PrefetchScalarGridSpec(
            num_scalar_prefetch=2,
            grid=(B, int(it.shape[0])),
            in_specs=[
                pl.BlockSpec((1, N_PAIRS, LANES, TILE), lambda b, s, it, jt: (b, 0, 0, it[s])),
                pl.BlockSpec((1, N_PAIRS, TILE, LANES), lambda b, s, it, jt: (b, 0, jt[s], 0)),
                pl.BlockSpec((1, N_PAIRS, LANES, TILE), lambda b, s, it, jt: (b, 0, 0, jt[s])),
                pl.BlockSpec((1, nq, MOBA_W), lambda b, s, it, jt: (b, 0, 0)),
                pl.BlockSpec((1, TILE, LANES), lambda b, s, it, jt: (jt[s], 0, 0)),
            ],
            out_specs=pl.BlockSpec((1, TILE, MOBA_W), lambda b, s, it, jt: (b, it[s], 0)),
            scratch_shapes=_STATE_SCRATCH(),
        ),
        out_shape=jax.ShapeDtypeStruct((B, S, MOBA_W), jnp.float32),
        compiler_params=pltpu.CompilerParams(
            dimension_semantics=("arbitrary", "arbitrary"), vmem_limit_bytes=VMEM_LIMIT),
        name="moba",
    )(it, jt, qT, k, vT, kmean, kaug)


def _cmp_kernel(n_cmp, n_slc, qT_ref, kc_ref, vcT_ref, ovT_ref, o_ref, sel_ref):
    i = pl.program_id(1)
    C = kc_ref.shape[1]
    kc = kc_ref[0]
    vcT = vcT_ref[0]
    c_iota = lax.broadcasted_iota(jnp.int32, (C, TILE), 0)
    t_abs = i * TILE + lax.broadcasted_iota(jnp.int32, (C, TILE), 1)
    cmask = (c_iota * CMP_STRIDE + (CMP_LEN - 1) <= t_abs) & (c_iota < n_cmp)

    psum = [jnp.zeros((C, TILE), jnp.float32) for _ in range(NSA_GROUPS)]
    for p in range(N_PAIRS):
        qT = qT_ref[0, p]
        halves = []
        for g in range(NSA_GROUPS):
            qm = _keep_half(qT, g)
            sT = jnp.where(cmask, _dot(kc, qm), NEG_INF)
            m = jnp.max(sT, axis=0, keepdims=True)
            e = jnp.where(cmask, jnp.exp(sT - m), 0.0)
            pn = e / jnp.maximum(jnp.sum(e, axis=0, keepdims=True), 1e-30)
            psum[g] = psum[g] + pn
            halves.append(_dot(vcT[g * HALF:(g + 1) * HALF], pn.astype(jnp.bfloat16)))
        o_ref[0, :, p * LANES:(p + 1) * LANES] = jnp.concatenate(halves, axis=0).T

    R = sel_ref.shape[2]
    j_iota = lax.broadcasted_iota(jnp.int32, (R, TILE), 0)
    t_q = i * TILE + lax.broadcasted_iota(jnp.int32, (R, TILE), 1)
    own = lax.shift_right_logical(t_q, int(np.log2(SLC_BLOCK)))
    forced = (j_iota == 0) | (j_iota == own) | (j_iota == own - 1)
    causal = j_iota <= own
    ovT = ovT_ref[...]
    for g in range(NSA_GROUPS):
        hi = psum[g].astype(jnp.bfloat16)
        lo = (psum[g] - hi.astype(jnp.float32)).astype(jnp.bfloat16)
        imp = (_dot(ovT, hi) + _dot(ovT, lo))[0:R]
        vals = jnp.where(forced, FORCED_SCORE, jnp.where(causal, imp, NEG_INF))
        rank = _rank_rows(vals, n_slc)
        sel = (rank < min(SLC_TOPN, n_slc)) & causal
        sel_ref[0, g] = jnp.where(sel, 0.0, NEG_INF).astype(jnp.bfloat16)


def _cmp(qbT, kc, vcT, ovT, n_cmp, n_slc):
    B, _, _, S = qbT.shape
    C = kc.shape[1]
    nq = S // TILE
    R = 64
    return pl.pallas_call(
        functools.partial(_cmp_kernel, n_cmp, n_slc),
        grid=(B, nq),
        in_specs=[
            pl.BlockSpec((1, N_PAIRS, LANES, TILE), lambda b, i: (b, 0, 0, i)),
            pl.BlockSpec((1, C, KV_W), lambda b, i: (b, 0, 0)),
            pl.BlockSpec((1, KV_W, C), lambda b, i: (b, 0, 0)),
            pl.BlockSpec(ovT.shape, lambda b, i: (0, 0)),
        ],
        out_specs=[pl.BlockSpec((1, TILE, NSA_W), lambda b, i: (b, i, 0)),
                   pl.BlockSpec((1, NSA_GROUPS, R, TILE), lambda b, i: (b, 0, 0, i))],
        out_shape=[jax.ShapeDtypeStruct((B, S, NSA_W), jnp.float32),
                   jax.ShapeDtypeStruct((B, NSA_GROUPS, R, S), jnp.bfloat16)],
        compiler_params=pltpu.CompilerParams(
            dimension_semantics=("arbitrary", "arbitrary"), vmem_limit_bytes=VMEM_LIMIT),
        name="cmp",
    )(qbT, kc, vcT, ovT)


def _slc_kernel(it_ref, jt_ref, qT_ref, k_ref, vT_ref, sel_ref, kaug_ref, o_ref,
                qaug_ref, m_ref, l_ref, acc_ref, s_ref):
    s = pl.program_id(1)
    i = it_ref[s]
    j = jt_ref[s]
    n_sel = sel_ref.shape[2]

    @pl.when(j == 0)
    def _():
        _init_state(m_ref, l_ref, acc_ref)
        for p in range(N_PAIRS):
            qT = qT_ref[0, p]
            for g in range(NSA_GROUPS):
                h = 2 * p + g
                qm = _keep_half(qT, g)
                ali = _alibi_rows(LANES - n_sel, TILE, 0, SLOPES[g * NSA_HPG + p], i)
                qaug_ref[h] = jnp.concatenate(
                    [qm, sel_ref[0, g], ali.astype(jnp.bfloat16)], axis=0)

    def step(diag):
        kfull = jnp.concatenate([k_ref[0], kaug_ref[0]], axis=1)

        def score(h):
            return _dot(kfull, qaug_ref[h])

        def value(h):
            return vT_ref[0, (h % 2) * HALF:(h % 2 + 1) * HALF]

        _attend_heads(score, value, _lower_tri() if diag else None,
                      s_ref, m_ref, l_ref, acc_ref)
        if diag:
            for p in range(N_PAIRS):
                _finalize_pair(o_ref, l_ref, acc_ref, p)

    @pl.when(j < i)
    def _():
        step(False)

    @pl.when(j == i)
    def _():
        step(True)


def _slc(qbT, ksl, vslT, selT, kaug):
    B, _, _, S = qbT.shape
    nq = S // TILE
    R = selT.shape[2]
    it, jt = _tri_tables(nq)
    return pl.pallas_call(
        _slc_kernel,
        grid_spec=pltpu.PrefetchScalarGridSpec(
            num_scalar_prefetch=2,
            grid=(B, int(it.shape[0])),
            in_specs=[
                pl.BlockSpec((1, N_PAIRS, LANES, TILE), lambda b, s, it, jt: (b, 0, 0, it[s])),
                pl.BlockSpec((1, TILE, KV_W), lambda b, s, it, jt: (b, jt[s], 0)),
                pl.BlockSpec((1, KV_W, TILE), lambda b, s, it, jt: (b, 0, jt[s])),
                pl.BlockSpec((1, NSA_GROUPS, R, TILE), lambda b, s, it, jt: (b, 0, 0, it[s])),
                pl.BlockSpec((1, TILE, LANES), lambda b, s, it, jt: (jt[s], 0, 0)),
            ],
            out_specs=pl.BlockSpec((1, TILE, NSA_W), lambda b, s, it, jt: (b, it[s], 0)),
            scratch_shapes=_STATE_SCRATCH(),
        ),
        out_shape=jax.ShapeDtypeStruct((B, S, NSA_W), jnp.float32),
        compiler_params=pltpu.CompilerParams(
            dimension_semantics=("arbitrary", "arbitrary"), vmem_limit_bytes=VMEM_LIMIT),
        name="slc",
    )(it, jt, qbT, ksl, vslT, selT, kaug)


def _win_kernel(qT_ref, k2_ref, k1_ref, k0_ref, v2_ref, v1_ref, v0_ref, kaug_ref, o_ref,
                qaug_ref, m_ref, l_ref, acc_ref, s_ref):
    i = pl.program_id(1)
    _init_state(m_ref, l_ref, acc_ref)
    for p in range(N_PAIRS):
        qT = qT_ref[0, p]
        for g in range(NSA_GROUPS):
            h = 2 * p + g
            qm = _keep_half(qT, g)
            ali = _alibi_rows(LANES, TILE, 0, SLOPES[g * NSA_HPG + p], i)
            qaug_ref[h] = jnp.concatenate([qm, ali.astype(jnp.bfloat16)], axis=0)

    def step(delta, k_ref, vT_ref):
        jabs = jnp.maximum(i - delta, 0)
        kfull = jnp.concatenate([k_ref[0], kaug_ref[jabs]], axis=1)
        if delta == 0:
            keep = _lower_tri()
        elif delta == WINDOW // TILE:
            keep = jnp.logical_not(_lower_tri())
        else:
            keep = None

        def score(h):
            return _dot(kfull, qaug_ref[h])

        def value(h):
            return vT_ref[0, (h % 2) * HALF:(h % 2 + 1) * HALF]

        _attend_heads(score, value, keep, s_ref, m_ref, l_ref, acc_ref)

    @pl.when(i >= 2)
    def _():
        step(2, k2_ref, v2_ref)

    @pl.when(i >= 1)
    def _():
        step(1, k1_ref, v1_ref)

    step(0, k0_ref, v0_ref)
    for p in range(N_PAIRS):
        _finalize_pair(o_ref, l_ref, acc_ref, p)


def _win(qbT, kwi, vwiT, kaug):
    B, _, _, S = qbT.shape
    nq = S // TILE
    kspec = lambda d: pl.BlockSpec((1, TILE, KV_W), lambda b, i: (b, jnp.maximum(i - d, 0), 0))
    vspec = lambda d: pl.BlockSpec((1, KV_W, TILE), lambda b, i: (b, 0, jnp.maximum(i - d, 0)))
    return pl.pallas_call(
        _win_kernel,
        grid=(B, nq),
        in_specs=[
            pl.BlockSpec((1, N_PAIRS, LANES, TILE), lambda b, i: (b, 0, 0, i)),
            kspec(2), kspec(1), kspec(0), vspec(2), vspec(1), vspec(0),
            pl.BlockSpec(kaug.shape, lambda b, i: (0, 0, 0)),
        ],
        out_specs=pl.BlockSpec((1, TILE, NSA_W), lambda b, i: (b, i, 0)),
        out_shape=jax.ShapeDtypeStruct((B, S, NSA_W), jnp.float32),
        scratch_shapes=_STATE_SCRATCH(),
        compiler_params=pltpu.CompilerParams(
            dimension_semantics=("arbitrary", "arbitrary"), vmem_limit_bytes=VMEM_LIMIT),
        name="win",
    )(qbT, kwi, kwi, kwi, vwiT, vwiT, vwiT, kaug)


def _out_kernel(x_ref, oa_ref, sza_ref, oc_ref, os_ref, ow_ref, gexp_ref, szb_ref,
                w_ref, g_ref, y_ref):
    mixa = oa_ref[0] * sza_ref[0]
    ge = gexp_ref[0]
    ob = (ge[:, 0:NSA_W] * oc_ref[0] + ge[:, NSA_W:2 * NSA_W] * os_ref[0]
          + ge[:, 2 * NSA_W:3 * NSA_W] * ow_ref[0])
    mix = jnp.concatenate([mixa, ob * szb_ref[0]], axis=1).astype(jnp.bfloat16)
    y = _dot(mix, w_ref[...])
    r = lax.rsqrt(jnp.mean(y * y, axis=-1, keepdims=True) + RMS_EPS)
    y_ref[0] = x_ref[0] + y * r * g_ref[...]


def _out(x, oa, sza, oc, os_, ow, gexp, szb, w, g):
    B, S, _ = x.shape
    rows = lambda wd: pl.BlockSpec((1, TILE, wd), lambda b, i: (b, i, 0))
    return pl.pallas_call(
        _out_kernel,
        grid=(B, S // TILE),
        in_specs=[rows(D_MODEL), rows(MOBA_W), rows(MOBA_W), rows(NSA_W), rows(NSA_W),
                  rows(NSA_W), rows(3 * NSA_W), rows(NSA_W),
                  pl.BlockSpec(w.shape, lambda b, i: (0, 0)),
                  pl.BlockSpec((1, D_MODEL), lambda b, i: (0, 0))],
        out_specs=rows(D_MODEL),
        out_shape=jax.ShapeDtypeStruct((B, S, D_MODEL), jnp.float32),
        compiler_params=pltpu.CompilerParams(
            dimension_semantics=("arbitrary", "arbitrary"), vmem_limit_bytes=VMEM_LIMIT),
        name="out",
    )(x, oa, sza, oc, os_, ow, gexp, szb, w, g)


def _compress_weights(w1, w2):
    half = (CMP_LEN // 2) * HEAD_DIM
    w1r = w1.reshape(2, CMP_LEN // 2, HEAD_DIM, CMP_HIDDEN)
    z = jnp.zeros_like(w1r)
    g0 = jnp.concatenate([w1r, z], axis=-1)
    g1 = jnp.concatenate([z, w1r], axis=-1)
    both = jnp.stack([g0, g1], axis=2)
    both = both.reshape(2, half * NSA_GROUPS, NSA_GROUPS * CMP_HIDDEN).astype(jnp.bfloat16)
    zz = jnp.zeros_like(w2)
    w2bd = jnp.concatenate([jnp.concatenate([w2, zz], axis=1),
                            jnp.concatenate([zz, w2], axis=1)], axis=0).astype(jnp.bfloat16)
    return both[0], both[1], w2bd


def _overlap_T(n_cmp_pad, n_slc, rows):
    c = np.arange(n_cmp_pad)[None, :] * CMP_STRIDE
    j = np.arange(rows)[:, None] * SLC_BLOCK
    ov = (c < j + SLC_BLOCK) & (c + CMP_LEN > j) & (np.arange(rows)[:, None] < n_slc)
    return jnp.asarray(ov.astype(np.float32), jnp.bfloat16)


def _layer(x, pre_g, post_g, w_in, pos_k, pos_v, w_k1, w_k2, w_v1, w_v2, w_out):
    B, S, _ = x.shape
    nq = S // TILE
    n_cmp = (S - CMP_LEN) // CMP_STRIDE + 1
    n_slc = S // SLC_BLOCK
    C = S // CMP_STRIDE
    nat_cols, tr_cols, out_rows = _column_maps()

    wn = jnp.take(w_in, nat_cols, axis=1).astype(jnp.bfloat16)
    wt = jnp.take(w_in, tr_cols, axis=1).T.astype(jnp.bfloat16)
    (ka, kmean, kcm, vcm, ksl, kwi, sza, gexp, szb,
     qaT, vaT, qbT, vslT, vwiT) = _proj(x, pre_g.reshape(1, D_MODEL), wn, wt)

    wkt, wkb, w2k = _compress_weights(w_k1, w_k2)
    wvt, wvb, w2v = _compress_weights(w_v1, w_v2)
    chunk = CMP_STRIDE * KV_W
    kc, vcT = _compress(
        kcm.reshape(B, C, chunk), vcm.reshape(B, C, chunk), wkt, wkb, wvt, wvb, w2k, w2v,
        pos_k.reshape(1, CMP_LEN * HEAD_DIM).astype(jnp.bfloat16),
        pos_v.reshape(1, CMP_LEN * HEAD_DIM).astype(jnp.bfloat16),
        w_k1.astype(jnp.bfloat16), w_v1.astype(jnp.bfloat16), n_cmp)

    oc, selT = _cmp(qbT, kc, vcT, _overlap_T(C, n_slc, LANES), n_cmp, n_slc)
    oa = _moba(qaT, ka, vaT, kmean.reshape(B, nq, MOBA_W), _key_aug_table(nq, 16, 1))
    osl = _slc(qbT, ksl, vslT, selT, _key_aug_table(nq, selT.shape[2], TILE // SLC_BLOCK))
    ow = _win(qbT, kwi, vwiT, _key_aug_table(nq, 0, 1))

    w_o = jnp.take(w_out, out_rows, axis=0).astype(jnp.bfloat16)
    return _out(x, oa, sza, oc, osl, ow, gexp, szb, w_o, post_g.reshape(1, D_MODEL))


def kernel(x, pre_norm_g, post_norm_g, w_in, cmp_pos_k, cmp_pos_v,
           w_cmp_k1, w_cmp_k2, w_cmp_v1, w_cmp_v2, w_out):
    for l in range(pre_norm_g.shape[0]):
        x = _layer(x, pre_norm_g[l], post_norm_g[l], w_in[l], cmp_pos_k[l], cmp_pos_v[l],
                   w_cmp_k1[l], w_cmp_k2[l], w_cmp_v1[l], w_cmp_v2[l], w_out[l])
    return x
```

```python
import functools

import numpy as np
import jax
import jax.numpy as jnp
from jax import lax
from jax.experimental import pallas as pl
from jax.experimental.pallas import tpu as pltpu

D_MODEL = 1024
HEAD_DIM = 64
N_HEADS = 8
N_PAIRS = N_HEADS // 2
NSA_GROUPS = 2
NSA_HPG = N_HEADS // NSA_GROUPS
MOBA_BLOCK = 256
MOBA_TOPK = 3
CMP_LEN = 32
CMP_STRIDE = 16
CMP_HIDDEN = 128
SLC_BLOCK = 64
SLC_TOPN = 16
WINDOW = 512
RMS_EPS = 1e-6
NEG_INF = -1e30
FORCED_SCORE = 1e9
SCALE = HEAD_DIM ** -0.5

TILE = 256
LANES = 128
HALF = HEAD_DIM
MOBA_W = N_HEADS * HEAD_DIM
NSA_W = N_HEADS * HEAD_DIM
KV_W = NSA_GROUPS * HEAD_DIM
SLOPES = tuple(2.0 ** (-(i + 1)) for i in range(N_HEADS))

VMEM_LIMIT = 48 * 1024 * 1024

_OFF = dict(qa=0, ka=512, va=1024, za=1536, qb=2048, kcm=2560, vcm=2688,
            ksl=2816, vsl=2944, kwi=3072, vwi=3200, gate=3328, zb=3352)
_NAT = dict(ka=(0, 512), kcm=(512, 640), vcm=(640, 768), ksl=(768, 896),
            kwi=(896, 1024), za=(1024, 1536), gexp=(1536, 3072), zb=(3072, 3584))
N_NAT = 3584
_TR = dict(qa=(0, 512), va=(512, 1024), qb=(1024, 1536), vsl=(1536, 1664),
           vwi=(1664, 1792))
N_TR = 1792


def _pair_perm():
    idx = np.zeros(NSA_W, np.int32)
    for p in range(N_PAIRS):
        for g in range(NSA_GROUPS):
            for d in range(HEAD_DIM):
                idx[p * LANES + g * HALF + d] = (g * NSA_HPG + p) * HEAD_DIM + d
    return idx


def _column_maps():
    pp = _pair_perm()
    nat = np.zeros(N_NAT, np.int32)
    nat[0:512] = _OFF["ka"] + np.arange(512)
    nat[512:640] = _OFF["kcm"] + np.arange(128)
    nat[640:768] = _OFF["vcm"] + np.arange(128)
    nat[768:896] = _OFF["ksl"] + np.arange(128)
    nat[896:1024] = _OFF["kwi"] + np.arange(128)
    nat[1024:1536] = _OFF["za"] + np.arange(512)
    for r in range(3):
        nat[1536 + r * 512:1536 + (r + 1) * 512] = _OFF["gate"] + (pp // HEAD_DIM) * 3 + r
    nat[3072:3584] = _OFF["zb"] + pp
    tr = np.zeros(N_TR, np.int32)
    tr[0:512] = _OFF["qa"] + np.arange(512)
    tr[512:1024] = _OFF["va"] + np.arange(512)
    tr[1024:1536] = _OFF["qb"] + pp
    tr[1536:1664] = _OFF["vsl"] + np.arange(128)
    tr[1664:1792] = _OFF["vwi"] + np.arange(128)
    out_rows = np.concatenate([np.arange(512), 512 + pp]).astype(np.int32)
    return nat, tr, out_rows


def _sigmoid(x):
    return 1.0 / (1.0 + jnp.exp(-x))


def _silu(x):
    return x * _sigmoid(x)


def _dot(a, b):
    return jnp.dot(a, b, preferred_element_type=jnp.float32)


def _proj_kernel(x_ref, g_ref, wn_ref, wt_ref,
                 ka_ref, kmean_ref, kcm_ref, vcm_ref, ksl_ref, kwi_ref,
                 sza_ref, gexp_ref, szb_ref,
                 qaT_ref, vaT_ref, qbT_ref, vslT_ref, vwiT_ref):
    x = x_ref[0]
    r = lax.rsqrt(jnp.mean(x * x, axis=-1, keepdims=True) + RMS_EPS)
    h = (x * r * g_ref[...]).astype(jnp.bfloat16)

    def nat(name):
        a, b = _NAT[name]
        return _dot(h, wn_ref[:, a:b])

    ka = nat("ka")
    kmean_ref[0, 0] = jnp.mean(ka, axis=0, keepdims=True)
    for p in range(N_PAIRS):
        ka_ref[0, p] = ka[:, p * LANES:(p + 1) * LANES].astype(jnp.bfloat16)
    kcm_ref[0] = nat("kcm").astype(jnp.bfloat16)
    vcm_ref[0] = nat("vcm").astype(jnp.bfloat16)
    ksl_ref[0] = nat("ksl").astype(jnp.bfloat16)
    kwi_ref[0] = nat("kwi").astype(jnp.bfloat16)
    sza_ref[0] = _silu(nat("za"))
    gexp_ref[0] = _sigmoid(nat("gexp"))
    szb_ref[0] = _silu(nat("zb"))

    def tr(name):
        a, b = _TR[name]
        return lax.dot_general(wt_ref[a:b, :], h, (((1,), (1,)), ((), ())),
                               preferred_element_type=jnp.float32)

    qa = tr("qa") * SCALE
    va = tr("va")
    qb = tr("qb") * SCALE
    for p in range(N_PAIRS):
        sl = slice(p * LANES, (p + 1) * LANES)
        qaT_ref[0, p] = qa[sl].astype(jnp.bfloat16)
        vaT_ref[0, p] = va[sl].astype(jnp.bfloat16)
        qbT_ref[0, p] = qb[sl].astype(jnp.bfloat16)
    vslT_ref[0] = tr("vsl").astype(jnp.bfloat16)
    vwiT_ref[0] = tr("vwi").astype(jnp.bfloat16)


def _proj(x, g, wn, wt):
    B, S, _ = x.shape
    nq = S // TILE
    bf, f32 = jnp.bfloat16, jnp.float32
    slab_nat = lambda: pl.BlockSpec((1, N_PAIRS, TILE, LANES), lambda b, i: (b, 0, i, 0))
    slab_tr = lambda: pl.BlockSpec((1, N_PAIRS, LANES, TILE), lambda b, i: (b, 0, 0, i))
    rows = lambda w: pl.BlockSpec((1, TILE, w), lambda b, i: (b, i, 0))
    out_shape = [
        jax.ShapeDtypeStruct((B, N_PAIRS, S, LANES), bf),
        jax.ShapeDtypeStruct((B, nq, 1, MOBA_W), f32),
        jax.ShapeDtypeStruct((B, S, KV_W), bf),
        jax.ShapeDtypeStruct((B, S, KV_W), bf),
        jax.ShapeDtypeStruct((B, S, KV_W), bf),
        jax.ShapeDtypeStruct((B, S, KV_W), bf),
        jax.ShapeDtypeStruct((B, S, MOBA_W), f32),
        jax.ShapeDtypeStruct((B, S, 3 * NSA_W), f32),
        jax.ShapeDtypeStruct((B, S, NSA_W), f32),
        jax.ShapeDtypeStruct((B, N_PAIRS, LANES, S), bf),
        jax.ShapeDtypeStruct((B, N_PAIRS, LANES, S), bf),
        jax.ShapeDtypeStruct((B, N_PAIRS, LANES, S), bf),
        jax.ShapeDtypeStruct((B, KV_W, S), bf),
        jax.ShapeDtypeStruct((B, KV_W, S), bf),
    ]
    out_specs = [
        slab_nat(),
        pl.BlockSpec((1, 1, 1, MOBA_W), lambda b, i: (b, i, 0, 0)),
        rows(KV_W), rows(KV_W), rows(KV_W), rows(KV_W),
        rows(MOBA_W), rows(3 * NSA_W), rows(NSA_W),
        slab_tr(), slab_tr(), slab_tr(),
        pl.BlockSpec((1, KV_W, TILE), lambda b, i: (b, 0, i)),
        pl.BlockSpec((1, KV_W, TILE), lambda b, i: (b, 0, i)),
    ]
    return pl.pallas_call(
        _proj_kernel,
        grid=(B, nq),
        in_specs=[
            pl.BlockSpec((1, TILE, D_MODEL), lambda b, i: (b, i, 0)),
            pl.BlockSpec((1, D_MODEL), lambda b, i: (0, 0)),
            pl.BlockSpec((D_MODEL, N_NAT), lambda b, i: (0, 0)),
            pl.BlockSpec((N_TR, D_MODEL), lambda b, i: (0, 0)),
        ],
        out_specs=out_specs,
        out_shape=out_shape,
        compiler_params=pltpu.CompilerParams(
            dimension_semantics=("arbitrary", "arbitrary"),
            vmem_limit_bytes=VMEM_LIMIT),
        name="proj",
    )(x, g, wn, wt)


def _compress_kernel(n_cmp, xk_ref, xv_ref, wkt_ref, wkb_ref, wvt_ref, wvb_ref,
                     w2k_ref, w2v_ref, pek_ref, pev_ref, w1k_ref, w1v_ref,
                     kc_ref, vcT_ref):
    C = xk_ref.shape[1]
    row = lax.broadcasted_iota(jnp.int32, (C, KV_W), 0)

    def phi(x_ref, wt_ref, wb_ref, w2_ref, pe_ref, w1_ref):
        x = x_ref[0]
        top = _dot(x, wt_ref[...])
        bot = _dot(x, wb_ref[...])
        peb = _dot(jnp.broadcast_to(pe_ref[...], (8, pe_ref.shape[1])), w1_ref[...])[0:1]
        peb = jnp.concatenate([peb, peb], axis=1)
        hid = top + pltpu.roll(bot, C - 1, 0) + peb
        out = _dot(_silu(hid).astype(jnp.bfloat16), w2_ref[...])
        return jnp.where(row < n_cmp, out, 0.0)

    kc_ref[0] = phi(xk_ref, wkt_ref, wkb_ref, w2k_ref, pek_ref, w1k_ref).astype(jnp.bfloat16)
    vc = phi(xv_ref, wvt_ref, wvb_ref, w2v_ref, pev_ref, w1v_ref)
    vcT_ref[0] = vc.T.astype(jnp.bfloat16)


def _compress(xk, xv, wkt, wkb, wvt, wvb, w2k, w2v, pek, pev, w1k, w1v, n_cmp):
    B, C, F = xk.shape
    full = lambda a: pl.BlockSpec(a.shape, lambda b: (0,) * a.ndim)
    return pl.pallas_call(
        functools.partial(_compress_kernel, n_cmp),
        grid=(B,),
        in_specs=[pl.BlockSpec((1, C, F), lambda b: (b, 0, 0)),
                  pl.BlockSpec((1, C, F), lambda b: (b, 0, 0)),
                  full(wkt), full(wkb), full(wvt), full(wvb), full(w2k), full(w2v),
                  full(pek), full(pev), full(w1k), full(w1v)],
        out_specs=[pl.BlockSpec((1, C, KV_W), lambda b: (b, 0, 0)),
                   pl.BlockSpec((1, KV_W, C), lambda b: (b, 0, 0))],
        out_shape=[jax.ShapeDtypeStruct((B, C, KV_W), jnp.bfloat16),
                   jax.ShapeDtypeStruct((B, KV_W, C), jnp.bfloat16)],
        compiler_params=pltpu.CompilerParams(
            dimension_semantics=("arbitrary",), vmem_limit_bytes=VMEM_LIMIT),
        name="compress",
    )(xk, xv, wkt, wkb, wvt, wvb, w2k, w2v, pek, pev, w1k, w1v)


def _keep_half(qT, half):
    z = jnp.zeros((HALF, qT.shape[1]), qT.dtype)
    if half == 0:
        return jnp.concatenate([qT[0:HALF], z], axis=0)
    return jnp.concatenate([z, qT[HALF:2 * HALF]], axis=0)


def _rank_rows(vals, n_rows):
    R, T = vals.shape
    j = lax.broadcasted_iota(jnp.int32, (R, T), 0)
    rank = jnp.zeros((R, T), jnp.int32)
    for m in range(n_rows):
        vm = vals[m:m + 1, :]
        beats = (vm > vals) | ((vm == vals) & (j > m))
        rank = rank + beats.astype(jnp.int32)
    return rank


def _alibi_rows(n_rows, T, first, slope, i):
    r = lax.broadcasted_iota(jnp.int32, (n_rows, T), 0)
    trel = lax.broadcasted_iota(jnp.int32, (n_rows, T), 1).astype(jnp.float32)
    base = (i * TILE).astype(jnp.float32)
    out = jnp.where(r == first, -slope * trel, 0.0)
    out = jnp.where(r == first + 1, -slope * base, out)
    out = jnp.where((r == first + 2) | (r == first + 3), slope, out)
    return out


def _key_aug_table(nq, n_sel, blocks_per_tile):
    t = np.zeros((nq, TILE, LANES), np.float32)
    krel = np.arange(TILE)
    for j in range(nq):
        if n_sel:
            blk = j * blocks_per_tile + krel // (TILE // blocks_per_tile)
            t[j, krel, blk] = 1.0
        t[j, :, n_sel] = 1.0
        t[j, :, n_sel + 1] = 1.0
        t[j, :, n_sel + 2] = krel
        t[j, :, n_sel + 3] = TILE * j
    return jnp.asarray(t.reshape(nq // 2, 2 * TILE, LANES), jnp.bfloat16)


ONES_ROWS = 16
ACC_ROWS = HALF + ONES_ROWS


def _win_key_aug_table():
    n_tiles = WINDOW // TILE + 1
    t = np.zeros((n_tiles * TILE, LANES), np.float32)
    for slot in range(n_tiles):
        delta = n_tiles - 1 - slot
        rows = slice(slot * TILE, (slot + 1) * TILE)
        t[rows, 0] = 1.0
        t[rows, 1] = np.arange(TILE)
        t[rows, 2] = TILE * delta
        if delta > 0:
            t[rows, 2 + delta] = 1.0
    return jnp.asarray(t, jnp.bfloat16)


def _value_rows(vT_h):
    return jnp.concatenate([vT_h, jnp.ones((ONES_ROWS, vT_h.shape[1]), vT_h.dtype)], axis=0)


def _attend_heads(score_fn, value_fn, keeps, s_ref, mx_ref, m_ref, acc_ref):
    for h in range(N_HEADS):
        s = score_fn(h)
        if any(k is not None for k in keeps):
            s = jnp.concatenate(
                [s[r * TILE:(r + 1) * TILE] if k is None
                 else jnp.where(k, s[r * TILE:(r + 1) * TILE], NEG_INF)
                 for r, k in enumerate(keeps)], axis=0)
        s_ref[h] = s
        mx_ref[h] = jnp.max(s, axis=0, keepdims=True)
    for h in range(N_HEADS):
        m_prev = m_ref[h]
        m_new = jnp.maximum(m_prev, mx_ref[h])
        alpha = jnp.exp(m_prev - m_new)
        p = jnp.exp(s_ref[h] - m_new).astype(jnp.bfloat16)
        acc_ref[h] = alpha * acc_ref[h] + _dot(value_fn(h), p)
        m_ref[h] = m_new


def _init_state(m_ref, acc_ref):
    m_ref[...] = jnp.full(m_ref.shape, NEG_INF, jnp.float32)
    acc_ref[...] = jnp.zeros(acc_ref.shape, jnp.float32)


def _tile_iotas():
    k = lax.broadcasted_iota(jnp.int32, (TILE, TILE), 0)
    t = lax.broadcasted_iota(jnp.int32, (TILE, TILE), 1)
    return k, t


def _causal_keeps(d):
    k, t = _tile_iotas()
    return (k <= t + TILE * d, k <= t + TILE * (d - 1))


def _finalize_pair(o_ref, acc_ref, p):
    a0 = acc_ref[2 * p]
    a1 = acc_ref[2 * p + 1]
    oT = jnp.concatenate([a0[0:HALF] / a0[HALF:HALF + 1], a1[0:HALF] / a1[HALF:HALF + 1]], axis=0)
    o_ref[0, :, p * LANES:(p + 1) * LANES] = oT.T


def _pair_tables(nq):
    it, jt = [], []
    for i in range(nq):
        for jj in range(i // 2 + 1):
            it.append(i)
            jt.append(jj)
    return jnp.asarray(it, jnp.int32), jnp.asarray(jt, jnp.int32)


def _state_scratch(n_keys):
    return [
        pltpu.VMEM((N_HEADS, 2 * LANES, TILE), jnp.bfloat16),
        pltpu.VMEM((N_HEADS, 1, TILE), jnp.float32),
        pltpu.VMEM((N_HEADS, ACC_ROWS, TILE), jnp.float32),
        pltpu.VMEM((N_HEADS, n_keys, TILE), jnp.float32),
        pltpu.VMEM((N_HEADS, 1, TILE), jnp.float32),
    ]


def _moba_kernel(nq, it_ref, jt_ref, qT_ref, k_ref, vT_ref, kmean_ref, kaug_ref, o_ref,
                 qaug_ref, m_ref, acc_ref, s_ref, mx_ref):
    s = pl.program_id(1)
    i = it_ref[s]
    jj = jt_ref[s]
    n_sel = 16

    @pl.when(jj == 0)
    def _():
        _init_state(m_ref, acc_ref)
        n_iota = lax.broadcasted_iota(jnp.int32, (nq, TILE), 0)
        for p in range(N_PAIRS):
            qT = qT_ref[0, p]
            km = kmean_ref[0, :, p * LANES:(p + 1) * LANES].astype(jnp.bfloat16)
            for hh in range(2):
                h = 2 * p + hh
                qm = _keep_half(qT, hh)
                gate = jnp.where(n_iota < i, _dot(km, qm), NEG_INF)
                rank = _rank_rows(gate, nq)
                sel = ((rank < MOBA_TOPK) & (n_iota < i)) | (n_iota == i)
                selneg = jnp.where(sel, 0.0, NEG_INF)
                if nq < n_sel:
                    selneg = jnp.concatenate(
                        [selneg, jnp.zeros((n_sel - nq, TILE), jnp.float32)], axis=0)
                ali = _alibi_rows(LANES - n_sel, TILE, 0, SLOPES[h], i)
                aug = jnp.concatenate([selneg, ali], axis=0).astype(jnp.bfloat16)
                qaug_ref[h] = jnp.concatenate([qm, aug], axis=0)

    def step(diag):
        kaug = kaug_ref[0]

        def score(h):
            return _dot(jnp.concatenate([k_ref[0, h // 2], kaug], axis=1), qaug_ref[h])

        def value(h):
            return _value_rows(vT_ref[0, h // 2, (h % 2) * HALF:(h % 2 + 1) * HALF])

        keeps = _causal_keeps(i - 2 * jj) if diag else (None, None)
        _attend_heads(score, value, keeps, s_ref, mx_ref, m_ref, acc_ref)
        if diag:
            for p in range(N_PAIRS):
                _finalize_pair(o_ref, acc_ref, p)

    @pl.when(2 * jj + 1 < i)
    def _():
        step(False)

    @pl.when(2 * jj + 1 >= i)
    def _():
        step(True)


def _moba(qT, k, vT, kmean, kaug):
    B, _, _, S = qT.shape
    nq = S // TILE
    it, jt = _pair_tables(nq)
    return pl.pallas_call(
        functools.partial(_moba_kernel, nq),
        grid_spec=pltpu.PrefetchScalarGridSpec(
            num_scalar_prefetch=2,
            grid=(B, int(it.shape[0])),
            in_specs=[
                pl.BlockSpec((1, N_PAIRS, LANES, TILE), lambda b, s, it, jt: (b, 0, 0, it[s])),
                pl.BlockSpec((1, N_PAIRS, 2 * TILE, LANES), lambda b, s, it, jt: (b, 0, jt[s], 0)),
                pl.BlockSpec((1, N_PAIRS, LANES, 2 * TILE), lambda b, s, it, jt: (b, 0, 0, jt[s])),
                pl.BlockSpec((1, nq, MOBA_W), lambda b, s, it, jt: (b, 0, 0)),
                pl.BlockSpec((1, 2 * TILE, LANES), lambda b, s, it, jt: (jt[s], 0, 0)),
            ],
            out_specs=pl.BlockSpec((1, TILE, MOBA_W), lambda b, s, it, jt: (b, it[s], 0)),
            scratch_shapes=_state_scratch(2 * TILE),
        ),
        out_shape=jax.ShapeDtypeStruct((B, S, MOBA_W), jnp.float32),
        compiler_params=pltpu.CompilerParams(
            dimension_semantics=("arbitrary", "arbitrary"), vmem_limit_bytes=VMEM_LIMIT),
        name="moba",
    )(it, jt, qT, k, vT, kmean, kaug)


def _cmp_kernel(n_cmp, n_slc, qT_ref, kc_ref, vcT_ref, ovT_ref, o_ref, sel_ref):
    i = pl.program_id(1)
    C = kc_ref.shape[1]
    kc = kc_ref[0]
    vcT = vcT_ref[0]
    c_iota = lax.broadcasted_iota(jnp.int32, (C, TILE), 0)
    t_abs = i * TILE + lax.broadcasted_iota(jnp.int32, (C, TILE), 1)
    cmask = (c_iota * CMP_STRIDE + (CMP_LEN - 1) <= t_abs) & (c_iota < n_cmp)

    psum = [jnp.zeros((C, TILE), jnp.float32) for _ in range(NSA_GROUPS)]
    for p in range(N_PAIRS):
        qT = qT_ref[0, p]
        halves = []
        for g in range(NSA_GROUPS):
            qm = _keep_half(qT, g)
            sT = jnp.where(cmask, _dot(kc, qm), NEG_INF)
            m = jnp.max(sT, axis=0, keepdims=True)
            e = jnp.where(cmask, jnp.exp(sT - m), 0.0)
            pn = e / jnp.maximum(jnp.sum(e, axis=0, keepdims=True), 1e-30)
            psum[g] = psum[g] + pn
            halves.append(_dot(vcT[g * HALF:(g + 1) * HALF], pn.astype(jnp.bfloat16)))
        o_ref[0, :, p * LANES:(p + 1) * LANES] = jnp.concatenate(halves, axis=0).T

    R = sel_ref.shape[2]
    j_iota = lax.broadcasted_iota(jnp.int32, (R, TILE), 0)
    t_q = i * TILE + lax.broadcasted_iota(jnp.int32, (R, TILE), 1)
    own = lax.shift_right_logical(t_q, int(np.log2(SLC_BLOCK)))
    forced = (j_iota == 0) | (j_iota == own) | (j_iota == own - 1)
    causal = j_iota <= own
    ovT = ovT_ref[...]
    for g in range(NSA_GROUPS):
        hi = psum[g].astype(jnp.bfloat16)
        lo = (psum[g] - hi.astype(jnp.float32)).astype(jnp.bfloat16)
        imp = (_dot(ovT, hi) + _dot(ovT, lo))[0:R]
        vals = jnp.where(forced, FORCED_SCORE, jnp.where(causal, imp, NEG_INF))
        rank = _rank_rows(vals, n_slc)
        sel = (rank < min(SLC_TOPN, n_slc)) & causal
        sel_ref[0, g] = jnp.where(sel, 0.0, NEG_INF).astype(jnp.bfloat16)


def _cmp(qbT, kc, vcT, ovT, n_cmp, n_slc):
    B, _, _, S = qbT.shape
    C = kc.shape[1]
    nq = S // TILE
    R = 64
    return pl.pallas_call(
        functools.partial(_cmp_kernel, n_cmp, n_slc),
        grid=(B, nq),
        in_specs=[
            pl.BlockSpec((1, N_PAIRS, LANES, TILE), lambda b, i: (b, 0, 0, i)),
            pl.BlockSpec((1, C, KV_W), lambda b, i: (b, 0, 0)),
            pl.BlockSpec((1, KV_W, C), lambda b, i: (b, 0, 0)),
            pl.BlockSpec(ovT.shape, lambda b, i: (0, 0)),
        ],
        out_specs=[pl.BlockSpec((1, TILE, NSA_W), lambda b, i: (b, i, 0)),
                   pl.BlockSpec((1, NSA_GROUPS, R, TILE), lambda b, i: (b, 0, 0, i))],
        out_shape=[jax.ShapeDtypeStruct((B, S, NSA_W), jnp.float32),
                   jax.ShapeDtypeStruct((B, NSA_GROUPS, R, S), jnp.bfloat16)],
        compiler_params=pltpu.CompilerParams(
            dimension_semantics=("arbitrary", "arbitrary"), vmem_limit_bytes=VMEM_LIMIT),
        name="cmp",
    )(qbT, kc, vcT, ovT)


def _slc_kernel(it_ref, jt_ref, qT_ref, k_ref, vT_ref, sel_ref, kaug_ref, o_ref,
                qaug_ref, m_ref, acc_ref, s_ref, mx_ref):
    s = pl.program_id(1)
    i = it_ref[s]
    jj = jt_ref[s]
    n_sel = sel_ref.shape[2]

    @pl.when(jj == 0)
    def _():
        _init_state(m_ref, acc_ref)
        for p in range(N_PAIRS):
            qT = qT_ref[0, p]
            for g in range(NSA_GROUPS):
                h = 2 * p + g
                qm = _keep_half(qT, g)
                ali = _alibi_rows(LANES - n_sel, TILE, 0, SLOPES[g * NSA_HPG + p], i)
                qaug_ref[h] = jnp.concatenate(
                    [qm, sel_ref[0, g], ali.astype(jnp.bfloat16)], axis=0)

    def step(diag):
        kfull = jnp.concatenate([k_ref[0], kaug_ref[0]], axis=1)

        def score(h):
            return _dot(kfull, qaug_ref[h])

        def value(h):
            return _value_rows(vT_ref[0, (h % 2) * HALF:(h % 2 + 1) * HALF])

        keeps = _causal_keeps(i - 2 * jj) if diag else (None, None)
        _attend_heads(score, value, keeps, s_ref, mx_ref, m_ref, acc_ref)
        if diag:
            for p in range(N_PAIRS):
                _finalize_pair(o_ref, acc_ref, p)

    @pl.when(2 * jj + 1 < i)
    def _():
        step(False)

    @pl.when(2 * jj + 1 >= i)
    def _():
        step(True)


def _slc(qbT, ksl, vslT, selT, kaug):
    B, _, _, S = qbT.shape
    nq = S // TILE
    R = selT.shape[2]
    it, jt = _pair_tables(nq)
    return pl.pallas_call(
        _slc_kernel,
        grid_spec=pltpu.PrefetchScalarGridSpec(
            num_scalar_prefetch=2,
            grid=(B, int(it.shape[0])),
            in_specs=[
                pl.BlockSpec((1, N_PAIRS, LANES, TILE), lambda b, s, it, jt: (b, 0, 0, it[s])),
                pl.BlockSpec((1, 2 * TILE, KV_W), lambda b, s, it, jt: (b, jt[s], 0)),
                pl.BlockSpec((1, KV_W, 2 * TILE), lambda b, s, it, jt: (b, 0, jt[s])),
                pl.BlockSpec((1, NSA_GROUPS, R, TILE), lambda b, s, it, jt: (b, 0, 0, it[s])),
                pl.BlockSpec((1, 2 * TILE, LANES), lambda b, s, it, jt: (jt[s], 0, 0)),
            ],
            out_specs=pl.BlockSpec((1, TILE, NSA_W), lambda b, s, it, jt: (b, it[s], 0)),
            scratch_shapes=_state_scratch(2 * TILE),
        ),
        out_shape=jax.ShapeDtypeStruct((B, S, NSA_W), jnp.float32),
        compiler_params=pltpu.CompilerParams(
            dimension_semantics=("arbitrary", "arbitrary"), vmem_limit_bytes=VMEM_LIMIT),
        name="slc",
    )(it, jt, qbT, ksl, vslT, selT, kaug)


def _win_kernel(qT_ref, k2_ref, k1_ref, k0_ref, v2_ref, v1_ref, v0_ref, kaug_ref, o_ref,
                qaug_ref, m_ref, acc_ref, s_ref, mx_ref):
    i = pl.program_id(1)
    _init_state(m_ref, acc_ref)
    r = lax.broadcasted_iota(jnp.int32, (LANES, TILE), 0)
    trel = lax.broadcasted_iota(jnp.int32, (LANES, TILE), 1).astype(jnp.float32)
    absent2 = jnp.where(i >= 2, 0.0, NEG_INF)
    absent1 = jnp.where(i >= 1, 0.0, NEG_INF)
    for p in range(N_PAIRS):
        qT = qT_ref[0, p]
        for g in range(NSA_GROUPS):
            h = 2 * p + g
            slope = SLOPES[g * NSA_HPG + p]
            aug = jnp.where(r == 0, -slope * trel, 0.0)
            aug = jnp.where(r == 1, slope, aug)
            aug = jnp.where(r == 2, -slope, aug)
            aug = jnp.where(r == 3, absent1, aug)
            aug = jnp.where(r == 4, absent2, aug)
            qaug_ref[h] = jnp.concatenate([_keep_half(qT, g), aug.astype(jnp.bfloat16)], axis=0)

    kfull = jnp.concatenate(
        [jnp.concatenate([k2_ref[0], k1_ref[0], k0_ref[0]], axis=0), kaug_ref[...]], axis=1)
    vT = jnp.concatenate([v2_ref[0], v1_ref[0], v0_ref[0]], axis=1)
    k, t = _tile_iotas()

    def score(h):
        return _dot(kfull, qaug_ref[h])

    def value(h):
        return _value_rows(vT[(h % 2) * HALF:(h % 2 + 1) * HALF])

    _attend_heads(score, value, (k > t, None, k <= t), s_ref, mx_ref, m_ref, acc_ref)
    for p in range(N_PAIRS):
        _finalize_pair(o_ref, acc_ref, p)


def _win(qbT, kwi, vwiT, kaug):
    B, _, _, S = qbT.shape
    nq = S // TILE
    kspec = lambda d: pl.BlockSpec((1, TILE, KV_W), lambda b, i: (b, jnp.maximum(i - d, 0), 0))
    vspec = lambda d: pl.BlockSpec((1, KV_W, TILE), lambda b, i: (b, 0, jnp.maximum(i - d, 0)))
    return pl.pallas_call(
        _win_kernel,
        grid=(B, nq),
        in_specs=[
            pl.BlockSpec((1, N_PAIRS, LANES, TILE), lambda b, i: (b, 0, 0, i)),
            kspec(2), kspec(1), kspec(0), vspec(2), vspec(1), vspec(0),
            pl.BlockSpec(kaug.shape, lambda b, i: (0, 0)),
        ],
        out_specs=pl.BlockSpec((1, TILE, NSA_W), lambda b, i: (b, i, 0)),
        out_shape=jax.ShapeDtypeStruct((B, S, NSA_W), jnp.float32),
        scratch_shapes=_state_scratch(kaug.shape[0]),
        compiler_params=pltpu.CompilerParams(
            dimension_semantics=("arbitrary", "arbitrary"), vmem_limit_bytes=VMEM_LIMIT),
        name="win",
    )(qbT, kwi, kwi, kwi, vwiT, vwiT, vwiT, kaug)


def _out_kernel(x_ref, oa_ref, sza_ref, oc_ref, os_ref, ow_ref, gexp_ref, szb_ref,
                w_ref, g_ref, y_ref):
    mixa = oa_ref[0] * sza_ref[0]
    ge = gexp_ref[0]
    ob = (ge[:, 0:NSA_W] * oc_ref[0] + ge[:, NSA_W:2 * NSA_W] * os_ref[0]
          + ge[:, 2 * NSA_W:3 * NSA_W] * ow_ref[0])
    mix = jnp.concatenate([mixa, ob * szb_ref[0]], axis=1).astype(jnp.bfloat16)
    y = _dot(mix, w_ref[...])
    r = lax.rsqrt(jnp.mean(y * y, axis=-1, keepdims=True) + RMS_EPS)
    y_ref[0] = x_ref[0] + y * r * g_ref[...]


def _out(x, oa, sza, oc, os_, ow, gexp, szb, w, g):
    B, S, _ = x.shape
    rows = lambda wd: pl.BlockSpec((1, TILE, wd), lambda b, i: (b, i, 0))
    return pl.pallas_call(
        _out_kernel,
        grid=(B, S // TILE),
        in_specs=[rows(D_MODEL), rows(MOBA_W), rows(MOBA_W), rows(NSA_W), rows(NSA_W),
                  rows(NSA_W), rows(3 * NSA_W), rows(NSA_W),
                  pl.BlockSpec(w.shape, lambda b, i: (0, 0)),
                  pl.BlockSpec((1, D_MODEL), lambda b, i: (0, 0))],
        out_specs=rows(D_MODEL),
        out_shape=jax.ShapeDtypeStruct((B, S, D_MODEL), jnp.float32),
        compiler_params=pltpu.CompilerParams(
            dimension_semantics=("arbitrary", "arbitrary"), vmem_limit_bytes=VMEM_LIMIT),
        name="out",
    )(x, oa, sza, oc, os_, ow, gexp, szb, w, g)


def _compress_weights(w1, w2):
    half = (CMP_LEN // 2) * HEAD_DIM
    w1r = w1.reshape(2, CMP_LEN // 2, HEAD_DIM, CMP_HIDDEN)
    z = jnp.zeros_like(w1r)
    g0 = jnp.concatenate([w1r, z], axis=-1)
    g1 = jnp.concatenate([z, w1r], axis=-1)
    both = jnp.stack([g0, g1], axis=2)
    both = both.reshape(2, half * NSA_GROUPS, NSA_GROUPS * CMP_HIDDEN).astype(jnp.bfloat16)
    zz = jnp.zeros_like(w2)
    w2bd = jnp.concatenate([jnp.concatenate([w2, zz], axis=1),
                            jnp.concatenate([zz, w2], axis=1)], axis=0).astype(jnp.bfloat16)
    return both[0], both[1], w2bd


def _overlap_T(n_cmp_pad, n_slc, rows):
    c = np.arange(n_cmp_pad)[None, :] * CMP_STRIDE
    j = np.arange(rows)[:, None] * SLC_BLOCK
    ov = (c < j + SLC_BLOCK) & (c + CMP_LEN > j) & (np.arange(rows)[:, None] < n_slc)
    return jnp.asarray(ov.astype(np.float32), jnp.bfloat16)


def _layer(x, pre_g, post_g, w_in, pos_k, pos_v, w_k1, w_k2, w_v1, w_v2, w_out):
    B, S, _ = x.shape
    nq = S // TILE
    n_cmp = (S - CMP_LEN) // CMP_STRIDE + 1
    n_slc = S // SLC_BLOCK
    C = S // CMP_STRIDE
    nat_cols, tr_cols, out_rows = _column_maps()

    wn = jnp.take(w_in, nat_cols, axis=1).astype(jnp.bfloat16)
    wt = jnp.take(w_in, tr_cols, axis=1).T.astype(jnp.bfloat16)
    (ka, kmean, kcm, vcm, ksl, kwi, sza, gexp, szb,
     qaT, vaT, qbT, vslT, vwiT) = _proj(x, pre_g.reshape(1, D_MODEL), wn, wt)

    wkt, wkb, w2k = _compress_weights(w_k1, w_k2)
    wvt, wvb, w2v = _compress_weights(w_v1, w_v2)
    chunk = CMP_STRIDE * KV_W
    kc, vcT = _compress(
        kcm.reshape(B, C, chunk), vcm.reshape(B, C, chunk), wkt, wkb, wvt, wvb, w2k, w2v,
        pos_k.reshape(1, CMP_LEN * HEAD_DIM).astype(jnp.bfloat16),
        pos_v.reshape(1, CMP_LEN * HEAD_DIM).astype(jnp.bfloat16),
        w_k1.astype(jnp.bfloat16), w_v1.astype(jnp.bfloat16), n_cmp)

    oc, selT = _cmp(qbT, kc, vcT, _overlap_T(C, n_slc, LANES), n_cmp, n_slc)
    oa = _moba(qaT, ka, vaT, kmean.reshape(B, nq, MOBA_W), _key_aug_table(nq, 16, 1))
    osl = _slc(qbT, ksl, vslT, selT, _key_aug_table(nq, selT.shape[2], TILE // SLC_BLOCK))
    ow = _win(qbT, kwi, vwiT, _win_key_aug_table())

    w_o = jnp.take(w_out, out_rows, axis=0).astype(jnp.bfloat16)
    return _out(x, oa, sza, oc, osl, ow, gexp, szb, w_o, post_g.reshape(1, D_MODEL))


def kernel(x, pre_norm_g, post_norm_g, w_in, cmp_pos_k, cmp_pos_v,
           w_cmp_k1, w_cmp_k2, w_cmp_v1, w_cmp_v2, w_out):
    for l in range(pre_norm_g.shape[0]):
        x = _layer(x, pre_norm_g[l], post_norm_g[l], w_in[l], cmp_pos_k[l], cmp_pos_v[l],
                   w_cmp_k1[l], w_cmp_k2[l], w_cmp_v1[l], w_cmp_v2[l], w_out[l])
    return x
```

```python
import functools

import numpy as np
import jax
import jax.numpy as jnp
from jax import lax
from jax.experimental import pallas as pl
from jax.experimental.pallas import tpu as pltpu

D_MODEL = 1024
HEAD_DIM = 64
N_HEADS = 8
N_PAIRS = N_HEADS // 2
NSA_GROUPS = 2
NSA_HPG = N_HEADS // NSA_GROUPS
MOBA_BLOCK = 256
MOBA_TOPK = 3
CMP_LEN = 32
CMP_STRIDE = 16
CMP_HIDDEN = 128
SLC_BLOCK = 64
SLC_TOPN = 16
WINDOW = 512
RMS_EPS = 1e-6
NEG_INF = -1e30
FORCED_SCORE = 1e9
SCALE = HEAD_DIM ** -0.5

TILE = 256
LANES = 128
HALF = HEAD_DIM
SUBLANES = 8
MOBA_W = N_HEADS * HEAD_DIM
NSA_W = N_HEADS * HEAD_DIM
KV_W = NSA_GROUPS * HEAD_DIM
N_GATES = 3 * N_HEADS
GATE_ROWS = 32
SLOPES = tuple(2.0 ** (-(i + 1)) for i in range(N_HEADS))
BRANCH_CMP, BRANCH_SLC, BRANCH_WIN = 0, 1, 2

VMEM_LIMIT = 48 * 1024 * 1024

_OFF = dict(qa=0, ka=512, va=1024, za=1536, qb=2048, kcm=2560, vcm=2688,
            ksl=2816, vsl=2944, kwi=3072, vwi=3200, gate=3328, zb=3352)
_NAT = dict(ka=(0, 512), kcm=(512, 640), vcm=(640, 768), ksl=(768, 896), kwi=(896, 1024))
N_NAT = 1024
_TR = dict(qa=(0, 512), va=(512, 1024), qb=(1024, 1536), vsl=(1536, 1664),
           vwi=(1664, 1792), za=(1792, 2304), zb=(2304, 2816), gate=(2816, 2848))
N_TR = 2848


def _pair_perm():
    idx = np.zeros(NSA_W, np.int32)
    for p in range(N_PAIRS):
        for g in range(NSA_GROUPS):
            for d in range(HEAD_DIM):
                idx[p * LANES + g * HALF + d] = (g * NSA_HPG + p) * HEAD_DIM + d
    return idx


def _column_maps():
    pp = _pair_perm()
    nat = np.zeros(N_NAT, np.int32)
    nat[0:512] = _OFF["ka"] + np.arange(512)
    nat[512:640] = _OFF["kcm"] + np.arange(128)
    nat[640:768] = _OFF["vcm"] + np.arange(128)
    nat[768:896] = _OFF["ksl"] + np.arange(128)
    nat[896:1024] = _OFF["kwi"] + np.arange(128)
    tr = np.zeros(N_TR - (GATE_ROWS - N_GATES), np.int32)
    tr[0:512] = _OFF["qa"] + np.arange(512)
    tr[512:1024] = _OFF["va"] + np.arange(512)
    tr[1024:1536] = _OFF["qb"] + pp
    tr[1536:1664] = _OFF["vsl"] + np.arange(128)
    tr[1664:1792] = _OFF["vwi"] + np.arange(128)
    tr[1792:2304] = _OFF["za"] + np.arange(512)
    tr[2304:2816] = _OFF["zb"] + pp
    tr[2816:2816 + N_GATES] = _OFF["gate"] + np.arange(N_GATES)
    out_rows = np.concatenate([np.arange(512), 512 + pp]).astype(np.int32)
    return nat, tr, out_rows


def _gate_row(g, p, branch):
    return (g * NSA_HPG + p) * 3 + branch


def _sigmoid(x):
    return 1.0 / (1.0 + jnp.exp(-x))


def _silu(x):
    return x * _sigmoid(x)


def _dot(a, b):
    return jnp.dot(a, b, preferred_element_type=jnp.float32)


def _proj_kernel(x_ref, g_ref, wn_ref, wt_ref,
                 ka_ref, kmean_ref, kcm_ref, vcm_ref, ksl_ref, kwi_ref,
                 qaT_ref, vaT_ref, qbT_ref, vslT_ref, vwiT_ref,
                 szaT_ref, szbT_ref, gateT_ref):
    x = x_ref[0]
    r = lax.rsqrt(jnp.mean(x * x, axis=-1, keepdims=True) + RMS_EPS)
    h = (x * r * g_ref[...]).astype(jnp.bfloat16)

    def nat(name):
        a, b = _NAT[name]
        return _dot(h, wn_ref[:, a:b])

    ka = nat("ka")
    kmean_ref[0, 0] = jnp.mean(ka, axis=0, keepdims=True)
    for p in range(N_PAIRS):
        ka_ref[0, p] = ka[:, p * LANES:(p + 1) * LANES].astype(jnp.bfloat16)
    kcm_ref[0] = nat("kcm").astype(jnp.bfloat16)
    vcm_ref[0] = nat("vcm").astype(jnp.bfloat16)
    ksl_ref[0] = nat("ksl").astype(jnp.bfloat16)
    kwi_ref[0] = nat("kwi").astype(jnp.bfloat16)

    def tr(name):
        a, b = _TR[name]
        return lax.dot_general(wt_ref[a:b, :], h, (((1,), (1,)), ((), ())),
                               preferred_element_type=jnp.float32)

    qa = tr("qa") * SCALE
    va = tr("va")
    qb = tr("qb") * SCALE
    sza = _silu(tr("za"))
    szb = _silu(tr("zb"))
    for p in range(N_PAIRS):
        sl = slice(p * LANES, (p + 1) * LANES)
        qaT_ref[0, p] = qa[sl].astype(jnp.bfloat16)
        vaT_ref[0, p] = va[sl].astype(jnp.bfloat16)
        qbT_ref[0, p] = qb[sl].astype(jnp.bfloat16)
        szaT_ref[0, p] = sza[sl]
        szbT_ref[0, p] = szb[sl]
    vslT_ref[0] = tr("vsl").astype(jnp.bfloat16)
    vwiT_ref[0] = tr("vwi").astype(jnp.bfloat16)
    gateT_ref[0] = _sigmoid(tr("gate"))


def _proj(x, g, wn, wt):
    B, S, _ = x.shape
    nq = S // TILE
    bf, f32 = jnp.bfloat16, jnp.float32
    slab_nat = lambda: pl.BlockSpec((1, N_PAIRS, TILE, LANES), lambda b, i: (b, 0, i, 0))
    slab_tr = lambda: pl.BlockSpec((1, N_PAIRS, LANES, TILE), lambda b, i: (b, 0, 0, i))
    rows = lambda w: pl.BlockSpec((1, TILE, w), lambda b, i: (b, i, 0))
    chans = lambda c: pl.BlockSpec((1, c, TILE), lambda b, i: (b, 0, i))
    out_shape = [
        jax.ShapeDtypeStruct((B, N_PAIRS, S, LANES), bf),
        jax.ShapeDtypeStruct((B, nq, 1, MOBA_W), f32),
        jax.ShapeDtypeStruct((B, S, KV_W), bf),
        jax.ShapeDtypeStruct((B, S, KV_W), bf),
        jax.ShapeDtypeStruct((B, S, KV_W), bf),
        jax.ShapeDtypeStruct((B, S, KV_W), bf),
        jax.ShapeDtypeStruct((B, N_PAIRS, LANES, S), bf),
        jax.ShapeDtypeStruct((B, N_PAIRS, LANES, S), bf),
        jax.ShapeDtypeStruct((B, N_PAIRS, LANES, S), bf),
        jax.ShapeDtypeStruct((B, KV_W, S), bf),
        jax.ShapeDtypeStruct((B, KV_W, S), bf),
        jax.ShapeDtypeStruct((B, N_PAIRS, LANES, S), f32),
        jax.ShapeDtypeStruct((B, N_PAIRS, LANES, S), f32),
        jax.ShapeDtypeStruct((B, GATE_ROWS, S), f32),
    ]
    out_specs = [
        slab_nat(),
        pl.BlockSpec((1, 1, 1, MOBA_W), lambda b, i: (b, i, 0, 0)),
        rows(KV_W), rows(KV_W), rows(KV_W), rows(KV_W),
        slab_tr(), slab_tr(), slab_tr(),
        chans(KV_W), chans(KV_W),
        slab_tr(), slab_tr(),
        chans(GATE_ROWS),
    ]
    return pl.pallas_call(
        _proj_kernel,
        grid=(B, nq),
        in_specs=[
            pl.BlockSpec((1, TILE, D_MODEL), lambda b, i: (b, i, 0)),
            pl.BlockSpec((1, D_MODEL), lambda b, i: (0, 0)),
            pl.BlockSpec((D_MODEL, N_NAT), lambda b, i: (0, 0)),
            pl.BlockSpec((N_TR, D_MODEL), lambda b, i: (0, 0)),
        ],
        out_specs=out_specs,
        out_shape=out_shape,
        compiler_params=pltpu.CompilerParams(
            dimension_semantics=("arbitrary", "arbitrary"),
            vmem_limit_bytes=VMEM_LIMIT),
        name="proj",
    )(x, g, wn, wt)


def _compress_kernel(n_cmp, xk_ref, xv_ref, wkt_ref, wkb_ref, wvt_ref, wvb_ref,
                     w2k_ref, w2v_ref, pek_ref, pev_ref, w1k_ref, w1v_ref,
                     kc_ref, vcT_ref):
    C = xk_ref.shape[1]
    row = lax.broadcasted_iota(jnp.int32, (C, KV_W), 0)

    def phi(x_ref, wt_ref, wb_ref, w2_ref, pe_ref, w1_ref):
        x = x_ref[0]
        top = _dot(x, wt_ref[...])
        bot = _dot(x, wb_ref[...])
        peb = _dot(jnp.broadcast_to(pe_ref[...], (8, pe_ref.shape[1])), w1_ref[...])[0:1]
        peb = jnp.concatenate([peb, peb], axis=1)
        hid = top + pltpu.roll(bot, C - 1, 0) + peb
        out = _dot(_silu(hid).astype(jnp.bfloat16), w2_ref[...])
        return jnp.where(row < n_cmp, out, 0.0)

    kc_ref[0] = phi(xk_ref, wkt_ref, wkb_ref, w2k_ref, pek_ref, w1k_ref).astype(jnp.bfloat16)
    vc = phi(xv_ref, wvt_ref, wvb_ref, w2v_ref, pev_ref, w1v_ref)
    vcT_ref[0] = vc.T.astype(jnp.bfloat16)


def _compress(xk, xv, wkt, wkb, wvt, wvb, w2k, w2v, pek, pev, w1k, w1v, n_cmp):
    B, C, F = xk.shape
    full = lambda a: pl.BlockSpec(a.shape, lambda b: (0,) * a.ndim)
    return pl.pallas_call(
        functools.partial(_compress_kernel, n_cmp),
        grid=(B,),
        in_specs=[pl.BlockSpec((1, C, F), lambda b: (b, 0, 0)),
                  pl.BlockSpec((1, C, F), lambda b: (b, 0, 0)),
                  full(wkt), full(wkb), full(wvt), full(wvb), full(w2k), full(w2v),
                  full(pek), full(pev), full(w1k), full(w1v)],
        out_specs=[pl.BlockSpec((1, C, KV_W), lambda b: (b, 0, 0)),
                   pl.BlockSpec((1, KV_W, C), lambda b: (b, 0, 0))],
        out_shape=[jax.ShapeDtypeStruct((B, C, KV_W), jnp.bfloat16),
                   jax.ShapeDtypeStruct((B, KV_W, C), jnp.bfloat16)],
        compiler_params=pltpu.CompilerParams(
            dimension_semantics=("arbitrary",), vmem_limit_bytes=VMEM_LIMIT),
        name="compress",
    )(xk, xv, wkt, wkb, wvt, wvb, w2k, w2v, pek, pev, w1k, w1v)


def _keep_half(qT, half):
    z = jnp.zeros((HALF, qT.shape[1]), qT.dtype)
    if half == 0:
        return jnp.concatenate([qT[0:HALF], z], axis=0)
    return jnp.concatenate([z, qT[HALF:2 * HALF]], axis=0)


def _rank_rows(vals, n_rows):
    R, T = vals.shape
    tiles = [vals[a:a + SUBLANES] for a in range(0, R, SUBLANES)]
    ranks = [jnp.zeros((SUBLANES, T), jnp.int32) for _ in tiles]
    j_in = lax.broadcasted_iota(jnp.int32, (SUBLANES, T), 0)
    for m in range(n_rows):
        vm = vals[m:m + 1, :]
        for a, tile in enumerate(tiles):
            lo = a * SUBLANES
            if lo > m:
                beats = vm >= tile
            elif lo + SUBLANES - 1 <= m:
                beats = vm > tile
            else:
                beats = (vm > tile) | ((vm == tile) & (j_in > m - lo))
            ranks[a] = ranks[a] + beats.astype(jnp.int32)
    return jnp.concatenate(ranks, axis=0)


def _alibi_rows(n_rows, T, first, slope, i):
    r = lax.broadcasted_iota(jnp.int32, (n_rows, T), 0)
    trel = lax.broadcasted_iota(jnp.int32, (n_rows, T), 1).astype(jnp.float32)
    base = (i * TILE).astype(jnp.float32)
    out = jnp.where(r == first, -slope * trel, 0.0)
    out = jnp.where(r == first + 1, -slope * base, out)
    out = jnp.where((r == first + 2) | (r == first + 3), slope, out)
    return out


def _key_aug_table(nq, n_sel, blocks_per_tile):
    t = np.zeros((nq, TILE, LANES), np.float32)
    krel = np.arange(TILE)
    for j in range(nq):
        if n_sel:
            blk = j * blocks_per_tile + krel // (TILE // blocks_per_tile)
            t[j, krel, blk] = 1.0
        t[j, :, n_sel] = 1.0
        t[j, :, n_sel + 1] = 1.0
        t[j, :, n_sel + 2] = krel
        t[j, :, n_sel + 3] = TILE * j
    return jnp.asarray(t.reshape(nq // 2, 2 * TILE, LANES), jnp.bfloat16)


ONES_ROWS = 16
ACC_ROWS = HALF + ONES_ROWS


def _win_key_aug_table():
    n_tiles = WINDOW // TILE + 1
    t = np.zeros((n_tiles * TILE, LANES), np.float32)
    for slot in range(n_tiles):
        delta = n_tiles - 1 - slot
        rows = slice(slot * TILE, (slot + 1) * TILE)
        t[rows, 0] = 1.0
        t[rows, 1] = np.arange(TILE)
        t[rows, 2] = TILE * delta
        if delta > 0:
            t[rows, 2 + delta] = 1.0
    return jnp.asarray(t, jnp.bfloat16)


def _value_rows(vT_h):
    return jnp.concatenate([vT_h, jnp.ones((ONES_ROWS, vT_h.shape[1]), vT_h.dtype)], axis=0)


def _attend_heads(score_fn, value_fn, keeps, s_ref, mx_ref, m_ref, acc_ref):
    for h in range(N_HEADS):
        s = score_fn(h)
        if any(k is not None for k in keeps):
            s = jnp.concatenate(
                [s[r * TILE:(r + 1) * TILE] if k is None
                 else jnp.where(k, s[r * TILE:(r + 1) * TILE], NEG_INF)
                 for r, k in enumerate(keeps)], axis=0)
        s_ref[h] = s
        mx_ref[h] = jnp.max(s, axis=0, keepdims=True)
    for h in range(N_HEADS):
        m_prev = m_ref[h]
        m_new = jnp.maximum(m_prev, mx_ref[h])
        alpha = jnp.exp(m_prev - m_new)
        p = jnp.exp(s_ref[h] - m_new).astype(jnp.bfloat16)
        acc_ref[h] = alpha * acc_ref[h] + _dot(value_fn(h), p)
        m_ref[h] = m_new


def _init_state(m_ref, acc_ref):
    m_ref[...] = jnp.full(m_ref.shape, NEG_INF, jnp.float32)
    acc_ref[...] = jnp.zeros(acc_ref.shape, jnp.float32)


def _tile_iotas():
    k = lax.broadcasted_iota(jnp.int32, (TILE, TILE), 0)
    t = lax.broadcasted_iota(jnp.int32, (TILE, TILE), 1)
    return k, t


def _causal_keeps(d):
    k, t = _tile_iotas()
    return (k <= t + TILE * d, k <= t + TILE * (d - 1))


def _store_pair(o_ref, p, o0T, o1T, szT, gates):
    if gates is not None:
        o0T = o0T * gates[0]
        o1T = o1T * gates[1]
    oT = jnp.concatenate([o0T, o1T], axis=0) * szT
    o_ref[0, :, p * LANES:(p + 1) * LANES] = oT.T


def _finalize_pair(o_ref, acc_ref, p, szT, gates=None):
    a0 = acc_ref[2 * p]
    a1 = acc_ref[2 * p + 1]
    _store_pair(o_ref, p, a0[0:HALF] / a0[HALF:HALF + 1], a1[0:HALF] / a1[HALF:HALF + 1],
                szT, gates)


def _nsa_gates(gate_ref, p, branch):
    return [gate_ref[0, _gate_row(g, p, branch):_gate_row(g, p, branch) + 1, :]
            for g in range(NSA_GROUPS)]


def _pair_tables(nq):
    it, jt = [], []
    for i in range(nq):
        for jj in range(i // 2 + 1):
            it.append(i)
            jt.append(jj)
    return jnp.asarray(it, jnp.int32), jnp.asarray(jt, jnp.int32)


def _state_scratch(n_keys):
    return [
        pltpu.VMEM((N_HEADS, 2 * LANES, TILE), jnp.bfloat16),
        pltpu.VMEM((N_HEADS, 1, TILE), jnp.float32),
        pltpu.VMEM((N_HEADS, ACC_ROWS, TILE), jnp.float32),
        pltpu.VMEM((N_HEADS, n_keys, TILE), jnp.float32),
        pltpu.VMEM((N_HEADS, 1, TILE), jnp.float32),
    ]


def _moba_kernel(nq, it_ref, jt_ref, qT_ref, k_ref, vT_ref, kmean_ref, kaug_ref, sz_ref, o_ref,
                 qaug_ref, m_ref, acc_ref, s_ref, mx_ref):
    s = pl.program_id(1)
    i = it_ref[s]
    jj = jt_ref[s]
    n_sel = 16

    @pl.when(jj == 0)
    def _():
        _init_state(m_ref, acc_ref)
        n_iota = lax.broadcasted_iota(jnp.int32, (nq, TILE), 0)
        for p in range(N_PAIRS):
            qT = qT_ref[0, p]
            km = kmean_ref[0, :, p * LANES:(p + 1) * LANES].astype(jnp.bfloat16)
            for hh in range(2):
                h = 2 * p + hh
                qm = _keep_half(qT, hh)
                gate = jnp.where(n_iota < i, _dot(km, qm), NEG_INF)
                rank = _rank_rows(gate, nq)
                sel = ((rank < MOBA_TOPK) & (n_iota < i)) | (n_iota == i)
                selneg = jnp.where(sel, 0.0, NEG_INF)
                if nq < n_sel:
                    selneg = jnp.concatenate(
                        [selneg, jnp.zeros((n_sel - nq, TILE), jnp.float32)], axis=0)
                ali = _alibi_rows(LANES - n_sel, TILE, 0, SLOPES[h], i)
                aug = jnp.concatenate([selneg, ali], axis=0).astype(jnp.bfloat16)
                qaug_ref[h] = jnp.concatenate([qm, aug], axis=0)

    def step(diag):
        kaug = kaug_ref[0]

        def score(h):
            return _dot(jnp.concatenate([k_ref[0, h // 2], kaug], axis=1), qaug_ref[h])

        def value(h):
            return _value_rows(vT_ref[0, h // 2, (h % 2) * HALF:(h % 2 + 1) * HALF])

        keeps = _causal_keeps(i - 2 * jj) if diag else (None, None)
        _attend_heads(score, value, keeps, s_ref, mx_ref, m_ref, acc_ref)
        if diag:
            for p in range(N_PAIRS):
                _finalize_pair(o_ref, acc_ref, p, sz_ref[0, p])

    @pl.when(2 * jj + 1 < i)
    def _():
        step(False)

    @pl.when(2 * jj + 1 >= i)
    def _():
        step(True)


def _moba(qT, k, vT, kmean, kaug, szT):
    B, _, _, S = qT.shape
    nq = S // TILE
    it, jt = _pair_tables(nq)
    return pl.pallas_call(
        functools.partial(_moba_kernel, nq),
        grid_spec=pltpu.PrefetchScalarGridSpec(
            num_scalar_prefetch=2,
            grid=(B, int(it.shape[0])),
            in_specs=[
                pl.BlockSpec((1, N_PAIRS, LANES, TILE), lambda b, s, it, jt: (b, 0, 0, it[s])),
                pl.BlockSpec((1, N_PAIRS, 2 * TILE, LANES), lambda b, s, it, jt: (b, 0, jt[s], 0)),
                pl.BlockSpec((1, N_PAIRS, LANES, 2 * TILE), lambda b, s, it, jt: (b, 0, 0, jt[s])),
                pl.BlockSpec((1, nq, MOBA_W), lambda b, s, it, jt: (b, 0, 0)),
                pl.BlockSpec((1, 2 * TILE, LANES), lambda b, s, it, jt: (jt[s], 0, 0)),
                pl.BlockSpec((1, N_PAIRS, LANES, TILE), lambda b, s, it, jt: (b, 0, 0, it[s])),
            ],
            out_specs=pl.BlockSpec((1, TILE, MOBA_W), lambda b, s, it, jt: (b, it[s], 0)),
            scratch_shapes=_state_scratch(2 * TILE),
        ),
        out_shape=jax.ShapeDtypeStruct((B, S, MOBA_W), jnp.float32),
        compiler_params=pltpu.CompilerParams(
            dimension_semantics=("arbitrary", "arbitrary"), vmem_limit_bytes=VMEM_LIMIT),
        name="moba",
    )(it, jt, qT, k, vT, kmean, kaug, szT)


CMP_BAND = TILE // CMP_STRIDE + 1


def _cmp_query_aug_table():
    t = np.zeros((LANES, TILE), np.float32)
    trel = np.arange(TILE)
    for u in range(CMP_BAND):
        t[u, trel < CMP_STRIDE * (u - 1) + CMP_LEN - 1] = NEG_INF
    t[CMP_BAND, :] = NEG_INF
    return jnp.asarray(t, jnp.bfloat16)


def _cmp_kernel(n_slc, qT_ref, kc_ref, vcT_ref, ovT_ref, qaug_ref, sz_ref, gate_ref,
                o_ref, sel_ref, s_ref, mx_ref):
    i = pl.program_id(1)
    C = kc_ref.shape[1]
    c_rel = (lax.broadcasted_iota(jnp.int32, (C, LANES), 0)
             - (TILE // CMP_STRIDE) * i + 1)
    u = lax.broadcasted_iota(jnp.int32, (C, LANES), 1)
    kaug = jnp.where((c_rel == u) & (u < CMP_BAND), 1.0, 0.0)
    kaug = jnp.where((u == CMP_BAND) & (c_rel >= CMP_BAND), 1.0, kaug)
    kfull = jnp.concatenate([kc_ref[0], kaug.astype(jnp.bfloat16)], axis=1)
    qaug = qaug_ref[...]

    for p in range(N_PAIRS):
        qT = qT_ref[0, p]
        for g in range(NSA_GROUPS):
            h = 2 * p + g
            s = _dot(kfull, jnp.concatenate([_keep_half(qT, g), qaug], axis=0))
            s_ref[h] = s
            mx_ref[h] = jnp.max(s, axis=0, keepdims=True)

    t_abs = i * TILE + lax.broadcasted_iota(jnp.int32, (1, TILE), 1)
    seen = t_abs >= CMP_LEN - 1
    ones = jnp.ones((ONES_ROWS, C), jnp.bfloat16)
    vals = [jnp.concatenate([vcT_ref[0, g * HALF:(g + 1) * HALF], ones, ovT_ref[...]], axis=0)
            for g in range(NSA_GROUPS)]
    R = sel_ref.shape[2]
    imp = [jnp.zeros((R, TILE), jnp.float32) for _ in range(NSA_GROUPS)]
    for p in range(N_PAIRS):
        outs = []
        for g in range(NSA_GROUPS):
            h = 2 * p + g
            pr = jnp.exp(s_ref[h] - mx_ref[h]).astype(jnp.bfloat16)
            acc = _dot(vals[g], pr)
            inv = jnp.where(seen, 1.0 / acc[HALF:HALF + 1], 0.0)
            outs.append(acc[0:HALF] * inv)
            imp[g] = imp[g] + acc[ACC_ROWS:ACC_ROWS + R] * inv
        _store_pair(o_ref, p, outs[0], outs[1], sz_ref[0, p], _nsa_gates(gate_ref, p, BRANCH_CMP))

    j_iota = lax.broadcasted_iota(jnp.int32, (R, TILE), 0)
    t_q = i * TILE + lax.broadcasted_iota(jnp.int32, (R, TILE), 1)
    own = lax.shift_right_logical(t_q, int(np.log2(SLC_BLOCK)))
    forced = (j_iota == 0) | (j_iota == own) | (j_iota == own - 1)
    causal = j_iota <= own
    for g in range(NSA_GROUPS):
        v = jnp.where(forced, FORCED_SCORE, jnp.where(causal, imp[g], NEG_INF))
        rank = _rank_rows(v, n_slc)
        sel = (rank < min(SLC_TOPN, n_slc)) & causal
        sel_ref[0, g] = jnp.where(sel, 0.0, NEG_INF).astype(jnp.bfloat16)


def _cmp(qbT, kc, vcT, ovT, qaug, szT, gateT, n_slc):
    B, _, _, S = qbT.shape
    C = kc.shape[1]
    nq = S // TILE
    R = ovT.shape[0]
    return pl.pallas_call(
        functools.partial(_cmp_kernel, n_slc),
        grid=(B, nq),
        in_specs=[
            pl.BlockSpec((1, N_PAIRS, LANES, TILE), lambda b, i: (b, 0, 0, i)),
            pl.BlockSpec((1, C, KV_W), lambda b, i: (b, 0, 0)),
            pl.BlockSpec((1, KV_W, C), lambda b, i: (b, 0, 0)),
            pl.BlockSpec(ovT.shape, lambda b, i: (0, 0)),
            pl.BlockSpec(qaug.shape, lambda b, i: (0, 0)),
            pl.BlockSpec((1, N_PAIRS, LANES, TILE), lambda b, i: (b, 0, 0, i)),
            pl.BlockSpec((1, GATE_ROWS, TILE), lambda b, i: (b, 0, i)),
        ],
        out_specs=[pl.BlockSpec((1, TILE, NSA_W), lambda b, i: (b, i, 0)),
                   pl.BlockSpec((1, NSA_GROUPS, R, TILE), lambda b, i: (b, 0, 0, i))],
        out_shape=[jax.ShapeDtypeStruct((B, S, NSA_W), jnp.float32),
                   jax.ShapeDtypeStruct((B, NSA_GROUPS, R, S), jnp.bfloat16)],
        scratch_shapes=[pltpu.VMEM((N_HEADS, C, TILE), jnp.float32),
                        pltpu.VMEM((N_HEADS, 1, TILE), jnp.float32)],
        compiler_params=pltpu.CompilerParams(
            dimension_semantics=("arbitrary", "arbitrary"), vmem_limit_bytes=VMEM_LIMIT),
        name="cmp",
    )(qbT, kc, vcT, ovT, qaug, szT, gateT)


def _slc_kernel(it_ref, jt_ref, qT_ref, k_ref, vT_ref, sel_ref, kaug_ref, sz_ref, gate_ref, o_ref,
                qaug_ref, m_ref, acc_ref, s_ref, mx_ref):
    s = pl.program_id(1)
    i = it_ref[s]
    jj = jt_ref[s]
    n_sel = sel_ref.shape[2]

    @pl.when(jj == 0)
    def _():
        _init_state(m_ref, acc_ref)
        for p in range(N_PAIRS):
            qT = qT_ref[0, p]
            for g in range(NSA_GROUPS):
                h = 2 * p + g
                qm = _keep_half(qT, g)
                ali = _alibi_rows(LANES - n_sel, TILE, 0, SLOPES[g * NSA_HPG + p], i)
                qaug_ref[h] = jnp.concatenate(
                    [qm, sel_ref[0, g], ali.astype(jnp.bfloat16)], axis=0)

    def step(diag):
        kfull = jnp.concatenate([k_ref[0], kaug_ref[0]], axis=1)

        def score(h):
            return _dot(kfull, qaug_ref[h])

        def value(h):
            return _value_rows(vT_ref[0, (h % 2) * HALF:(h % 2 + 1) * HALF])

        keeps = _causal_keeps(i - 2 * jj) if diag else (None, None)
        _attend_heads(score, value, keeps, s_ref, mx_ref, m_ref, acc_ref)
        if diag:
            for p in range(N_PAIRS):
                _finalize_pair(o_ref, acc_ref, p, sz_ref[0, p], _nsa_gates(gate_ref, p, BRANCH_SLC))

    @pl.when(2 * jj + 1 < i)
    def _():
        step(False)

    @pl.when(2 * jj + 1 >= i)
    def _():
        step(True)


def _slc(qbT, ksl, vslT, selT, kaug, szT, gateT):
    B, _, _, S = qbT.shape
    nq = S // TILE
    R = selT.shape[2]
    it, jt = _pair_tables(nq)
    return pl.pallas_call(
        _slc_kernel,
        grid_spec=pltpu.PrefetchScalarGridSpec(
            num_scalar_prefetch=2,
            grid=(B, int(it.shape[0])),
            in_specs=[
                pl.BlockSpec((1, N_PAIRS, LANES, TILE), lambda b, s, it, jt: (b, 0, 0, it[s])),
                pl.BlockSpec((1, 2 * TILE, KV_W), lambda b, s, it, jt: (b, jt[s], 0)),
                pl.BlockSpec((1, KV_W, 2 * TILE), lambda b, s, it, jt: (b, 0, jt[s])),
                pl.BlockSpec((1, NSA_GROUPS, R, TILE), lambda b, s, it, jt: (b, 0, 0, it[s])),
                pl.BlockSpec((1, 2 * TILE, LANES), lambda b, s, it, jt: (jt[s], 0, 0)),
                pl.BlockSpec((1, N_PAIRS, LANES, TILE), lambda b, s, it, jt: (b, 0, 0, it[s])),
                pl.BlockSpec((1, GATE_ROWS, TILE), lambda b, s, it, jt: (b, 0, it[s])),
            ],
            out_specs=pl.BlockSpec((1, TILE, NSA_W), lambda b, s, it, jt: (b, it[s], 0)),
            scratch_shapes=_state_scratch(2 * TILE),
        ),
        out_shape=jax.ShapeDtypeStruct((B, S, NSA_W), jnp.float32),
        compiler_params=pltpu.CompilerParams(
            dimension_semantics=("arbitrary", "arbitrary"), vmem_limit_bytes=VMEM_LIMIT),
        name="slc",
    )(it, jt, qbT, ksl, vslT, selT, kaug, szT, gateT)


def _win_kernel(qT_ref, k2_ref, k1_ref, k0_ref, v2_ref, v1_ref, v0_ref, kaug_ref, sz_ref, gate_ref,
                o_ref, qaug_ref, m_ref, acc_ref, s_ref, mx_ref):
    i = pl.program_id(1)
    _init_state(m_ref, acc_ref)
    r = lax.broadcasted_iota(jnp.int32, (LANES, TILE), 0)
    trel = lax.broadcasted_iota(jnp.int32, (LANES, TILE), 1).astype(jnp.float32)
    absent2 = jnp.where(i >= 2, 0.0, NEG_INF)
    absent1 = jnp.where(i >= 1, 0.0, NEG_INF)
    for p in range(N_PAIRS):
        qT = qT_ref[0, p]
        for g in range(NSA_GROUPS):
            h = 2 * p + g
            slope = SLOPES[g * NSA_HPG + p]
            aug = jnp.where(r == 0, -slope * trel, 0.0)
            aug = jnp.where(r == 1, slope, aug)
            aug = jnp.where(r == 2, -slope, aug)
            aug = jnp.where(r == 3, absent1, aug)
            aug = jnp.where(r == 4, absent2, aug)
            qaug_ref[h] = jnp.concatenate([_keep_half(qT, g), aug.astype(jnp.bfloat16)], axis=0)

    kfull = jnp.concatenate(
        [jnp.concatenate([k2_ref[0], k1_ref[0], k0_ref[0]], axis=0), kaug_ref[...]], axis=1)
    vT = jnp.concatenate([v2_ref[0], v1_ref[0], v0_ref[0]], axis=1)
    k, t = _tile_iotas()

    def score(h):
        return _dot(kfull, qaug_ref[h])

    def value(h):
        return _value_rows(vT[(h % 2) * HALF:(h % 2 + 1) * HALF])

    _attend_heads(score, value, (k > t, None, k <= t), s_ref, mx_ref, m_ref, acc_ref)
    for p in range(N_PAIRS):
        _finalize_pair(o_ref, acc_ref, p, sz_ref[0, p], _nsa_gates(gate_ref, p, BRANCH_WIN))


def _win(qbT, kwi, vwiT, kaug, szT, gateT):
    B, _, _, S = qbT.shape
    nq = S // TILE
    kspec = lambda d: pl.BlockSpec((1, TILE, KV_W), lambda b, i: (b, jnp.maximum(i - d, 0), 0))
    vspec = lambda d: pl.BlockSpec((1, KV_W, TILE), lambda b, i: (b, 0, jnp.maximum(i - d, 0)))
    return pl.pallas_call(
        _win_kernel,
        grid=(B, nq),
        in_specs=[
            pl.BlockSpec((1, N_PAIRS, LANES, TILE), lambda b, i: (b, 0, 0, i)),
            kspec(2), kspec(1), kspec(0), vspec(2), vspec(1), vspec(0),
            pl.BlockSpec(kaug.shape, lambda b, i: (0, 0)),
            pl.BlockSpec((1, N_PAIRS, LANES, TILE), lambda b, i: (b, 0, 0, i)),
            pl.BlockSpec((1, GATE_ROWS, TILE), lambda b, i: (b, 0, i)),
        ],
        out_specs=pl.BlockSpec((1, TILE, NSA_W), lambda b, i: (b, i, 0)),
        out_shape=jax.ShapeDtypeStruct((B, S, NSA_W), jnp.float32),
        scratch_shapes=_state_scratch(kaug.shape[0]),
        compiler_params=pltpu.CompilerParams(
            dimension_semantics=("arbitrary", "arbitrary"), vmem_limit_bytes=VMEM_LIMIT),
        name="win",
    )(qbT, kwi, kwi, kwi, vwiT, vwiT, vwiT, kaug, szT, gateT)


def _out_kernel(x_ref, oa_ref, oc_ref, os_ref, ow_ref, w_ref, g_ref, y_ref):
    ob = oc_ref[0] + os_ref[0] + ow_ref[0]
    mix = jnp.concatenate([oa_ref[0], ob], axis=1).astype(jnp.bfloat16)
    y = _dot(mix, w_ref[...])
    r = lax.rsqrt(jnp.mean(y * y, axis=-1, keepdims=True) + RMS_EPS)
    y_ref[0] = x_ref[0] + y * r * g_ref[...]


def _out(x, oa, oc, os_, ow, w, g):
    B, S, _ = x.shape
    rows = lambda wd: pl.BlockSpec((1, TILE, wd), lambda b, i: (b, i, 0))
    return pl.pallas_call(
        _out_kernel,
        grid=(B, S // TILE),
        in_specs=[rows(D_MODEL), rows(MOBA_W), rows(NSA_W), rows(NSA_W), rows(NSA_W),
                  pl.BlockSpec(w.shape, lambda b, i: (0, 0)),
                  pl.BlockSpec((1, D_MODEL), lambda b, i: (0, 0))],
        out_specs=rows(D_MODEL),
        out_shape=jax.ShapeDtypeStruct((B, S, D_MODEL), jnp.float32),
        compiler_params=pltpu.CompilerParams(
            dimension_semantics=("arbitrary", "arbitrary"), vmem_limit_bytes=VMEM_LIMIT),
        name="out",
    )(x, oa, oc, os_, ow, w, g)


def _compress_weights(w1, w2):
    half = (CMP_LEN // 2) * HEAD_DIM
    w1r = w1.reshape(2, CMP_LEN // 2, HEAD_DIM, CMP_HIDDEN)
    z = jnp.zeros_like(w1r)
    g0 = jnp.concatenate([w1r, z], axis=-1)
    g1 = jnp.concatenate([z, w1r], axis=-1)
    both = jnp.stack([g0, g1], axis=2)
    both = both.reshape(2, half * NSA_GROUPS, NSA_GROUPS * CMP_HIDDEN).astype(jnp.bfloat16)
    zz = jnp.zeros_like(w2)
    w2bd = jnp.concatenate([jnp.concatenate([w2, zz], axis=1),
                            jnp.concatenate([zz, w2], axis=1)], axis=0).astype(jnp.bfloat16)
    return both[0], both[1], w2bd


def _overlap_T(n_cmp_pad, n_slc, rows):
    c = np.arange(n_cmp_pad)[None, :] * CMP_STRIDE
    j = np.arange(rows)[:, None] * SLC_BLOCK
    ov = (c < j + SLC_BLOCK) & (c + CMP_LEN > j) & (np.arange(rows)[:, None] < n_slc)
    return jnp.asarray(ov.astype(np.float32), jnp.bfloat16)


def _layer(x, pre_g, post_g, w_in, pos_k, pos_v, w_k1, w_k2, w_v1, w_v2, w_out):
    B, S, _ = x.shape
    nq = S // TILE
    n_cmp = (S - CMP_LEN) // CMP_STRIDE + 1
    n_slc = S // SLC_BLOCK
    C = S // CMP_STRIDE
    nat_cols, tr_cols, out_rows = _column_maps()

    wn = jnp.take(w_in, nat_cols, axis=1).astype(jnp.bfloat16)
    wt = jnp.take(w_in, tr_cols, axis=1).T
    wt = jnp.concatenate([wt, jnp.zeros((N_TR - wt.shape[0], D_MODEL), wt.dtype)], axis=0)
    (ka, kmean, kcm, vcm, ksl, kwi, qaT, vaT, qbT, vslT, vwiT,
     szaT, szbT, gateT) = _proj(x, pre_g.reshape(1, D_MODEL), wn, wt.astype(jnp.bfloat16))

    wkt, wkb, w2k = _compress_weights(w_k1, w_k2)
    wvt, wvb, w2v = _compress_weights(w_v1, w_v2)
    chunk = CMP_STRIDE * KV_W
    kc, vcT = _compress(
        kcm.reshape(B, C, chunk), vcm.reshape(B, C, chunk), wkt, wkb, wvt, wvb, w2k, w2v,
        pos_k.reshape(1, CMP_LEN * HEAD_DIM).astype(jnp.bfloat16),
        pos_v.reshape(1, CMP_LEN * HEAD_DIM).astype(jnp.bfloat16),
        w_k1.astype(jnp.bfloat16), w_v1.astype(jnp.bfloat16), n_cmp)

    oc, selT = _cmp(qbT, kc, vcT, _overlap_T(C, n_slc, SLC_BLOCK), _cmp_query_aug_table(),
                    szbT, gateT, n_slc)
    oa = _moba(qaT, ka, vaT, kmean.reshape(B, nq, MOBA_W), _key_aug_table(nq, 16, 1), szaT)
    osl = _slc(qbT, ksl, vslT, selT, _key_aug_table(nq, selT.shape[2], TILE // SLC_BLOCK),
               szbT, gateT)
    ow = _win(qbT, kwi, vwiT, _win_key_aug_table(), szbT, gateT)

    w_o = jnp.take(w_out, out_rows, axis=0).astype(jnp.bfloat16)
    return _out(x, oa, oc, osl, ow, w_o, post_g.reshape(1, D_MODEL))


def kernel(x, pre_norm_g, post_norm_g, w_in, cmp_pos_k, cmp_pos_v,
           w_cmp_k1, w_cmp_k2, w_cmp_v1, w_cmp_v2, w_out):
    for l in range(pre_norm_g.shape[0]):
        x = _layer(x, pre_norm_g[l], post_norm_g[l], w_in[l], cmp_pos_k[l], cmp_pos_v[l],
                   w_cmp_k1[l], w_cmp_k2[l], w_cmp_v1[l], w_cmp_v2[l], w_out[l])
    return x
```

```python
import functools

import numpy as np
import jax
import jax.numpy as jnp
from jax import lax
from jax.experimental import pallas as pl
from jax.experimental.pallas import tpu as pltpu

D_MODEL = 1024
HEAD_DIM = 64
N_HEADS = 8
N_PAIRS = N_HEADS // 2
NSA_GROUPS = 2
NSA_HPG = N_HEADS // NSA_GROUPS
MOBA_BLOCK = 256
MOBA_TOPK = 3
CMP_LEN = 32
CMP_STRIDE = 16
CMP_HIDDEN = 128
SLC_BLOCK = 64
SLC_TOPN = 16
WINDOW = 512
RMS_EPS = 1e-6
NEG_INF = -1e30
FORCED_SCORE = 1e9
SCALE = HEAD_DIM ** -0.5

TILE = 256
QT = 2 * TILE
LANES = 128
HALF = HEAD_DIM
SUBLANES = 8
MOBA_W = N_HEADS * HEAD_DIM
NSA_W = N_HEADS * HEAD_DIM
KV_W = NSA_GROUPS * HEAD_DIM
N_GATES = 3 * N_HEADS
GATE_ROWS = 32
SLOPES = tuple(2.0 ** (-(i + 1)) for i in range(N_HEADS))
BRANCH_CMP, BRANCH_SLC, BRANCH_WIN = 0, 1, 2

VMEM_LIMIT = 48 * 1024 * 1024

_OFF = dict(qa=0, ka=512, va=1024, za=1536, qb=2048, kcm=2560, vcm=2688,
            ksl=2816, vsl=2944, kwi=3072, vwi=3200, gate=3328, zb=3352)
_NAT = dict(ka=(0, 512), kcm=(512, 640), vcm=(640, 768), ksl=(768, 896), kwi=(896, 1024))
N_NAT = 1024
_TR = dict(qa=(0, 512), va=(512, 1024), qb=(1024, 1536), vsl=(1536, 1664),
           vwi=(1664, 1792), za=(1792, 2304), zb=(2304, 2816), gate=(2816, 2848))
N_TR = 2848


def _pair_perm():
    idx = np.zeros(NSA_W, np.int32)
    for p in range(N_PAIRS):
        for g in range(NSA_GROUPS):
            for d in range(HEAD_DIM):
                idx[p * LANES + g * HALF + d] = (g * NSA_HPG + p) * HEAD_DIM + d
    return idx


def _column_maps():
    pp = _pair_perm()
    nat = np.zeros(N_NAT, np.int32)
    nat[0:512] = _OFF["ka"] + np.arange(512)
    nat[512:640] = _OFF["kcm"] + np.arange(128)
    nat[640:768] = _OFF["vcm"] + np.arange(128)
    nat[768:896] = _OFF["ksl"] + np.arange(128)
    nat[896:1024] = _OFF["kwi"] + np.arange(128)
    tr = np.zeros(N_TR - (GATE_ROWS - N_GATES), np.int32)
    tr[0:512] = _OFF["qa"] + np.arange(512)
    tr[512:1024] = _OFF["va"] + np.arange(512)
    tr[1024:1536] = _OFF["qb"] + pp
    tr[1536:1664] = _OFF["vsl"] + np.arange(128)
    tr[1664:1792] = _OFF["vwi"] + np.arange(128)
    tr[1792:2304] = _OFF["za"] + np.arange(512)
    tr[2304:2816] = _OFF["zb"] + pp
    tr[2816:2816 + N_GATES] = _OFF["gate"] + np.arange(N_GATES)
    out_rows = np.concatenate([np.arange(512), 512 + pp]).astype(np.int32)
    return nat, tr, out_rows


def _gate_row(g, p, branch):
    return (g * NSA_HPG + p) * 3 + branch


def _sigmoid(x):
    return 1.0 / (1.0 + jnp.exp(-x))


def _silu(x):
    return x * _sigmoid(x)


def _dot(a, b):
    return jnp.dot(a, b, preferred_element_type=jnp.float32)


def _proj_kernel(x_ref, g_ref, wn_ref, wt_ref,
                 ka_ref, kmean_ref, kcm_ref, vcm_ref, ksl_ref, kwi_ref,
                 qaT_ref, vaT_ref, qbT_ref, vslT_ref, vwiT_ref,
                 szaT_ref, szbT_ref, gateT_ref):
    x = x_ref[0]
    r = lax.rsqrt(jnp.mean(x * x, axis=-1, keepdims=True) + RMS_EPS)
    h = (x * r * g_ref[...]).astype(jnp.bfloat16)

    def nat(name):
        a, b = _NAT[name]
        return _dot(h, wn_ref[:, a:b])

    ka = nat("ka")
    kmean_ref[0, 0] = jnp.mean(ka, axis=0, keepdims=True)
    for p in range(N_PAIRS):
        ka_ref[0, p] = ka[:, p * LANES:(p + 1) * LANES].astype(jnp.bfloat16)
    kcm_ref[0] = nat("kcm").astype(jnp.bfloat16)
    vcm_ref[0] = nat("vcm").astype(jnp.bfloat16)
    ksl_ref[0] = nat("ksl").astype(jnp.bfloat16)
    kwi_ref[0] = nat("kwi").astype(jnp.bfloat16)

    def tr(name):
        a, b = _TR[name]
        return lax.dot_general(wt_ref[a:b, :], h, (((1,), (1,)), ((), ())),
                               preferred_element_type=jnp.float32)

    qa = tr("qa") * SCALE
    va = tr("va")
    qb = tr("qb") * SCALE
    sza = _silu(tr("za"))
    szb = _silu(tr("zb"))
    for p in range(N_PAIRS):
        sl = slice(p * LANES, (p + 1) * LANES)
        qaT_ref[0, p] = qa[sl].astype(jnp.bfloat16)
        vaT_ref[0, p] = va[sl].astype(jnp.bfloat16)
        qbT_ref[0, p] = qb[sl].astype(jnp.bfloat16)
        szaT_ref[0, p] = sza[sl]
        szbT_ref[0, p] = szb[sl]
    vslT_ref[0] = tr("vsl").astype(jnp.bfloat16)
    vwiT_ref[0] = tr("vwi").astype(jnp.bfloat16)
    gateT_ref[0] = _sigmoid(tr("gate"))


def _proj(x, g, wn, wt):
    B, S, _ = x.shape
    nq = S // TILE
    bf, f32 = jnp.bfloat16, jnp.float32
    slab_nat = lambda: pl.BlockSpec((1, N_PAIRS, TILE, LANES), lambda b, i: (b, 0, i, 0))
    slab_tr = lambda: pl.BlockSpec((1, N_PAIRS, LANES, TILE), lambda b, i: (b, 0, 0, i))
    rows = lambda w: pl.BlockSpec((1, TILE, w), lambda b, i: (b, i, 0))
    chans = lambda c: pl.BlockSpec((1, c, TILE), lambda b, i: (b, 0, i))
    out_shape = [
        jax.ShapeDtypeStruct((B, N_PAIRS, S, LANES), bf),
        jax.ShapeDtypeStruct((B, nq, 1, MOBA_W), f32),
        jax.ShapeDtypeStruct((B, S, KV_W), bf),
        jax.ShapeDtypeStruct((B, S, KV_W), bf),
        jax.ShapeDtypeStruct((B, S, KV_W), bf),
        jax.ShapeDtypeStruct((B, S, KV_W), bf),
        jax.ShapeDtypeStruct((B, N_PAIRS, LANES, S), bf),
        jax.ShapeDtypeStruct((B, N_PAIRS, LANES, S), bf),
        jax.ShapeDtypeStruct((B, N_PAIRS, LANES, S), bf),
        jax.ShapeDtypeStruct((B, KV_W, S), bf),
        jax.ShapeDtypeStruct((B, KV_W, S), bf),
        jax.ShapeDtypeStruct((B, N_PAIRS, LANES, S), f32),
        jax.ShapeDtypeStruct((B, N_PAIRS, LANES, S), f32),
        jax.ShapeDtypeStruct((B, GATE_ROWS, S), f32),
    ]
    out_specs = [
        slab_nat(),
        pl.BlockSpec((1, 1, 1, MOBA_W), lambda b, i: (b, i, 0, 0)),
        rows(KV_W), rows(KV_W), rows(KV_W), rows(KV_W),
        slab_tr(), slab_tr(), slab_tr(),
        chans(KV_W), chans(KV_W),
        slab_tr(), slab_tr(),
        chans(GATE_ROWS),
    ]
    return pl.pallas_call(
        _proj_kernel,
        grid=(B, nq),
        in_specs=[
            pl.BlockSpec((1, TILE, D_MODEL), lambda b, i: (b, i, 0)),
            pl.BlockSpec((1, D_MODEL), lambda b, i: (0, 0)),
            pl.BlockSpec((D_MODEL, N_NAT), lambda b, i: (0, 0)),
            pl.BlockSpec((N_TR, D_MODEL), lambda b, i: (0, 0)),
        ],
        out_specs=out_specs,
        out_shape=out_shape,
        compiler_params=pltpu.CompilerParams(
            dimension_semantics=("arbitrary", "arbitrary"),
            vmem_limit_bytes=VMEM_LIMIT),
        name="proj",
    )(x, g, wn, wt)


def _compress_kernel(n_cmp, xk_ref, xv_ref, wkt_ref, wkb_ref, wvt_ref, wvb_ref,
                     w2k_ref, w2v_ref, pek_ref, pev_ref, w1k_ref, w1v_ref,
                     kc_ref, vcT_ref):
    C = xk_ref.shape[1]
    row = lax.broadcasted_iota(jnp.int32, (C, KV_W), 0)

    def phi(x_ref, wt_ref, wb_ref, w2_ref, pe_ref, w1_ref):
        x = x_ref[0]
        top = _dot(x, wt_ref[...])
        bot = _dot(x, wb_ref[...])
        peb = _dot(jnp.broadcast_to(pe_ref[...], (8, pe_ref.shape[1])), w1_ref[...])[0:1]
        peb = jnp.concatenate([peb, peb], axis=1)
        hid = top + pltpu.roll(bot, C - 1, 0) + peb
        out = _dot(_silu(hid).astype(jnp.bfloat16), w2_ref[...])
        return jnp.where(row < n_cmp, out, 0.0)

    kc_ref[0] = phi(xk_ref, wkt_ref, wkb_ref, w2k_ref, pek_ref, w1k_ref).astype(jnp.bfloat16)
    vc = phi(xv_ref, wvt_ref, wvb_ref, w2v_ref, pev_ref, w1v_ref)
    vcT_ref[0] = vc.T.astype(jnp.bfloat16)


def _compress(xk, xv, wkt, wkb, wvt, wvb, w2k, w2v, pek, pev, w1k, w1v, n_cmp):
    B, C, F = xk.shape
    full = lambda a: pl.BlockSpec(a.shape, lambda b: (0,) * a.ndim)
    return pl.pallas_call(
        functools.partial(_compress_kernel, n_cmp),
        grid=(B,),
        in_specs=[pl.BlockSpec((1, C, F), lambda b: (b, 0, 0)),
                  pl.BlockSpec((1, C, F), lambda b: (b, 0, 0)),
                  full(wkt), full(wkb), full(wvt), full(wvb), full(w2k), full(w2v),
                  full(pek), full(pev), full(w1k), full(w1v)],
        out_specs=[pl.BlockSpec((1, C, KV_W), lambda b: (b, 0, 0)),
                   pl.BlockSpec((1, KV_W, C), lambda b: (b, 0, 0))],
        out_shape=[jax.ShapeDtypeStruct((B, C, KV_W), jnp.bfloat16),
                   jax.ShapeDtypeStruct((B, KV_W, C), jnp.bfloat16)],
        compiler_params=pltpu.CompilerParams(
            dimension_semantics=("arbitrary",), vmem_limit_bytes=VMEM_LIMIT),
        name="compress",
    )(xk, xv, wkt, wkb, wvt, wvb, w2k, w2v, pek, pev, w1k, w1v)


def _keep_half(qT, half):
    z = jnp.zeros((HALF, qT.shape[1]), qT.dtype)
    if half == 0:
        return jnp.concatenate([qT[0:HALF], z], axis=0)
    return jnp.concatenate([z, qT[HALF:2 * HALF]], axis=0)


def _rank_rows(vals, n_rows):
    R, T = vals.shape
    tiles = [vals[a:a + SUBLANES] for a in range(0, R, SUBLANES)]
    ranks = [jnp.zeros((SUBLANES, T), jnp.int32) for _ in tiles]
    j_in = lax.broadcasted_iota(jnp.int32, (SUBLANES, T), 0)
    for m in range(n_rows):
        vm = vals[m:m + 1, :]
        for a, tile in enumerate(tiles):
            lo = a * SUBLANES
            if lo > m:
                beats = vm >= tile
            elif lo + SUBLANES - 1 <= m:
                beats = vm > tile
            else:
                beats = (vm > tile) | ((vm == tile) & (j_in > m - lo))
            ranks[a] = ranks[a] + beats.astype(jnp.int32)
    return jnp.concatenate(ranks, axis=0)


def _query_tile(cols, tile0):
    return tile0 + lax.shift_right_logical(cols, int(np.log2(TILE)))


def _alibi_rows(n_rows, T, first, slope, tile0):
    r = lax.broadcasted_iota(jnp.int32, (n_rows, T), 0)
    c = lax.broadcasted_iota(jnp.int32, (n_rows, T), 1)
    trel = (c & (TILE - 1)).astype(jnp.float32)
    base = (_query_tile(c, tile0) * TILE).astype(jnp.float32)
    out = jnp.where(r == first, -slope * trel, 0.0)
    out = jnp.where(r == first + 1, -slope * base, out)
    out = jnp.where((r == first + 2) | (r == first + 3), slope, out)
    return out


def _key_aug_table(nq, n_sel, blocks_per_tile):
    t = np.zeros((nq, TILE, LANES), np.float32)
    krel = np.arange(TILE)
    for j in range(nq):
        if n_sel:
            blk = j * blocks_per_tile + krel // (TILE // blocks_per_tile)
            t[j, krel, blk] = 1.0
        t[j, :, n_sel] = 1.0
        t[j, :, n_sel + 1] = 1.0
        t[j, :, n_sel + 2] = krel
        t[j, :, n_sel + 3] = TILE * j
    return jnp.asarray(t.reshape(nq // 2, 2 * TILE, LANES), jnp.bfloat16)


ONES_ROWS = 16
ACC_ROWS = HALF + ONES_ROWS


def _win_key_aug_table():
    n_tiles = WINDOW // TILE + 1
    t = np.zeros((n_tiles * TILE, LANES), np.float32)
    for slot in range(n_tiles):
        delta = n_tiles - 1 - slot
        rows = slice(slot * TILE, (slot + 1) * TILE)
        t[rows, 0] = 1.0
        t[rows, 1] = np.arange(TILE)
        t[rows, 2] = TILE * delta
        if delta > 0:
            t[rows, 2 + delta] = 1.0
    return jnp.asarray(t, jnp.bfloat16)


def _value_rows(vT_h):
    return jnp.concatenate([vT_h, jnp.ones((ONES_ROWS, vT_h.shape[1]), vT_h.dtype)], axis=0)


def _attend_heads(score_fn, value_fn, keeps, s_ref, mx_ref, m_ref, acc_ref):
    for h in range(N_HEADS):
        s = score_fn(h)
        if any(k is not None for k in keeps):
            s = jnp.concatenate(
                [s[r * TILE:(r + 1) * TILE] if k is None
                 else jnp.where(k, s[r * TILE:(r + 1) * TILE], NEG_INF)
                 for r, k in enumerate(keeps)], axis=0)
        s_ref[h] = s
        mx_ref[h] = jnp.max(s, axis=0, keepdims=True)
    for h in range(N_HEADS):
        m_prev = m_ref[h]
        m_new = jnp.maximum(m_prev, mx_ref[h])
        alpha = jnp.exp(m_prev - m_new)
        p = jnp.exp(s_ref[h] - m_new).astype(jnp.bfloat16)
        acc_ref[h] = alpha * acc_ref[h] + _dot(value_fn(h), p)
        m_ref[h] = m_new


def _init_state(m_ref, acc_ref):
    m_ref[...] = jnp.full(m_ref.shape, NEG_INF, jnp.float32)
    acc_ref[...] = jnp.zeros(acc_ref.shape, jnp.float32)


def _tile_iotas(T=TILE):
    k = lax.broadcasted_iota(jnp.int32, (TILE, T), 0)
    t = lax.broadcasted_iota(jnp.int32, (TILE, T), 1)
    return k, t


def _causal_keeps():
    k, t = _tile_iotas(QT)
    return (k <= t, k + TILE <= t)


def _store_pair(o_ref, p, o0T, o1T, szT, gates):
    if gates is not None:
        o0T = o0T * gates[0]
        o1T = o1T * gates[1]
    oT = jnp.concatenate([o0T, o1T], axis=0) * szT
    o_ref[0, :, p * LANES:(p + 1) * LANES] = oT.T


def _finalize_pair(o_ref, acc_ref, p, szT, gates=None):
    a0 = acc_ref[2 * p]
    a1 = acc_ref[2 * p + 1]
    _store_pair(o_ref, p, a0[0:HALF] / a0[HALF:HALF + 1], a1[0:HALF] / a1[HALF:HALF + 1],
                szT, gates)


def _nsa_gates(gate_ref, p, branch):
    return [gate_ref[0, _gate_row(g, p, branch):_gate_row(g, p, branch) + 1, :]
            for g in range(NSA_GROUPS)]


def _pair_tables(n_pairs):
    it, jt = [], []
    for ii in range(n_pairs):
        for jj in range(ii + 1):
            it.append(ii)
            jt.append(jj)
    return jnp.asarray(it, jnp.int32), jnp.asarray(jt, jnp.int32)


def _state_scratch(n_keys, T=TILE):
    return [
        pltpu.VMEM((N_HEADS, 2 * LANES, T), jnp.bfloat16),
        pltpu.VMEM((N_HEADS, 1, T), jnp.float32),
        pltpu.VMEM((N_HEADS, ACC_ROWS, T), jnp.float32),
        pltpu.VMEM((N_HEADS, n_keys, T), jnp.float32),
        pltpu.VMEM((N_HEADS, 1, T), jnp.float32),
    ]


def _moba_kernel(nq, it_ref, jt_ref, qT_ref, k_ref, vT_ref, kmean_ref, kaug_ref, sz_ref, o_ref,
                 qaug_ref, m_ref, acc_ref, s_ref, mx_ref):
    s = pl.program_id(1)
    ii = it_ref[s]
    jj = jt_ref[s]
    n_sel = 16

    @pl.when(jj == 0)
    def _():
        _init_state(m_ref, acc_ref)
        n_iota = lax.broadcasted_iota(jnp.int32, (nq, QT), 0)
        own = _query_tile(lax.broadcasted_iota(jnp.int32, (nq, QT), 1), 2 * ii)
        for p in range(N_PAIRS):
            qT = qT_ref[0, p]
            km = kmean_ref[0, :, p * LANES:(p + 1) * LANES].astype(jnp.bfloat16)
            for hh in range(2):
                h = 2 * p + hh
                qm = _keep_half(qT, hh)
                gate = jnp.where(n_iota < own, _dot(km, qm), NEG_INF)
                rank = _rank_rows(gate, nq)
                sel = ((rank < MOBA_TOPK) & (n_iota < own)) | (n_iota == own)
                selneg = jnp.where(sel, 0.0, NEG_INF)
                if nq < n_sel:
                    selneg = jnp.concatenate(
                        [selneg, jnp.zeros((n_sel - nq, QT), jnp.float32)], axis=0)
                ali = _alibi_rows(LANES - n_sel, QT, 0, SLOPES[h], 2 * ii)
                aug = jnp.concatenate([selneg, ali], axis=0).astype(jnp.bfloat16)
                qaug_ref[h] = jnp.concatenate([qm, aug], axis=0)

    def step(diag):
        kaug = kaug_ref[0]

        def score(h):
            return _dot(jnp.concatenate([k_ref[0, h // 2], kaug], axis=1), qaug_ref[h])

        def value(h):
            return _value_rows(vT_ref[0, h // 2, (h % 2) * HALF:(h % 2 + 1) * HALF])

        keeps = _causal_keeps() if diag else (None, None)
        _attend_heads(score, value, keeps, s_ref, mx_ref, m_ref, acc_ref)
        if diag:
            for p in range(N_PAIRS):
                _finalize_pair(o_ref, acc_ref, p, sz_ref[0, p])

    @pl.when(jj < ii)
    def _():
        step(False)

    @pl.when(jj == ii)
    def _():
        step(True)


def _moba(qT, k, vT, kmean, kaug, szT):
    B, _, _, S = qT.shape
    nq = S // TILE
    it, jt = _pair_tables(S // QT)
    return pl.pallas_call(
        functools.partial(_moba_kernel, nq),
        grid_spec=pltpu.PrefetchScalarGridSpec(
            num_scalar_prefetch=2,
            grid=(B, int(it.shape[0])),
            in_specs=[
                pl.BlockSpec((1, N_PAIRS, LANES, QT), lambda b, s, it, jt: (b, 0, 0, it[s])),
                pl.BlockSpec((1, N_PAIRS, 2 * TILE, LANES), lambda b, s, it, jt: (b, 0, jt[s], 0)),
                pl.BlockSpec((1, N_PAIRS, LANES, 2 * TILE), lambda b, s, it, jt: (b, 0, 0, jt[s])),
                pl.BlockSpec((1, nq, MOBA_W), lambda b, s, it, jt: (b, 0, 0)),
                pl.BlockSpec((1, 2 * TILE, LANES), lambda b, s, it, jt: (jt[s], 0, 0)),
                pl.BlockSpec((1, N_PAIRS, LANES, QT), lambda b, s, it, jt: (b, 0, 0, it[s])),
            ],
            out_specs=pl.BlockSpec((1, QT, MOBA_W), lambda b, s, it, jt: (b, it[s], 0)),
            scratch_shapes=_state_scratch(2 * TILE, QT),
        ),
        out_shape=jax.ShapeDtypeStruct((B, S, MOBA_W), jnp.float32),
        compiler_params=pltpu.CompilerParams(
            dimension_semantics=("arbitrary", "arbitrary"), vmem_limit_bytes=VMEM_LIMIT),
        name="moba",
    )(it, jt, qT, k, vT, kmean, kaug, szT)


CMP_BAND = TILE // CMP_STRIDE + 1


def _cmp_query_aug_table():
    t = np.zeros((LANES, TILE), np.float32)
    trel = np.arange(TILE)
    for u in range(CMP_BAND):
        t[u, trel < CMP_STRIDE * (u - 1) + CMP_LEN - 1] = NEG_INF
    t[CMP_BAND, :] = NEG_INF
    return jnp.asarray(t, jnp.bfloat16)


def _cmp_kernel(n_slc, qT_ref, kc_ref, vcT_ref, ovT_ref, qaug_ref, sz_ref, gate_ref,
                o_ref, sel_ref, s_ref, mx_ref):
    i = pl.program_id(1)
    C = kc_ref.shape[1]
    c_rel = (lax.broadcasted_iota(jnp.int32, (C, LANES), 0)
             - (TILE // CMP_STRIDE) * i + 1)
    u = lax.broadcasted_iota(jnp.int32, (C, LANES), 1)
    kaug = jnp.where((c_rel == u) & (u < CMP_BAND), 1.0, 0.0)
    kaug = jnp.where((u == CMP_BAND) & (c_rel >= CMP_BAND), 1.0, kaug)
    kfull = jnp.concatenate([kc_ref[0], kaug.astype(jnp.bfloat16)], axis=1)
    qaug = qaug_ref[...]

    for p in range(N_PAIRS):
        qT = qT_ref[0, p]
        for g in range(NSA_GROUPS):
            h = 2 * p + g
            s = _dot(kfull, jnp.concatenate([_keep_half(qT, g), qaug], axis=0))
            s_ref[h] = s
            mx_ref[h] = jnp.max(s, axis=0, keepdims=True)

    t_abs = i * TILE + lax.broadcasted_iota(jnp.int32, (1, TILE), 1)
    seen = t_abs >= CMP_LEN - 1
    ones = jnp.ones((ONES_ROWS, C), jnp.bfloat16)
    vals = [jnp.concatenate([vcT_ref[0, g * HALF:(g + 1) * HALF], ones, ovT_ref[...]], axis=0)
            for g in range(NSA_GROUPS)]
    R = sel_ref.shape[2]
    imp = [jnp.zeros((R, TILE), jnp.float32) for _ in range(NSA_GROUPS)]
    for p in range(N_PAIRS):
        outs = []
        for g in range(NSA_GROUPS):
            h = 2 * p + g
            pr = jnp.exp(s_ref[h] - mx_ref[h]).astype(jnp.bfloat16)
            acc = _dot(vals[g], pr)
            inv = jnp.where(seen, 1.0 / acc[HALF:HALF + 1], 0.0)
            outs.append(acc[0:HALF] * inv)
            imp[g] = imp[g] + acc[ACC_ROWS:ACC_ROWS + R] * inv
        _store_pair(o_ref, p, outs[0], outs[1], sz_ref[0, p], _nsa_gates(gate_ref, p, BRANCH_CMP))

    j_iota = lax.broadcasted_iota(jnp.int32, (R, TILE), 0)
    t_q = i * TILE + lax.broadcasted_iota(jnp.int32, (R, TILE), 1)
    own = lax.shift_right_logical(t_q, int(np.log2(SLC_BLOCK)))
    forced = (j_iota == 0) | (j_iota == own) | (j_iota == own - 1)
    causal = j_iota <= own
    for g in range(NSA_GROUPS):
        v = jnp.where(forced, FORCED_SCORE, jnp.where(causal, imp[g], NEG_INF))
        rank = _rank_rows(v, n_slc)
        sel = (rank < min(SLC_TOPN, n_slc)) & causal
        sel_ref[0, g] = jnp.where(sel, 0.0, NEG_INF).astype(jnp.bfloat16)


def _cmp(qbT, kc, vcT, ovT, qaug, szT, gateT, n_slc):
    B, _, _, S = qbT.shape
    C = kc.shape[1]
    nq = S // TILE
    R = ovT.shape[0]
    return pl.pallas_call(
        functools.partial(_cmp_kernel, n_slc),
        grid=(B, nq),
        in_specs=[
            pl.BlockSpec((1, N_PAIRS, LANES, TILE), lambda b, i: (b, 0, 0, i)),
            pl.BlockSpec((1, C, KV_W), lambda b, i: (b, 0, 0)),
            pl.BlockSpec((1, KV_W, C), lambda b, i: (b, 0, 0)),
            pl.BlockSpec(ovT.shape, lambda b, i: (0, 0)),
            pl.BlockSpec(qaug.shape, lambda b, i: (0, 0)),
            pl.BlockSpec((1, N_PAIRS, LANES, TILE), lambda b, i: (b, 0, 0, i)),
            pl.BlockSpec((1, GATE_ROWS, TILE), lambda b, i: (b, 0, i)),
        ],
        out_specs=[pl.BlockSpec((1, TILE, NSA_W), lambda b, i: (b, i, 0)),
                   pl.BlockSpec((1, NSA_GROUPS, R, TILE), lambda b, i: (b, 0, 0, i))],
        out_shape=[jax.ShapeDtypeStruct((B, S, NSA_W), jnp.float32),
                   jax.ShapeDtypeStruct((B, NSA_GROUPS, R, S), jnp.bfloat16)],
        scratch_shapes=[pltpu.VMEM((N_HEADS, C, TILE), jnp.float32),
                        pltpu.VMEM((N_HEADS, 1, TILE), jnp.float32)],
        compiler_params=pltpu.CompilerParams(
            dimension_semantics=("arbitrary", "arbitrary"), vmem_limit_bytes=VMEM_LIMIT),
        name="cmp",
    )(qbT, kc, vcT, ovT, qaug, szT, gateT)


def _slc_kernel(it_ref, jt_ref, qT_ref, k_ref, vT_ref, sel_ref, kaug_ref, sz_ref, gate_ref, o_ref,
                qaug_ref, m_ref, acc_ref, s_ref, mx_ref):
    s = pl.program_id(1)
    ii = it_ref[s]
    jj = jt_ref[s]
    n_sel = sel_ref.shape[2]

    @pl.when(jj == 0)
    def _():
        _init_state(m_ref, acc_ref)
        for p in range(N_PAIRS):
            qT = qT_ref[0, p]
            for g in range(NSA_GROUPS):
                h = 2 * p + g
                qm = _keep_half(qT, g)
                ali = _alibi_rows(LANES - n_sel, QT, 0, SLOPES[g * NSA_HPG + p], 2 * ii)
                qaug_ref[h] = jnp.concatenate(
                    [qm, sel_ref[0, g], ali.astype(jnp.bfloat16)], axis=0)

    def step(diag):
        kfull = jnp.concatenate([k_ref[0], kaug_ref[0]], axis=1)

        def score(h):
            return _dot(kfull, qaug_ref[h])

        def value(h):
            return _value_rows(vT_ref[0, (h % 2) * HALF:(h % 2 + 1) * HALF])

        keeps = _causal_keeps() if diag else (None, None)
        _attend_heads(score, value, keeps, s_ref, mx_ref, m_ref, acc_ref)
        if diag:
            for p in range(N_PAIRS):
                _finalize_pair(o_ref, acc_ref, p, sz_ref[0, p], _nsa_gates(gate_ref, p, BRANCH_SLC))

    @pl.when(jj < ii)
    def _():
        step(False)

    @pl.when(jj == ii)
    def _():
        step(True)


def _slc(qbT, ksl, vslT, selT, kaug, szT, gateT):
    B, _, _, S = qbT.shape
    R = selT.shape[2]
    it, jt = _pair_tables(S // QT)
    return pl.pallas_call(
        _slc_kernel,
        grid_spec=pltpu.PrefetchScalarGridSpec(
            num_scalar_prefetch=2,
            grid=(B, int(it.shape[0])),
            in_specs=[
                pl.BlockSpec((1, N_PAIRS, LANES, QT), lambda b, s, it, jt: (b, 0, 0, it[s])),
                pl.BlockSpec((1, 2 * TILE, KV_W), lambda b, s, it, jt: (b, jt[s], 0)),
                pl.BlockSpec((1, KV_W, 2 * TILE), lambda b, s, it, jt: (b, 0, jt[s])),
                pl.BlockSpec((1, NSA_GROUPS, R, QT), lambda b, s, it, jt: (b, 0, 0, it[s])),
                pl.BlockSpec((1, 2 * TILE, LANES), lambda b, s, it, jt: (jt[s], 0, 0)),
                pl.BlockSpec((1, N_PAIRS, LANES, QT), lambda b, s, it, jt: (b, 0, 0, it[s])),
                pl.BlockSpec((1, GATE_ROWS, QT), lambda b, s, it, jt: (b, 0, it[s])),
            ],
            out_specs=pl.BlockSpec((1, QT, NSA_W), lambda b, s, it, jt: (b, it[s], 0)),
            scratch_shapes=_state_scratch(2 * TILE, QT),
        ),
        out_shape=jax.ShapeDtypeStruct((B, S, NSA_W), jnp.float32),
        compiler_params=pltpu.CompilerParams(
            dimension_semantics=("arbitrary", "arbitrary"), vmem_limit_bytes=VMEM_LIMIT),
        name="slc",
    )(it, jt, qbT, ksl, vslT, selT, kaug, szT, gateT)


def _win_kernel(qT_ref, k2_ref, k1_ref, k0_ref, v2_ref, v1_ref, v0_ref, kaug_ref, sz_ref, gate_ref,
                o_ref, qaug_ref, m_ref, acc_ref, s_ref, mx_ref):
    i = pl.program_id(1)
    _init_state(m_ref, acc_ref)
    r = lax.broadcasted_iota(jnp.int32, (LANES, TILE), 0)
    trel = lax.broadcasted_iota(jnp.int32, (LANES, TILE), 1).astype(jnp.float32)
    absent2 = jnp.where(i >= 2, 0.0, NEG_INF)
    absent1 = jnp.where(i >= 1, 0.0, NEG_INF)
    for p in range(N_PAIRS):
        qT = qT_ref[0, p]
        for g in range(NSA_GROUPS):
            h = 2 * p + g
            slope = SLOPES[g * NSA_HPG + p]
            aug = jnp.where(r == 0, -slope * trel, 0.0)
            aug = jnp.where(r == 1, slope, aug)
            aug = jnp.where(r == 2, -slope, aug)
            aug = jnp.where(r == 3, absent1, aug)
            aug = jnp.where(r == 4, absent2, aug)
            qaug_ref[h] = jnp.concatenate([_keep_half(qT, g), aug.astype(jnp.bfloat16)], axis=0)

    kfull = jnp.concatenate(
        [jnp.concatenate([k2_ref[0], k1_ref[0], k0_ref[0]], axis=0), kaug_ref[...]], axis=1)
    vT = jnp.concatenate([v2_ref[0], v1_ref[0], v0_ref[0]], axis=1)
    k, t = _tile_iotas()

    def score(h):
        return _dot(kfull, qaug_ref[h])

    def value(h):
        return _value_rows(vT[(h % 2) * HALF:(h % 2 + 1) * HALF])

    _attend_heads(score, value, (k > t, None, k <= t), s_ref, mx_ref, m_ref, acc_ref)
    for p in range(N_PAIRS):
        _finalize_pair(o_ref, acc_ref, p, sz_ref[0, p], _nsa_gates(gate_ref, p, BRANCH_WIN))


def _win(qbT, kwi, vwiT, kaug, szT, gateT):
    B, _, _, S = qbT.shape
    nq = S // TILE
    kspec = lambda d: pl.BlockSpec((1, TILE, KV_W), lambda b, i: (b, jnp.maximum(i - d, 0), 0))
    vspec = lambda d: pl.BlockSpec((1, KV_W, TILE), lambda b, i: (b, 0, jnp.maximum(i - d, 0)))
    return pl.pallas_call(
        _win_kernel,
        grid=(B, nq),
        in_specs=[
            pl.BlockSpec((1, N_PAIRS, LANES, TILE), lambda b, i: (b, 0, 0, i)),
            kspec(2), kspec(1), kspec(0), vspec(2), vspec(1), vspec(0),
            pl.BlockSpec(kaug.shape, lambda b, i: (0, 0)),
            pl.BlockSpec((1, N_PAIRS, LANES, TILE), lambda b, i: (b, 0, 0, i)),
            pl.BlockSpec((1, GATE_ROWS, TILE), lambda b, i: (b, 0, i)),
        ],
        out_specs=pl.BlockSpec((1, TILE, NSA_W), lambda b, i: (b, i, 0)),
        out_shape=jax.ShapeDtypeStruct((B, S, NSA_W), jnp.float32),
        scratch_shapes=_state_scratch(kaug.shape[0]),
        compiler_params=pltpu.CompilerParams(
            dimension_semantics=("arbitrary", "arbitrary"), vmem_limit_bytes=VMEM_LIMIT),
        name="win",
    )(qbT, kwi, kwi, kwi, vwiT, vwiT, vwiT, kaug, szT, gateT)


def _out_kernel(x_ref, oa_ref, oc_ref, os_ref, ow_ref, w_ref, g_ref, y_ref):
    ob = oc_ref[0] + os_ref[0] + ow_ref[0]
    mix = jnp.concatenate([oa_ref[0], ob], axis=1).astype(jnp.bfloat16)
    y = _dot(mix, w_ref[...])
    r = lax.rsqrt(jnp.mean(y * y, axis=-1, keepdims=True) + RMS_EPS)
    y_ref[0] = x_ref[0] + y * r * g_ref[...]


def _out(x, oa, oc, os_, ow, w, g):
    B, S, _ = x.shape
    rows = lambda wd: pl.BlockSpec((1, TILE, wd), lambda b, i: (b, i, 0))
    return pl.pallas_call(
        _out_kernel,
        grid=(B, S // TILE),
        in_specs=[rows(D_MODEL), rows(MOBA_W), rows(NSA_W), rows(NSA_W), rows(NSA_W),
                  pl.BlockSpec(w.shape, lambda b, i: (0, 0)),
                  pl.BlockSpec((1, D_MODEL), lambda b, i: (0, 0))],
        out_specs=rows(D_MODEL),
        out_shape=jax.ShapeDtypeStruct((B, S, D_MODEL), jnp.float32),
        compiler_params=pltpu.CompilerParams(
            dimension_semantics=("arbitrary", "arbitrary"), vmem_limit_bytes=VMEM_LIMIT),
        name="out",
    )(x, oa, oc, os_, ow, w, g)


def _compress_weights(w1, w2):
    half = (CMP_LEN // 2) * HEAD_DIM
    w1r = w1.reshape(2, CMP_LEN // 2, HEAD_DIM, CMP_HIDDEN)
    z = jnp.zeros_like(w1r)
    g0 = jnp.concatenate([w1r, z], axis=-1)
    g1 = jnp.concatenate([z, w1r], axis=-1)
    both = jnp.stack([g0, g1], axis=2)
    both = both.reshape(2, half * NSA_GROUPS, NSA_GROUPS * CMP_HIDDEN).astype(jnp.bfloat16)
    zz = jnp.zeros_like(w2)
    w2bd = jnp.concatenate([jnp.concatenate([w2, zz], axis=1),
                            jnp.concatenate([zz, w2], axis=1)], axis=0).astype(jnp.bfloat16)
    return both[0], both[1], w2bd


def _overlap_T(n_cmp_pad, n_slc, rows):
    c = np.arange(n_cmp_pad)[None, :] * CMP_STRIDE
    j = np.arange(rows)[:, None] * SLC_BLOCK
    ov = (c < j + SLC_BLOCK) & (c + CMP_LEN > j) & (np.arange(rows)[:, None] < n_slc)
    return jnp.asarray(ov.astype(np.float32), jnp.bfloat16)


def _layer(x, pre_g, post_g, w_in, pos_k, pos_v, w_k1, w_k2, w_v1, w_v2, w_out):
    B, S, _ = x.shape
    nq = S // TILE
    n_cmp = (S - CMP_LEN) // CMP_STRIDE + 1
    n_slc = S // SLC_BLOCK
    C = S // CMP_STRIDE
    nat_cols, tr_cols, out_rows = _column_maps()

    wn = jnp.take(w_in, nat_cols, axis=1).astype(jnp.bfloat16)
    wt = jnp.take(w_in, tr_cols, axis=1).T
    wt = jnp.concatenate([wt, jnp.zeros((N_TR - wt.shape[0], D_MODEL), wt.dtype)], axis=0)
    (ka, kmean, kcm, vcm, ksl, kwi, qaT, vaT, qbT, vslT, vwiT,
     szaT, szbT, gateT) = _proj(x, pre_g.reshape(1, D_MODEL), wn, wt.astype(jnp.bfloat16))

    wkt, wkb, w2k = _compress_weights(w_k1, w_k2)
    wvt, wvb, w2v = _compress_weights(w_v1, w_v2)
    chunk = CMP_STRIDE * KV_W
    kc, vcT = _compress(
        kcm.reshape(B, C, chunk), vcm.reshape(B, C, chunk), wkt, wkb, wvt, wvb, w2k, w2v,
        pos_k.reshape(1, CMP_LEN * HEAD_DIM).astype(jnp.bfloat16),
        pos_v.reshape(1, CMP_LEN * HEAD_DIM).astype(jnp.bfloat16),
        w_k1.astype(jnp.bfloat16), w_v1.astype(jnp.bfloat16), n_cmp)

    oc, selT = _cmp(qbT, kc, vcT, _overlap_T(C, n_slc, SLC_BLOCK), _cmp_query_aug_table(),
                    szbT, gateT, n_slc)
    oa = _moba(qaT, ka, vaT, kmean.reshape(B, nq, MOBA_W), _key_aug_table(nq, 16, 1), szaT)
    osl = _slc(qbT, ksl, vslT, selT, _key_aug_table(nq, selT.shape[2], TILE // SLC_BLOCK),
               szbT, gateT)
    ow = _win(qbT, kwi, vwiT, _win_key_aug_table(), szbT, gateT)

    w_o = jnp.take(w_out, out_rows, axis=0).astype(jnp.bfloat16)
    return _out(x, oa, oc, osl, ow, w_o, post_g.reshape(1, D_MODEL))


def kernel(x, pre_norm_g, post_norm_g, w_in, cmp_pos_k, cmp_pos_v,
           w_cmp_k1, w_cmp_k2, w_cmp_v1, w_cmp_v2, w_out):
    for l in range(pre_norm_g.shape[0]):
        x = _layer(x, pre_norm_g[l], post_norm_g[l], w_in[l], cmp_pos_k[l], cmp_pos_v[l],
                   w_cmp_k1[l], w_cmp_k2[l], w_cmp_v1[l], w_cmp_v2[l], w_out[l])
    return x
```

```python
import functools

import numpy as np
import jax
import jax.numpy as jnp
from jax import lax
from jax.experimental import pallas as pl
from jax.experimental.pallas import tpu as pltpu

D_MODEL = 1024
HEAD_DIM = 64
N_HEADS = 8
N_PAIRS = N_HEADS // 2
NSA_GROUPS = 2
NSA_HPG = N_HEADS // NSA_GROUPS
MOBA_BLOCK = 256
MOBA_TOPK = 3
CMP_LEN = 32
CMP_STRIDE = 16
CMP_HIDDEN = 128
SLC_BLOCK = 64
SLC_TOPN = 16
WINDOW = 512
RMS_EPS = 1e-6
NEG_INF = -1e30
FORCED_SCORE = 1e9
SCALE = HEAD_DIM ** -0.5
LOG2E = float(np.log2(np.e))

TILE = 256
QT = 2 * TILE
ROWS = 2 * TILE
LANES = 128
HALF = HEAD_DIM
SUBLANES = 8
MOBA_W = N_HEADS * HEAD_DIM
NSA_W = N_HEADS * HEAD_DIM
KV_W = NSA_GROUPS * HEAD_DIM
N_GATES = 3 * N_HEADS
GATE_ROWS = 32
SLOPES = tuple(2.0 ** (-(i + 1)) for i in range(N_HEADS))
BRANCH_CMP, BRANCH_SLC, BRANCH_WIN = 0, 1, 2

VMEM_LIMIT = 48 * 1024 * 1024

_OFF = dict(qa=0, ka=512, va=1024, za=1536, qb=2048, kcm=2560, vcm=2688,
            ksl=2816, vsl=2944, kwi=3072, vwi=3200, gate=3328, zb=3352)
_NAT = dict(ka=(0, 512), kcm=(512, 640), vcm=(640, 768), ksl=(768, 896), kwi=(896, 1024))
N_NAT = 1024
_TR = dict(qa=(0, 512), va=(512, 1024), qb=(1024, 1536), vsl=(1536, 1664),
           vwi=(1664, 1792), za=(1792, 2304), zb=(2304, 2816), gate=(2816, 2848))
N_TR = 2848


def _pair_perm():
    idx = np.zeros(NSA_W, np.int32)
    for p in range(N_PAIRS):
        for g in range(NSA_GROUPS):
            for d in range(HEAD_DIM):
                idx[p * LANES + g * HALF + d] = (g * NSA_HPG + p) * HEAD_DIM + d
    return idx


def _column_maps():
    pp = _pair_perm()
    nat = np.zeros(N_NAT, np.int32)
    nat[0:512] = _OFF["ka"] + np.arange(512)
    nat[512:640] = _OFF["kcm"] + np.arange(128)
    nat[640:768] = _OFF["vcm"] + np.arange(128)
    nat[768:896] = _OFF["ksl"] + np.arange(128)
    nat[896:1024] = _OFF["kwi"] + np.arange(128)
    tr = np.zeros(N_TR - (GATE_ROWS - N_GATES), np.int32)
    tr[0:512] = _OFF["qa"] + np.arange(512)
    tr[512:1024] = _OFF["va"] + np.arange(512)
    tr[1024:1536] = _OFF["qb"] + pp
    tr[1536:1664] = _OFF["vsl"] + np.arange(128)
    tr[1664:1792] = _OFF["vwi"] + np.arange(128)
    tr[1792:2304] = _OFF["za"] + np.arange(512)
    tr[2304:2816] = _OFF["zb"] + pp
    tr[2816:2816 + N_GATES] = _OFF["gate"] + np.arange(N_GATES)
    out_rows = np.concatenate([np.arange(512), 512 + pp]).astype(np.int32)
    return nat, tr, out_rows


def _gate_row(g, p, branch):
    return (g * NSA_HPG + p) * 3 + branch


def _sigmoid(x):
    return 1.0 / (1.0 + jnp.exp(-x))


def _silu(x):
    return x * _sigmoid(x)


def _dot(a, b):
    return jnp.dot(a, b, preferred_element_type=jnp.float32)


def _proj_kernel(x_ref, g_ref, wn_ref, wt_ref,
                 ka_ref, kmean_ref, kcm_ref, vcm_ref, ksl_ref, kwi_ref,
                 qaT_ref, vaT_ref, qbT_ref, vslT_ref, vwiT_ref,
                 szaT_ref, szbT_ref, gateT_ref):
    x = x_ref[0]
    r = lax.rsqrt(jnp.mean(x * x, axis=-1, keepdims=True) + RMS_EPS)
    h = (x * r * g_ref[...]).astype(jnp.bfloat16)

    natural = _dot(h, wn_ref[...])

    def nat(name):
        a, b = _NAT[name]
        return natural[:, a:b]

    ka = nat("ka")
    for blk in range(x.shape[0] // MOBA_BLOCK):
        kmean_ref[0, blk] = jnp.mean(ka[blk * MOBA_BLOCK:(blk + 1) * MOBA_BLOCK],
                                     axis=0, keepdims=True)
    for p in range(N_PAIRS):
        ka_ref[0, p] = ka[:, p * LANES:(p + 1) * LANES].astype(jnp.bfloat16)
    kcm_ref[0] = nat("kcm").astype(jnp.bfloat16)
    vcm_ref[0] = nat("vcm").astype(jnp.bfloat16)
    ksl_ref[0] = nat("ksl").astype(jnp.bfloat16)
    kwi_ref[0] = nat("kwi").astype(jnp.bfloat16)

    def tr(name):
        a, b = _TR[name]
        return lax.dot_general(wt_ref[a:b, :], h, (((1,), (1,)), ((), ())),
                               preferred_element_type=jnp.float32)

    qa = tr("qa") * (SCALE * LOG2E)
    va = tr("va")
    qb = tr("qb") * (SCALE * LOG2E)
    sza = _silu(tr("za"))
    szb = _silu(tr("zb"))
    for p in range(N_PAIRS):
        sl = slice(p * LANES, (p + 1) * LANES)
        qaT_ref[0, p] = qa[sl].astype(jnp.bfloat16)
        vaT_ref[0, p] = va[sl].astype(jnp.bfloat16)
        qbT_ref[0, p] = qb[sl].astype(jnp.bfloat16)
        szaT_ref[0, p] = sza[sl]
        szbT_ref[0, p] = szb[sl]
    vslT_ref[0] = tr("vsl").astype(jnp.bfloat16)
    vwiT_ref[0] = tr("vwi").astype(jnp.bfloat16)
    gateT_ref[0] = _sigmoid(tr("gate"))


def _proj(x, g, wn, wt):
    B, S, _ = x.shape
    nq = S // TILE
    bf, f32 = jnp.bfloat16, jnp.float32
    slab_nat = lambda: pl.BlockSpec((1, N_PAIRS, ROWS, LANES), lambda b, i: (b, 0, i, 0))
    slab_tr = lambda: pl.BlockSpec((1, N_PAIRS, LANES, ROWS), lambda b, i: (b, 0, 0, i))
    rows = lambda w: pl.BlockSpec((1, ROWS, w), lambda b, i: (b, i, 0))
    chans = lambda c: pl.BlockSpec((1, c, ROWS), lambda b, i: (b, 0, i))
    out_shape = [
        jax.ShapeDtypeStruct((B, N_PAIRS, S, LANES), bf),
        jax.ShapeDtypeStruct((B, nq, 1, MOBA_W), f32),
        jax.ShapeDtypeStruct((B, S, KV_W), bf),
        jax.ShapeDtypeStruct((B, S, KV_W), bf),
        jax.ShapeDtypeStruct((B, S, KV_W), bf),
        jax.ShapeDtypeStruct((B, S, KV_W), bf),
        jax.ShapeDtypeStruct((B, N_PAIRS, LANES, S), bf),
        jax.ShapeDtypeStruct((B, N_PAIRS, LANES, S), bf),
        jax.ShapeDtypeStruct((B, N_PAIRS, LANES, S), bf),
        jax.ShapeDtypeStruct((B, KV_W, S), bf),
        jax.ShapeDtypeStruct((B, KV_W, S), bf),
        jax.ShapeDtypeStruct((B, N_PAIRS, LANES, S), f32),
        jax.ShapeDtypeStruct((B, N_PAIRS, LANES, S), f32),
        jax.ShapeDtypeStruct((B, GATE_ROWS, S), f32),
    ]
    out_specs = [
        slab_nat(),
        pl.BlockSpec((1, ROWS // MOBA_BLOCK, 1, MOBA_W), lambda b, i: (b, i, 0, 0)),
        rows(KV_W), rows(KV_W), rows(KV_W), rows(KV_W),
        slab_tr(), slab_tr(), slab_tr(),
        chans(KV_W), chans(KV_W),
        slab_tr(), slab_tr(),
        chans(GATE_ROWS),
    ]
    return pl.pallas_call(
        _proj_kernel,
        grid=(B, S // ROWS),
        in_specs=[
            pl.BlockSpec((1, ROWS, D_MODEL), lambda b, i: (b, i, 0)),
            pl.BlockSpec((1, D_MODEL), lambda b, i: (0, 0)),
            pl.BlockSpec((D_MODEL, N_NAT), lambda b, i: (0, 0)),
            pl.BlockSpec((N_TR, D_MODEL), lambda b, i: (0, 0)),
        ],
        out_specs=out_specs,
        out_shape=out_shape,
        compiler_params=pltpu.CompilerParams(
            dimension_semantics=("arbitrary", "arbitrary"),
            vmem_limit_bytes=VMEM_LIMIT),
        name="proj",
    )(x, g, wn, wt)


def _compress_kernel(n_cmp, xk_ref, xv_ref, wkt_ref, wkb_ref, wvt_ref, wvb_ref,
                     w2k_ref, w2v_ref, pek_ref, pev_ref, w1k_ref, w1v_ref,
                     kc_ref, vcT_ref):
    C = xk_ref.shape[1]
    row = lax.broadcasted_iota(jnp.int32, (C, KV_W), 0)

    def phi(x_ref, wt_ref, wb_ref, w2_ref, pe_ref, w1_ref):
        x = x_ref[0]
        top = _dot(x, wt_ref[...])
        bot = _dot(x, wb_ref[...])
        peb = _dot(jnp.broadcast_to(pe_ref[...], (8, pe_ref.shape[1])), w1_ref[...])[0:1]
        peb = jnp.concatenate([peb, peb], axis=1)
        hid = top + pltpu.roll(bot, C - 1, 0) + peb
        out = _dot(_silu(hid).astype(jnp.bfloat16), w2_ref[...])
        return jnp.where(row < n_cmp, out, 0.0)

    kc_ref[0] = phi(xk_ref, wkt_ref, wkb_ref, w2k_ref, pek_ref, w1k_ref).astype(jnp.bfloat16)
    vc = phi(xv_ref, wvt_ref, wvb_ref, w2v_ref, pev_ref, w1v_ref)
    vcT_ref[0] = vc.T.astype(jnp.bfloat16)


def _compress(xk, xv, wkt, wkb, wvt, wvb, w2k, w2v, pek, pev, w1k, w1v, n_cmp):
    B, C, F = xk.shape
    full = lambda a: pl.BlockSpec(a.shape, lambda b: (0,) * a.ndim)
    return pl.pallas_call(
        functools.partial(_compress_kernel, n_cmp),
        grid=(B,),
        in_specs=[pl.BlockSpec((1, C, F), lambda b: (b, 0, 0)),
                  pl.BlockSpec((1, C, F), lambda b: (b, 0, 0)),
                  full(wkt), full(wkb), full(wvt), full(wvb), full(w2k), full(w2v),
                  full(pek), full(pev), full(w1k), full(w1v)],
        out_specs=[pl.BlockSpec((1, C, KV_W), lambda b: (b, 0, 0)),
                   pl.BlockSpec((1, KV_W, C), lambda b: (b, 0, 0))],
        out_shape=[jax.ShapeDtypeStruct((B, C, KV_W), jnp.bfloat16),
                   jax.ShapeDtypeStruct((B, KV_W, C), jnp.bfloat16)],
        compiler_params=pltpu.CompilerParams(
            dimension_semantics=("arbitrary",), vmem_limit_bytes=VMEM_LIMIT),
        name="compress",
    )(xk, xv, wkt, wkb, wvt, wvb, w2k, w2v, pek, pev, w1k, w1v)


def _keep_half(qT, half):
    z = jnp.zeros((HALF, qT.shape[1]), qT.dtype)
    if half == 0:
        return jnp.concatenate([qT[0:HALF], z], axis=0)
    return jnp.concatenate([z, qT[HALF:2 * HALF]], axis=0)


def _rank_rows(vals, n_rows):
    R, T = vals.shape
    tiles = [vals[a:a + SUBLANES] for a in range(0, R, SUBLANES)]
    ranks = [jnp.zeros((SUBLANES, T), jnp.int32) for _ in tiles]
    j_in = lax.broadcasted_iota(jnp.int32, (SUBLANES, T), 0)
    for m in range(n_rows):
        vm = vals[m:m + 1, :]
        for a, tile in enumerate(tiles):
            lo = a * SUBLANES
            if lo > m:
                beats = vm >= tile
            elif lo + SUBLANES - 1 <= m:
                beats = vm > tile
            else:
                beats = (vm > tile) | ((vm == tile) & (j_in > m - lo))
            ranks[a] = ranks[a] + beats.astype(jnp.int32)
    return jnp.concatenate(ranks, axis=0)


def _query_tile(cols, tile0):
    return tile0 + lax.shift_right_logical(cols, int(np.log2(TILE)))


def _split_bf16(x):
    hi = x.astype(jnp.bfloat16).astype(jnp.float32)
    return hi, x - hi


def _slope_parts(slope):
    c = np.float32(slope * LOG2E)
    hi = np.asarray(c, jnp.bfloat16).astype(np.float32)
    lo = np.asarray(c - hi, jnp.bfloat16).astype(np.float32)
    return float(hi), float(lo)


def _alibi_rows(n_rows, T, first, slope, tile0):
    r = lax.broadcasted_iota(jnp.int32, (n_rows, T), 0)
    c = lax.broadcasted_iota(jnp.int32, (n_rows, T), 1)
    t_abs = ((c & (TILE - 1)) + _query_tile(c, tile0) * TILE).astype(jnp.float32)
    c_hi, c_lo = _slope_parts(slope)
    w_hi, w_lo = _split_bf16(np.float32(slope * LOG2E) * t_abs)
    out = jnp.where(r == first, -w_hi, 0.0)
    out = jnp.where(r == first + 1, -w_lo, out)
    out = jnp.where((r == first + 2) | (r == first + 4), c_hi, out)
    out = jnp.where((r == first + 3) | (r == first + 5), c_lo, out)
    return out


def _key_aug_table(nq, n_sel, blocks_per_tile):
    t = np.zeros((nq, TILE, LANES), np.float32)
    krel = np.arange(TILE)
    for j in range(nq):
        if n_sel:
            blk = j * blocks_per_tile + krel // (TILE // blocks_per_tile)
            t[j, krel, blk] = 1.0
        t[j, :, n_sel:n_sel + 2] = 1.0
        t[j, :, n_sel + 2:n_sel + 4] = krel[:, None]
        t[j, :, n_sel + 4:n_sel + 6] = TILE * j
    return jnp.asarray(t.reshape(nq // 2, 2 * TILE, LANES), jnp.bfloat16)


ONES_ROWS = 16
ACC_ROWS = HALF + ONES_ROWS


def _win_key_aug_table():
    n_tiles = WINDOW // TILE + 1
    t = np.zeros((n_tiles * TILE, LANES), np.float32)
    for slot in range(n_tiles):
        delta = n_tiles - 1 - slot
        rows = slice(slot * TILE, (slot + 1) * TILE)
        t[rows, 0:2] = 1.0
        t[rows, 2:4] = np.arange(TILE)[:, None]
        t[rows, 4:6] = TILE * delta
        if delta > 0:
            t[rows, 5 + delta] = 1.0
    return jnp.asarray(t, jnp.bfloat16)


def _value_rows(vT_h):
    return jnp.concatenate([vT_h, jnp.ones((ONES_ROWS, vT_h.shape[1]), vT_h.dtype)], axis=0)


def _attend_heads(score_fn, value_fn, keeps, s_ref, mx_ref, m_ref, acc_ref):
    for h in range(N_HEADS):
        s = score_fn(h)
        if any(k is not None for k in keeps):
            s = jnp.concatenate(
                [s[r * TILE:(r + 1) * TILE] if k is None
                 else jnp.where(k, s[r * TILE:(r + 1) * TILE], NEG_INF)
                 for r, k in enumerate(keeps)], axis=0)
        s_ref[h] = s
        mx_ref[h] = jnp.max(s, axis=0, keepdims=True)
    for h in range(N_HEADS):
        m_prev = m_ref[h]
        m_new = jnp.maximum(m_prev, mx_ref[h])
        alpha = jnp.exp2(m_prev - m_new)
        p = jnp.exp2(s_ref[h] - m_new).astype(jnp.bfloat16)
        acc_ref[h] = alpha * acc_ref[h] + _dot(value_fn(h), p)
        m_ref[h] = m_new


def _init_state(m_ref, acc_ref):
    m_ref[...] = jnp.full(m_ref.shape, NEG_INF, jnp.float32)
    acc_ref[...] = jnp.zeros(acc_ref.shape, jnp.float32)


def _tile_iotas(T=TILE):
    k = lax.broadcasted_iota(jnp.int32, (TILE, T), 0)
    t = lax.broadcasted_iota(jnp.int32, (TILE, T), 1)
    return k, t


def _causal_keeps():
    k, t = _tile_iotas(QT)
    return (k <= t, k + TILE <= t)


def _store_pair(o_ref, p, o0T, o1T, szT, gates):
    if gates is not None:
        o0T = o0T * gates[0]
        o1T = o1T * gates[1]
    oT = jnp.concatenate([o0T, o1T], axis=0) * szT
    o_ref[0, :, p * LANES:(p + 1) * LANES] = oT.T


def _finalize_pair(o_ref, acc_ref, p, szT, gates=None):
    a0 = acc_ref[2 * p]
    a1 = acc_ref[2 * p + 1]
    _store_pair(o_ref, p, a0[0:HALF] / a0[HALF:HALF + 1], a1[0:HALF] / a1[HALF:HALF + 1],
                szT, gates)


def _nsa_gates(gate_ref, p, branch):
    return [gate_ref[0, _gate_row(g, p, branch):_gate_row(g, p, branch) + 1, :]
            for g in range(NSA_GROUPS)]


def _pair_tables(n_pairs):
    it, jt = [], []
    for ii in range(n_pairs):
        for jj in range(ii + 1):
            it.append(ii)
            jt.append(jj)
    return jnp.asarray(it, jnp.int32), jnp.asarray(jt, jnp.int32)


def _state_scratch(n_keys, T=TILE):
    return [
        pltpu.VMEM((N_HEADS, 2 * LANES, T), jnp.bfloat16),
        pltpu.VMEM((N_HEADS, 1, T), jnp.float32),
        pltpu.VMEM((N_HEADS, ACC_ROWS, T), jnp.float32),
        pltpu.VMEM((N_HEADS, n_keys, T), jnp.float32),
        pltpu.VMEM((N_HEADS, 1, T), jnp.float32),
    ]


def _moba_kernel(nq, it_ref, jt_ref, qT_ref, k_ref, vT_ref, kmean_ref, kaug_ref, sz_ref, o_ref,
                 qaug_ref, m_ref, acc_ref, s_ref, mx_ref):
    s = pl.program_id(1)
    ii = it_ref[s]
    jj = jt_ref[s]
    n_sel = 16

    @pl.when(jj == 0)
    def _():
        _init_state(m_ref, acc_ref)
        n_iota = lax.broadcasted_iota(jnp.int32, (nq, QT), 0)
        own = _query_tile(lax.broadcasted_iota(jnp.int32, (nq, QT), 1), 2 * ii)
        for p in range(N_PAIRS):
            qT = qT_ref[0, p]
            km = kmean_ref[0, :, p * LANES:(p + 1) * LANES].astype(jnp.bfloat16)
            for hh in range(2):
                h = 2 * p + hh
                qm = _keep_half(qT, hh)
                gate = jnp.where(n_iota < own, _dot(km, qm), NEG_INF)
                rank = _rank_rows(gate, nq)
                sel = ((rank < MOBA_TOPK) & (n_iota < own)) | (n_iota == own)
                selneg = jnp.where(sel, 0.0, NEG_INF)
                if nq < n_sel:
                    selneg = jnp.concatenate(
                        [selneg, jnp.zeros((n_sel - nq, QT), jnp.float32)], axis=0)
                ali = _alibi_rows(LANES - n_sel, QT, 0, SLOPES[h], 2 * ii)
                aug = jnp.concatenate([selneg, ali], axis=0).astype(jnp.bfloat16)
                qaug_ref[h] = jnp.concatenate([qm, aug], axis=0)

    def step(diag):
        kaug = kaug_ref[0]

        def score(h):
            return _dot(jnp.concatenate([k_ref[0, h // 2], kaug], axis=1), qaug_ref[h])

        def value(h):
            return _value_rows(vT_ref[0, h // 2, (h % 2) * HALF:(h % 2 + 1) * HALF])

        keeps = _causal_keeps() if diag else (None, None)
        _attend_heads(score, value, keeps, s_ref, mx_ref, m_ref, acc_ref)
        if diag:
            for p in range(N_PAIRS):
                _finalize_pair(o_ref, acc_ref, p, sz_ref[0, p])

    @pl.when(jj < ii)
    def _():
        step(False)

    @pl.when(jj == ii)
    def _():
        step(True)


def _moba(qT, k, vT, kmean, kaug, szT):
    B, _, _, S = qT.shape
    nq = S // TILE
    it, jt = _pair_tables(S // QT)
    return pl.pallas_call(
        functools.partial(_moba_kernel, nq),
        grid_spec=pltpu.PrefetchScalarGridSpec(
            num_scalar_prefetch=2,
            grid=(B, int(it.shape[0])),
            in_specs=[
                pl.BlockSpec((1, N_PAIRS, LANES, QT), lambda b, s, it, jt: (b, 0, 0, it[s])),
                pl.BlockSpec((1, N_PAIRS, 2 * TILE, LANES), lambda b, s, it, jt: (b, 0, jt[s], 0)),
                pl.BlockSpec((1, N_PAIRS, LANES, 2 * TILE), lambda b, s, it, jt: (b, 0, 0, jt[s])),
                pl.BlockSpec((1, nq, MOBA_W), lambda b, s, it, jt: (b, 0, 0)),
                pl.BlockSpec((1, 2 * TILE, LANES), lambda b, s, it, jt: (jt[s], 0, 0)),
                pl.BlockSpec((1, N_PAIRS, LANES, QT), lambda b, s, it, jt: (b, 0, 0, it[s])),
            ],
            out_specs=pl.BlockSpec((1, QT, MOBA_W), lambda b, s, it, jt: (b, it[s], 0)),
            scratch_shapes=_state_scratch(2 * TILE, QT),
        ),
        out_shape=jax.ShapeDtypeStruct((B, S, MOBA_W), jnp.float32),
        compiler_params=pltpu.CompilerParams(
            dimension_semantics=("arbitrary", "arbitrary"), vmem_limit_bytes=VMEM_LIMIT),
        name="moba",
    )(it, jt, qT, k, vT, kmean, kaug, szT)


CMP_BAND = TILE // CMP_STRIDE + 1


def _cmp_query_aug_table():
    t = np.zeros((LANES, TILE), np.float32)
    trel = np.arange(TILE)
    for u in range(CMP_BAND):
        t[u, trel < CMP_STRIDE * (u - 1) + CMP_LEN - 1] = NEG_INF
    t[CMP_BAND, :] = NEG_INF
    return jnp.asarray(t, jnp.bfloat16)


def _cmp_kernel(n_slc, qT_ref, kc_ref, vcT_ref, ovT_ref, qaug_ref, sz_ref, gate_ref,
                o_ref, sel_ref, s_ref, mx_ref):
    i = pl.program_id(1)
    C = kc_ref.shape[1]
    c_rel = (lax.broadcasted_iota(jnp.int32, (C, LANES), 0)
             - (TILE // CMP_STRIDE) * i + 1)
    u = lax.broadcasted_iota(jnp.int32, (C, LANES), 1)
    kaug = jnp.where((c_rel == u) & (u < CMP_BAND), 1.0, 0.0)
    kaug = jnp.where((u == CMP_BAND) & (c_rel >= CMP_BAND), 1.0, kaug)
    kfull = jnp.concatenate([kc_ref[0], kaug.astype(jnp.bfloat16)], axis=1)
    qaug = qaug_ref[...]

    for p in range(N_PAIRS):
        qT = qT_ref[0, p]
        for g in range(NSA_GROUPS):
            h = 2 * p + g
            s = _dot(kfull, jnp.concatenate([_keep_half(qT, g), qaug], axis=0))
            s_ref[h] = s
            mx_ref[h] = jnp.max(s, axis=0, keepdims=True)

    t_abs = i * TILE + lax.broadcasted_iota(jnp.int32, (1, TILE), 1)
    seen = t_abs >= CMP_LEN - 1
    ones = jnp.ones((ONES_ROWS, C), jnp.bfloat16)
    vals = [jnp.concatenate([vcT_ref[0, g * HALF:(g + 1) * HALF], ones, ovT_ref[...]], axis=0)
            for g in range(NSA_GROUPS)]
    R = sel_ref.shape[2]
    imp = [jnp.zeros((R, TILE), jnp.float32) for _ in range(NSA_GROUPS)]
    for p in range(N_PAIRS):
        outs = []
        for g in range(NSA_GROUPS):
            h = 2 * p + g
            pr = jnp.exp2(s_ref[h] - mx_ref[h]).astype(jnp.bfloat16)
            acc = _dot(vals[g], pr)
            inv = jnp.where(seen, 1.0 / acc[HALF:HALF + 1], 0.0)
            outs.append(acc[0:HALF] * inv)
            imp[g] = imp[g] + acc[ACC_ROWS:ACC_ROWS + R] * inv
        _store_pair(o_ref, p, outs[0], outs[1], sz_ref[0, p], _nsa_gates(gate_ref, p, BRANCH_CMP))

    j_iota = lax.broadcasted_iota(jnp.int32, (R, TILE), 0)
    t_q = i * TILE + lax.broadcasted_iota(jnp.int32, (R, TILE), 1)
    own = lax.shift_right_logical(t_q, int(np.log2(SLC_BLOCK)))
    forced = (j_iota == 0) | (j_iota == own) | (j_iota == own - 1)
    causal = j_iota <= own
    for g in range(NSA_GROUPS):
        v = jnp.where(forced, FORCED_SCORE, jnp.where(causal, imp[g], NEG_INF))
        rank = _rank_rows(v, n_slc)
        sel = (rank < min(SLC_TOPN, n_slc)) & causal
        sel_ref[0, g] = jnp.where(sel, 0.0, NEG_INF).astype(jnp.bfloat16)


def _cmp(qbT, kc, vcT, ovT, qaug, szT, gateT, n_slc):
    B, _, _, S = qbT.shape
    C = kc.shape[1]
    nq = S // TILE
    R = ovT.shape[0]
    return pl.pallas_call(
        functools.partial(_cmp_kernel, n_slc),
        grid=(B, nq),
        in_specs=[
            pl.BlockSpec((1, N_PAIRS, LANES, TILE), lambda b, i: (b, 0, 0, i)),
            pl.BlockSpec((1, C, KV_W), lambda b, i: (b, 0, 0)),
            pl.BlockSpec((1, KV_W, C), lambda b, i: (b, 0, 0)),
            pl.BlockSpec(ovT.shape, lambda b, i: (0, 0)),
            pl.BlockSpec(qaug.shape, lambda b, i: (0, 0)),
            pl.BlockSpec((1, N_PAIRS, LANES, TILE), lambda b, i: (b, 0, 0, i)),
            pl.BlockSpec((1, GATE_ROWS, TILE), lambda b, i: (b, 0, i)),
        ],
        out_specs=[pl.BlockSpec((1, TILE, NSA_W), lambda b, i: (b, i, 0)),
                   pl.BlockSpec((1, NSA_GROUPS, R, TILE), lambda b, i: (b, 0, 0, i))],
        out_shape=[jax.ShapeDtypeStruct((B, S, NSA_W), jnp.float32),
                   jax.ShapeDtypeStruct((B, NSA_GROUPS, R, S), jnp.bfloat16)],
        scratch_shapes=[pltpu.VMEM((N_HEADS, C, TILE), jnp.float32),
                        pltpu.VMEM((N_HEADS, 1, TILE), jnp.float32)],
        compiler_params=pltpu.CompilerParams(
            dimension_semantics=("arbitrary", "arbitrary"), vmem_limit_bytes=VMEM_LIMIT),
        name="cmp",
    )(qbT, kc, vcT, ovT, qaug, szT, gateT)


def _slc_kernel(it_ref, jt_ref, qT_ref, k_ref, vT_ref, sel_ref, kaug_ref, sz_ref, gate_ref, o_ref,
                qaug_ref, m_ref, acc_ref, s_ref, mx_ref):
    s = pl.program_id(1)
    ii = it_ref[s]
    jj = jt_ref[s]
    n_sel = sel_ref.shape[2]

    @pl.when(jj == 0)
    def _():
        _init_state(m_ref, acc_ref)
        for p in range(N_PAIRS):
            qT = qT_ref[0, p]
            for g in range(NSA_GROUPS):
                h = 2 * p + g
                qm = _keep_half(qT, g)
                ali = _alibi_rows(LANES - n_sel, QT, 0, SLOPES[g * NSA_HPG + p], 2 * ii)
                qaug_ref[h] = jnp.concatenate(
                    [qm, sel_ref[0, g], ali.astype(jnp.bfloat16)], axis=0)

    def step(diag):
        kfull = jnp.concatenate([k_ref[0], kaug_ref[0]], axis=1)

        def score(h):
            return _dot(kfull, qaug_ref[h])

        def value(h):
            return _value_rows(vT_ref[0, (h % 2) * HALF:(h % 2 + 1) * HALF])

        keeps = _causal_keeps() if diag else (None, None)
        _attend_heads(score, value, keeps, s_ref, mx_ref, m_ref, acc_ref)
        if diag:
            for p in range(N_PAIRS):
                _finalize_pair(o_ref, acc_ref, p, sz_ref[0, p], _nsa_gates(gate_ref, p, BRANCH_SLC))

    @pl.when(jj < ii)
    def _():
        step(False)

    @pl.when(jj == ii)
    def _():
        step(True)


def _slc(qbT, ksl, vslT, selT, kaug, szT, gateT):
    B, _, _, S = qbT.shape
    R = selT.shape[2]
    it, jt = _pair_tables(S // QT)
    return pl.pallas_call(
        _slc_kernel,
        grid_spec=pltpu.PrefetchScalarGridSpec(
            num_scalar_prefetch=2,
            grid=(B, int(it.shape[0])),
            in_specs=[
                pl.BlockSpec((1, N_PAIRS, LANES, QT), lambda b, s, it, jt: (b, 0, 0, it[s])),
                pl.BlockSpec((1, 2 * TILE, KV_W), lambda b, s, it, jt: (b, jt[s], 0)),
                pl.BlockSpec((1, KV_W, 2 * TILE), lambda b, s, it, jt: (b, 0, jt[s])),
                pl.BlockSpec((1, NSA_GROUPS, R, QT), lambda b, s, it, jt: (b, 0, 0, it[s])),
                pl.BlockSpec((1, 2 * TILE, LANES), lambda b, s, it, jt: (jt[s], 0, 0)),
                pl.BlockSpec((1, N_PAIRS, LANES, QT), lambda b, s, it, jt: (b, 0, 0, it[s])),
                pl.BlockSpec((1, GATE_ROWS, QT), lambda b, s, it, jt: (b, 0, it[s])),
            ],
            out_specs=pl.BlockSpec((1, QT, NSA_W), lambda b, s, it, jt: (b, it[s], 0)),
            scratch_shapes=_state_scratch(2 * TILE, QT),
        ),
        out_shape=jax.ShapeDtypeStruct((B, S, NSA_W), jnp.float32),
        compiler_params=pltpu.CompilerParams(
            dimension_semantics=("arbitrary", "arbitrary"), vmem_limit_bytes=VMEM_LIMIT),
        name="slc",
    )(it, jt, qbT, ksl, vslT, selT, kaug, szT, gateT)


def _win_kernel(qT_ref, k2_ref, k1_ref, k0_ref, v2_ref, v1_ref, v0_ref, kaug_ref, sz_ref, gate_ref,
                o_ref, qaug_ref, m_ref, acc_ref, s_ref, mx_ref):
    i = pl.program_id(1)
    _init_state(m_ref, acc_ref)
    r = lax.broadcasted_iota(jnp.int32, (LANES, TILE), 0)
    trel = lax.broadcasted_iota(jnp.int32, (LANES, TILE), 1).astype(jnp.float32)
    absent2 = jnp.where(i >= 2, 0.0, NEG_INF)
    absent1 = jnp.where(i >= 1, 0.0, NEG_INF)
    for p in range(N_PAIRS):
        qT = qT_ref[0, p]
        for g in range(NSA_GROUPS):
            h = 2 * p + g
            slope = SLOPES[g * NSA_HPG + p]
            c_hi, c_lo = _slope_parts(slope)
            w_hi, w_lo = _split_bf16(np.float32(slope * LOG2E) * trel)
            aug = jnp.where(r == 0, -w_hi, 0.0)
            aug = jnp.where(r == 1, -w_lo, aug)
            aug = jnp.where(r == 2, c_hi, aug)
            aug = jnp.where(r == 3, c_lo, aug)
            aug = jnp.where(r == 4, -c_hi, aug)
            aug = jnp.where(r == 5, -c_lo, aug)
            aug = jnp.where(r == 6, absent1, aug)
            aug = jnp.where(r == 7, absent2, aug)
            qaug_ref[h] = jnp.concatenate([_keep_half(qT, g), aug.astype(jnp.bfloat16)], axis=0)

    kfull = jnp.concatenate(
        [jnp.concatenate([k2_ref[0], k1_ref[0], k0_ref[0]], axis=0), kaug_ref[...]], axis=1)
    vT = jnp.concatenate([v2_ref[0], v1_ref[0], v0_ref[0]], axis=1)
    k, t = _tile_iotas()

    def score(h):
        return _dot(kfull, qaug_ref[h])

    def value(h):
        return _value_rows(vT[(h % 2) * HALF:(h % 2 + 1) * HALF])

    _attend_heads(score, value, (k > t, None, k <= t), s_ref, mx_ref, m_ref, acc_ref)
    for p in range(N_PAIRS):
        _finalize_pair(o_ref, acc_ref, p, sz_ref[0, p], _nsa_gates(gate_ref, p, BRANCH_WIN))


def _win(qbT, kwi, vwiT, kaug, szT, gateT):
    B, _, _, S = qbT.shape
    nq = S // TILE
    kspec = lambda d: pl.BlockSpec((1, TILE, KV_W), lambda b, i: (b, jnp.maximum(i - d, 0), 0))
    vspec = lambda d: pl.BlockSpec((1, KV_W, TILE), lambda b, i: (b, 0, jnp.maximum(i - d, 0)))
    return pl.pallas_call(
        _win_kernel,
        grid=(B, nq),
        in_specs=[
            pl.BlockSpec((1, N_PAIRS, LANES, TILE), lambda b, i: (b, 0, 0, i)),
            kspec(2), kspec(1), kspec(0), vspec(2), vspec(1), vspec(0),
            pl.BlockSpec(kaug.shape, lambda b, i: (0, 0)),
            pl.BlockSpec((1, N_PAIRS, LANES, TILE), lambda b, i: (b, 0, 0, i)),
            pl.BlockSpec((1, GATE_ROWS, TILE), lambda b, i: (b, 0, i)),
        ],
        out_specs=pl.BlockSpec((1, TILE, NSA_W), lambda b, i: (b, i, 0)),
        out_shape=jax.ShapeDtypeStruct((B, S, NSA_W), jnp.float32),
        scratch_shapes=_state_scratch(kaug.shape[0]),
        compiler_params=pltpu.CompilerParams(
            dimension_semantics=("arbitrary", "arbitrary"), vmem_limit_bytes=VMEM_LIMIT),
        name="win",
    )(qbT, kwi, kwi, kwi, vwiT, vwiT, vwiT, kaug, szT, gateT)


def _out_kernel(x_ref, oa_ref, oc_ref, os_ref, ow_ref, w_ref, g_ref, y_ref):
    ob = oc_ref[0] + os_ref[0] + ow_ref[0]
    mix = jnp.concatenate([oa_ref[0], ob], axis=1).astype(jnp.bfloat16)
    y = _dot(mix, w_ref[...])
    r = lax.rsqrt(jnp.mean(y * y, axis=-1, keepdims=True) + RMS_EPS)
    y_ref[0] = x_ref[0] + y * r * g_ref[...]


def _out(x, oa, oc, os_, ow, w, g):
    B, S, _ = x.shape
    rows = lambda wd: pl.BlockSpec((1, ROWS, wd), lambda b, i: (b, i, 0))
    return pl.pallas_call(
        _out_kernel,
        grid=(B, S // ROWS),
        in_specs=[rows(D_MODEL), rows(MOBA_W), rows(NSA_W), rows(NSA_W), rows(NSA_W),
                  pl.BlockSpec(w.shape, lambda b, i: (0, 0)),
                  pl.BlockSpec((1, D_MODEL), lambda b, i: (0, 0))],
        out_specs=rows(D_MODEL),
        out_shape=jax.ShapeDtypeStruct((B, S, D_MODEL), jnp.float32),
        compiler_params=pltpu.CompilerParams(
            dimension_semantics=("arbitrary", "arbitrary"), vmem_limit_bytes=VMEM_LIMIT),
        name="out",
    )(x, oa, oc, os_, ow, w, g)


def _compress_weights(w1, w2):
    half = (CMP_LEN // 2) * HEAD_DIM
    w1r = w1.reshape(2, CMP_LEN // 2, HEAD_DIM, CMP_HIDDEN)
    z = jnp.zeros_like(w1r)
    g0 = jnp.concatenate([w1r, z], axis=-1)
    g1 = jnp.concatenate([z, w1r], axis=-1)
    both = jnp.stack([g0, g1], axis=2)
    both = both.reshape(2, half * NSA_GROUPS, NSA_GROUPS * CMP_HIDDEN).astype(jnp.bfloat16)
    zz = jnp.zeros_like(w2)
    w2bd = jnp.concatenate([jnp.concatenate([w2, zz], axis=1),
                            jnp.concatenate([zz, w2], axis=1)], axis=0).astype(jnp.bfloat16)
    return both[0], both[1], w2bd


def _overlap_T(n_cmp_pad, n_slc, rows):
    c = np.arange(n_cmp_pad)[None, :] * CMP_STRIDE
    j = np.arange(rows)[:, None] * SLC_BLOCK
    ov = (c < j + SLC_BLOCK) & (c + CMP_LEN > j) & (np.arange(rows)[:, None] < n_slc)
    return jnp.asarray(ov.astype(np.float32), jnp.bfloat16)


def _layer(x, pre_g, post_g, w_in, pos_k, pos_v, w_k1, w_k2, w_v1, w_v2, w_out):
    B, S, _ = x.shape
    nq = S // TILE
    n_cmp = (S - CMP_LEN) // CMP_STRIDE + 1
    n_slc = S // SLC_BLOCK
    C = S // CMP_STRIDE
    nat_cols, tr_cols, out_rows = _column_maps()

    wn = jnp.take(w_in, nat_cols, axis=1).astype(jnp.bfloat16)
    wt = jnp.take(w_in, tr_cols, axis=1).T
    wt = jnp.concatenate([wt, jnp.zeros((N_TR - wt.shape[0], D_MODEL), wt.dtype)], axis=0)
    (ka, kmean, kcm, vcm, ksl, kwi, qaT, vaT, qbT, vslT, vwiT,
     szaT, szbT, gateT) = _proj(x, pre_g.reshape(1, D_MODEL), wn, wt.astype(jnp.bfloat16))

    wkt, wkb, w2k = _compress_weights(w_k1, w_k2)
    wvt, wvb, w2v = _compress_weights(w_v1, w_v2)
    chunk = CMP_STRIDE * KV_W
    kc, vcT = _compress(
        kcm.reshape(B, C, chunk), vcm.reshape(B, C, chunk), wkt, wkb, wvt, wvb, w2k, w2v,
        pos_k.reshape(1, CMP_LEN * HEAD_DIM).astype(jnp.bfloat16),
        pos_v.reshape(1, CMP_LEN * HEAD_DIM).astype(jnp.bfloat16),
        w_k1.astype(jnp.bfloat16), w_v1.astype(jnp.bfloat16), n_cmp)

    oc, selT = _cmp(qbT, kc, vcT, _overlap_T(C, n_slc, SLC_BLOCK), _cmp_query_aug_table(),
                    szbT, gateT, n_slc)
    oa = _moba(qaT, ka, vaT, kmean.reshape(B, nq, MOBA_W), _key_aug_table(nq, 16, 1), szaT)
    osl = _slc(qbT, ksl, vslT, selT, _key_aug_table(nq, selT.shape[2], TILE // SLC_BLOCK),
               szbT, gateT)
    ow = _win(qbT, kwi, vwiT, _win_key_aug_table(), szbT, gateT)

    w_o = jnp.take(w_out, out_rows, axis=0).astype(jnp.bfloat16)
    return _out(x, oa, oc, osl, ow, w_o, post_g.reshape(1, D_MODEL))


def kernel(x, pre_norm_g, post_norm_g, w_in, cmp_pos_k, cmp_pos_v,
           w_cmp_k1, w_cmp_k2, w_cmp_v1, w_cmp_v2, w_out):
    for l in range(pre_norm_g.shape[0]):
        x = _layer(x, pre_norm_g[l], post_norm_g[l], w_in[l], cmp_pos_k[l], cmp_pos_v[l],
                   w_cmp_k1[l], w_cmp_k2[l], w_cmp_v1[l], w_cmp_v2[l], w_out[l])
    return x
```

```python
import functools

import numpy as np
import jax
import jax.numpy as jnp
from jax import lax
from jax.experimental import pallas as pl
from jax.experimental.pallas import tpu as pltpu

D_MODEL = 1024
HEAD_DIM = 64
N_HEADS = 8
N_PAIRS = N_HEADS // 2
NSA_GROUPS = 2
NSA_HPG = N_HEADS // NSA_GROUPS
MOBA_BLOCK = 256
MOBA_TOPK = 3
CMP_LEN = 32
CMP_STRIDE = 16
CMP_HIDDEN = 128
SLC_BLOCK = 64
SLC_TOPN = 16
WINDOW = 512
RMS_EPS = 1e-6
NEG_INF = -1e30
FORCED_SCORE = 1e9
SCALE = HEAD_DIM ** -0.5
LOG2E = float(np.log2(np.e))

TILE = 256
QT = 2 * TILE
ROWS = 2 * TILE
LANES = 128
HALF = HEAD_DIM
SUBLANES = 8
MOBA_W = N_HEADS * HEAD_DIM
NSA_W = N_HEADS * HEAD_DIM
KV_W = NSA_GROUPS * HEAD_DIM
N_GATES = 3 * N_HEADS
GATE_ROWS = 32
SLOPES = tuple(2.0 ** (-(i + 1)) for i in range(N_HEADS))
BRANCH_CMP, BRANCH_SLC, BRANCH_WIN = 0, 1, 2

VMEM_LIMIT = 48 * 1024 * 1024

_OFF = dict(qa=0, ka=512, va=1024, za=1536, qb=2048, kcm=2560, vcm=2688,
            ksl=2816, vsl=2944, kwi=3072, vwi=3200, gate=3328, zb=3352)
_NAT = dict(ka=(0, 512), kcm=(512, 640), vcm=(640, 768), ksl=(768, 896), kwi=(896, 1024))
N_NAT = 1024
_TR = dict(qa=(0, 512), va=(512, 1024), qb=(1024, 1536), vsl=(1536, 1664),
           vwi=(1664, 1792), za=(1792, 2304), zb=(2304, 2816), gate=(2816, 2848))
N_TR = 2848


def _pair_perm():
    idx = np.zeros(NSA_W, np.int32)
    for p in range(N_PAIRS):
        for g in range(NSA_GROUPS):
            for d in range(HEAD_DIM):
                idx[p * LANES + g * HALF + d] = (g * NSA_HPG + p) * HEAD_DIM + d
    return idx


def _column_maps():
    pp = _pair_perm()
    nat = np.zeros(N_NAT, np.int32)
    nat[0:512] = _OFF["ka"] + np.arange(512)
    nat[512:640] = _OFF["kcm"] + np.arange(128)
    nat[640:768] = _OFF["vcm"] + np.arange(128)
    nat[768:896] = _OFF["ksl"] + np.arange(128)
    nat[896:1024] = _OFF["kwi"] + np.arange(128)
    tr = np.zeros(N_TR - (GATE_ROWS - N_GATES), np.int32)
    tr[0:512] = _OFF["qa"] + np.arange(512)
    tr[512:1024] = _OFF["va"] + np.arange(512)
    tr[1024:1536] = _OFF["qb"] + pp
    tr[1536:1664] = _OFF["vsl"] + np.arange(128)
    tr[1664:1792] = _OFF["vwi"] + np.arange(128)
    tr[1792:2304] = _OFF["za"] + np.arange(512)
    tr[2304:2816] = _OFF["zb"] + pp
    tr[2816:2816 + N_GATES] = _OFF["gate"] + np.arange(N_GATES)
    out_rows = np.concatenate([np.arange(512), 512 + pp]).astype(np.int32)
    return nat, tr, out_rows


def _gate_row(g, p, branch):
    return (g * NSA_HPG + p) * 3 + branch


def _sigmoid(x):
    return 1.0 / (1.0 + jnp.exp(-x))


def _silu(x):
    return x * _sigmoid(x)


def _dot(a, b):
    return jnp.dot(a, b, preferred_element_type=jnp.float32)


def _proj_kernel(x_ref, g_ref, wn_ref, wt_ref,
                 ka_ref, kmean_ref, kcm_ref, vcm_ref, ksl_ref, kwi_ref,
                 qaT_ref, vaT_ref, qbT_ref, vslT_ref, vwiT_ref,
                 szaT_ref, szbT_ref, gateT_ref):
    x = x_ref[0]
    r = lax.rsqrt(jnp.mean(x * x, axis=-1, keepdims=True) + RMS_EPS)
    h = (x * r * g_ref[...]).astype(jnp.bfloat16)

    natural = _dot(h, wn_ref[...])

    def nat(name):
        a, b = _NAT[name]
        return natural[:, a:b]

    ka = nat("ka")
    for blk in range(x.shape[0] // MOBA_BLOCK):
        kmean_ref[0, blk] = jnp.mean(ka[blk * MOBA_BLOCK:(blk + 1) * MOBA_BLOCK],
                                     axis=0, keepdims=True)
    for p in range(N_PAIRS):
        ka_ref[0, p] = ka[:, p * LANES:(p + 1) * LANES].astype(jnp.bfloat16)
    kcm_ref[0] = nat("kcm").astype(jnp.bfloat16)
    vcm_ref[0] = nat("vcm").astype(jnp.bfloat16)
    ksl_ref[0] = nat("ksl").astype(jnp.bfloat16)
    kwi_ref[0] = nat("kwi").astype(jnp.bfloat16)

    def tr(name):
        a, b = _TR[name]
        return lax.dot_general(wt_ref[a:b, :], h, (((1,), (1,)), ((), ())),
                               preferred_element_type=jnp.float32)

    qa = tr("qa") * (SCALE * LOG2E)
    va = tr("va")
    qb = tr("qb") * (SCALE * LOG2E)
    sza = _silu(tr("za"))
    szb = _silu(tr("zb"))
    for p in range(N_PAIRS):
        sl = slice(p * LANES, (p + 1) * LANES)
        qaT_ref[0, p] = qa[sl].astype(jnp.bfloat16)
        vaT_ref[0, p] = va[sl].astype(jnp.bfloat16)
        qbT_ref[0, p] = qb[sl].astype(jnp.bfloat16)
        szaT_ref[0, p] = sza[sl]
        szbT_ref[0, p] = szb[sl]
    vslT_ref[0] = tr("vsl").astype(jnp.bfloat16)
    vwiT_ref[0] = tr("vwi").astype(jnp.bfloat16)
    gateT_ref[0] = _sigmoid(tr("gate"))


def _proj(x, g, wn, wt):
    B, S, _ = x.shape
    nq = S // TILE
    bf, f32 = jnp.bfloat16, jnp.float32
    slab_nat = lambda: pl.BlockSpec((1, N_PAIRS, ROWS, LANES), lambda b, i: (b, 0, i, 0))
    slab_tr = lambda: pl.BlockSpec((1, N_PAIRS, LANES, ROWS), lambda b, i: (b, 0, 0, i))
    rows = lambda w: pl.BlockSpec((1, ROWS, w), lambda b, i: (b, i, 0))
    chans = lambda c: pl.BlockSpec((1, c, ROWS), lambda b, i: (b, 0, i))
    out_shape = [
        jax.ShapeDtypeStruct((B, N_PAIRS, S, LANES), bf),
        jax.ShapeDtypeStruct((B, nq, 1, MOBA_W), f32),
        jax.ShapeDtypeStruct((B, S, KV_W), bf),
        jax.ShapeDtypeStruct((B, S, KV_W), bf),
        jax.ShapeDtypeStruct((B, S, KV_W), bf),
        jax.ShapeDtypeStruct((B, S, KV_W), bf),
        jax.ShapeDtypeStruct((B, N_PAIRS, LANES, S), bf),
        jax.ShapeDtypeStruct((B, N_PAIRS, LANES, S), bf),
        jax.ShapeDtypeStruct((B, N_PAIRS, LANES, S), bf),
        jax.ShapeDtypeStruct((B, KV_W, S), bf),
        jax.ShapeDtypeStruct((B, KV_W, S), bf),
        jax.ShapeDtypeStruct((B, N_PAIRS, LANES, S), f32),
        jax.ShapeDtypeStruct((B, N_PAIRS, LANES, S), f32),
        jax.ShapeDtypeStruct((B, GATE_ROWS, S), f32),
    ]
    out_specs = [
        slab_nat(),
        pl.BlockSpec((1, ROWS // MOBA_BLOCK, 1, MOBA_W), lambda b, i: (b, i, 0, 0)),
        rows(KV_W), rows(KV_W), rows(KV_W), rows(KV_W),
        slab_tr(), slab_tr(), slab_tr(),
        chans(KV_W), chans(KV_W),
        slab_tr(), slab_tr(),
        chans(GATE_ROWS),
    ]
    return pl.pallas_call(
        _proj_kernel,
        grid=(B, S // ROWS),
        in_specs=[
            pl.BlockSpec((1, ROWS, D_MODEL), lambda b, i: (b, i, 0)),
            pl.BlockSpec((1, D_MODEL), lambda b, i: (0, 0)),
            pl.BlockSpec((D_MODEL, N_NAT), lambda b, i: (0, 0)),
            pl.BlockSpec((N_TR, D_MODEL), lambda b, i: (0, 0)),
        ],
        out_specs=out_specs,
        out_shape=out_shape,
        compiler_params=pltpu.CompilerParams(
            dimension_semantics=("arbitrary", "arbitrary"),
            vmem_limit_bytes=VMEM_LIMIT),
        name="proj",
    )(x, g, wn, wt)


def _compress_kernel(n_cmp, xk_ref, xv_ref, wkt_ref, wkb_ref, wvt_ref, wvb_ref,
                     w2k_ref, w2v_ref, pek_ref, pev_ref, w1k_ref, w1v_ref,
                     kc_ref, vcT_ref):
    C = xk_ref.shape[1]
    row = lax.broadcasted_iota(jnp.int32, (C, KV_W), 0)

    def phi(x_ref, wt_ref, wb_ref, w2_ref, pe_ref, w1_ref):
        x = x_ref[0]
        top = _dot(x, wt_ref[...])
        bot = _dot(x, wb_ref[...])
        peb = _dot(jnp.broadcast_to(pe_ref[...], (8, pe_ref.shape[1])), w1_ref[...])[0:1]
        peb = jnp.concatenate([peb, peb], axis=1)
        hid = top + pltpu.roll(bot, C - 1, 0) + peb
        out = _dot(_silu(hid).astype(jnp.bfloat16), w2_ref[...])
        return jnp.where(row < n_cmp, out, 0.0)

    kc_ref[0] = phi(xk_ref, wkt_ref, wkb_ref, w2k_ref, pek_ref, w1k_ref).astype(jnp.bfloat16)
    vc = phi(xv_ref, wvt_ref, wvb_ref, w2v_ref, pev_ref, w1v_ref)
    vcT_ref[0] = vc.T.astype(jnp.bfloat16)


def _compress(xk, xv, wkt, wkb, wvt, wvb, w2k, w2v, pek, pev, w1k, w1v, n_cmp):
    B, C, F = xk.shape
    full = lambda a: pl.BlockSpec(a.shape, lambda b: (0,) * a.ndim)
    return pl.pallas_call(
        functools.partial(_compress_kernel, n_cmp),
        grid=(B,),
        in_specs=[pl.BlockSpec((1, C, F), lambda b: (b, 0, 0)),
                  pl.BlockSpec((1, C, F), lambda b: (b, 0, 0)),
                  full(wkt), full(wkb), full(wvt), full(wvb), full(w2k), full(w2v),
                  full(pek), full(pev), full(w1k), full(w1v)],
        out_specs=[pl.BlockSpec((1, C, KV_W), lambda b: (b, 0, 0)),
                   pl.BlockSpec((1, KV_W, C), lambda b: (b, 0, 0))],
        out_shape=[jax.ShapeDtypeStruct((B, C, KV_W), jnp.bfloat16),
                   jax.ShapeDtypeStruct((B, KV_W, C), jnp.bfloat16)],
        compiler_params=pltpu.CompilerParams(
            dimension_semantics=("arbitrary",), vmem_limit_bytes=VMEM_LIMIT),
        name="compress",
    )(xk, xv, wkt, wkb, wvt, wvb, w2k, w2v, pek, pev, w1k, w1v)


def _keep_half(qT, half):
    z = jnp.zeros((HALF, qT.shape[1]), qT.dtype)
    if half == 0:
        return jnp.concatenate([qT[0:HALF], z], axis=0)
    return jnp.concatenate([z, qT[HALF:2 * HALF]], axis=0)


def _rank_rows(vals, n_rows):
    R, T = vals.shape
    tiles = [vals[a:a + SUBLANES] for a in range(0, R, SUBLANES)]
    ranks = [jnp.zeros((SUBLANES, T), jnp.int32) for _ in tiles]
    j_in = lax.broadcasted_iota(jnp.int32, (SUBLANES, T), 0)
    for m in range(n_rows):
        vm = vals[m:m + 1, :]
        for a, tile in enumerate(tiles):
            lo = a * SUBLANES
            if lo > m:
                beats = vm >= tile
            elif lo + SUBLANES - 1 <= m:
                beats = vm > tile
            else:
                beats = (vm > tile) | ((vm == tile) & (j_in > m - lo))
            ranks[a] = ranks[a] + beats.astype(jnp.int32)
    return jnp.concatenate(ranks, axis=0)


def _query_tile(cols, tile0):
    return tile0 + lax.shift_right_logical(cols, int(np.log2(TILE)))


def _split_bf16(x):
    hi = x.astype(jnp.bfloat16).astype(jnp.float32)
    return hi, x - hi


def _slope_parts(slope):
    c = np.float32(slope * LOG2E)
    hi = np.asarray(c, jnp.bfloat16).astype(np.float32)
    lo = np.asarray(c - hi, jnp.bfloat16).astype(np.float32)
    return float(hi), float(lo)


def _alibi_rows(n_rows, T, first, slope, tile0):
    r = lax.broadcasted_iota(jnp.int32, (n_rows, T), 0)
    c = lax.broadcasted_iota(jnp.int32, (n_rows, T), 1)
    t_abs = ((c & (TILE - 1)) + _query_tile(c, tile0) * TILE).astype(jnp.float32)
    c_hi, c_lo = _slope_parts(slope)
    w_hi, w_lo = _split_bf16(np.float32(slope * LOG2E) * t_abs)
    out = jnp.where(r == first, -w_hi, 0.0)
    out = jnp.where(r == first + 1, -w_lo, out)
    out = jnp.where((r == first + 2) | (r == first + 4), c_hi, out)
    out = jnp.where((r == first + 3) | (r == first + 5), c_lo, out)
    return out


def _key_aug_table(nq, n_sel, blocks_per_tile):
    t = np.zeros((nq, TILE, LANES), np.float32)
    krel = np.arange(TILE)
    for j in range(nq):
        if n_sel:
            blk = j * blocks_per_tile + krel // (TILE // blocks_per_tile)
            t[j, krel, blk] = 1.0
        t[j, :, n_sel:n_sel + 2] = 1.0
        t[j, :, n_sel + 2:n_sel + 4] = krel[:, None]
        t[j, :, n_sel + 4:n_sel + 6] = TILE * j
    return jnp.asarray(t.reshape(nq // 2, 2 * TILE, LANES), jnp.bfloat16)


ONES_ROWS = 16
ACC_ROWS = HALF + ONES_ROWS


def _win_key_aug_table():
    n_tiles = WINDOW // TILE + 1
    t = np.zeros((n_tiles * TILE, LANES), np.float32)
    for slot in range(n_tiles):
        delta = n_tiles - 1 - slot
        rows = slice(slot * TILE, (slot + 1) * TILE)
        t[rows, 0:2] = 1.0
        t[rows, 2:4] = np.arange(TILE)[:, None]
        t[rows, 4:6] = TILE * delta
        if delta > 0:
            t[rows, 5 + delta] = 1.0
    return jnp.asarray(t, jnp.bfloat16)


def _value_rows(vT_h):
    return jnp.concatenate([vT_h, jnp.ones((ONES_ROWS, vT_h.shape[1]), vT_h.dtype)], axis=0)


def _stage_scores(h, score_fn, keeps, s_ref, mx_ref):
    s = score_fn(h)
    if any(k is not None for k in keeps):
        s = jnp.concatenate(
            [s[r * TILE:(r + 1) * TILE] if k is None
             else jnp.where(k, s[r * TILE:(r + 1) * TILE], NEG_INF)
             for r, k in enumerate(keeps)], axis=0)
    s_ref[h] = s
    mx_ref[h] = jnp.max(s, axis=0, keepdims=True)


def _consume_scores(h, value_fn, s_ref, mx_ref, m_ref, acc_ref):
    m_prev = m_ref[h]
    m_new = jnp.maximum(m_prev, mx_ref[h])
    alpha = jnp.exp2(m_prev - m_new)
    p = jnp.exp2(s_ref[h] - m_new).astype(jnp.bfloat16)
    acc_ref[h] = alpha * acc_ref[h] + _dot(value_fn(h), p)
    m_ref[h] = m_new


def _attend_heads(score_fn, value_fn, keeps, s_ref, mx_ref, m_ref, acc_ref):
    for h in range(N_HEADS):
        _stage_scores(h, score_fn, keeps, s_ref, mx_ref)
    for h in range(N_HEADS):
        _consume_scores(h, value_fn, s_ref, mx_ref, m_ref, acc_ref)


def _pipelined_step(s, n_steps, cur_diag, prev_diag, score_fn, value_fn, finalize_fn,
                    s_ref, mx_ref, m_ref, acc_ref):
    first = s == 0
    last = s == n_steps
    middle = jnp.logical_not(first | last)
    odd = lax.rem(s, 2) == 1
    no_keeps = (None, None)

    @pl.when(first)
    def _():
        _init_state(m_ref, acc_ref)
        keeps = _causal_keeps()
        for h in range(N_HEADS):
            _stage_scores(h, score_fn, keeps, s_ref[0], mx_ref[0])

    def both(parity, keeps):
        for h in range(N_HEADS):
            _stage_scores(h, score_fn, keeps, s_ref[parity], mx_ref[parity])
            _consume_scores(h, value_fn, s_ref[1 - parity], mx_ref[1 - parity], m_ref, acc_ref)

    for parity in (0, 1):
        on_parity = odd if parity else jnp.logical_not(odd)

        @pl.when(middle & on_parity & jnp.logical_not(cur_diag))
        def _():
            both(parity, no_keeps)

        @pl.when(middle & on_parity & cur_diag)
        def _():
            both(parity, _causal_keeps())

        @pl.when(last & on_parity)
        def _():
            for h in range(N_HEADS):
                _consume_scores(h, value_fn, s_ref[1 - parity], mx_ref[1 - parity],
                                m_ref, acc_ref)

    @pl.when(jnp.logical_not(first) & prev_diag)
    def _():
        finalize_fn()
        _init_state(m_ref, acc_ref)


def _init_state(m_ref, acc_ref):
    m_ref[...] = jnp.full(m_ref.shape, NEG_INF, jnp.float32)
    acc_ref[...] = jnp.zeros(acc_ref.shape, jnp.float32)


def _tile_iotas(T=TILE):
    k = lax.broadcasted_iota(jnp.int32, (TILE, T), 0)
    t = lax.broadcasted_iota(jnp.int32, (TILE, T), 1)
    return k, t


def _causal_keeps():
    k, t = _tile_iotas(QT)
    return (k <= t, k + TILE <= t)


def _store_pair(o_ref, p, o0T, o1T, szT, gates):
    if gates is not None:
        o0T = o0T * gates[0]
        o1T = o1T * gates[1]
    oT = jnp.concatenate([o0T, o1T], axis=0) * szT
    o_ref[0, :, p * LANES:(p + 1) * LANES] = oT.T


def _finalize_pair(o_ref, acc_ref, p, szT, gates=None):
    a0 = acc_ref[2 * p]
    a1 = acc_ref[2 * p + 1]
    _store_pair(o_ref, p, a0[0:HALF] / a0[HALF:HALF + 1], a1[0:HALF] / a1[HALF:HALF + 1],
                szT, gates)


def _nsa_gates(gate_ref, p, branch):
    return [gate_ref[0, _gate_row(g, p, branch):_gate_row(g, p, branch) + 1, :]
            for g in range(NSA_GROUPS)]


def _pair_tables(n_pairs):
    steps = [(ii, jj) for ii in range(n_pairs) for jj in range(ii + 1)]
    cur = steps + [steps[-1]]
    prev = [steps[0]] + steps
    cols = [[c[0] for c in cur], [c[1] for c in cur], [p[0] for p in prev], [p[1] for p in prev]]
    return tuple(jnp.asarray(c, jnp.int32) for c in cols), len(steps)


def _state_scratch(n_keys, T=TILE, pipelined=False):
    stages = 2 if pipelined else 1
    return ([
        pltpu.VMEM((N_HEADS, 2 * LANES, T), jnp.bfloat16),
        pltpu.VMEM((N_HEADS, 1, T), jnp.float32),
        pltpu.VMEM((N_HEADS, ACC_ROWS, T), jnp.float32),
    ] + [pltpu.VMEM((N_HEADS, n_keys, T), jnp.float32)] * stages
      + [pltpu.VMEM((N_HEADS, 1, T), jnp.float32)] * stages)


def _moba_kernel(nq, n_steps, ci_ref, cj_ref, pi_ref, pj_ref,
                 qT_ref, k_ref, vT_ref, kmean_ref, kaug_ref, sz_ref, o_ref,
                 qaug_ref, m_ref, acc_ref, s0_ref, s1_ref, mx0_ref, mx1_ref):
    s = pl.program_id(1)
    ii = ci_ref[s]
    jj = cj_ref[s]
    n_sel = 16

    @pl.when((jj == 0) & (s < n_steps))
    def _():
        n_iota = lax.broadcasted_iota(jnp.int32, (nq, QT), 0)
        own = _query_tile(lax.broadcasted_iota(jnp.int32, (nq, QT), 1), 2 * ii)
        for p in range(N_PAIRS):
            qT = qT_ref[0, p]
            km = kmean_ref[0, :, p * LANES:(p + 1) * LANES].astype(jnp.bfloat16)
            for hh in range(2):
                h = 2 * p + hh
                qm = _keep_half(qT, hh)
                gate = jnp.where(n_iota < own, _dot(km, qm), NEG_INF)
                rank = _rank_rows(gate, nq)
                sel = ((rank < MOBA_TOPK) & (n_iota < own)) | (n_iota == own)
                selneg = jnp.where(sel, 0.0, NEG_INF)
                if nq < n_sel:
                    selneg = jnp.concatenate(
                        [selneg, jnp.zeros((n_sel - nq, QT), jnp.float32)], axis=0)
                ali = _alibi_rows(LANES - n_sel, QT, 0, SLOPES[h], 2 * ii)
                aug = jnp.concatenate([selneg, ali], axis=0).astype(jnp.bfloat16)
                qaug_ref[h] = jnp.concatenate([qm, aug], axis=0)

    def score(h):
        return _dot(jnp.concatenate([k_ref[0, h // 2], kaug_ref[0]], axis=1), qaug_ref[h])

    def value(h):
        return _value_rows(vT_ref[0, h // 2, (h % 2) * HALF:(h % 2 + 1) * HALF])

    def finalize():
        for p in range(N_PAIRS):
            _finalize_pair(o_ref, acc_ref, p, sz_ref[0, p])

    _pipelined_step(s, n_steps, jj == ii, pj_ref[s] == pi_ref[s], score, value, finalize,
                    (s0_ref, s1_ref), (mx0_ref, mx1_ref), m_ref, acc_ref)


def _moba(qT, k, vT, kmean, kaug, szT):
    B, _, _, S = qT.shape
    nq = S // TILE
    tables, n_steps = _pair_tables(S // QT)
    cur_q = lambda b, s, ci, cj, pi, pj: (b, 0, 0, ci[s])
    prev_q = lambda b, s, ci, cj, pi, pj: (b, 0, 0, pi[s])
    return pl.pallas_call(
        functools.partial(_moba_kernel, nq, n_steps),
        grid_spec=pltpu.PrefetchScalarGridSpec(
            num_scalar_prefetch=4,
            grid=(B, n_steps + 1),
            in_specs=[
                pl.BlockSpec((1, N_PAIRS, LANES, QT), cur_q),
                pl.BlockSpec((1, N_PAIRS, 2 * TILE, LANES),
                             lambda b, s, ci, cj, pi, pj: (b, 0, cj[s], 0)),
                pl.BlockSpec((1, N_PAIRS, LANES, 2 * TILE),
                             lambda b, s, ci, cj, pi, pj: (b, 0, 0, pj[s])),
                pl.BlockSpec((1, nq, MOBA_W), lambda b, s, ci, cj, pi, pj: (b, 0, 0)),
                pl.BlockSpec((1, 2 * TILE, LANES), lambda b, s, ci, cj, pi, pj: (cj[s], 0, 0)),
                pl.BlockSpec((1, N_PAIRS, LANES, QT), prev_q),
            ],
            out_specs=pl.BlockSpec((1, QT, MOBA_W), lambda b, s, ci, cj, pi, pj: (b, pi[s], 0)),
            scratch_shapes=_state_scratch(2 * TILE, QT, pipelined=True),
        ),
        out_shape=jax.ShapeDtypeStruct((B, S, MOBA_W), jnp.float32),
        compiler_params=pltpu.CompilerParams(
            dimension_semantics=("arbitrary", "arbitrary"), vmem_limit_bytes=VMEM_LIMIT),
        name="moba",
    )(*tables, qT, k, vT, kmean, kaug, szT)


CMP_BAND = TILE // CMP_STRIDE + 1


def _cmp_query_aug_table():
    t = np.zeros((LANES, TILE), np.float32)
    trel = np.arange(TILE)
    for u in range(CMP_BAND):
        t[u, trel < CMP_STRIDE * (u - 1) + CMP_LEN - 1] = NEG_INF
    t[CMP_BAND, :] = NEG_INF
    return jnp.asarray(t, jnp.bfloat16)


def _cmp_kernel(n_slc, qT_ref, kc_ref, vcT_ref, ovT_ref, qaug_ref, sz_ref, gate_ref,
                o_ref, sel_ref, s_ref, mx_ref):
    i = pl.program_id(1)
    C = kc_ref.shape[1]
    c_rel = (lax.broadcasted_iota(jnp.int32, (C, LANES), 0)
             - (TILE // CMP_STRIDE) * i + 1)
    u = lax.broadcasted_iota(jnp.int32, (C, LANES), 1)
    kaug = jnp.where((c_rel == u) & (u < CMP_BAND), 1.0, 0.0)
    kaug = jnp.where((u == CMP_BAND) & (c_rel >= CMP_BAND), 1.0, kaug)
    kfull = jnp.concatenate([kc_ref[0], kaug.astype(jnp.bfloat16)], axis=1)
    qaug = qaug_ref[...]

    for p in range(N_PAIRS):
        qT = qT_ref[0, p]
        for g in range(NSA_GROUPS):
            h = 2 * p + g
            s = _dot(kfull, jnp.concatenate([_keep_half(qT, g), qaug], axis=0))
            s_ref[h] = s
            mx_ref[h] = jnp.max(s, axis=0, keepdims=True)

    t_abs = i * TILE + lax.broadcasted_iota(jnp.int32, (1, TILE), 1)
    seen = t_abs >= CMP_LEN - 1
    ones = jnp.ones((ONES_ROWS, C), jnp.bfloat16)
    vals = [jnp.concatenate([vcT_ref[0, g * HALF:(g + 1) * HALF], ones, ovT_ref[...]], axis=0)
            for g in range(NSA_GROUPS)]
    R = sel_ref.shape[2]
    imp = [jnp.zeros((R, TILE), jnp.float32) for _ in range(NSA_GROUPS)]
    for p in range(N_PAIRS):
        outs = []
        for g in range(NSA_GROUPS):
            h = 2 * p + g
            pr = jnp.exp2(s_ref[h] - mx_ref[h]).astype(jnp.bfloat16)
            acc = _dot(vals[g], pr)
            inv = jnp.where(seen, 1.0 / acc[HALF:HALF + 1], 0.0)
            outs.append(acc[0:HALF] * inv)
            imp[g] = imp[g] + acc[ACC_ROWS:ACC_ROWS + R] * inv
        _store_pair(o_ref, p, outs[0], outs[1], sz_ref[0, p], _nsa_gates(gate_ref, p, BRANCH_CMP))

    j_iota = lax.broadcasted_iota(jnp.int32, (R, TILE), 0)
    t_q = i * TILE + lax.broadcasted_iota(jnp.int32, (R, TILE), 1)
    own = lax.shift_right_logical(t_q, int(np.log2(SLC_BLOCK)))
    forced = (j_iota == 0) | (j_iota == own) | (j_iota == own - 1)
    causal = j_iota <= own
    for g in range(NSA_GROUPS):
        v = jnp.where(forced, FORCED_SCORE, jnp.where(causal, imp[g], NEG_INF))
        rank = _rank_rows(v, n_slc)
        sel = (rank < min(SLC_TOPN, n_slc)) & causal
        sel_ref[0, g] = jnp.where(sel, 0.0, NEG_INF).astype(jnp.bfloat16)


def _cmp(qbT, kc, vcT, ovT, qaug, szT, gateT, n_slc):
    B, _, _, S = qbT.shape
    C = kc.shape[1]
    nq = S // TILE
    R = ovT.shape[0]
    return pl.pallas_call(
        functools.partial(_cmp_kernel, n_slc),
        grid=(B, nq),
        in_specs=[
            pl.BlockSpec((1, N_PAIRS, LANES, TILE), lambda b, i: (b, 0, 0, i)),
            pl.BlockSpec((1, C, KV_W), lambda b, i: (b, 0, 0)),
            pl.BlockSpec((1, KV_W, C), lambda b, i: (b, 0, 0)),
            pl.BlockSpec(ovT.shape, lambda b, i: (0, 0)),
            pl.BlockSpec(qaug.shape, lambda b, i: (0, 0)),
            pl.BlockSpec((1, N_PAIRS, LANES, TILE), lambda b, i: (b, 0, 0, i)),
            pl.BlockSpec((1, GATE_ROWS, TILE), lambda b, i: (b, 0, i)),
        ],
        out_specs=[pl.BlockSpec((1, TILE, NSA_W), lambda b, i: (b, i, 0)),
                   pl.BlockSpec((1, NSA_GROUPS, R, TILE), lambda b, i: (b, 0, 0, i))],
        out_shape=[jax.ShapeDtypeStruct((B, S, NSA_W), jnp.float32),
                   jax.ShapeDtypeStruct((B, NSA_GROUPS, R, S), jnp.bfloat16)],
        scratch_shapes=[pltpu.VMEM((N_HEADS, C, TILE), jnp.float32),
                        pltpu.VMEM((N_HEADS, 1, TILE), jnp.float32)],
        compiler_params=pltpu.CompilerParams(
            dimension_semantics=("arbitrary", "arbitrary"), vmem_limit_bytes=VMEM_LIMIT),
        name="cmp",
    )(qbT, kc, vcT, ovT, qaug, szT, gateT)


def _slc_kernel(n_steps, ci_ref, cj_ref, pi_ref, pj_ref,
                qT_ref, k_ref, vT_ref, sel_ref, kaug_ref, sz_ref, gate_ref, o_ref,
                qaug_ref, m_ref, acc_ref, s0_ref, s1_ref, mx0_ref, mx1_ref):
    s = pl.program_id(1)
    ii = ci_ref[s]
    jj = cj_ref[s]
    n_sel = sel_ref.shape[2]

    @pl.when((jj == 0) & (s < n_steps))
    def _():
        for p in range(N_PAIRS):
            qT = qT_ref[0, p]
            for g in range(NSA_GROUPS):
                h = 2 * p + g
                qm = _keep_half(qT, g)
                ali = _alibi_rows(LANES - n_sel, QT, 0, SLOPES[g * NSA_HPG + p], 2 * ii)
                qaug_ref[h] = jnp.concatenate(
                    [qm, sel_ref[0, g], ali.astype(jnp.bfloat16)], axis=0)

    def score(h):
        return _dot(jnp.concatenate([k_ref[0], kaug_ref[0]], axis=1), qaug_ref[h])

    def value(h):
        return _value_rows(vT_ref[0, (h % 2) * HALF:(h % 2 + 1) * HALF])

    def finalize():
        for p in range(N_PAIRS):
            _finalize_pair(o_ref, acc_ref, p, sz_ref[0, p], _nsa_gates(gate_ref, p, BRANCH_SLC))

    _pipelined_step(s, n_steps, jj == ii, pj_ref[s] == pi_ref[s], score, value, finalize,
                    (s0_ref, s1_ref), (mx0_ref, mx1_ref), m_ref, acc_ref)


def _slc(qbT, ksl, vslT, selT, kaug, szT, gateT):
    B, _, _, S = qbT.shape
    R = selT.shape[2]
    tables, n_steps = _pair_tables(S // QT)
    cur_q = lambda b, s, ci, cj, pi, pj: (b, 0, 0, ci[s])
    prev_q = lambda b, s, ci, cj, pi, pj: (b, 0, 0, pi[s])
    return pl.pallas_call(
        functools.partial(_slc_kernel, n_steps),
        grid_spec=pltpu.PrefetchScalarGridSpec(
            num_scalar_prefetch=4,
            grid=(B, n_steps + 1),
            in_specs=[
                pl.BlockSpec((1, N_PAIRS, LANES, QT), cur_q),
                pl.BlockSpec((1, 2 * TILE, KV_W), lambda b, s, ci, cj, pi, pj: (b, cj[s], 0)),
                pl.BlockSpec((1, KV_W, 2 * TILE), lambda b, s, ci, cj, pi, pj: (b, 0, pj[s])),
                pl.BlockSpec((1, NSA_GROUPS, R, QT), cur_q),
                pl.BlockSpec((1, 2 * TILE, LANES), lambda b, s, ci, cj, pi, pj: (cj[s], 0, 0)),
                pl.BlockSpec((1, N_PAIRS, LANES, QT), prev_q),
                pl.BlockSpec((1, GATE_ROWS, QT), lambda b, s, ci, cj, pi, pj: (b, 0, pi[s])),
            ],
            out_specs=pl.BlockSpec((1, QT, NSA_W), lambda b, s, ci, cj, pi, pj: (b, pi[s], 0)),
            scratch_shapes=_state_scratch(2 * TILE, QT, pipelined=True),
        ),
        out_shape=jax.ShapeDtypeStruct((B, S, NSA_W), jnp.float32),
        compiler_params=pltpu.CompilerParams(
            dimension_semantics=("arbitrary", "arbitrary"), vmem_limit_bytes=VMEM_LIMIT),
        name="slc",
    )(*tables, qbT, ksl, vslT, selT, kaug, szT, gateT)


def _win_kernel(nq, qT_ref, k2_ref, k1_ref, k0_ref, v2_ref, v1_ref, v0_ref, kaug_ref, sz_ref,
                gate_ref, o_ref, qaug_ref, m_ref, acc_ref, s0_ref, s1_ref, mx0_ref, mx1_ref):
    i = pl.program_id(1)
    first = i == 0
    last = i == nq
    odd = lax.rem(i, 2) == 1
    s_ref, mx_ref = (s0_ref, s1_ref), (mx0_ref, mx1_ref)

    @pl.when(jnp.logical_not(last))
    def _():
        r = lax.broadcasted_iota(jnp.int32, (LANES, TILE), 0)
        trel = lax.broadcasted_iota(jnp.int32, (LANES, TILE), 1).astype(jnp.float32)
        absent2 = jnp.where(i >= 2, 0.0, NEG_INF)
        absent1 = jnp.where(i >= 1, 0.0, NEG_INF)
        for p in range(N_PAIRS):
            qT = qT_ref[0, p]
            for g in range(NSA_GROUPS):
                h = 2 * p + g
                slope = SLOPES[g * NSA_HPG + p]
                c_hi, c_lo = _slope_parts(slope)
                w_hi, w_lo = _split_bf16(np.float32(slope * LOG2E) * trel)
                aug = jnp.where(r == 0, -w_hi, 0.0)
                aug = jnp.where(r == 1, -w_lo, aug)
                aug = jnp.where(r == 2, c_hi, aug)
                aug = jnp.where(r == 3, c_lo, aug)
                aug = jnp.where(r == 4, -c_hi, aug)
                aug = jnp.where(r == 5, -c_lo, aug)
                aug = jnp.where(r == 6, absent1, aug)
                aug = jnp.where(r == 7, absent2, aug)
                qaug_ref[h] = jnp.concatenate(
                    [_keep_half(qT, g), aug.astype(jnp.bfloat16)], axis=0)

    def score(h):
        kfull = jnp.concatenate(
            [jnp.concatenate([k2_ref[0], k1_ref[0], k0_ref[0]], axis=0), kaug_ref[...]], axis=1)
        return _dot(kfull, qaug_ref[h])

    def value(h):
        lo, hi = (h % 2) * HALF, (h % 2 + 1) * HALF
        return _value_rows(
            jnp.concatenate([v2_ref[0, lo:hi], v1_ref[0, lo:hi], v0_ref[0, lo:hi]], axis=1))

    def keeps():
        k, t = _tile_iotas()
        return (k > t, None, k <= t)

    def stage(h, parity, masks):
        _stage_scores(h, score, masks, s_ref[parity], mx_ref[parity])

    def consume(h, parity):
        _consume_scores(h, value, s_ref[1 - parity], mx_ref[1 - parity], m_ref, acc_ref)

    @pl.when(first)
    def _():
        masks = keeps()
        for h in range(N_HEADS):
            stage(h, 0, masks)

    for parity in (0, 1):
        on_parity = odd if parity else jnp.logical_not(odd)

        @pl.when(on_parity & jnp.logical_not(first | last))
        def _():
            _init_state(m_ref, acc_ref)
            masks = keeps()
            for h in range(N_HEADS):
                stage(h, parity, masks)
                consume(h, parity)

        @pl.when(on_parity & last)
        def _():
            _init_state(m_ref, acc_ref)
            for h in range(N_HEADS):
                consume(h, parity)

    @pl.when(jnp.logical_not(first))
    def _():
        for p in range(N_PAIRS):
            _finalize_pair(o_ref, acc_ref, p, sz_ref[0, p], _nsa_gates(gate_ref, p, BRANCH_WIN))


def _win(qbT, kwi, vwiT, kaug, szT, gateT):
    B, _, _, S = qbT.shape
    nq = S // TILE
    cur = lambda i: jnp.minimum(i, nq - 1)
    prev = lambda i: jnp.maximum(i - 1, 0)
    kspec = lambda d: pl.BlockSpec((1, TILE, KV_W),
                                   lambda b, i: (b, jnp.maximum(cur(i) - d, 0), 0))
    vspec = lambda d: pl.BlockSpec((1, KV_W, TILE),
                                   lambda b, i: (b, 0, jnp.maximum(i - 1 - d, 0)))
    return pl.pallas_call(
        functools.partial(_win_kernel, nq),
        grid=(B, nq + 1),
        in_specs=[
            pl.BlockSpec((1, N_PAIRS, LANES, TILE), lambda b, i: (b, 0, 0, cur(i))),
            kspec(2), kspec(1), kspec(0), vspec(2), vspec(1), vspec(0),
            pl.BlockSpec(kaug.shape, lambda b, i: (0, 0)),
            pl.BlockSpec((1, N_PAIRS, LANES, TILE), lambda b, i: (b, 0, 0, prev(i))),
            pl.BlockSpec((1, GATE_ROWS, TILE), lambda b, i: (b, 0, prev(i))),
        ],
        out_specs=pl.BlockSpec((1, TILE, NSA_W), lambda b, i: (b, prev(i), 0)),
        out_shape=jax.ShapeDtypeStruct((B, S, NSA_W), jnp.float32),
        scratch_shapes=_state_scratch(kaug.shape[0], pipelined=True),
        compiler_params=pltpu.CompilerParams(
            dimension_semantics=("arbitrary", "arbitrary"), vmem_limit_bytes=VMEM_LIMIT),
        name="win",
    )(qbT, kwi, kwi, kwi, vwiT, vwiT, vwiT, kaug, szT, gateT)


def _out_kernel(x_ref, oa_ref, oc_ref, os_ref, ow_ref, w_ref, g_ref, y_ref):
    ob = oc_ref[0] + os_ref[0] + ow_ref[0]
    mix = jnp.concatenate([oa_ref[0], ob], axis=1).astype(jnp.bfloat16)
    y = _dot(mix, w_ref[...])
    r = lax.rsqrt(jnp.mean(y * y, axis=-1, keepdims=True) + RMS_EPS)
    y_ref[0] = x_ref[0] + y * r * g_ref[...]


def _out(x, oa, oc, os_, ow, w, g):
    B, S, _ = x.shape
    rows = lambda wd: pl.BlockSpec((1, ROWS, wd), lambda b, i: (b, i, 0))
    return pl.pallas_call(
        _out_kernel,
        grid=(B, S // ROWS),
        in_specs=[rows(D_MODEL), rows(MOBA_W), rows(NSA_W), rows(NSA_W), rows(NSA_W),
                  pl.BlockSpec(w.shape, lambda b, i: (0, 0)),
                  pl.BlockSpec((1, D_MODEL), lambda b, i: (0, 0))],
        out_specs=rows(D_MODEL),
        out_shape=jax.ShapeDtypeStruct((B, S, D_MODEL), jnp.float32),
        compiler_params=pltpu.CompilerParams(
            dimension_semantics=("arbitrary", "arbitrary"), vmem_limit_bytes=VMEM_LIMIT),
        name="out",
    )(x, oa, oc, os_, ow, w, g)


def _compress_weights(w1, w2):
    half = (CMP_LEN // 2) * HEAD_DIM
    w1r = w1.reshape(2, CMP_LEN // 2, HEAD_DIM, CMP_HIDDEN)
    z = jnp.zeros_like(w1r)
    g0 = jnp.concatenate([w1r, z], axis=-1)
    g1 = jnp.concatenate([z, w1r], axis=-1)
    both = jnp.stack([g0, g1], axis=2)
    both = both.reshape(2, half * NSA_GROUPS, NSA_GROUPS * CMP_HIDDEN).astype(jnp.bfloat16)
    zz = jnp.zeros_like(w2)
    w2bd = jnp.concatenate([jnp.concatenate([w2, zz], axis=1),
                            jnp.concatenate([zz, w2], axis=1)], axis=0).astype(jnp.bfloat16)
    return both[0], both[1], w2bd


def _overlap_T(n_cmp_pad, n_slc, rows):
    c = np.arange(n_cmp_pad)[None, :] * CMP_STRIDE
    j = np.arange(rows)[:, None] * SLC_BLOCK
    ov = (c < j + SLC_BLOCK) & (c + CMP_LEN > j) & (np.arange(rows)[:, None] < n_slc)
    return jnp.asarray(ov.astype(np.float32), jnp.bfloat16)


def _layer(x, pre_g, post_g, w_in, pos_k, pos_v, w_k1, w_k2, w_v1, w_v2, w_out):
    B, S, _ = x.shape
    nq = S // TILE
    n_cmp = (S - CMP_LEN) // CMP_STRIDE + 1
    n_slc = S // SLC_BLOCK
    C = S // CMP_STRIDE
    nat_cols, tr_cols, out_rows = _column_maps()

    wn = jnp.take(w_in, nat_cols, axis=1).astype(jnp.bfloat16)
    wt = jnp.take(w_in, tr_cols, axis=1).T
    wt = jnp.concatenate([wt, jnp.zeros((N_TR - wt.shape[0], D_MODEL), wt.dtype)], axis=0)
    (ka, kmean, kcm, vcm, ksl, kwi, qaT, vaT, qbT, vslT, vwiT,
     szaT, szbT, gateT) = _proj(x, pre_g.reshape(1, D_MODEL), wn, wt.astype(jnp.bfloat16))

    wkt, wkb, w2k = _compress_weights(w_k1, w_k2)
    wvt, wvb, w2v = _compress_weights(w_v1, w_v2)
    chunk = CMP_STRIDE * KV_W
    kc, vcT = _compress(
        kcm.reshape(B, C, chunk), vcm.reshape(B, C, chunk), wkt, wkb, wvt, wvb, w2k, w2v,
        pos_k.reshape(1, CMP_LEN * HEAD_DIM).astype(jnp.bfloat16),
        pos_v.reshape(1, CMP_LEN * HEAD_DIM).astype(jnp.bfloat16),
        w_k1.astype(jnp.bfloat16), w_v1.astype(jnp.bfloat16), n_cmp)

    oc, selT = _cmp(qbT, kc, vcT, _overlap_T(C, n_slc, SLC_BLOCK), _cmp_query_aug_table(),
                    szbT, gateT, n_slc)
    oa = _moba(qaT, ka, vaT, kmean.reshape(B, nq, MOBA_W), _key_aug_table(nq, 16, 1), szaT)
    osl = _slc(qbT, ksl, vslT, selT, _key_aug_table(nq, selT.shape[2], TILE // SLC_BLOCK),
               szbT, gateT)
    ow = _win(qbT, kwi, vwiT, _win_key_aug_table(), szbT, gateT)

    w_o = jnp.take(w_out, out_rows, axis=0).astype(jnp.bfloat16)
    return _out(x, oa, oc, osl, ow, w_o, post_g.reshape(1, D_MODEL))


def kernel(x, pre_norm_g, post_norm_g, w_in, cmp_pos_k, cmp_pos_v,
           w_cmp_k1, w_cmp_k2, w_cmp_v1, w_cmp_v2, w_out):
    for l in range(pre_norm_g.shape[0]):
        x = _layer(x, pre_norm_g[l], post_norm_g[l], w_in[l], cmp_pos_k[l], cmp_pos_v[l],
                   w_cmp_k1[l], w_cmp_k2[l], w_cmp_v1[l], w_cmp_v2[l], w_out[l])
    return x
```

```python
import functools

import numpy as np
import jax
import jax.numpy as jnp
from jax import lax
from jax.experimental import pallas as pl
from jax.experimental.pallas import tpu as pltpu

D_MODEL = 1024
HEAD_DIM = 64
N_HEADS = 8
N_PAIRS = N_HEADS // 2
NSA_GROUPS = 2
NSA_HPG = N_HEADS // NSA_GROUPS
MOBA_BLOCK = 256
MOBA_TOPK = 3
CMP_LEN = 32
CMP_STRIDE = 16
CMP_HIDDEN = 128
SLC_BLOCK = 64
SLC_TOPN = 16
WINDOW = 512
RMS_EPS = 1e-6
NEG_INF = -1e30
FORCED_SCORE = 1e9
SCALE = HEAD_DIM ** -0.5
LOG2E = float(np.log2(np.e))

TILE = 256
QT = 2 * TILE
ROWS = 2 * TILE
LANES = 128
HALF = HEAD_DIM
SUBLANES = 8
MOBA_W = N_HEADS * HEAD_DIM
NSA_W = N_HEADS * HEAD_DIM
KV_W = NSA_GROUPS * HEAD_DIM
N_GATES = 3 * N_HEADS
GATE_ROWS = 32
SLOPES = tuple(2.0 ** (-(i + 1)) for i in range(N_HEADS))
BRANCH_CMP, BRANCH_SLC, BRANCH_WIN = 0, 1, 2

VMEM_LIMIT = 48 * 1024 * 1024

_OFF = dict(qa=0, ka=512, va=1024, za=1536, qb=2048, kcm=2560, vcm=2688,
            ksl=2816, vsl=2944, kwi=3072, vwi=3200, gate=3328, zb=3352)
_NAT = dict(ka=(0, 512), kcm=(512, 640), vcm=(640, 768), ksl=(768, 896), kwi=(896, 1024))
N_NAT = 1024
_TR = dict(qa=(0, 512), va=(512, 1024), qb=(1024, 1536), vsl=(1536, 1664),
           vwi=(1664, 1792), za=(1792, 2304), zb=(2304, 2816), gate=(2816, 2848))
N_TR = 2848


def _to_pair_slabs(w, axis):
    w = jnp.moveaxis(w, axis, -1)
    lead = w.shape[:-1]
    w = w.reshape(lead + (NSA_GROUPS, NSA_HPG, HEAD_DIM))
    w = jnp.swapaxes(w, -3, -2).reshape(lead + (NSA_W,))
    return jnp.moveaxis(w, -1, axis)


def _projection_weights(w_in):
    col = lambda name, width: w_in[:, _OFF[name]:_OFF[name] + width]
    natural = jnp.concatenate(
        [col("ka", MOBA_W), col("kcm", KV_W), col("vcm", KV_W), col("ksl", KV_W), col("kwi", KV_W)],
        axis=1)
    transposed = jnp.concatenate(
        [col("qa", MOBA_W), col("va", MOBA_W), _to_pair_slabs(col("qb", NSA_W), 1),
         col("vsl", KV_W), col("vwi", KV_W), col("za", MOBA_W), _to_pair_slabs(col("zb", NSA_W), 1),
         col("gate", N_GATES), jnp.zeros((D_MODEL, GATE_ROWS - N_GATES), w_in.dtype)], axis=1)
    return natural.astype(jnp.bfloat16), transposed.T.astype(jnp.bfloat16)


def _gate_row(g, p, branch):
    return (g * NSA_HPG + p) * 3 + branch


def _sigmoid(x):
    return 1.0 / (1.0 + jnp.exp(-x))


def _silu(x):
    return x * _sigmoid(x)


def _dot(a, b):
    return jnp.dot(a, b, preferred_element_type=jnp.float32)


def _proj_kernel(x_ref, g_ref, wn_ref, wt_ref,
                 ka_ref, kmean_ref, kcm_ref, vcm_ref, ksl_ref, kwi_ref,
                 qaT_ref, vaT_ref, qbT_ref, vslT_ref, vwiT_ref,
                 szaT_ref, szbT_ref, gateT_ref):
    x = x_ref[0]
    r = lax.rsqrt(jnp.mean(x * x, axis=-1, keepdims=True) + RMS_EPS)
    h = (x * r * g_ref[...]).astype(jnp.bfloat16)

    natural = _dot(h, wn_ref[...])

    def nat(name):
        a, b = _NAT[name]
        return natural[:, a:b]

    ka = nat("ka")
    for blk in range(x.shape[0] // MOBA_BLOCK):
        kmean_ref[0, blk] = jnp.mean(ka[blk * MOBA_BLOCK:(blk + 1) * MOBA_BLOCK],
                                     axis=0, keepdims=True)
    for p in range(N_PAIRS):
        ka_ref[0, p] = ka[:, p * LANES:(p + 1) * LANES].astype(jnp.bfloat16)
    kcm_ref[0] = nat("kcm").astype(jnp.bfloat16)
    vcm_ref[0] = nat("vcm").astype(jnp.bfloat16)
    ksl_ref[0] = nat("ksl").astype(jnp.bfloat16)
    kwi_ref[0] = nat("kwi").astype(jnp.bfloat16)

    def tr(name):
        a, b = _TR[name]
        return lax.dot_general(wt_ref[a:b, :], h, (((1,), (1,)), ((), ())),
                               preferred_element_type=jnp.float32)

    qa = tr("qa") * (SCALE * LOG2E)
    va = tr("va")
    qb = tr("qb") * (SCALE * LOG2E)
    sza = _silu(tr("za"))
    szb = _silu(tr("zb"))
    for p in range(N_PAIRS):
        sl = slice(p * LANES, (p + 1) * LANES)
        qaT_ref[0, p] = qa[sl].astype(jnp.bfloat16)
        vaT_ref[0, p] = va[sl].astype(jnp.bfloat16)
        qbT_ref[0, p] = qb[sl].astype(jnp.bfloat16)
        szaT_ref[0, p] = sza[sl]
        szbT_ref[0, p] = szb[sl]
    vslT_ref[0] = tr("vsl").astype(jnp.bfloat16)
    vwiT_ref[0] = tr("vwi").astype(jnp.bfloat16)
    gateT_ref[0] = _sigmoid(tr("gate"))


def _proj(x, g, wn, wt):
    B, S, _ = x.shape
    nq = S // TILE
    bf, f32 = jnp.bfloat16, jnp.float32
    slab_nat = lambda: pl.BlockSpec((1, N_PAIRS, ROWS, LANES), lambda b, i: (b, 0, i, 0))
    slab_tr = lambda: pl.BlockSpec((1, N_PAIRS, LANES, ROWS), lambda b, i: (b, 0, 0, i))
    rows = lambda w: pl.BlockSpec((1, ROWS, w), lambda b, i: (b, i, 0))
    chans = lambda c: pl.BlockSpec((1, c, ROWS), lambda b, i: (b, 0, i))
    out_shape = [
        jax.ShapeDtypeStruct((B, N_PAIRS, S, LANES), bf),
        jax.ShapeDtypeStruct((B, nq, 1, MOBA_W), f32),
        jax.ShapeDtypeStruct((B, S, KV_W), bf),
        jax.ShapeDtypeStruct((B, S, KV_W), bf),
        jax.ShapeDtypeStruct((B, S, KV_W), bf),
        jax.ShapeDtypeStruct((B, S, KV_W), bf),
        jax.ShapeDtypeStruct((B, N_PAIRS, LANES, S), bf),
        jax.ShapeDtypeStruct((B, N_PAIRS, LANES, S), bf),
        jax.ShapeDtypeStruct((B, N_PAIRS, LANES, S), bf),
        jax.ShapeDtypeStruct((B, KV_W, S), bf),
        jax.ShapeDtypeStruct((B, KV_W, S), bf),
        jax.ShapeDtypeStruct((B, N_PAIRS, LANES, S), f32),
        jax.ShapeDtypeStruct((B, N_PAIRS, LANES, S), f32),
        jax.ShapeDtypeStruct((B, GATE_ROWS, S), f32),
    ]
    out_specs = [
        slab_nat(),
        pl.BlockSpec((1, ROWS // MOBA_BLOCK, 1, MOBA_W), lambda b, i: (b, i, 0, 0)),
        rows(KV_W), rows(KV_W), rows(KV_W), rows(KV_W),
        slab_tr(), slab_tr(), slab_tr(),
        chans(KV_W), chans(KV_W),
        slab_tr(), slab_tr(),
        chans(GATE_ROWS),
    ]
    return pl.pallas_call(
        _proj_kernel,
        grid=(B, S // ROWS),
        in_specs=[
            pl.BlockSpec((1, ROWS, D_MODEL), lambda b, i: (b, i, 0)),
            pl.BlockSpec((1, D_MODEL), lambda b, i: (0, 0)),
            pl.BlockSpec((D_MODEL, N_NAT), lambda b, i: (0, 0)),
            pl.BlockSpec((N_TR, D_MODEL), lambda b, i: (0, 0)),
        ],
        out_specs=out_specs,
        out_shape=out_shape,
        compiler_params=pltpu.CompilerParams(
            dimension_semantics=("arbitrary", "arbitrary"),
            vmem_limit_bytes=VMEM_LIMIT),
        name="proj",
    )(x, g, wn, wt)


def _compress_kernel(n_cmp, xk_ref, xv_ref, wkt_ref, wkb_ref, wvt_ref, wvb_ref,
                     w2k_ref, w2v_ref, pek_ref, pev_ref, w1k_ref, w1v_ref,
                     kc_ref, vcT_ref):
    C = xk_ref.shape[1]
    row = lax.broadcasted_iota(jnp.int32, (C, KV_W), 0)

    def phi(x_ref, wt_ref, wb_ref, w2_ref, pe_ref, w1_ref):
        x = x_ref[0]
        top = _dot(x, wt_ref[...])
        bot = _dot(x, wb_ref[...])
        peb = _dot(jnp.broadcast_to(pe_ref[...], (8, pe_ref.shape[1])), w1_ref[...])[0:1]
        peb = jnp.concatenate([peb, peb], axis=1)
        hid = top + pltpu.roll(bot, C - 1, 0) + peb
        out = _dot(_silu(hid).astype(jnp.bfloat16), w2_ref[...])
        return jnp.where(row < n_cmp, out, 0.0)

    kc_ref[0] = phi(xk_ref, wkt_ref, wkb_ref, w2k_ref, pek_ref, w1k_ref).astype(jnp.bfloat16)
    vc = phi(xv_ref, wvt_ref, wvb_ref, w2v_ref, pev_ref, w1v_ref)
    vcT_ref[0] = vc.T.astype(jnp.bfloat16)


def _compress(xk, xv, wkt, wkb, wvt, wvb, w2k, w2v, pek, pev, w1k, w1v, n_cmp):
    B, C, F = xk.shape
    full = lambda a: pl.BlockSpec(a.shape, lambda b: (0,) * a.ndim)
    return pl.pallas_call(
        functools.partial(_compress_kernel, n_cmp),
        grid=(B,),
        in_specs=[pl.BlockSpec((1, C, F), lambda b: (b, 0, 0)),
                  pl.BlockSpec((1, C, F), lambda b: (b, 0, 0)),
                  full(wkt), full(wkb), full(wvt), full(wvb), full(w2k), full(w2v),
                  full(pek), full(pev), full(w1k), full(w1v)],
        out_specs=[pl.BlockSpec((1, C, KV_W), lambda b: (b, 0, 0)),
                   pl.BlockSpec((1, KV_W, C), lambda b: (b, 0, 0))],
        out_shape=[jax.ShapeDtypeStruct((B, C, KV_W), jnp.bfloat16),
                   jax.ShapeDtypeStruct((B, KV_W, C), jnp.bfloat16)],
        compiler_params=pltpu.CompilerParams(
            dimension_semantics=("arbitrary",), vmem_limit_bytes=VMEM_LIMIT),
        name="compress",
    )(xk, xv, wkt, wkb, wvt, wvb, w2k, w2v, pek, pev, w1k, w1v)


def _keep_half(qT, half):
    z = jnp.zeros((HALF, qT.shape[1]), qT.dtype)
    if half == 0:
        return jnp.concatenate([qT[0:HALF], z], axis=0)
    return jnp.concatenate([z, qT[HALF:2 * HALF]], axis=0)


def _rank_rows(vals, n_rows):
    R, T = vals.shape
    tiles = [vals[a:a + SUBLANES] for a in range(0, R, SUBLANES)]
    ranks = [jnp.zeros((SUBLANES, T), jnp.int32) for _ in tiles]
    j_in = lax.broadcasted_iota(jnp.int32, (SUBLANES, T), 0)
    for m in range(n_rows):
        vm = vals[m:m + 1, :]
        for a, tile in enumerate(tiles):
            lo = a * SUBLANES
            if lo > m:
                beats = vm >= tile
            elif lo + SUBLANES - 1 <= m:
                beats = vm > tile
            else:
                beats = (vm > tile) | ((vm == tile) & (j_in > m - lo))
            ranks[a] = ranks[a] + beats.astype(jnp.int32)
    return jnp.concatenate(ranks, axis=0)


def _query_tile(cols, tile0):
    return tile0 + lax.shift_right_logical(cols, int(np.log2(TILE)))


def _split_bf16(x):
    hi = x.astype(jnp.bfloat16).astype(jnp.float32)
    return hi, x - hi


def _slope_parts(slope):
    c = np.float32(slope * LOG2E)
    hi = np.asarray(c, jnp.bfloat16).astype(np.float32)
    lo = np.asarray(c - hi, jnp.bfloat16).astype(np.float32)
    return float(hi), float(lo)


def _alibi_rows(n_rows, T, first, slope, tile0):
    r = lax.broadcasted_iota(jnp.int32, (n_rows, T), 0)
    c = lax.broadcasted_iota(jnp.int32, (n_rows, T), 1)
    t_abs = ((c & (TILE - 1)) + _query_tile(c, tile0) * TILE).astype(jnp.float32)
    c_hi, c_lo = _slope_parts(slope)
    w_hi, w_lo = _split_bf16(np.float32(slope * LOG2E) * t_abs)
    out = jnp.where(r == first, -w_hi, 0.0)
    out = jnp.where(r == first + 1, -w_lo, out)
    out = jnp.where((r == first + 2) | (r == first + 4), c_hi, out)
    out = jnp.where((r == first + 3) | (r == first + 5), c_lo, out)
    return out


def _key_aug_table(nq, n_sel, blocks_per_tile):
    t = np.zeros((nq, TILE, LANES), np.float32)
    krel = np.arange(TILE)
    for j in range(nq):
        if n_sel:
            blk = j * blocks_per_tile + krel // (TILE // blocks_per_tile)
            t[j, krel, blk] = 1.0
        t[j, :, n_sel:n_sel + 2] = 1.0
        t[j, :, n_sel + 2:n_sel + 4] = krel[:, None]
        t[j, :, n_sel + 4:n_sel + 6] = TILE * j
    return jnp.asarray(t.reshape(nq // 2, 2 * TILE, LANES), jnp.bfloat16)


ONES_ROWS = 16
ACC_ROWS = HALF + ONES_ROWS


def _win_key_aug_table():
    n_tiles = WINDOW // TILE + 1
    t = np.zeros((n_tiles * TILE, LANES), np.float32)
    for slot in range(n_tiles):
        delta = n_tiles - 1 - slot
        rows = slice(slot * TILE, (slot + 1) * TILE)
        t[rows, 0:2] = 1.0
        t[rows, 2:4] = np.arange(TILE)[:, None]
        t[rows, 4:6] = TILE * delta
        if delta > 0:
            t[rows, 5 + delta] = 1.0
    return jnp.asarray(t, jnp.bfloat16)


def _value_rows(vT_h):
    return jnp.concatenate([vT_h, jnp.ones((ONES_ROWS, vT_h.shape[1]), vT_h.dtype)], axis=0)


def _stage_scores(h, score_fn, blocks, cols, s_ref, mx_ref):
    c0, c1 = cols
    base = blocks[0][0]
    s_all = score_fn(h, base, blocks[-1][1], c0, c1)
    mx = None
    for r0, r1, keep in blocks:
        s = s_all[r0 - base:r1 - base]
        if keep is not None:
            s = jnp.where(keep, s, NEG_INF)
        s_ref[h, r0:r1, c0:c1] = s
        part = jnp.max(s, axis=0, keepdims=True)
        mx = part if mx is None else jnp.maximum(mx, part)
    mx_ref[h, :, c0:c1] = mx


def _consume_scores(h, value_fn, rows, cols, s_ref, mx_ref, m_ref, acc_ref):
    (r0, r1), (c0, c1) = rows, cols
    m_prev = m_ref[h, :, c0:c1]
    m_new = jnp.maximum(m_prev, mx_ref[h, :, c0:c1])
    alpha = jnp.exp2(m_prev - m_new)
    p = jnp.exp2(s_ref[h, r0:r1, c0:c1] - m_new).astype(jnp.bfloat16)
    acc_ref[h, :, c0:c1] = alpha * acc_ref[h, :, c0:c1] + _dot(value_fn(h, r0, r1), p)
    m_ref[h, :, c0:c1] = m_new


def _attend_heads(score_fn, value_fn, parts, s_ref, mx_ref, m_ref, acc_ref):
    for h in range(N_HEADS):
        for cols, blocks in parts:
            _stage_scores(h, score_fn, blocks, cols, s_ref, mx_ref)
    for h in range(N_HEADS):
        for cols, blocks in parts:
            rows = (min(b[0] for b in blocks), max(b[1] for b in blocks))
            _consume_scores(h, value_fn, rows, cols, s_ref, mx_ref, m_ref, acc_ref)


def _init_state(m_ref, acc_ref):
    m_ref[...] = jnp.full(m_ref.shape, NEG_INF, jnp.float32)
    acc_ref[...] = jnp.zeros(acc_ref.shape, jnp.float32)


def _tile_iotas(T=TILE):
    k = lax.broadcasted_iota(jnp.int32, (TILE, T), 0)
    t = lax.broadcasted_iota(jnp.int32, (TILE, T), 1)
    return k, t


def _pair_step_parts(diag):
    if not diag:
        return [((0, QT), [(0, 2 * TILE, None)])]
    k, t = _tile_iotas()
    tri = k <= t
    return [((0, TILE), [(0, TILE, tri)]),
            ((TILE, QT), [(0, TILE, None), (TILE, 2 * TILE, tri)])]


def _store_pair(o_ref, p, o0T, o1T, szT, gates):
    if gates is not None:
        o0T = o0T * gates[0]
        o1T = o1T * gates[1]
    oT = jnp.concatenate([o0T, o1T], axis=0) * szT
    o_ref[0, :, p * LANES:(p + 1) * LANES] = oT.T


def _finalize_pair(o_ref, acc_ref, p, szT, gates=None):
    a0 = acc_ref[2 * p]
    a1 = acc_ref[2 * p + 1]
    _store_pair(o_ref, p, a0[0:HALF] / a0[HALF:HALF + 1], a1[0:HALF] / a1[HALF:HALF + 1],
                szT, gates)


def _nsa_gates(gate_ref, p, branch):
    return [gate_ref[0, _gate_row(g, p, branch):_gate_row(g, p, branch) + 1, :]
            for g in range(NSA_GROUPS)]


def _pair_tables(n_pairs):
    it, jt = [], []
    for ii in range(n_pairs):
        for jj in range(ii + 1):
            it.append(ii)
            jt.append(jj)
    return jnp.asarray(it, jnp.int32), jnp.asarray(jt, jnp.int32)


def _state_scratch(n_keys, T=TILE):
    return [
        pltpu.VMEM((N_HEADS, 2 * LANES, T), jnp.bfloat16),
        pltpu.VMEM((N_HEADS, 1, T), jnp.float32),
        pltpu.VMEM((N_HEADS, ACC_ROWS, T), jnp.float32),
        pltpu.VMEM((N_HEADS, n_keys, T), jnp.float32),
        pltpu.VMEM((N_HEADS, 1, T), jnp.float32),
    ]


def _moba_kernel(nq, it_ref, jt_ref, qT_ref, k_ref, vT_ref, kmean_ref, kaug_ref, sz_ref, o_ref,
                 qaug_ref, m_ref, acc_ref, s_ref, mx_ref):
    s = pl.program_id(1)
    ii = it_ref[s]
    jj = jt_ref[s]
    n_sel = 16

    @pl.when(jj == 0)
    def _():
        _init_state(m_ref, acc_ref)
        n_iota = lax.broadcasted_iota(jnp.int32, (nq, QT), 0)
        own = _query_tile(lax.broadcasted_iota(jnp.int32, (nq, QT), 1), 2 * ii)
        for p in range(N_PAIRS):
            qT = qT_ref[0, p]
            km = kmean_ref[0, :, p * LANES:(p + 1) * LANES].astype(jnp.bfloat16)
            for hh in range(2):
                h = 2 * p + hh
                qm = _keep_half(qT, hh)
                gate = jnp.where(n_iota < own, _dot(km, qm), NEG_INF)
                rank = _rank_rows(gate, nq)
                sel = ((rank < MOBA_TOPK) & (n_iota < own)) | (n_iota == own)
                selneg = jnp.where(sel, 0.0, NEG_INF)
                if nq < n_sel:
                    selneg = jnp.concatenate(
                        [selneg, jnp.zeros((n_sel - nq, QT), jnp.float32)], axis=0)
                ali = _alibi_rows(LANES - n_sel, QT, 0, SLOPES[h], 2 * ii)
                aug = jnp.concatenate([selneg, ali], axis=0).astype(jnp.bfloat16)
                qaug_ref[h] = jnp.concatenate([qm, aug], axis=0)

    def score(h, r0, r1, c0, c1):
        kfull = jnp.concatenate([k_ref[0, h // 2, r0:r1], kaug_ref[0, r0:r1]], axis=1)
        return _dot(kfull, qaug_ref[h, :, c0:c1])

    def value(h, r0, r1):
        return _value_rows(vT_ref[0, h // 2, (h % 2) * HALF:(h % 2 + 1) * HALF, r0:r1])

    def step(diag):
        _attend_heads(score, value, _pair_step_parts(diag), s_ref, mx_ref, m_ref, acc_ref)
        if diag:
            for p in range(N_PAIRS):
                _finalize_pair(o_ref, acc_ref, p, sz_ref[0, p])

    @pl.when(jj < ii)
    def _():
        step(False)

    @pl.when(jj == ii)
    def _():
        step(True)


def _moba(qT, k, vT, kmean, kaug, szT):
    B, _, _, S = qT.shape
    nq = S // TILE
    it, jt = _pair_tables(S // QT)
    return pl.pallas_call(
        functools.partial(_moba_kernel, nq),
        grid_spec=pltpu.PrefetchScalarGridSpec(
            num_scalar_prefetch=2,
            grid=(B, int(it.shape[0])),
            in_specs=[
                pl.BlockSpec((1, N_PAIRS, LANES, QT), lambda b, s, it, jt: (b, 0, 0, it[s])),
                pl.BlockSpec((1, N_PAIRS, 2 * TILE, LANES), lambda b, s, it, jt: (b, 0, jt[s], 0)),
                pl.BlockSpec((1, N_PAIRS, LANES, 2 * TILE), lambda b, s, it, jt: (b, 0, 0, jt[s])),
                pl.BlockSpec((1, nq, MOBA_W), lambda b, s, it, jt: (b, 0, 0)),
                pl.BlockSpec((1, 2 * TILE, LANES), lambda b, s, it, jt: (jt[s], 0, 0)),
                pl.BlockSpec((1, N_PAIRS, LANES, QT), lambda b, s, it, jt: (b, 0, 0, it[s])),
            ],
            out_specs=pl.BlockSpec((1, QT, MOBA_W), lambda b, s, it, jt: (b, it[s], 0)),
            scratch_shapes=_state_scratch(2 * TILE, QT),
        ),
        out_shape=jax.ShapeDtypeStruct((B, S, MOBA_W), jnp.float32),
        compiler_params=pltpu.CompilerParams(
            dimension_semantics=("arbitrary", "arbitrary"), vmem_limit_bytes=VMEM_LIMIT),
        name="moba",
    )(it, jt, qT, k, vT, kmean, kaug, szT)


CMP_BAND = TILE // CMP_STRIDE + 1
RANK_ROWS_STEP = 16


def _cmp_query_aug_table():
    t = np.zeros((LANES, TILE), np.float32)
    trel = np.arange(TILE)
    for u in range(CMP_BAND):
        t[u, trel < CMP_STRIDE * (u - 1) + CMP_LEN - 1] = NEG_INF
    t[CMP_BAND, :] = NEG_INF
    return jnp.asarray(t, jnp.bfloat16)


def _cmp_kernel(n_slc, qT_ref, kc_ref, vcT_ref, ovT_ref, qaug_ref, sz_ref, gate_ref,
                o_ref, sel_ref, s_ref, mx_ref):
    i = pl.program_id(1)
    C = kc_ref.shape[1]
    c_rel = (lax.broadcasted_iota(jnp.int32, (C, LANES), 0)
             - (TILE // CMP_STRIDE) * i + 1)
    u = lax.broadcasted_iota(jnp.int32, (C, LANES), 1)
    kaug = jnp.where((c_rel == u) & (u < CMP_BAND), 1.0, 0.0)
    kaug = jnp.where((u == CMP_BAND) & (c_rel >= CMP_BAND), 1.0, kaug)
    kfull = jnp.concatenate([kc_ref[0], kaug.astype(jnp.bfloat16)], axis=1)
    qaug = qaug_ref[...]

    for p in range(N_PAIRS):
        qT = qT_ref[0, p]
        for g in range(NSA_GROUPS):
            h = 2 * p + g
            s = _dot(kfull, jnp.concatenate([_keep_half(qT, g), qaug], axis=0))
            s_ref[h] = s
            mx_ref[h] = jnp.max(s, axis=0, keepdims=True)

    t_abs = i * TILE + lax.broadcasted_iota(jnp.int32, (1, TILE), 1)
    seen = t_abs >= CMP_LEN - 1
    ones = jnp.ones((ONES_ROWS, C), jnp.bfloat16)
    vals = [jnp.concatenate([vcT_ref[0, g * HALF:(g + 1) * HALF], ones, ovT_ref[...]], axis=0)
            for g in range(NSA_GROUPS)]
    R = sel_ref.shape[2]
    imp = [jnp.zeros((R, TILE), jnp.float32) for _ in range(NSA_GROUPS)]
    for p in range(N_PAIRS):
        outs = []
        for g in range(NSA_GROUPS):
            h = 2 * p + g
            pr = jnp.exp2(s_ref[h] - mx_ref[h]).astype(jnp.bfloat16)
            acc = _dot(vals[g], pr)
            inv = jnp.where(seen, 1.0 / acc[HALF:HALF + 1], 0.0)
            outs.append(acc[0:HALF] * inv)
            imp[g] = imp[g] + acc[ACC_ROWS:ACC_ROWS + R] * inv
        _store_pair(o_ref, p, outs[0], outs[1], sz_ref[0, p], _nsa_gates(gate_ref, p, BRANCH_CMP))

    j_iota = lax.broadcasted_iota(jnp.int32, (R, TILE), 0)
    t_q = i * TILE + lax.broadcasted_iota(jnp.int32, (R, TILE), 1)
    own = lax.shift_right_logical(t_q, int(np.log2(SLC_BLOCK)))
    forced = (j_iota == 0) | (j_iota == own) | (j_iota == own - 1)
    causal = j_iota <= own

    def select(n_rows):
        for g in range(NSA_GROUPS):
            v = jnp.where(forced, FORCED_SCORE, jnp.where(causal, imp[g], NEG_INF))[0:n_rows]
            rank = _rank_rows(v, min(n_rows, n_slc))
            sel = (rank < min(SLC_TOPN, n_slc)) & causal[0:n_rows]
            selneg = jnp.where(sel, 0.0, NEG_INF)
            if n_rows < R:
                selneg = jnp.concatenate(
                    [selneg, jnp.full((R - n_rows, TILE), NEG_INF, jnp.float32)], axis=0)
            sel_ref[0, g] = selneg.astype(jnp.bfloat16)

    blocks_per_tile = TILE // SLC_BLOCK
    variants = list(range(RANK_ROWS_STEP, R, RANK_ROWS_STEP)) + [R]
    for idx, n_rows in enumerate(variants):
        lo = 0 if idx == 0 else variants[idx - 1] // blocks_per_tile
        hi = n_rows // blocks_per_tile

        @pl.when((i >= lo) & (i < hi) if n_rows < R else i >= lo)
        def _():
            select(n_rows)


def _cmp(qbT, kc, vcT, ovT, qaug, szT, gateT, n_slc):
    B, _, _, S = qbT.shape
    C = kc.shape[1]
    nq = S // TILE
    R = ovT.shape[0]
    return pl.pallas_call(
        functools.partial(_cmp_kernel, n_slc),
        grid=(B, nq),
        in_specs=[
            pl.BlockSpec((1, N_PAIRS, LANES, TILE), lambda b, i: (b, 0, 0, i)),
            pl.BlockSpec((1, C, KV_W), lambda b, i: (b, 0, 0)),
            pl.BlockSpec((1, KV_W, C), lambda b, i: (b, 0, 0)),
            pl.BlockSpec(ovT.shape, lambda b, i: (0, 0)),
            pl.BlockSpec(qaug.shape, lambda b, i: (0, 0)),
            pl.BlockSpec((1, N_PAIRS, LANES, TILE), lambda b, i: (b, 0, 0, i)),
            pl.BlockSpec((1, GATE_ROWS, TILE), lambda b, i: (b, 0, i)),
        ],
        out_specs=[pl.BlockSpec((1, TILE, NSA_W), lambda b, i: (b, i, 0)),
                   pl.BlockSpec((1, NSA_GROUPS, R, TILE), lambda b, i: (b, 0, 0, i))],
        out_shape=[jax.ShapeDtypeStruct((B, S, NSA_W), jnp.float32),
                   jax.ShapeDtypeStruct((B, NSA_GROUPS, R, S), jnp.bfloat16)],
        scratch_shapes=[pltpu.VMEM((N_HEADS, C, TILE), jnp.float32),
                        pltpu.VMEM((N_HEADS, 1, TILE), jnp.float32)],
        compiler_params=pltpu.CompilerParams(
            dimension_semantics=("arbitrary", "arbitrary"), vmem_limit_bytes=VMEM_LIMIT),
        name="cmp",
    )(qbT, kc, vcT, ovT, qaug, szT, gateT)


def _slc_kernel(it_ref, jt_ref, qT_ref, k_ref, vT_ref, sel_ref, kaug_ref, sz_ref, gate_ref, o_ref,
                qaug_ref, m_ref, acc_ref, s_ref, mx_ref):
    s = pl.program_id(1)
    ii = it_ref[s]
    jj = jt_ref[s]
    n_sel = sel_ref.shape[2]

    @pl.when(jj == 0)
    def _():
        _init_state(m_ref, acc_ref)
        for p in range(N_PAIRS):
            qT = qT_ref[0, p]
            for g in range(NSA_GROUPS):
                h = 2 * p + g
                qm = _keep_half(qT, g)
                ali = _alibi_rows(LANES - n_sel, QT, 0, SLOPES[g * NSA_HPG + p], 2 * ii)
                qaug_ref[h] = jnp.concatenate(
                    [qm, sel_ref[0, g], ali.astype(jnp.bfloat16)], axis=0)

    def score(h, r0, r1, c0, c1):
        kfull = jnp.concatenate([k_ref[0, r0:r1], kaug_ref[0, r0:r1]], axis=1)
        return _dot(kfull, qaug_ref[h, :, c0:c1])

    def value(h, r0, r1):
        return _value_rows(vT_ref[0, (h % 2) * HALF:(h % 2 + 1) * HALF, r0:r1])

    def step(diag):
        _attend_heads(score, value, _pair_step_parts(diag), s_ref, mx_ref, m_ref, acc_ref)
        if diag:
            for p in range(N_PAIRS):
                _finalize_pair(o_ref, acc_ref, p, sz_ref[0, p], _nsa_gates(gate_ref, p, BRANCH_SLC))

    @pl.when(jj < ii)
    def _():
        step(False)

    @pl.when(jj == ii)
    def _():
        step(True)


def _slc(qbT, ksl, vslT, selT, kaug, szT, gateT):
    B, _, _, S = qbT.shape
    R = selT.shape[2]
    it, jt = _pair_tables(S // QT)
    return pl.pallas_call(
        _slc_kernel,
        grid_spec=pltpu.PrefetchScalarGridSpec(
            num_scalar_prefetch=2,
            grid=(B, int(it.shape[0])),
            in_specs=[
                pl.BlockSpec((1, N_PAIRS, LANES, QT), lambda b, s, it, jt: (b, 0, 0, it[s])),
                pl.BlockSpec((1, 2 * TILE, KV_W), lambda b, s, it, jt: (b, jt[s], 0)),
                pl.BlockSpec((1, KV_W, 2 * TILE), lambda b, s, it, jt: (b, 0, jt[s])),
                pl.BlockSpec((1, NSA_GROUPS, R, QT), lambda b, s, it, jt: (b, 0, 0, it[s])),
                pl.BlockSpec((1, 2 * TILE, LANES), lambda b, s, it, jt: (jt[s], 0, 0)),
                pl.BlockSpec((1, N_PAIRS, LANES, QT), lambda b, s, it, jt: (b, 0, 0, it[s])),
                pl.BlockSpec((1, GATE_ROWS, QT), lambda b, s, it, jt: (b, 0, it[s])),
            ],
            out_specs=pl.BlockSpec((1, QT, NSA_W), lambda b, s, it, jt: (b, it[s], 0)),
            scratch_shapes=_state_scratch(2 * TILE, QT),
        ),
        out_shape=jax.ShapeDtypeStruct((B, S, NSA_W), jnp.float32),
        compiler_params=pltpu.CompilerParams(
            dimension_semantics=("arbitrary", "arbitrary"), vmem_limit_bytes=VMEM_LIMIT),
        name="slc",
    )(it, jt, qbT, ksl, vslT, selT, kaug, szT, gateT)


def _win_kernel(qT_ref, k2_ref, k1_ref, k0_ref, v2_ref, v1_ref, v0_ref, kaug_ref, sz_ref, gate_ref,
                o_ref, qaug_ref, m_ref, acc_ref, s_ref, mx_ref):
    i = pl.program_id(1)
    _init_state(m_ref, acc_ref)
    r = lax.broadcasted_iota(jnp.int32, (LANES, TILE), 0)
    trel = lax.broadcasted_iota(jnp.int32, (LANES, TILE), 1).astype(jnp.float32)
    absent2 = jnp.where(i >= 2, 0.0, NEG_INF)
    absent1 = jnp.where(i >= 1, 0.0, NEG_INF)
    for p in range(N_PAIRS):
        qT = qT_ref[0, p]
        for g in range(NSA_GROUPS):
            h = 2 * p + g
            slope = SLOPES[g * NSA_HPG + p]
            c_hi, c_lo = _slope_parts(slope)
            w_hi, w_lo = _split_bf16(np.float32(slope * LOG2E) * trel)
            aug = jnp.where(r == 0, -w_hi, 0.0)
            aug = jnp.where(r == 1, -w_lo, aug)
            aug = jnp.where(r == 2, c_hi, aug)
            aug = jnp.where(r == 3, c_lo, aug)
            aug = jnp.where(r == 4, -c_hi, aug)
            aug = jnp.where(r == 5, -c_lo, aug)
            aug = jnp.where(r == 6, absent1, aug)
            aug = jnp.where(r == 7, absent2, aug)
            qaug_ref[h] = jnp.concatenate([_keep_half(qT, g), aug.astype(jnp.bfloat16)], axis=0)

    kfull = jnp.concatenate(
        [jnp.concatenate([k2_ref[0], k1_ref[0], k0_ref[0]], axis=0), kaug_ref[...]], axis=1)
    vT = jnp.concatenate([v2_ref[0], v1_ref[0], v0_ref[0]], axis=1)
    n_keys = kfull.shape[0]
    k, t = _tile_iotas()

    def score(h, r0, r1, c0, c1):
        return _dot(kfull[r0:r1], qaug_ref[h, :, c0:c1])

    def value(h, r0, r1):
        return _value_rows(vT[(h % 2) * HALF:(h % 2 + 1) * HALF, r0:r1])

    blocks = [(0, TILE, k > t), (TILE, n_keys - TILE, None), (n_keys - TILE, n_keys, k <= t)]
    _attend_heads(score, value, [((0, TILE), blocks)], s_ref, mx_ref, m_ref, acc_ref)
    for p in range(N_PAIRS):
        _finalize_pair(o_ref, acc_ref, p, sz_ref[0, p], _nsa_gates(gate_ref, p, BRANCH_WIN))


def _win(qbT, kwi, vwiT, kaug, szT, gateT):
    B, _, _, S = qbT.shape
    nq = S // TILE
    kspec = lambda d: pl.BlockSpec((1, TILE, KV_W), lambda b, i: (b, jnp.maximum(i - d, 0), 0))
    vspec = lambda d: pl.BlockSpec((1, KV_W, TILE), lambda b, i: (b, 0, jnp.maximum(i - d, 0)))
    return pl.pallas_call(
        _win_kernel,
        grid=(B, nq),
        in_specs=[
            pl.BlockSpec((1, N_PAIRS, LANES, TILE), lambda b, i: (b, 0, 0, i)),
            kspec(2), kspec(1), kspec(0), vspec(2), vspec(1), vspec(0),
            pl.BlockSpec(kaug.shape, lambda b, i: (0, 0)),
            pl.BlockSpec((1, N_PAIRS, LANES, TILE), lambda b, i: (b, 0, 0, i)),
            pl.BlockSpec((1, GATE_ROWS, TILE), lambda b, i: (b, 0, i)),
        ],
        out_specs=pl.BlockSpec((1, TILE, NSA_W), lambda b, i: (b, i, 0)),
        out_shape=jax.ShapeDtypeStruct((B, S, NSA_W), jnp.float32),
        scratch_shapes=_state_scratch(kaug.shape[0]),
        compiler_params=pltpu.CompilerParams(
            dimension_semantics=("arbitrary", "arbitrary"), vmem_limit_bytes=VMEM_LIMIT),
        name="win",
    )(qbT, kwi, kwi, kwi, vwiT, vwiT, vwiT, kaug, szT, gateT)


def _out_kernel(x_ref, oa_ref, oc_ref, os_ref, ow_ref, w_ref, g_ref, y_ref):
    ob = oc_ref[0] + os_ref[0] + ow_ref[0]
    mix = jnp.concatenate([oa_ref[0], ob], axis=1).astype(jnp.bfloat16)
    y = _dot(mix, w_ref[...])
    r = lax.rsqrt(jnp.mean(y * y, axis=-1, keepdims=True) + RMS_EPS)
    y_ref[0] = x_ref[0] + y * r * g_ref[...]


def _out(x, oa, oc, os_, ow, w, g):
    B, S, _ = x.shape
    rows = lambda wd: pl.BlockSpec((1, ROWS, wd), lambda b, i: (b, i, 0))
    return pl.pallas_call(
        _out_kernel,
        grid=(B, S // ROWS),
        in_specs=[rows(D_MODEL), rows(MOBA_W), rows(NSA_W), rows(NSA_W), rows(NSA_W),
                  pl.BlockSpec(w.shape, lambda b, i: (0, 0)),
                  pl.BlockSpec((1, D_MODEL), lambda b, i: (0, 0))],
        out_specs=rows(D_MODEL),
        out_shape=jax.ShapeDtypeStruct((B, S, D_MODEL), jnp.float32),
        compiler_params=pltpu.CompilerParams(
            dimension_semantics=("arbitrary", "arbitrary"), vmem_limit_bytes=VMEM_LIMIT),
        name="out",
    )(x, oa, oc, os_, ow, w, g)


def _compress_weights(w1, w2):
    half = (CMP_LEN // 2) * HEAD_DIM
    w1r = w1.reshape(2, CMP_LEN // 2, HEAD_DIM, CMP_HIDDEN)
    z = jnp.zeros_like(w1r)
    g0 = jnp.concatenate([w1r, z], axis=-1)
    g1 = jnp.concatenate([z, w1r], axis=-1)
    both = jnp.stack([g0, g1], axis=2)
    both = both.reshape(2, half * NSA_GROUPS, NSA_GROUPS * CMP_HIDDEN).astype(jnp.bfloat16)
    zz = jnp.zeros_like(w2)
    w2bd = jnp.concatenate([jnp.concatenate([w2, zz], axis=1),
                            jnp.concatenate([zz, w2], axis=1)], axis=0).astype(jnp.bfloat16)
    return both[0], both[1], w2bd


def _overlap_T(n_cmp_pad, n_slc, rows):
    c = np.arange(n_cmp_pad)[None, :] * CMP_STRIDE
    j = np.arange(rows)[:, None] * SLC_BLOCK
    ov = (c < j + SLC_BLOCK) & (c + CMP_LEN > j) & (np.arange(rows)[:, None] < n_slc)
    return jnp.asarray(ov.astype(np.float32), jnp.bfloat16)


def _layer(x, pre_g, post_g, w_in, pos_k, pos_v, w_k1, w_k2, w_v1, w_v2, w_out):
    B, S, _ = x.shape
    nq = S // TILE
    n_cmp = (S - CMP_LEN) // CMP_STRIDE + 1
    n_slc = S // SLC_BLOCK
    C = S // CMP_STRIDE
    wn, wt = _projection_weights(w_in)
    (ka, kmean, kcm, vcm, ksl, kwi, qaT, vaT, qbT, vslT, vwiT,
     szaT, szbT, gateT) = _proj(x, pre_g.reshape(1, D_MODEL), wn, wt)

    wkt, wkb, w2k = _compress_weights(w_k1, w_k2)
    wvt, wvb, w2v = _compress_weights(w_v1, w_v2)
    chunk = CMP_STRIDE * KV_W
    kc, vcT = _compress(
        kcm.reshape(B, C, chunk), vcm.reshape(B, C, chunk), wkt, wkb, wvt, wvb, w2k, w2v,
        pos_k.reshape(1, CMP_LEN * HEAD_DIM).astype(jnp.bfloat16),
        pos_v.reshape(1, CMP_LEN * HEAD_DIM).astype(jnp.bfloat16),
        w_k1.astype(jnp.bfloat16), w_v1.astype(jnp.bfloat16), n_cmp)

    oc, selT = _cmp(qbT, kc, vcT, _overlap_T(C, n_slc, SLC_BLOCK), _cmp_query_aug_table(),
                    szbT, gateT, n_slc)
    oa = _moba(qaT, ka, vaT, kmean.reshape(B, nq, MOBA_W), _key_aug_table(nq, 16, 1), szaT)
    osl = _slc(qbT, ksl, vslT, selT, _key_aug_table(nq, selT.shape[2], TILE // SLC_BLOCK),
               szbT, gateT)
    ow = _win(qbT, kwi, vwiT, _win_key_aug_table(), szbT, gateT)

    w_o = jnp.concatenate([w_out[:MOBA_W], _to_pair_slabs(w_out[MOBA_W:], 0)],
                          axis=0).astype(jnp.bfloat16)
    return _out(x, oa, oc, osl, ow, w_o, post_g.reshape(1, D_MODEL))


def kernel(x, pre_norm_g, post_norm_g, w_in, cmp_pos_k, cmp_pos_v,
           w_cmp_k1, w_cmp_k2, w_cmp_v1, w_cmp_v2, w_out):
    for l in range(pre_norm_g.shape[0]):
        x = _layer(x, pre_norm_g[l], post_norm_g[l], w_in[l], cmp_pos_k[l], cmp_pos_v[l],
                   w_cmp_k1[l], w_cmp_k2[l], w_cmp_v1[l], w_cmp_v2[l], w_out[l])
    return x
```

```python
import functools

import numpy as np
import jax
import jax.numpy as jnp
from jax import lax
from jax.experimental import pallas as pl
from jax.experimental.pallas import tpu as pltpu

D_MODEL = 1024
HEAD_DIM = 64
N_HEADS = 8
N_PAIRS = N_HEADS // 2
NSA_GROUPS = 2
NSA_HPG = N_HEADS // NSA_GROUPS
MOBA_BLOCK = 256
MOBA_TOPK = 3
CMP_LEN = 32
CMP_STRIDE = 16
CMP_HIDDEN = 128
SLC_BLOCK = 64
SLC_TOPN = 16
WINDOW = 512
RMS_EPS = 1e-6
NEG_INF = -1e30
FORCED_SCORE = 1e9
SCALE = HEAD_DIM ** -0.5
LOG2E = float(np.log2(np.e))

TILE = 256
QT = 2 * TILE
ROWS = 2 * TILE
LANES = 128
HALF = HEAD_DIM
SUBLANES = 8
MOBA_W = N_HEADS * HEAD_DIM
NSA_W = N_HEADS * HEAD_DIM
KV_W = NSA_GROUPS * HEAD_DIM
N_GATES = 3 * N_HEADS
GATE_ROWS = 32
SLOPES = tuple(2.0 ** (-(i + 1)) for i in range(N_HEADS))
BRANCH_CMP, BRANCH_SLC, BRANCH_WIN = 0, 1, 2
BRANCH_DTYPE = jnp.bfloat16

VMEM_LIMIT = 48 * 1024 * 1024

_OFF = dict(qa=0, ka=512, va=1024, za=1536, qb=2048, kcm=2560, vcm=2688,
            ksl=2816, vsl=2944, kwi=3072, vwi=3200, gate=3328, zb=3352)
_NAT = dict(ka=(0, 512), kcm=(512, 640), vcm=(640, 768), ksl=(768, 896), kwi=(896, 1024))
N_NAT = 1024
_TR = dict(qa=(0, 512), va=(512, 1024), qb=(1024, 1536), vsl=(1536, 1664),
           vwi=(1664, 1792), za=(1792, 2304), zb=(2304, 2816), gate=(2816, 2848))
N_TR = 2848


def _to_pair_slabs(w, axis):
    w = jnp.moveaxis(w, axis, -1)
    lead = w.shape[:-1]
    w = w.reshape(lead + (NSA_GROUPS, NSA_HPG, HEAD_DIM))
    w = jnp.swapaxes(w, -3, -2).reshape(lead + (NSA_W,))
    return jnp.moveaxis(w, -1, axis)


def _projection_weights(w_in):
    col = lambda name, width: w_in[:, _OFF[name]:_OFF[name] + width]
    natural = jnp.concatenate(
        [col("ka", MOBA_W), col("kcm", KV_W), col("vcm", KV_W), col("ksl", KV_W), col("kwi", KV_W)],
        axis=1)
    transposed = jnp.concatenate(
        [col("qa", MOBA_W), col("va", MOBA_W), _to_pair_slabs(col("qb", NSA_W), 1),
         col("vsl", KV_W), col("vwi", KV_W), col("za", MOBA_W), _to_pair_slabs(col("zb", NSA_W), 1),
         col("gate", N_GATES), jnp.zeros((D_MODEL, GATE_ROWS - N_GATES), w_in.dtype)], axis=1)
    return natural.astype(jnp.bfloat16), transposed.T.astype(jnp.bfloat16)


def _gate_row(g, p, branch):
    return (g * NSA_HPG + p) * 3 + branch


def _sigmoid(x):
    return 1.0 / (1.0 + jnp.exp(-x))


def _silu(x):
    return x * _sigmoid(x)


def _dot(a, b):
    return jnp.dot(a, b, preferred_element_type=jnp.float32)


def _proj_kernel(x_ref, g_ref, wn_ref, wt_ref,
                 ka_ref, kmean_ref, kcm_ref, vcm_ref, ksl_ref, kwi_ref,
                 qaT_ref, vaT_ref, qbT_ref, vslT_ref, vwiT_ref,
                 szaT_ref, szbT_ref, gateT_ref):
    x = x_ref[0]
    r = lax.rsqrt(jnp.mean(x * x, axis=-1, keepdims=True) + RMS_EPS)
    h = (x * r * g_ref[...]).astype(jnp.bfloat16)

    natural = _dot(h, wn_ref[...])

    def nat(name):
        a, b = _NAT[name]
        return natural[:, a:b]

    ka = nat("ka")
    for blk in range(x.shape[0] // MOBA_BLOCK):
        kmean_ref[0, blk] = jnp.mean(ka[blk * MOBA_BLOCK:(blk + 1) * MOBA_BLOCK],
                                     axis=0, keepdims=True)
    for p in range(N_PAIRS):
        ka_ref[0, p] = ka[:, p * LANES:(p + 1) * LANES].astype(jnp.bfloat16)
    kcm_ref[0] = nat("kcm").astype(jnp.bfloat16)
    vcm_ref[0] = nat("vcm").astype(jnp.bfloat16)
    ksl_ref[0] = nat("ksl").astype(jnp.bfloat16)
    kwi_ref[0] = nat("kwi").astype(jnp.bfloat16)

    def tr(name):
        a, b = _TR[name]
        return lax.dot_general(wt_ref[a:b, :], h, (((1,), (1,)), ((), ())),
                               preferred_element_type=jnp.float32)

    qa = tr("qa") * (SCALE * LOG2E)
    va = tr("va")
    qb = tr("qb") * (SCALE * LOG2E)
    sza = _silu(tr("za"))
    szb = _silu(tr("zb"))
    for p in range(N_PAIRS):
        sl = slice(p * LANES, (p + 1) * LANES)
        qaT_ref[0, p] = qa[sl].astype(jnp.bfloat16)
        vaT_ref[0, p] = va[sl].astype(jnp.bfloat16)
        qbT_ref[0, p] = qb[sl].astype(jnp.bfloat16)
        szaT_ref[0, p] = sza[sl]
        szbT_ref[0, p] = szb[sl]
    vslT_ref[0] = tr("vsl").astype(jnp.bfloat16)
    vwiT_ref[0] = tr("vwi").astype(jnp.bfloat16)
    gateT_ref[0] = _sigmoid(tr("gate"))


def _proj(x, g, wn, wt):
    B, S, _ = x.shape
    nq = S // TILE
    bf, f32 = jnp.bfloat16, jnp.float32
    slab_nat = lambda: pl.BlockSpec((1, N_PAIRS, ROWS, LANES), lambda b, i: (b, 0, i, 0))
    slab_tr = lambda: pl.BlockSpec((1, N_PAIRS, LANES, ROWS), lambda b, i: (b, 0, 0, i))
    rows = lambda w: pl.BlockSpec((1, ROWS, w), lambda b, i: (b, i, 0))
    chans = lambda c: pl.BlockSpec((1, c, ROWS), lambda b, i: (b, 0, i))
    out_shape = [
        jax.ShapeDtypeStruct((B, N_PAIRS, S, LANES), bf),
        jax.ShapeDtypeStruct((B, nq, 1, MOBA_W), f32),
        jax.ShapeDtypeStruct((B, S, KV_W), bf),
        jax.ShapeDtypeStruct((B, S, KV_W), bf),
        jax.ShapeDtypeStruct((B, S, KV_W), bf),
        jax.ShapeDtypeStruct((B, S, KV_W), bf),
        jax.ShapeDtypeStruct((B, N_PAIRS, LANES, S), bf),
        jax.ShapeDtypeStruct((B, N_PAIRS, LANES, S), bf),
        jax.ShapeDtypeStruct((B, N_PAIRS, LANES, S), bf),
        jax.ShapeDtypeStruct((B, KV_W, S), bf),
        jax.ShapeDtypeStruct((B, KV_W, S), bf),
        jax.ShapeDtypeStruct((B, N_PAIRS, LANES, S), f32),
        jax.ShapeDtypeStruct((B, N_PAIRS, LANES, S), f32),
        jax.ShapeDtypeStruct((B, GATE_ROWS, S), f32),
    ]
    out_specs = [
        slab_nat(),
        pl.BlockSpec((1, ROWS // MOBA_BLOCK, 1, MOBA_W), lambda b, i: (b, i, 0, 0)),
        rows(KV_W), rows(KV_W), rows(KV_W), rows(KV_W),
        slab_tr(), slab_tr(), slab_tr(),
        chans(KV_W), chans(KV_W),
        slab_tr(), slab_tr(),
        chans(GATE_ROWS),
    ]
    return pl.pallas_call(
        _proj_kernel,
        grid=(B, S // ROWS),
        in_specs=[
            pl.BlockSpec((1, ROWS, D_MODEL), lambda b, i: (b, i, 0)),
            pl.BlockSpec((1, D_MODEL), lambda b, i: (0, 0)),
            pl.BlockSpec((D_MODEL, N_NAT), lambda b, i: (0, 0)),
            pl.BlockSpec((N_TR, D_MODEL), lambda b, i: (0, 0)),
        ],
        out_specs=out_specs,
        out_shape=out_shape,
        compiler_params=pltpu.CompilerParams(
            dimension_semantics=("arbitrary", "arbitrary"),
            vmem_limit_bytes=VMEM_LIMIT),
        name="proj",
    )(x, g, wn, wt)


def _compress_kernel(n_cmp, xk_ref, xv_ref, wkt_ref, wkb_ref, wvt_ref, wvb_ref,
                     w2k_ref, w2v_ref, pek_ref, pev_ref, w1k_ref, w1v_ref,
                     kc_ref, vcT_ref):
    C = xk_ref.shape[1]
    row = lax.broadcasted_iota(jnp.int32, (C, KV_W), 0)

    def phi(x_ref, wt_ref, wb_ref, w2_ref, pe_ref, w1_ref):
        x = x_ref[0]
        top = _dot(x, wt_ref[...])
        bot = _dot(x, wb_ref[...])
        peb = _dot(jnp.broadcast_to(pe_ref[...], (8, pe_ref.shape[1])), w1_ref[...])[0:1]
        peb = jnp.concatenate([peb, peb], axis=1)
        hid = top + pltpu.roll(bot, C - 1, 0) + peb
        out = _dot(_silu(hid).astype(jnp.bfloat16), w2_ref[...])
        return jnp.where(row < n_cmp, out, 0.0)

    kc_ref[0] = phi(xk_ref, wkt_ref, wkb_ref, w2k_ref, pek_ref, w1k_ref).astype(jnp.bfloat16)
    vc = phi(xv_ref, wvt_ref, wvb_ref, w2v_ref, pev_ref, w1v_ref)
    vcT_ref[0] = vc.T.astype(jnp.bfloat16)


def _compress(xk, xv, wkt, wkb, wvt, wvb, w2k, w2v, pek, pev, w1k, w1v, n_cmp):
    B, C, F = xk.shape
    full = lambda a: pl.BlockSpec(a.shape, lambda b: (0,) * a.ndim)
    return pl.pallas_call(
        functools.partial(_compress_kernel, n_cmp),
        grid=(B,),
        in_specs=[pl.BlockSpec((1, C, F), lambda b: (b, 0, 0)),
                  pl.BlockSpec((1, C, F), lambda b: (b, 0, 0)),
                  full(wkt), full(wkb), full(wvt), full(wvb), full(w2k), full(w2v),
                  full(pek), full(pev), full(w1k), full(w1v)],
        out_specs=[pl.BlockSpec((1, C, KV_W), lambda b: (b, 0, 0)),
                   pl.BlockSpec((1, KV_W, C), lambda b: (b, 0, 0))],
        out_shape=[jax.ShapeDtypeStruct((B, C, KV_W), jnp.bfloat16),
                   jax.ShapeDtypeStruct((B, KV_W, C), jnp.bfloat16)],
        compiler_params=pltpu.CompilerParams(
            dimension_semantics=("arbitrary",), vmem_limit_bytes=VMEM_LIMIT),
        name="compress",
    )(xk, xv, wkt, wkb, wvt, wvb, w2k, w2v, pek, pev, w1k, w1v)


def _keep_half(qT, half):
    z = jnp.zeros((HALF, qT.shape[1]), qT.dtype)
    if half == 0:
        return jnp.concatenate([qT[0:HALF], z], axis=0)
    return jnp.concatenate([z, qT[HALF:2 * HALF]], axis=0)


def _rank_rows(vals, n_rows):
    R, T = vals.shape
    tiles = [vals[a:a + SUBLANES] for a in range(0, R, SUBLANES)]
    ranks = [jnp.zeros((SUBLANES, T), jnp.int32) for _ in tiles]
    j_in = lax.broadcasted_iota(jnp.int32, (SUBLANES, T), 0)
    for m in range(n_rows):
        vm = vals[m:m + 1, :]
        for a, tile in enumerate(tiles):
            lo = a * SUBLANES
            if lo > m:
                beats = vm >= tile
            elif lo + SUBLANES - 1 <= m:
                beats = vm > tile
            else:
                beats = (vm > tile) | ((vm == tile) & (j_in > m - lo))
            ranks[a] = ranks[a] + beats.astype(jnp.int32)
    return jnp.concatenate(ranks, axis=0)


def _query_tile(cols, tile0):
    return tile0 + lax.shift_right_logical(cols, int(np.log2(TILE)))


def _split_bf16(x):
    hi = x.astype(jnp.bfloat16).astype(jnp.float32)
    return hi, x - hi


def _slope_parts(slope):
    c = np.float32(slope * LOG2E)
    hi = np.asarray(c, jnp.bfloat16).astype(np.float32)
    lo = np.asarray(c - hi, jnp.bfloat16).astype(np.float32)
    return float(hi), float(lo)


AUG_ROWS = 16


def _pad_rows(x, n_rows):
    return jnp.concatenate([x, jnp.zeros((n_rows - x.shape[0], x.shape[1]), x.dtype)], axis=0)


def _alibi_rows(T, slope, tile0):
    r = lax.broadcasted_iota(jnp.int32, (AUG_ROWS, T), 0)
    c = lax.broadcasted_iota(jnp.int32, (AUG_ROWS, T), 1)
    t_abs = ((c & (TILE - 1)) + _query_tile(c, tile0) * TILE).astype(jnp.float32)
    c_hi, c_lo = _slope_parts(slope)
    w_hi, w_lo = _split_bf16(np.float32(slope * LOG2E) * t_abs)
    out = jnp.where(r == 0, -w_hi, 0.0)
    out = jnp.where(r == 1, -w_lo, out)
    out = jnp.where((r == 2) | (r == 4), c_hi, out)
    out = jnp.where((r == 3) | (r == 5), c_lo, out)
    return out


def _key_aug_table(nq, n_sel, blocks_per_tile):
    t = np.zeros((nq, TILE, LANES), np.float32)
    krel = np.arange(TILE)
    for j in range(nq):
        if n_sel:
            blk = j * blocks_per_tile + krel // (TILE // blocks_per_tile)
            t[j, krel, blk] = 1.0
        t[j, :, n_sel:n_sel + 2] = 1.0
        t[j, :, n_sel + 2:n_sel + 4] = krel[:, None]
        t[j, :, n_sel + 4:n_sel + 6] = TILE * j
    return jnp.asarray(t.reshape(nq // 2, 2 * TILE, LANES), jnp.bfloat16)


ONES_ROWS = 16
ACC_ROWS = HALF + ONES_ROWS


WIN_TILES = WINDOW // TILE
WIN_KEYS = (WIN_TILES + QT // TILE) * TILE


def _win_key_aug_table():
    t = np.zeros((WIN_KEYS, LANES), np.float32)
    for slot in range(WIN_KEYS // TILE):
        rows = slice(slot * TILE, (slot + 1) * TILE)
        t[rows, 0:2] = 1.0
        t[rows, 2:4] = np.arange(TILE)[:, None]
        t[rows, 4:6] = TILE * (WIN_TILES - slot)
        if slot < WIN_TILES:
            t[rows, 6] = 1.0
    return jnp.asarray(t, jnp.bfloat16)


def _value_rows(vT_h):
    return jnp.concatenate([vT_h, jnp.ones((ONES_ROWS, vT_h.shape[1]), vT_h.dtype)], axis=0)


def _stage_scores(h, score_fn, blocks, cols, s_ref, mx_ref):
    c0, c1 = cols
    base = blocks[0][0]
    s_all = score_fn(h, base, blocks[-1][1], c0, c1)
    mx = None
    for r0, r1, keep in blocks:
        s = s_all[r0 - base:r1 - base]
        if keep is not None:
            s = jnp.where(keep, s, NEG_INF)
        s_ref[h, r0:r1, c0:c1] = s
        part = jnp.max(s, axis=0, keepdims=True)
        mx = part if mx is None else jnp.maximum(mx, part)
    mx_ref[h, :, c0:c1] = mx


def _consume_scores(h, value_fn, rows, cols, s_ref, mx_ref, m_ref, acc_ref):
    (r0, r1), (c0, c1) = rows, cols
    m_prev = m_ref[h, :, c0:c1]
    m_new = jnp.maximum(m_prev, mx_ref[h, :, c0:c1])
    alpha = jnp.exp2(m_prev - m_new)
    p = jnp.exp2(s_ref[h, r0:r1, c0:c1] - m_new).astype(jnp.bfloat16)
    acc_ref[h, :, c0:c1] = alpha * acc_ref[h, :, c0:c1] + _dot(value_fn(h, r0, r1), p)
    m_ref[h, :, c0:c1] = m_new


def _attend_heads(score_fn, value_fn, parts, s_ref, mx_ref, m_ref, acc_ref):
    for h in range(N_HEADS):
        for cols, blocks in parts:
            _stage_scores(h, score_fn, blocks, cols, s_ref, mx_ref)
    for h in range(N_HEADS):
        for cols, blocks in parts:
            rows = (min(b[0] for b in blocks), max(b[1] for b in blocks))
            _consume_scores(h, value_fn, rows, cols, s_ref, mx_ref, m_ref, acc_ref)


def _init_state(m_ref, acc_ref):
    m_ref[...] = jnp.full(m_ref.shape, NEG_INF, jnp.float32)
    acc_ref[...] = jnp.zeros(acc_ref.shape, jnp.float32)


def _tile_iotas(T=TILE):
    k = lax.broadcasted_iota(jnp.int32, (TILE, T), 0)
    t = lax.broadcasted_iota(jnp.int32, (TILE, T), 1)
    return k, t


def _pair_step_parts(diag):
    if not diag:
        return [((0, QT), [(0, 2 * TILE, None)])]
    k, t = _tile_iotas()
    tri = k <= t
    return [((0, TILE), [(0, TILE, tri)]),
            ((TILE, QT), [(0, TILE, None), (TILE, 2 * TILE, tri)])]


def _store_pair(o_ref, p, o0T, o1T, szT, gates):
    if gates is not None:
        o0T = o0T * gates[0]
        o1T = o1T * gates[1]
    oT = jnp.concatenate([o0T, o1T], axis=0) * szT
    o_ref[0, :, p * LANES:(p + 1) * LANES] = oT.T.astype(o_ref.dtype)


def _finalize_pair(o_ref, acc_ref, p, szT, gates=None):
    a0 = acc_ref[2 * p]
    a1 = acc_ref[2 * p + 1]
    _store_pair(o_ref, p, a0[0:HALF] / a0[HALF:HALF + 1], a1[0:HALF] / a1[HALF:HALF + 1],
                szT, gates)


def _nsa_gates(gate_ref, p, branch):
    return [gate_ref[0, _gate_row(g, p, branch):_gate_row(g, p, branch) + 1, :]
            for g in range(NSA_GROUPS)]


def _pair_tables(n_pairs):
    it, jt = [], []
    for ii in range(n_pairs):
        for jj in range(ii + 1):
            it.append(ii)
            jt.append(jj)
    return jnp.asarray(it, jnp.int32), jnp.asarray(jt, jnp.int32)


def _state_scratch(n_keys, T=TILE):
    return [
        pltpu.VMEM((N_HEADS, 2 * LANES, T), jnp.bfloat16),
        pltpu.VMEM((N_HEADS, 1, T), jnp.float32),
        pltpu.VMEM((N_HEADS, ACC_ROWS, T), jnp.float32),
        pltpu.VMEM((N_HEADS, n_keys, T), jnp.float32),
        pltpu.VMEM((N_HEADS, 1, T), jnp.float32),
    ]


def _moba_kernel(nq, it_ref, jt_ref, qT_ref, k_ref, vT_ref, kmean_ref, kaug_ref, sz_ref, o_ref,
                 qaug_ref, m_ref, acc_ref, s_ref, mx_ref):
    s = pl.program_id(1)
    ii = it_ref[s]
    jj = jt_ref[s]
    n_sel = 16

    @pl.when(jj == 0)
    def _():
        _init_state(m_ref, acc_ref)
        n_iota = lax.broadcasted_iota(jnp.int32, (nq, QT), 0)
        own = _query_tile(lax.broadcasted_iota(jnp.int32, (nq, QT), 1), 2 * ii)
        for p in range(N_PAIRS):
            qT = qT_ref[0, p]
            km = kmean_ref[0, :, p * LANES:(p + 1) * LANES].astype(jnp.bfloat16)
            for hh in range(2):
                h = 2 * p + hh
                qm = _keep_half(qT, hh)
                gate = jnp.where(n_iota < own, _dot(km, qm), NEG_INF)
                rank = _rank_rows(gate, nq)
                sel = ((rank < MOBA_TOPK) & (n_iota < own)) | (n_iota == own)
                selneg = jnp.where(sel, 0.0, NEG_INF)
                if nq < n_sel:
                    selneg = jnp.concatenate(
                        [selneg, jnp.zeros((n_sel - nq, QT), jnp.float32)], axis=0)
                ali = _alibi_rows(QT, SLOPES[h], 2 * ii)
                aug = jnp.concatenate([selneg, ali], axis=0).astype(jnp.bfloat16)
                qaug_ref[h] = _pad_rows(jnp.concatenate([qm, aug], axis=0), 2 * LANES)

    def score(h, r0, r1, c0, c1):
        kfull = jnp.concatenate([k_ref[0, h // 2, r0:r1], kaug_ref[0, r0:r1]], axis=1)
        return _dot(kfull, qaug_ref[h, :, c0:c1])

    def value(h, r0, r1):
        return _value_rows(vT_ref[0, h // 2, (h % 2) * HALF:(h % 2 + 1) * HALF, r0:r1])

    def step(diag):
        _attend_heads(score, value, _pair_step_parts(diag), s_ref, mx_ref, m_ref, acc_ref)
        if diag:
            for p in range(N_PAIRS):
                _finalize_pair(o_ref, acc_ref, p, sz_ref[0, p])

    @pl.when(jj < ii)
    def _():
        step(False)

    @pl.when(jj == ii)
    def _():
        step(True)


def _moba(qT, k, vT, kmean, kaug, szT):
    B, _, _, S = qT.shape
    nq = S // TILE
    it, jt = _pair_tables(S // QT)
    return pl.pallas_call(
        functools.partial(_moba_kernel, nq),
        grid_spec=pltpu.PrefetchScalarGridSpec(
            num_scalar_prefetch=2,
            grid=(B, int(it.shape[0])),
            in_specs=[
                pl.BlockSpec((1, N_PAIRS, LANES, QT), lambda b, s, it, jt: (b, 0, 0, it[s])),
                pl.BlockSpec((1, N_PAIRS, 2 * TILE, LANES), lambda b, s, it, jt: (b, 0, jt[s], 0)),
                pl.BlockSpec((1, N_PAIRS, LANES, 2 * TILE), lambda b, s, it, jt: (b, 0, 0, jt[s])),
                pl.BlockSpec((1, nq, MOBA_W), lambda b, s, it, jt: (b, 0, 0)),
                pl.BlockSpec((1, 2 * TILE, LANES), lambda b, s, it, jt: (jt[s], 0, 0)),
                pl.BlockSpec((1, N_PAIRS, LANES, QT), lambda b, s, it, jt: (b, 0, 0, it[s])),
            ],
            out_specs=pl.BlockSpec((1, QT, MOBA_W), lambda b, s, it, jt: (b, it[s], 0)),
            scratch_shapes=_state_scratch(2 * TILE, QT),
        ),
        out_shape=jax.ShapeDtypeStruct((B, S, MOBA_W), BRANCH_DTYPE),
        compiler_params=pltpu.CompilerParams(
            dimension_semantics=("arbitrary", "arbitrary"), vmem_limit_bytes=VMEM_LIMIT),
        name="moba",
    )(it, jt, qT, k, vT, kmean, kaug, szT)


CMP_BAND = TILE // CMP_STRIDE + 1
RANK_ROWS_STEP = 16


def _cmp_query_aug_table():
    t = np.zeros((LANES, TILE), np.float32)
    trel = np.arange(TILE)
    for u in range(CMP_BAND):
        t[u, trel < CMP_STRIDE * (u - 1) + CMP_LEN - 1] = NEG_INF
    t[CMP_BAND, :] = NEG_INF
    return jnp.asarray(t, jnp.bfloat16)


def _cmp_kernel(n_slc, qT_ref, kc_ref, vcT_ref, ovT_ref, qaug_ref, sz_ref, gate_ref,
                o_ref, sel_ref, s_ref, mx_ref):
    i = pl.program_id(1)
    C = kc_ref.shape[1]
    c_rel = (lax.broadcasted_iota(jnp.int32, (C, LANES), 0)
             - (TILE // CMP_STRIDE) * i + 1)
    u = lax.broadcasted_iota(jnp.int32, (C, LANES), 1)
    kaug = jnp.where((c_rel == u) & (u < CMP_BAND), 1.0, 0.0)
    kaug = jnp.where((u == CMP_BAND) & (c_rel >= CMP_BAND), 1.0, kaug)
    kfull = jnp.concatenate([kc_ref[0], kaug.astype(jnp.bfloat16)], axis=1)
    qaug = qaug_ref[...]

    for p in range(N_PAIRS):
        qT = qT_ref[0, p]
        for g in range(NSA_GROUPS):
            h = 2 * p + g
            s = _dot(kfull, jnp.concatenate([_keep_half(qT, g), qaug], axis=0))
            s_ref[h] = s
            mx_ref[h] = jnp.max(s, axis=0, keepdims=True)

    t_abs = i * TILE + lax.broadcasted_iota(jnp.int32, (1, TILE), 1)
    seen = t_abs >= CMP_LEN - 1
    ones = jnp.ones((ONES_ROWS, C), jnp.bfloat16)
    vals = [jnp.concatenate([vcT_ref[0, g * HALF:(g + 1) * HALF], ones, ovT_ref[...]], axis=0)
            for g in range(NSA_GROUPS)]
    R = sel_ref.shape[2]
    imp = [jnp.zeros((R, TILE), jnp.float32) for _ in range(NSA_GROUPS)]
    for p in range(N_PAIRS):
        outs = []
        for g in range(NSA_GROUPS):
            h = 2 * p + g
            pr = jnp.exp2(s_ref[h] - mx_ref[h]).astype(jnp.bfloat16)
            acc = _dot(vals[g], pr)
            inv = jnp.where(seen, 1.0 / acc[HALF:HALF + 1], 0.0)
            outs.append(acc[0:HALF] * inv)
            imp[g] = imp[g] + acc[ACC_ROWS:ACC_ROWS + R] * inv
        _store_pair(o_ref, p, outs[0], outs[1], sz_ref[0, p], _nsa_gates(gate_ref, p, BRANCH_CMP))

    j_iota = lax.broadcasted_iota(jnp.int32, (R, TILE), 0)
    t_q = i * TILE + lax.broadcasted_iota(jnp.int32, (R, TILE), 1)
    own = lax.shift_right_logical(t_q, int(np.log2(SLC_BLOCK)))
    forced = (j_iota == 0) | (j_iota == own) | (j_iota == own - 1)
    causal = j_iota <= own

    def select(n_rows):
        for g in range(NSA_GROUPS):
            v = jnp.where(forced, FORCED_SCORE, jnp.where(causal, imp[g], NEG_INF))[0:n_rows]
            rank = _rank_rows(v, min(n_rows, n_slc))
            sel = (rank < min(SLC_TOPN, n_slc)) & causal[0:n_rows]
            selneg = jnp.where(sel, 0.0, NEG_INF)
            if n_rows < R:
                selneg = jnp.concatenate(
                    [selneg, jnp.full((R - n_rows, TILE), NEG_INF, jnp.float32)], axis=0)
            sel_ref[0, g] = selneg.astype(jnp.bfloat16)

    blocks_per_tile = TILE // SLC_BLOCK
    variants = list(range(RANK_ROWS_STEP, R, RANK_ROWS_STEP)) + [R]
    for idx, n_rows in enumerate(variants):
        lo = 0 if idx == 0 else variants[idx - 1] // blocks_per_tile
        hi = n_rows // blocks_per_tile

        @pl.when((i >= lo) & (i < hi) if n_rows < R else i >= lo)
        def _():
            select(n_rows)


def _cmp(qbT, kc, vcT, ovT, qaug, szT, gateT, n_slc):
    B, _, _, S = qbT.shape
    C = kc.shape[1]
    nq = S // TILE
    R = ovT.shape[0]
    return pl.pallas_call(
        functools.partial(_cmp_kernel, n_slc),
        grid=(B, nq),
        in_specs=[
            pl.BlockSpec((1, N_PAIRS, LANES, TILE), lambda b, i: (b, 0, 0, i)),
            pl.BlockSpec((1, C, KV_W), lambda b, i: (b, 0, 0)),
            pl.BlockSpec((1, KV_W, C), lambda b, i: (b, 0, 0)),
            pl.BlockSpec(ovT.shape, lambda b, i: (0, 0)),
            pl.BlockSpec(qaug.shape, lambda b, i: (0, 0)),
            pl.BlockSpec((1, N_PAIRS, LANES, TILE), lambda b, i: (b, 0, 0, i)),
            pl.BlockSpec((1, GATE_ROWS, TILE), lambda b, i: (b, 0, i)),
        ],
        out_specs=[pl.BlockSpec((1, TILE, NSA_W), lambda b, i: (b, i, 0)),
                   pl.BlockSpec((1, NSA_GROUPS, R, TILE), lambda b, i: (b, 0, 0, i))],
        out_shape=[jax.ShapeDtypeStruct((B, S, NSA_W), BRANCH_DTYPE),
                   jax.ShapeDtypeStruct((B, NSA_GROUPS, R, S), jnp.bfloat16)],
        scratch_shapes=[pltpu.VMEM((N_HEADS, C, TILE), jnp.float32),
                        pltpu.VMEM((N_HEADS, 1, TILE), jnp.float32)],
        compiler_params=pltpu.CompilerParams(
            dimension_semantics=("arbitrary", "arbitrary"), vmem_limit_bytes=VMEM_LIMIT),
        name="cmp",
    )(qbT, kc, vcT, ovT, qaug, szT, gateT)


def _slc_kernel(it_ref, jt_ref, qT_ref, k_ref, vT_ref, sel_ref, kaug_ref, sz_ref, gate_ref, o_ref,
                qaug_ref, m_ref, acc_ref, s_ref, mx_ref):
    s = pl.program_id(1)
    ii = it_ref[s]
    jj = jt_ref[s]
    n_sel = sel_ref.shape[2]

    @pl.when(jj == 0)
    def _():
        _init_state(m_ref, acc_ref)
        for p in range(N_PAIRS):
            qT = qT_ref[0, p]
            for g in range(NSA_GROUPS):
                h = 2 * p + g
                qm = _keep_half(qT, g)
                ali = _alibi_rows(QT, SLOPES[g * NSA_HPG + p], 2 * ii)
                qaug_ref[h] = _pad_rows(
                    jnp.concatenate([qm, sel_ref[0, g], ali.astype(jnp.bfloat16)], axis=0),
                    2 * LANES)

    def score(h, r0, r1, c0, c1):
        kfull = jnp.concatenate([k_ref[0, r0:r1], kaug_ref[0, r0:r1]], axis=1)
        return _dot(kfull, qaug_ref[h, :, c0:c1])

    def value(h, r0, r1):
        return _value_rows(vT_ref[0, (h % 2) * HALF:(h % 2 + 1) * HALF, r0:r1])

    def step(diag):
        _attend_heads(score, value, _pair_step_parts(diag), s_ref, mx_ref, m_ref, acc_ref)
        if diag:
            for p in range(N_PAIRS):
                _finalize_pair(o_ref, acc_ref, p, sz_ref[0, p], _nsa_gates(gate_ref, p, BRANCH_SLC))

    @pl.when(jj < ii)
    def _():
        step(False)

    @pl.when(jj == ii)
    def _():
        step(True)


def _slc(qbT, ksl, vslT, selT, kaug, szT, gateT):
    B, _, _, S = qbT.shape
    R = selT.shape[2]
    it, jt = _pair_tables(S // QT)
    return pl.pallas_call(
        _slc_kernel,
        grid_spec=pltpu.PrefetchScalarGridSpec(
            num_scalar_prefetch=2,
            grid=(B, int(it.shape[0])),
            in_specs=[
                pl.BlockSpec((1, N_PAIRS, LANES, QT), lambda b, s, it, jt: (b, 0, 0, it[s])),
                pl.BlockSpec((1, 2 * TILE, KV_W), lambda b, s, it, jt: (b, jt[s], 0)),
                pl.BlockSpec((1, KV_W, 2 * TILE), lambda b, s, it, jt: (b, 0, jt[s])),
                pl.BlockSpec((1, NSA_GROUPS, R, QT), lambda b, s, it, jt: (b, 0, 0, it[s])),
                pl.BlockSpec((1, 2 * TILE, LANES), lambda b, s, it, jt: (jt[s], 0, 0)),
                pl.BlockSpec((1, N_PAIRS, LANES, QT), lambda b, s, it, jt: (b, 0, 0, it[s])),
                pl.BlockSpec((1, GATE_ROWS, QT), lambda b, s, it, jt: (b, 0, it[s])),
            ],
            out_specs=pl.BlockSpec((1, QT, NSA_W), lambda b, s, it, jt: (b, it[s], 0)),
            scratch_shapes=_state_scratch(2 * TILE, QT),
        ),
        out_shape=jax.ShapeDtypeStruct((B, S, NSA_W), BRANCH_DTYPE),
        compiler_params=pltpu.CompilerParams(
            dimension_semantics=("arbitrary", "arbitrary"), vmem_limit_bytes=VMEM_LIMIT),
        name="slc",
    )(it, jt, qbT, ksl, vslT, selT, kaug, szT, gateT)


def _win_kernel(qT_ref, kb_ref, kq_ref, vb_ref, vq_ref, kaug_ref, sz_ref, gate_ref,
                o_ref, qaug_ref, m_ref, acc_ref, s_ref, mx_ref):
    i = pl.program_id(1)
    _init_state(m_ref, acc_ref)
    r = lax.broadcasted_iota(jnp.int32, (AUG_ROWS, QT), 0)
    c = lax.broadcasted_iota(jnp.int32, (AUG_ROWS, QT), 1)
    trel = (c & (TILE - 1)).astype(jnp.float32)
    absent = jnp.where(i >= 1, 0.0, NEG_INF)
    for p in range(N_PAIRS):
        qT = qT_ref[0, p]
        for g in range(NSA_GROUPS):
            h = 2 * p + g
            slope = SLOPES[g * NSA_HPG + p]
            c_hi, c_lo = _slope_parts(slope)
            w_hi, w_lo = _split_bf16(np.float32(slope * LOG2E) * trel)
            aug = jnp.where(r == 0, -w_hi, 0.0)
            aug = jnp.where(r == 1, -w_lo, aug)
            aug = jnp.where(r == 2, c_hi, aug)
            aug = jnp.where(r == 3, c_lo, aug)
            aug = jnp.where(r == 4, -c_hi, aug)
            aug = jnp.where(r == 5, -c_lo, aug)
            aug = jnp.where(r == 6, absent, aug)
            qaug_ref[h] = _pad_rows(
                jnp.concatenate([_keep_half(qT, g), aug.astype(jnp.bfloat16)], axis=0), 2 * LANES)

    kfull = jnp.concatenate(
        [jnp.concatenate([kb_ref[0], kq_ref[0]], axis=0), kaug_ref[...]], axis=1)
    vT = jnp.concatenate([vb_ref[0], vq_ref[0]], axis=1)
    k, t = _tile_iotas()

    def score(h, r0, r1, c0, c1):
        return _dot(kfull[r0:r1], qaug_ref[h, :, c0:c1])

    def value(h, r0, r1):
        return _value_rows(vT[(h % 2) * HALF:(h % 2 + 1) * HALF, r0:r1])

    parts = []
    for q in range(QT // TILE):
        first, last = q * TILE, (q + WIN_TILES) * TILE
        parts.append(((q * TILE, (q + 1) * TILE),
                      [(first, first + TILE, k > t), (first + TILE, last, None),
                       (last, last + TILE, k <= t)]))
    _attend_heads(score, value, parts, s_ref, mx_ref, m_ref, acc_ref)
    for p in range(N_PAIRS):
        _finalize_pair(o_ref, acc_ref, p, sz_ref[0, p], _nsa_gates(gate_ref, p, BRANCH_WIN))


def _win(qbT, kwi, vwiT, kaug, szT, gateT):
    B, _, _, S = qbT.shape
    before = lambda i: jnp.maximum(i - 1, 0)
    return pl.pallas_call(
        _win_kernel,
        grid=(B, S // QT),
        in_specs=[
            pl.BlockSpec((1, N_PAIRS, LANES, QT), lambda b, i: (b, 0, 0, i)),
            pl.BlockSpec((1, WINDOW, KV_W), lambda b, i: (b, before(i), 0)),
            pl.BlockSpec((1, QT, KV_W), lambda b, i: (b, i, 0)),
            pl.BlockSpec((1, KV_W, WINDOW), lambda b, i: (b, 0, before(i))),
            pl.BlockSpec((1, KV_W, QT), lambda b, i: (b, 0, i)),
            pl.BlockSpec(kaug.shape, lambda b, i: (0, 0)),
            pl.BlockSpec((1, N_PAIRS, LANES, QT), lambda b, i: (b, 0, 0, i)),
            pl.BlockSpec((1, GATE_ROWS, QT), lambda b, i: (b, 0, i)),
        ],
        out_specs=pl.BlockSpec((1, QT, NSA_W), lambda b, i: (b, i, 0)),
        out_shape=jax.ShapeDtypeStruct((B, S, NSA_W), BRANCH_DTYPE),
        scratch_shapes=_state_scratch(WIN_KEYS, QT),
        compiler_params=pltpu.CompilerParams(
            dimension_semantics=("arbitrary", "arbitrary"), vmem_limit_bytes=VMEM_LIMIT),
        name="win",
    )(qbT, kwi, kwi, vwiT, vwiT, kaug, szT, gateT)


def _out_kernel(x_ref, oa_ref, oc_ref, os_ref, ow_ref, w_ref, g_ref, y_ref):
    f32 = jnp.float32
    ob = oc_ref[0].astype(f32) + os_ref[0].astype(f32) + ow_ref[0].astype(f32)
    mix = jnp.concatenate([oa_ref[0].astype(jnp.bfloat16), ob.astype(jnp.bfloat16)], axis=1)
    y = _dot(mix, w_ref[...])
    r = lax.rsqrt(jnp.mean(y * y, axis=-1, keepdims=True) + RMS_EPS)
    y_ref[0] = x_ref[0] + y * r * g_ref[...]


def _out(x, oa, oc, os_, ow, w, g):
    B, S, _ = x.shape
    rows = lambda wd: pl.BlockSpec((1, ROWS, wd), lambda b, i: (b, i, 0))
    return pl.pallas_call(
        _out_kernel,
        grid=(B, S // ROWS),
        in_specs=[rows(D_MODEL), rows(MOBA_W), rows(NSA_W), rows(NSA_W), rows(NSA_W),
                  pl.BlockSpec(w.shape, lambda b, i: (0, 0)),
                  pl.BlockSpec((1, D_MODEL), lambda b, i: (0, 0))],
        out_specs=rows(D_MODEL),
        out_shape=jax.ShapeDtypeStruct((B, S, D_MODEL), jnp.float32),
        compiler_params=pltpu.CompilerParams(
            dimension_semantics=("arbitrary", "arbitrary"), vmem_limit_bytes=VMEM_LIMIT),
        name="out",
    )(x, oa, oc, os_, ow, w, g)


def _compress_weights(w1, w2):
    half = (CMP_LEN // 2) * HEAD_DIM
    w1r = w1.reshape(2, CMP_LEN // 2, HEAD_DIM, CMP_HIDDEN)
    z = jnp.zeros_like(w1r)
    g0 = jnp.concatenate([w1r, z], axis=-1)
    g1 = jnp.concatenate([z, w1r], axis=-1)
    both = jnp.stack([g0, g1], axis=2)
    both = both.reshape(2, half * NSA_GROUPS, NSA_GROUPS * CMP_HIDDEN).astype(jnp.bfloat16)
    zz = jnp.zeros_like(w2)
    w2bd = jnp.concatenate([jnp.concatenate([w2, zz], axis=1),
                            jnp.concatenate([zz, w2], axis=1)], axis=0).astype(jnp.bfloat16)
    return both[0], both[1], w2bd


def _overlap_T(n_cmp_pad, n_slc, rows):
    c = np.arange(n_cmp_pad)[None, :] * CMP_STRIDE
    j = np.arange(rows)[:, None] * SLC_BLOCK
    ov = (c < j + SLC_BLOCK) & (c + CMP_LEN > j) & (np.arange(rows)[:, None] < n_slc)
    return jnp.asarray(ov.astype(np.float32), jnp.bfloat16)


def _layer(x, pre_g, post_g, w_in, pos_k, pos_v, w_k1, w_k2, w_v1, w_v2, w_out):
    B, S, _ = x.shape
    nq = S // TILE
    n_cmp = (S - CMP_LEN) // CMP_STRIDE + 1
    n_slc = S // SLC_BLOCK
    C = S // CMP_STRIDE
    wn, wt = _projection_weights(w_in)
    (ka, kmean, kcm, vcm, ksl, kwi, qaT, vaT, qbT, vslT, vwiT,
     szaT, szbT, gateT) = _proj(x, pre_g.reshape(1, D_MODEL), wn, wt)

    wkt, wkb, w2k = _compress_weights(w_k1, w_k2)
    wvt, wvb, w2v = _compress_weights(w_v1, w_v2)
    chunk = CMP_STRIDE * KV_W
    kc, vcT = _compress(
        kcm.reshape(B, C, chunk), vcm.reshape(B, C, chunk), wkt, wkb, wvt, wvb, w2k, w2v,
        pos_k.reshape(1, CMP_LEN * HEAD_DIM).astype(jnp.bfloat16),
        pos_v.reshape(1, CMP_LEN * HEAD_DIM).astype(jnp.bfloat16),
        w_k1.astype(jnp.bfloat16), w_v1.astype(jnp.bfloat16), n_cmp)

    oc, selT = _cmp(qbT, kc, vcT, _overlap_T(C, n_slc, SLC_BLOCK), _cmp_query_aug_table(),
                    szbT, gateT, n_slc)
    oa = _moba(qaT, ka, vaT, kmean.reshape(B, nq, MOBA_W), _key_aug_table(nq, 16, 1), szaT)
    osl = _slc(qbT, ksl, vslT, selT, _key_aug_table(nq, selT.shape[2], TILE // SLC_BLOCK),
               szbT, gateT)
    ow = _win(qbT, kwi, vwiT, _win_key_aug_table(), szbT, gateT)

    w_o = jnp.concatenate([w_out[:MOBA_W], _to_pair_slabs(w_out[MOBA_W:], 0)],
                          axis=0).astype(jnp.bfloat16)
    return _out(x, oa, oc, osl, ow, w_o, post_g.reshape(1, D_MODEL))


def kernel(x, pre_norm_g, post_norm_g, w_in, cmp_pos_k, cmp_pos_v,
           w_cmp_k1, w_cmp_k2, w_cmp_v1, w_cmp_v2, w_out):
    for l in range(pre_norm_g.shape[0]):
        x = _layer(x, pre_norm_g[l], post_norm_g[l], w_in[l], cmp_pos_k[l], cmp_pos_v[l],
                   w_cmp_k1[l], w_cmp_k2[l], w_cmp_v1[l], w_cmp_v2[l], w_out[l])
    return x
```

```python
import functools

import numpy as np
import jax
import jax.numpy as jnp
from jax import lax
from jax.experimental import pallas as pl
from jax.experimental.pallas import tpu as pltpu

D_MODEL = 1024
HEAD_DIM = 64
N_HEADS = 8
N_PAIRS = N_HEADS // 2
NSA_GROUPS = 2
NSA_HPG = N_HEADS // NSA_GROUPS
MOBA_BLOCK = 256
MOBA_TOPK = 3
CMP_LEN = 32
CMP_STRIDE = 16
CMP_HIDDEN = 128
SLC_BLOCK = 64
SLC_TOPN = 16
WINDOW = 512
RMS_EPS = 1e-6
NEG_INF = -1e30
FORCED_SCORE = 1e9
SCALE = HEAD_DIM ** -0.5
LOG2E = float(np.log2(np.e))

TILE = 256
QT = 2 * TILE
KEYS = 4 * TILE
ROWS = 2 * TILE
LANES = 128
HALF = HEAD_DIM
SUBLANES = 8
MOBA_W = N_HEADS * HEAD_DIM
NSA_W = N_HEADS * HEAD_DIM
KV_W = NSA_GROUPS * HEAD_DIM
N_GATES = 3 * N_HEADS
GATE_ROWS = 32
SLOPES = tuple(2.0 ** (-(i + 1)) for i in range(N_HEADS))
BRANCH_CMP, BRANCH_SLC, BRANCH_WIN = 0, 1, 2
BRANCH_DTYPE = jnp.bfloat16

VMEM_LIMIT = 48 * 1024 * 1024

_OFF = dict(qa=0, ka=512, va=1024, za=1536, qb=2048, kcm=2560, vcm=2688,
            ksl=2816, vsl=2944, kwi=3072, vwi=3200, gate=3328, zb=3352)
_NAT = dict(ka=(0, 512), kcm=(512, 640), vcm=(640, 768), ksl=(768, 896), kwi=(896, 1024))
N_NAT = 1024
_TR = dict(qa=(0, 512), va=(512, 1024), qb=(1024, 1536), vsl=(1536, 1664),
           vwi=(1664, 1792), za=(1792, 2304), zb=(2304, 2816), gate=(2816, 2848))
N_TR = 2848


def _to_pair_slabs(w, axis):
    w = jnp.moveaxis(w, axis, -1)
    lead = w.shape[:-1]
    w = w.reshape(lead + (NSA_GROUPS, NSA_HPG, HEAD_DIM))
    w = jnp.swapaxes(w, -3, -2).reshape(lead + (NSA_W,))
    return jnp.moveaxis(w, -1, axis)


def _projection_weights(w_in):
    col = lambda name, width: w_in[:, _OFF[name]:_OFF[name] + width]
    natural = jnp.concatenate(
        [col("ka", MOBA_W), col("kcm", KV_W), col("vcm", KV_W), col("ksl", KV_W), col("kwi", KV_W)],
        axis=1)
    transposed = jnp.concatenate(
        [col("qa", MOBA_W), col("va", MOBA_W), _to_pair_slabs(col("qb", NSA_W), 1),
         col("vsl", KV_W), col("vwi", KV_W), col("za", MOBA_W), _to_pair_slabs(col("zb", NSA_W), 1),
         col("gate", N_GATES), jnp.zeros((D_MODEL, GATE_ROWS - N_GATES), w_in.dtype)], axis=1)
    return natural.astype(jnp.bfloat16), transposed.T.astype(jnp.bfloat16)


def _gate_row(g, p, branch):
    return (g * NSA_HPG + p) * 3 + branch


def _sigmoid(x):
    return 1.0 / (1.0 + jnp.exp(-x))


def _silu(x):
    return x * _sigmoid(x)


def _dot(a, b):
    return jnp.dot(a, b, preferred_element_type=jnp.float32)


def _proj_kernel(x_ref, g_ref, wn_ref, wt_ref,
                 ka_ref, kmean_ref, kcm_ref, vcm_ref, ksl_ref, kwi_ref,
                 qaT_ref, vaT_ref, qbT_ref, vslT_ref, vwiT_ref,
                 szaT_ref, szbT_ref, gateT_ref):
    x = x_ref[0]
    r = lax.rsqrt(jnp.mean(x * x, axis=-1, keepdims=True) + RMS_EPS)
    h = (x * r * g_ref[...]).astype(jnp.bfloat16)

    natural = _dot(h, wn_ref[...])

    def nat(name):
        a, b = _NAT[name]
        return natural[:, a:b]

    ka = nat("ka")
    for blk in range(x.shape[0] // MOBA_BLOCK):
        kmean_ref[0, blk] = jnp.mean(ka[blk * MOBA_BLOCK:(blk + 1) * MOBA_BLOCK],
                                     axis=0, keepdims=True)
    for p in range(N_PAIRS):
        ka_ref[0, p] = ka[:, p * LANES:(p + 1) * LANES].astype(jnp.bfloat16)
    kcm_ref[0] = nat("kcm").astype(jnp.bfloat16)
    vcm_ref[0] = nat("vcm").astype(jnp.bfloat16)
    ksl_ref[0] = nat("ksl").astype(jnp.bfloat16)
    kwi_ref[0] = nat("kwi").astype(jnp.bfloat16)

    def tr(name):
        a, b = _TR[name]
        return lax.dot_general(wt_ref[a:b, :], h, (((1,), (1,)), ((), ())),
                               preferred_element_type=jnp.float32)

    qa = tr("qa") * (SCALE * LOG2E)
    va = tr("va")
    qb = tr("qb") * (SCALE * LOG2E)
    sza = _silu(tr("za"))
    szb = _silu(tr("zb"))
    for p in range(N_PAIRS):
        sl = slice(p * LANES, (p + 1) * LANES)
        qaT_ref[0, p] = qa[sl].astype(jnp.bfloat16)
        vaT_ref[0, p] = va[sl].astype(jnp.bfloat16)
        qbT_ref[0, p] = qb[sl].astype(jnp.bfloat16)
        szaT_ref[0, p] = sza[sl]
        szbT_ref[0, p] = szb[sl]
    vslT_ref[0] = tr("vsl").astype(jnp.bfloat16)
    vwiT_ref[0] = tr("vwi").astype(jnp.bfloat16)
    gateT_ref[0] = _sigmoid(tr("gate"))


def _proj(x, g, wn, wt):
    B, S, _ = x.shape
    nq = S // TILE
    bf, f32 = jnp.bfloat16, jnp.float32
    slab_nat = lambda: pl.BlockSpec((1, N_PAIRS, ROWS, LANES), lambda b, i: (b, 0, i, 0))
    slab_tr = lambda: pl.BlockSpec((1, N_PAIRS, LANES, ROWS), lambda b, i: (b, 0, 0, i))
    rows = lambda w: pl.BlockSpec((1, ROWS, w), lambda b, i: (b, i, 0))
    chans = lambda c: pl.BlockSpec((1, c, ROWS), lambda b, i: (b, 0, i))
    out_shape = [
        jax.ShapeDtypeStruct((B, N_PAIRS, S, LANES), bf),
        jax.ShapeDtypeStruct((B, nq, 1, MOBA_W), f32),
        jax.ShapeDtypeStruct((B, S, KV_W), bf),
        jax.ShapeDtypeStruct((B, S, KV_W), bf),
        jax.ShapeDtypeStruct((B, S, KV_W), bf),
        jax.ShapeDtypeStruct((B, S, KV_W), bf),
        jax.ShapeDtypeStruct((B, N_PAIRS, LANES, S), bf),
        jax.ShapeDtypeStruct((B, N_PAIRS, LANES, S), bf),
        jax.ShapeDtypeStruct((B, N_PAIRS, LANES, S), bf),
        jax.ShapeDtypeStruct((B, KV_W, S), bf),
        jax.ShapeDtypeStruct((B, KV_W, S), bf),
        jax.ShapeDtypeStruct((B, N_PAIRS, LANES, S), f32),
        jax.ShapeDtypeStruct((B, N_PAIRS, LANES, S), f32),
        jax.ShapeDtypeStruct((B, GATE_ROWS, S), f32),
    ]
    out_specs = [
        slab_nat(),
        pl.BlockSpec((1, ROWS // MOBA_BLOCK, 1, MOBA_W), lambda b, i: (b, i, 0, 0)),
        rows(KV_W), rows(KV_W), rows(KV_W), rows(KV_W),
        slab_tr(), slab_tr(), slab_tr(),
        chans(KV_W), chans(KV_W),
        slab_tr(), slab_tr(),
        chans(GATE_ROWS),
    ]
    return pl.pallas_call(
        _proj_kernel,
        grid=(B, S // ROWS),
        in_specs=[
            pl.BlockSpec((1, ROWS, D_MODEL), lambda b, i: (b, i, 0)),
            pl.BlockSpec((1, D_MODEL), lambda b, i: (0, 0)),
            pl.BlockSpec((D_MODEL, N_NAT), lambda b, i: (0, 0)),
            pl.BlockSpec((N_TR, D_MODEL), lambda b, i: (0, 0)),
        ],
        out_specs=out_specs,
        out_shape=out_shape,
        compiler_params=pltpu.CompilerParams(
            dimension_semantics=("arbitrary", "arbitrary"),
            vmem_limit_bytes=VMEM_LIMIT),
        name="proj",
    )(x, g, wn, wt)


def _compress_kernel(n_cmp, xk_ref, xv_ref, wkt_ref, wkb_ref, wvt_ref, wvb_ref,
                     w2k_ref, w2v_ref, pek_ref, pev_ref, w1k_ref, w1v_ref,
                     kc_ref, vcT_ref):
    C = xk_ref.shape[1]
    row = lax.broadcasted_iota(jnp.int32, (C, KV_W), 0)

    def phi(x_ref, wt_ref, wb_ref, w2_ref, pe_ref, w1_ref):
        x = x_ref[0]
        top = _dot(x, wt_ref[...])
        bot = _dot(x, wb_ref[...])
        peb = _dot(jnp.broadcast_to(pe_ref[...], (8, pe_ref.shape[1])), w1_ref[...])[0:1]
        peb = jnp.concatenate([peb, peb], axis=1)
        hid = top + pltpu.roll(bot, C - 1, 0) + peb
        out = _dot(_silu(hid).astype(jnp.bfloat16), w2_ref[...])
        return jnp.where(row < n_cmp, out, 0.0)

    kc_ref[0] = phi(xk_ref, wkt_ref, wkb_ref, w2k_ref, pek_ref, w1k_ref).astype(jnp.bfloat16)
    vc = phi(xv_ref, wvt_ref, wvb_ref, w2v_ref, pev_ref, w1v_ref)
    vcT_ref[0] = vc.T.astype(jnp.bfloat16)


def _compress(xk, xv, wkt, wkb, wvt, wvb, w2k, w2v, pek, pev, w1k, w1v, n_cmp):
    B, C, F = xk.shape
    full = lambda a: pl.BlockSpec(a.shape, lambda b: (0,) * a.ndim)
    return pl.pallas_call(
        functools.partial(_compress_kernel, n_cmp),
        grid=(B,),
        in_specs=[pl.BlockSpec((1, C, F), lambda b: (b, 0, 0)),
                  pl.BlockSpec((1, C, F), lambda b: (b, 0, 0)),
                  full(wkt), full(wkb), full(wvt), full(wvb), full(w2k), full(w2v),
                  full(pek), full(pev), full(w1k), full(w1v)],
        out_specs=[pl.BlockSpec((1, C, KV_W), lambda b: (b, 0, 0)),
                   pl.BlockSpec((1, KV_W, C), lambda b: (b, 0, 0))],
        out_shape=[jax.ShapeDtypeStruct((B, C, KV_W), jnp.bfloat16),
                   jax.ShapeDtypeStruct((B, KV_W, C), jnp.bfloat16)],
        compiler_params=pltpu.CompilerParams(
            dimension_semantics=("arbitrary",), vmem_limit_bytes=VMEM_LIMIT),
        name="compress",
    )(xk, xv, wkt, wkb, wvt, wvb, w2k, w2v, pek, pev, w1k, w1v)


def _keep_half(qT, half):
    z = jnp.zeros((HALF, qT.shape[1]), qT.dtype)
    if half == 0:
        return jnp.concatenate([qT[0:HALF], z], axis=0)
    return jnp.concatenate([z, qT[HALF:2 * HALF]], axis=0)


def _rank_rows(vals, n_rows):
    R, T = vals.shape
    tiles = [vals[a:a + SUBLANES] for a in range(0, R, SUBLANES)]
    ranks = [jnp.zeros((SUBLANES, T), jnp.int32) for _ in tiles]
    j_in = lax.broadcasted_iota(jnp.int32, (SUBLANES, T), 0)
    for m in range(n_rows):
        vm = vals[m:m + 1, :]
        for a, tile in enumerate(tiles):
            lo = a * SUBLANES
            if lo > m:
                beats = vm >= tile
            elif lo + SUBLANES - 1 <= m:
                beats = vm > tile
            else:
                beats = (vm > tile) | ((vm == tile) & (j_in > m - lo))
            ranks[a] = ranks[a] + beats.astype(jnp.int32)
    return jnp.concatenate(ranks, axis=0)


def _query_tile(cols, tile0):
    return tile0 + lax.shift_right_logical(cols, int(np.log2(TILE)))


def _split_bf16(x):
    hi = x.astype(jnp.bfloat16).astype(jnp.float32)
    return hi, x - hi


def _slope_parts(slope):
    c = np.float32(slope * LOG2E)
    hi = np.asarray(c, jnp.bfloat16).astype(np.float32)
    lo = np.asarray(c - hi, jnp.bfloat16).astype(np.float32)
    return float(hi), float(lo)


AUG_ROWS = 16


def _pad_rows(x, n_rows):
    return jnp.concatenate([x, jnp.zeros((n_rows - x.shape[0], x.shape[1]), x.dtype)], axis=0)


def _alibi_rows(T, slope, tile0):
    r = lax.broadcasted_iota(jnp.int32, (AUG_ROWS, T), 0)
    c = lax.broadcasted_iota(jnp.int32, (AUG_ROWS, T), 1)
    t_abs = ((c & (TILE - 1)) + _query_tile(c, tile0) * TILE).astype(jnp.float32)
    c_hi, c_lo = _slope_parts(slope)
    w_hi, w_lo = _split_bf16(np.float32(slope * LOG2E) * t_abs)
    out = jnp.where(r == 0, -w_hi, 0.0)
    out = jnp.where(r == 1, -w_lo, out)
    out = jnp.where((r == 2) | (r == 4), c_hi, out)
    out = jnp.where((r == 3) | (r == 5), c_lo, out)
    return out


def _key_aug_table(nq, n_sel, blocks_per_tile):
    t = np.zeros((nq, TILE, LANES), np.float32)
    krel = np.arange(TILE)
    for j in range(nq):
        if n_sel:
            blk = j * blocks_per_tile + krel // (TILE // blocks_per_tile)
            t[j, krel, blk] = 1.0
        t[j, :, n_sel:n_sel + 2] = 1.0
        t[j, :, n_sel + 2:n_sel + 4] = krel[:, None]
        t[j, :, n_sel + 4:n_sel + 6] = TILE * j
    return jnp.asarray(t.reshape(nq * TILE // KEYS, KEYS, LANES), jnp.bfloat16)


ONES_ROWS = 16
ACC_ROWS = HALF + ONES_ROWS


WIN_TILES = WINDOW // TILE
WIN_KEYS = (WIN_TILES + QT // TILE) * TILE


def _win_key_aug_table():
    t = np.zeros((WIN_KEYS, LANES), np.float32)
    for slot in range(WIN_KEYS // TILE):
        rows = slice(slot * TILE, (slot + 1) * TILE)
        t[rows, 0:2] = 1.0
        t[rows, 2:4] = np.arange(TILE)[:, None]
        t[rows, 4:6] = TILE * (WIN_TILES - slot)
        if slot < WIN_TILES:
            t[rows, 6] = 1.0
    return jnp.asarray(t, jnp.bfloat16)


def _value_rows(vT_h):
    return jnp.concatenate([vT_h, jnp.ones((ONES_ROWS, vT_h.shape[1]), vT_h.dtype)], axis=0)


def _stage_scores(h, score_fn, blocks, cols, s_ref, mx_ref):
    c0, c1 = cols
    base = blocks[0][0]
    s_all = score_fn(h, base, blocks[-1][1], c0, c1)
    mx = None
    for r0, r1, keep in blocks:
        s = s_all[r0 - base:r1 - base]
        if keep is not None:
            s = jnp.where(keep, s, NEG_INF)
        s_ref[h, r0:r1, c0:c1] = s
        part = jnp.max(s, axis=0, keepdims=True)
        mx = part if mx is None else jnp.maximum(mx, part)
    mx_ref[h, :, c0:c1] = mx


def _consume_scores(h, value_fn, rows, cols, s_ref, mx_ref, m_ref, acc_ref):
    (r0, r1), (c0, c1) = rows, cols
    m_prev = m_ref[h, :, c0:c1]
    m_new = jnp.maximum(m_prev, mx_ref[h, :, c0:c1])
    alpha = jnp.exp2(m_prev - m_new)
    p = jnp.exp2(s_ref[h, r0:r1, c0:c1] - m_new).astype(jnp.bfloat16)
    acc_ref[h, :, c0:c1] = alpha * acc_ref[h, :, c0:c1] + _dot(value_fn(h, r0, r1), p)
    m_ref[h, :, c0:c1] = m_new


def _attend_heads(score_fn, value_fn, parts, s_ref, mx_ref, m_ref, acc_ref):
    for h in range(N_HEADS):
        for cols, blocks in parts:
            _stage_scores(h, score_fn, blocks, cols, s_ref, mx_ref)
    for h in range(N_HEADS):
        for cols, blocks in parts:
            rows = (min(b[0] for b in blocks), max(b[1] for b in blocks))
            _consume_scores(h, value_fn, rows, cols, s_ref, mx_ref, m_ref, acc_ref)


def _init_state(m_ref, acc_ref):
    m_ref[...] = jnp.full(m_ref.shape, NEG_INF, jnp.float32)
    acc_ref[...] = jnp.zeros(acc_ref.shape, jnp.float32)


def _tile_iotas(T=TILE):
    k = lax.broadcasted_iota(jnp.int32, (TILE, T), 0)
    t = lax.broadcasted_iota(jnp.int32, (TILE, T), 1)
    return k, t


def _key_step_parts(kind):
    if kind == "past":
        return [((0, QT), [(0, KEYS, None)])]
    k, t = _tile_iotas()
    tri = k <= t
    own = QT if kind == "reach" else 0
    first = [(0, own, None)] if own else []
    return [((0, TILE), first + [(own, own + TILE, tri)]),
            ((TILE, QT), [(0, own + TILE, None), (own + TILE, own + QT, tri)])]


def _store_pair(o_ref, p, o0T, o1T, szT, gates):
    if gates is not None:
        o0T = o0T * gates[0]
        o1T = o1T * gates[1]
    oT = jnp.concatenate([o0T, o1T], axis=0) * szT
    o_ref[0, :, p * LANES:(p + 1) * LANES] = oT.T.astype(o_ref.dtype)


def _finalize_pair(o_ref, acc_ref, p, szT, gates=None):
    a0 = acc_ref[2 * p]
    a1 = acc_ref[2 * p + 1]
    _store_pair(o_ref, p, a0[0:HALF] / a0[HALF:HALF + 1], a1[0:HALF] / a1[HALF:HALF + 1],
                szT, gates)


def _nsa_gates(gate_ref, p, branch):
    return [gate_ref[0, _gate_row(g, p, branch):_gate_row(g, p, branch) + 1, :]
            for g in range(NSA_GROUPS)]


def _step_tables(n_q):
    it, jt = [], []
    for ii in range(n_q):
        for js in range(ii * QT // KEYS + 1):
            it.append(ii)
            jt.append(js)
    return jnp.asarray(it, jnp.int32), jnp.asarray(jt, jnp.int32)


def _key_steps(ii, js, score_fn, value_fn, finalize_fn, s_ref, mx_ref, m_ref, acc_ref):
    q0 = ii * QT
    k0 = js * KEYS
    for kind, when in (("past", k0 + KEYS <= q0), ("reach", k0 + QT == q0), ("own", k0 == q0)):
        @pl.when(when)
        def _():
            _attend_heads(score_fn, value_fn, _key_step_parts(kind), s_ref, mx_ref, m_ref, acc_ref)
            if kind != "past":
                finalize_fn()


def _state_scratch(n_keys, T=TILE):
    return [
        pltpu.VMEM((N_HEADS, 2 * LANES, T), jnp.bfloat16),
        pltpu.VMEM((N_HEADS, 1, T), jnp.float32),
        pltpu.VMEM((N_HEADS, ACC_ROWS, T), jnp.float32),
        pltpu.VMEM((N_HEADS, n_keys, T), jnp.float32),
        pltpu.VMEM((N_HEADS, 1, T), jnp.float32),
    ]


def _moba_kernel(nq, it_ref, jt_ref, qT_ref, k_ref, vT_ref, kmean_ref, kaug_ref, sz_ref, o_ref,
                 qaug_ref, m_ref, acc_ref, s_ref, mx_ref):
    s = pl.program_id(1)
    ii = it_ref[s]
    jj = jt_ref[s]
    n_sel = 16

    @pl.when(jj == 0)
    def _():
        _init_state(m_ref, acc_ref)
        n_iota = lax.broadcasted_iota(jnp.int32, (nq, QT), 0)
        own = _query_tile(lax.broadcasted_iota(jnp.int32, (nq, QT), 1), 2 * ii)
        for p in range(N_PAIRS):
            qT = qT_ref[0, p]
            km = kmean_ref[0, :, p * LANES:(p + 1) * LANES].astype(jnp.bfloat16)
            for hh in range(2):
                h = 2 * p + hh
                qm = _keep_half(qT, hh)
                gate = jnp.where(n_iota < own, _dot(km, qm), NEG_INF)
                rank = _rank_rows(gate, nq)
                sel = ((rank < MOBA_TOPK) & (n_iota < own)) | (n_iota == own)
                selneg = jnp.where(sel, 0.0, NEG_INF)
                if nq < n_sel:
                    selneg = jnp.concatenate(
                        [selneg, jnp.zeros((n_sel - nq, QT), jnp.float32)], axis=0)
                ali = _alibi_rows(QT, SLOPES[h], 2 * ii)
                aug = jnp.concatenate([selneg, ali], axis=0).astype(jnp.bfloat16)
                qaug_ref[h] = _pad_rows(jnp.concatenate([qm, aug], axis=0), 2 * LANES)

    def score(h, r0, r1, c0, c1):
        kfull = jnp.concatenate([k_ref[0, h // 2, r0:r1], kaug_ref[0, r0:r1]], axis=1)
        return _dot(kfull, qaug_ref[h, :, c0:c1])

    def value(h, r0, r1):
        return _value_rows(vT_ref[0, h // 2, (h % 2) * HALF:(h % 2 + 1) * HALF, r0:r1])

    def finalize():
        for p in range(N_PAIRS):
            _finalize_pair(o_ref, acc_ref, p, sz_ref[0, p])

    _key_steps(ii, jj, score, value, finalize, s_ref, mx_ref, m_ref, acc_ref)


def _moba(qT, k, vT, kmean, kaug, szT):
    B, _, _, S = qT.shape
    nq = S // TILE
    it, jt = _step_tables(S // QT)
    return pl.pallas_call(
        functools.partial(_moba_kernel, nq),
        grid_spec=pltpu.PrefetchScalarGridSpec(
            num_scalar_prefetch=2,
            grid=(B, int(it.shape[0])),
            in_specs=[
                pl.BlockSpec((1, N_PAIRS, LANES, QT), lambda b, s, it, jt: (b, 0, 0, it[s])),
                pl.BlockSpec((1, N_PAIRS, KEYS, LANES), lambda b, s, it, jt: (b, 0, jt[s], 0)),
                pl.BlockSpec((1, N_PAIRS, LANES, KEYS), lambda b, s, it, jt: (b, 0, 0, jt[s])),
                pl.BlockSpec((1, nq, MOBA_W), lambda b, s, it, jt: (b, 0, 0)),
                pl.BlockSpec((1, KEYS, LANES), lambda b, s, it, jt: (jt[s], 0, 0)),
                pl.BlockSpec((1, N_PAIRS, LANES, QT), lambda b, s, it, jt: (b, 0, 0, it[s])),
            ],
            out_specs=pl.BlockSpec((1, QT, MOBA_W), lambda b, s, it, jt: (b, it[s], 0)),
            scratch_shapes=_state_scratch(KEYS, QT),
        ),
        out_shape=jax.ShapeDtypeStruct((B, S, MOBA_W), BRANCH_DTYPE),
        compiler_params=pltpu.CompilerParams(
            dimension_semantics=("arbitrary", "arbitrary"), vmem_limit_bytes=VMEM_LIMIT),
        name="moba",
    )(it, jt, qT, k, vT, kmean, kaug, szT)


CMP_BAND = TILE // CMP_STRIDE + 1
RANK_ROWS_STEP = 16


def _cmp_query_aug_table():
    t = np.zeros((LANES, TILE), np.float32)
    trel = np.arange(TILE)
    for u in range(CMP_BAND):
        t[u, trel < CMP_STRIDE * (u - 1) + CMP_LEN - 1] = NEG_INF
    t[CMP_BAND, :] = NEG_INF
    return jnp.asarray(t, jnp.bfloat16)


def _cmp_kernel(n_slc, qT_ref, kc_ref, vcT_ref, ovT_ref, qaug_ref, sz_ref, gate_ref,
                o_ref, sel_ref, s_ref, mx_ref):
    i = pl.program_id(1)
    C = kc_ref.shape[1]
    c_rel = (lax.broadcasted_iota(jnp.int32, (C, LANES), 0)
             - (TILE // CMP_STRIDE) * i + 1)
    u = lax.broadcasted_iota(jnp.int32, (C, LANES), 1)
    kaug = jnp.where((c_rel == u) & (u < CMP_BAND), 1.0, 0.0)
    kaug = jnp.where((u == CMP_BAND) & (c_rel >= CMP_BAND), 1.0, kaug)
    kfull = jnp.concatenate([kc_ref[0], kaug.astype(jnp.bfloat16)], axis=1)
    qaug = qaug_ref[...]

    for p in range(N_PAIRS):
        qT = qT_ref[0, p]
        for g in range(NSA_GROUPS):
            h = 2 * p + g
            s = _dot(kfull, jnp.concatenate([_keep_half(qT, g), qaug], axis=0))
            s_ref[h] = s
            mx_ref[h] = jnp.max(s, axis=0, keepdims=True)

    t_abs = i * TILE + lax.broadcasted_iota(jnp.int32, (1, TILE), 1)
    seen = t_abs >= CMP_LEN - 1
    ones = jnp.ones((ONES_ROWS, C), jnp.bfloat16)
    vals = [jnp.concatenate([vcT_ref[0, g * HALF:(g + 1) * HALF], ones, ovT_ref[...]], axis=0)
            for g in range(NSA_GROUPS)]
    R = sel_ref.shape[2]
    imp = [jnp.zeros((R, TILE), jnp.float32) for _ in range(NSA_GROUPS)]
    for p in range(N_PAIRS):
        outs = []
        for g in range(NSA_GROUPS):
            h = 2 * p + g
            pr = jnp.exp2(s_ref[h] - mx_ref[h]).astype(jnp.bfloat16)
            acc = _dot(vals[g], pr)
            inv = jnp.where(seen, 1.0 / acc[HALF:HALF + 1], 0.0)
            outs.append(acc[0:HALF] * inv)
            imp[g] = imp[g] + acc[ACC_ROWS:ACC_ROWS + R] * inv
        _store_pair(o_ref, p, outs[0], outs[1], sz_ref[0, p], _nsa_gates(gate_ref, p, BRANCH_CMP))

    j_iota = lax.broadcasted_iota(jnp.int32, (R, TILE), 0)
    t_q = i * TILE + lax.broadcasted_iota(jnp.int32, (R, TILE), 1)
    own = lax.shift_right_logical(t_q, int(np.log2(SLC_BLOCK)))
    forced = (j_iota == 0) | (j_iota == own) | (j_iota == own - 1)
    causal = j_iota <= own

    def select(n_rows):
        for g in range(NSA_GROUPS):
            v = jnp.where(forced, FORCED_SCORE, jnp.where(causal, imp[g], NEG_INF))[0:n_rows]
            rank = _rank_rows(v, min(n_rows, n_slc))
            sel = (rank < min(SLC_TOPN, n_slc)) & causal[0:n_rows]
            selneg = jnp.where(sel, 0.0, NEG_INF)
            if n_rows < R:
                selneg = jnp.concatenate(
                    [selneg, jnp.full((R - n_rows, TILE), NEG_INF, jnp.float32)], axis=0)
            sel_ref[0, g] = selneg.astype(jnp.bfloat16)

    blocks_per_tile = TILE // SLC_BLOCK
    variants = list(range(RANK_ROWS_STEP, R, RANK_ROWS_STEP)) + [R]
    for idx, n_rows in enumerate(variants):
        lo = 0 if idx == 0 else variants[idx - 1] // blocks_per_tile
        hi = n_rows // blocks_per_tile

        @pl.when((i >= lo) & (i < hi) if n_rows < R else i >= lo)
        def _():
            select(n_rows)


def _cmp(qbT, kc, vcT, ovT, qaug, szT, gateT, n_slc):
    B, _, _, S = qbT.shape
    C = kc.shape[1]
    nq = S // TILE
    R = ovT.shape[0]
    return pl.pallas_call(
        functools.partial(_cmp_kernel, n_slc),
        grid=(B, nq),
        in_specs=[
            pl.BlockSpec((1, N_PAIRS, LANES, TILE), lambda b, i: (b, 0, 0, i)),
            pl.BlockSpec((1, C, KV_W), lambda b, i: (b, 0, 0)),
            pl.BlockSpec((1, KV_W, C), lambda b, i: (b, 0, 0)),
            pl.BlockSpec(ovT.shape, lambda b, i: (0, 0)),
            pl.BlockSpec(qaug.shape, lambda b, i: (0, 0)),
            pl.BlockSpec((1, N_PAIRS, LANES, TILE), lambda b, i: (b, 0, 0, i)),
            pl.BlockSpec((1, GATE_ROWS, TILE), lambda b, i: (b, 0, i)),
        ],
        out_specs=[pl.BlockSpec((1, TILE, NSA_W), lambda b, i: (b, i, 0)),
                   pl.BlockSpec((1, NSA_GROUPS, R, TILE), lambda b, i: (b, 0, 0, i))],
        out_shape=[jax.ShapeDtypeStruct((B, S, NSA_W), BRANCH_DTYPE),
                   jax.ShapeDtypeStruct((B, NSA_GROUPS, R, S), jnp.bfloat16)],
        scratch_shapes=[pltpu.VMEM((N_HEADS, C, TILE), jnp.float32),
                        pltpu.VMEM((N_HEADS, 1, TILE), jnp.float32)],
        compiler_params=pltpu.CompilerParams(
            dimension_semantics=("arbitrary", "arbitrary"), vmem_limit_bytes=VMEM_LIMIT),
        name="cmp",
    )(qbT, kc, vcT, ovT, qaug, szT, gateT)


def _slc_kernel(it_ref, jt_ref, qT_ref, k_ref, vT_ref, sel_ref, kaug_ref, sz_ref, gate_ref, o_ref,
                qaug_ref, m_ref, acc_ref, s_ref, mx_ref):
    s = pl.program_id(1)
    ii = it_ref[s]
    jj = jt_ref[s]
    n_sel = sel_ref.shape[2]

    @pl.when(jj == 0)
    def _():
        _init_state(m_ref, acc_ref)
        for p in range(N_PAIRS):
            qT = qT_ref[0, p]
            for g in range(NSA_GROUPS):
                h = 2 * p + g
                qm = _keep_half(qT, g)
                ali = _alibi_rows(QT, SLOPES[g * NSA_HPG + p], 2 * ii)
                qaug_ref[h] = _pad_rows(
                    jnp.concatenate([qm, sel_ref[0, g], ali.astype(jnp.bfloat16)], axis=0),
                    2 * LANES)

    def score(h, r0, r1, c0, c1):
        kfull = jnp.concatenate([k_ref[0, r0:r1], kaug_ref[0, r0:r1]], axis=1)
        return _dot(kfull, qaug_ref[h, :, c0:c1])

    def value(h, r0, r1):
        return _value_rows(vT_ref[0, (h % 2) * HALF:(h % 2 + 1) * HALF, r0:r1])

    def finalize():
        for p in range(N_PAIRS):
            _finalize_pair(o_ref, acc_ref, p, sz_ref[0, p], _nsa_gates(gate_ref, p, BRANCH_SLC))

    _key_steps(ii, jj, score, value, finalize, s_ref, mx_ref, m_ref, acc_ref)


def _slc(qbT, ksl, vslT, selT, kaug, szT, gateT):
    B, _, _, S = qbT.shape
    R = selT.shape[2]
    it, jt = _step_tables(S // QT)
    return pl.pallas_call(
        _slc_kernel,
        grid_spec=pltpu.PrefetchScalarGridSpec(
            num_scalar_prefetch=2,
            grid=(B, int(it.shape[0])),
            in_specs=[
                pl.BlockSpec((1, N_PAIRS, LANES, QT), lambda b, s, it, jt: (b, 0, 0, it[s])),
                pl.BlockSpec((1, KEYS, KV_W), lambda b, s, it, jt: (b, jt[s], 0)),
                pl.BlockSpec((1, KV_W, KEYS), lambda b, s, it, jt: (b, 0, jt[s])),
                pl.BlockSpec((1, NSA_GROUPS, R, QT), lambda b, s, it, jt: (b, 0, 0, it[s])),
                pl.BlockSpec((1, KEYS, LANES), lambda b, s, it, jt: (jt[s], 0, 0)),
                pl.BlockSpec((1, N_PAIRS, LANES, QT), lambda b, s, it, jt: (b, 0, 0, it[s])),
                pl.BlockSpec((1, GATE_ROWS, QT), lambda b, s, it, jt: (b, 0, it[s])),
            ],
            out_specs=pl.BlockSpec((1, QT, NSA_W), lambda b, s, it, jt: (b, it[s], 0)),
            scratch_shapes=_state_scratch(KEYS, QT),
        ),
        out_shape=jax.ShapeDtypeStruct((B, S, NSA_W), BRANCH_DTYPE),
        compiler_params=pltpu.CompilerParams(
            dimension_semantics=("arbitrary", "arbitrary"), vmem_limit_bytes=VMEM_LIMIT),
        name="slc",
    )(it, jt, qbT, ksl, vslT, selT, kaug, szT, gateT)


def _win_kernel(qT_ref, kb_ref, kq_ref, vb_ref, vq_ref, kaug_ref, sz_ref, gate_ref,
                o_ref, qaug_ref, m_ref, acc_ref, s_ref, mx_ref):
    i = pl.program_id(1)
    _init_state(m_ref, acc_ref)
    r = lax.broadcasted_iota(jnp.int32, (AUG_ROWS, QT), 0)
    c = lax.broadcasted_iota(jnp.int32, (AUG_ROWS, QT), 1)
    trel = (c & (TILE - 1)).astype(jnp.float32)
    absent = jnp.where(i >= 1, 0.0, NEG_INF)
    for p in range(N_PAIRS):
        qT = qT_ref[0, p]
        for g in range(NSA_GROUPS):
            h = 2 * p + g
            slope = SLOPES[g * NSA_HPG + p]
            c_hi, c_lo = _slope_parts(slope)
            w_hi, w_lo = _split_bf16(np.float32(slope * LOG2E) * trel)
            aug = jnp.where(r == 0, -w_hi, 0.0)
            aug = jnp.where(r == 1, -w_lo, aug)
            aug = jnp.where(r == 2, c_hi, aug)
            aug = jnp.where(r == 3, c_lo, aug)
            aug = jnp.where(r == 4, -c_hi, aug)
            aug = jnp.where(r == 5, -c_lo, aug)
            aug = jnp.where(r == 6, absent, aug)
            qaug_ref[h] = _pad_rows(
                jnp.concatenate([_keep_half(qT, g), aug.astype(jnp.bfloat16)], axis=0), 2 * LANES)

    kfull = jnp.concatenate(
        [jnp.concatenate([kb_ref[0], kq_ref[0]], axis=0), kaug_ref[...]], axis=1)
    vT = jnp.concatenate([vb_ref[0], vq_ref[0]], axis=1)
    k, t = _tile_iotas()

    def score(h, r0, r1, c0, c1):
        return _dot(kfull[r0:r1], qaug_ref[h, :, c0:c1])

    def value(h, r0, r1):
        return _value_rows(vT[(h % 2) * HALF:(h % 2 + 1) * HALF, r0:r1])

    parts = []
    for q in range(QT // TILE):
        first, last = q * TILE, (q + WIN_TILES) * TILE
        parts.append(((q * TILE, (q + 1) * TILE),
                      [(first, first + TILE, k > t), (first + TILE, last, None),
                       (last, last + TILE, k <= t)]))
    _attend_heads(score, value, parts, s_ref, mx_ref, m_ref, acc_ref)
    for p in range(N_PAIRS):
        _finalize_pair(o_ref, acc_ref, p, sz_ref[0, p], _nsa_gates(gate_ref, p, BRANCH_WIN))


def _win(qbT, kwi, vwiT, kaug, szT, gateT):
    B, _, _, S = qbT.shape
    before = lambda i: jnp.maximum(i - 1, 0)
    return pl.pallas_call(
        _win_kernel,
        grid=(B, S // QT),
        in_specs=[
            pl.BlockSpec((1, N_PAIRS, LANES, QT), lambda b, i: (b, 0, 0, i)),
            pl.BlockSpec((1, WINDOW, KV_W), lambda b, i: (b, before(i), 0)),
            pl.BlockSpec((1, QT, KV_W), lambda b, i: (b, i, 0)),
            pl.BlockSpec((1, KV_W, WINDOW), lambda b, i: (b, 0, before(i))),
            pl.BlockSpec((1, KV_W, QT), lambda b, i: (b, 0, i)),
            pl.BlockSpec(kaug.shape, lambda b, i: (0, 0)),
            pl.BlockSpec((1, N_PAIRS, LANES, QT), lambda b, i: (b, 0, 0, i)),
            pl.BlockSpec((1, GATE_ROWS, QT), lambda b, i: (b, 0, i)),
        ],
        out_specs=pl.BlockSpec((1, QT, NSA_W), lambda b, i: (b, i, 0)),
        out_shape=jax.ShapeDtypeStruct((B, S, NSA_W), BRANCH_DTYPE),
        scratch_shapes=_state_scratch(WIN_KEYS, QT),
        compiler_params=pltpu.CompilerParams(
            dimension_semantics=("arbitrary", "arbitrary"), vmem_limit_bytes=VMEM_LIMIT),
        name="win",
    )(qbT, kwi, kwi, vwiT, vwiT, kaug, szT, gateT)


def _out_kernel(x_ref, oa_ref, oc_ref, os_ref, ow_ref, w_ref, g_ref, y_ref):
    f32 = jnp.float32
    ob = oc_ref[0].astype(f32) + os_ref[0].astype(f32) + ow_ref[0].astype(f32)
    mix = jnp.concatenate([oa_ref[0].astype(jnp.bfloat16), ob.astype(jnp.bfloat16)], axis=1)
    y = _dot(mix, w_ref[...])
    r = lax.rsqrt(jnp.mean(y * y, axis=-1, keepdims=True) + RMS_EPS)
    y_ref[0] = x_ref[0] + y * r * g_ref[...]


def _out(x, oa, oc, os_, ow, w, g):
    B, S, _ = x.shape
    rows = lambda wd: pl.BlockSpec((1, ROWS, wd), lambda b, i: (b, i, 0))
    return pl.pallas_call(
        _out_kernel,
        grid=(B, S // ROWS),
        in_specs=[rows(D_MODEL), rows(MOBA_W), rows(NSA_W), rows(NSA_W), rows(NSA_W),
                  pl.BlockSpec(w.shape, lambda b, i: (0, 0)),
                  pl.BlockSpec((1, D_MODEL), lambda b, i: (0, 0))],
        out_specs=rows(D_MODEL),
        out_shape=jax.ShapeDtypeStruct((B, S, D_MODEL), jnp.float32),
        compiler_params=pltpu.CompilerParams(
            dimension_semantics=("arbitrary", "arbitrary"), vmem_limit_bytes=VMEM_LIMIT),
        name="out",
    )(x, oa, oc, os_, ow, w, g)


def _compress_weights(w1, w2):
    half = (CMP_LEN // 2) * HEAD_DIM
    w1r = w1.reshape(2, CMP_LEN // 2, HEAD_DIM, CMP_HIDDEN)
    z = jnp.zeros_like(w1r)
    g0 = jnp.concatenate([w1r, z], axis=-1)
    g1 = jnp.concatenate([z, w1r], axis=-1)
    both = jnp.stack([g0, g1], axis=2)
    both = both.reshape(2, half * NSA_GROUPS, NSA_GROUPS * CMP_HIDDEN).astype(jnp.bfloat16)
    zz = jnp.zeros_like(w2)
    w2bd = jnp.concatenate([jnp.concatenate([w2, zz], axis=1),
                            jnp.concatenate([zz, w2], axis=1)], axis=0).astype(jnp.bfloat16)
    return both[0], both[1], w2bd


def _overlap_T(n_cmp_pad, n_slc, rows):
    c = np.arange(n_cmp_pad)[None, :] * CMP_STRIDE
    j = np.arange(rows)[:, None] * SLC_BLOCK
    ov = (c < j + SLC_BLOCK) & (c + CMP_LEN > j) & (np.arange(rows)[:, None] < n_slc)
    return jnp.asarray(ov.astype(np.float32), jnp.bfloat16)


def _layer(x, pre_g, post_g, w_in, pos_k, pos_v, w_k1, w_k2, w_v1, w_v2, w_out):
    B, S, _ = x.shape
    nq = S // TILE
    n_cmp = (S - CMP_LEN) // CMP_STRIDE + 1
    n_slc = S // SLC_BLOCK
    C = S // CMP_STRIDE
    wn, wt = _projection_weights(w_in)
    (ka, kmean, kcm, vcm, ksl, kwi, qaT, vaT, qbT, vslT, vwiT,
     szaT, szbT, gateT) = _proj(x, pre_g.reshape(1, D_MODEL), wn, wt)

    wkt, wkb, w2k = _compress_weights(w_k1, w_k2)
    wvt, wvb, w2v = _compress_weights(w_v1, w_v2)
    chunk = CMP_STRIDE * KV_W
    kc, vcT = _compress(
        kcm.reshape(B, C, chunk), vcm.reshape(B, C, chunk), wkt, wkb, wvt, wvb, w2k, w2v,
        pos_k.reshape(1, CMP_LEN * HEAD_DIM).astype(jnp.bfloat16),
        pos_v.reshape(1, CMP_LEN * HEAD_DIM).astype(jnp.bfloat16),
        w_k1.astype(jnp.bfloat16), w_v1.astype(jnp.bfloat16), n_cmp)

    oc, selT = _cmp(qbT, kc, vcT, _overlap_T(C, n_slc, SLC_BLOCK), _cmp_query_aug_table(),
                    szbT, gateT, n_slc)
    oa = _moba(qaT, ka, vaT, kmean.reshape(B, nq, MOBA_W), _key_aug_table(nq, 16, 1), szaT)
    osl = _slc(qbT, ksl, vslT, selT, _key_aug_table(nq, selT.shape[2], TILE // SLC_BLOCK),
               szbT, gateT)
    ow = _win(qbT, kwi, vwiT, _win_key_aug_table(), szbT, gateT)

    w_o = jnp.concatenate([w_out[:MOBA_W], _to_pair_slabs(w_out[MOBA_W:], 0)],
                          axis=0).astype(jnp.bfloat16)
    return _out(x, oa, oc, osl, ow, w_o, post_g.reshape(1, D_MODEL))


def kernel(x, pre_norm_g, post_norm_g, w_in, cmp_pos_k, cmp_pos_v,
           w_cmp_k1, w_cmp_k2, w_cmp_v1, w_cmp_v2, w_out):
    for l in range(pre_norm_g.shape[0]):
        x = _layer(x, pre_norm_g[l], post_norm_g[l], w_in[l], cmp_pos_k[l], cmp_pos_v[l],
                   w_cmp_k1[l], w_cmp_k2[l], w_cmp_v1[l], w_cmp_v2[l], w_out[l])
    return x
```

```python
import functools

import numpy as np
import jax
import jax.numpy as jnp
from jax import lax
from jax.experimental import pallas as pl
from jax.experimental.pallas import tpu as pltpu

D_MODEL = 1024
HEAD_DIM = 64
N_HEADS = 8
N_PAIRS = N_HEADS // 2
NSA_GROUPS = 2
NSA_HPG = N_HEADS // NSA_GROUPS
MOBA_BLOCK = 256
MOBA_TOPK = 3
CMP_LEN = 32
CMP_STRIDE = 16
CMP_HIDDEN = 128
SLC_BLOCK = 64
SLC_TOPN = 16
WINDOW = 512
RMS_EPS = 1e-6
NEG_INF = -1e30
FORCED_SCORE = 1e9
SCALE = HEAD_DIM ** -0.5
LOG2E = float(np.log2(np.e))

TILE = 256
QT = 2 * TILE
KEYS = 4 * TILE
ROWS = 2 * TILE
LANES = 128
HALF = HEAD_DIM
SUBLANES = 8
MOBA_W = N_HEADS * HEAD_DIM
NSA_W = N_HEADS * HEAD_DIM
KV_W = NSA_GROUPS * HEAD_DIM
N_GATES = 3 * N_HEADS
GATE_ROWS = 32
SLOPES = tuple(2.0 ** (-(i + 1)) for i in range(N_HEADS))
BRANCH_CMP, BRANCH_SLC, BRANCH_WIN = 0, 1, 2
BRANCH_DTYPE = jnp.bfloat16

VMEM_LIMIT = 48 * 1024 * 1024

_OFF = dict(qa=0, ka=512, va=1024, za=1536, qb=2048, kcm=2560, vcm=2688,
            ksl=2816, vsl=2944, kwi=3072, vwi=3200, gate=3328, zb=3352)
_NAT = dict(ka=(0, 512), kcm=(512, 640), vcm=(640, 768), ksl=(768, 896), kwi=(896, 1024))
N_NAT = 1024
_TR = dict(qa=(0, 512), va=(512, 1024), qb=(1024, 1536), vsl=(1536, 1664),
           vwi=(1664, 1792), za=(1792, 2304), zb=(2304, 2816), gate=(2816, 2848))
N_TR = 2848


def _to_pair_slabs(w, axis):
    w = jnp.moveaxis(w, axis, -1)
    lead = w.shape[:-1]
    w = w.reshape(lead + (NSA_GROUPS, NSA_HPG, HEAD_DIM))
    w = jnp.swapaxes(w, -3, -2).reshape(lead + (NSA_W,))
    return jnp.moveaxis(w, -1, axis)


def _projection_weights(w_in):
    col = lambda name, width: w_in[:, _OFF[name]:_OFF[name] + width]
    natural = jnp.concatenate(
        [col("ka", MOBA_W), col("kcm", KV_W), col("vcm", KV_W), col("ksl", KV_W), col("kwi", KV_W)],
        axis=1)
    transposed = jnp.concatenate(
        [col("qa", MOBA_W), col("va", MOBA_W), _to_pair_slabs(col("qb", NSA_W), 1),
         col("vsl", KV_W), col("vwi", KV_W), col("za", MOBA_W), _to_pair_slabs(col("zb", NSA_W), 1),
         col("gate", N_GATES), jnp.zeros((D_MODEL, GATE_ROWS - N_GATES), w_in.dtype)], axis=1)
    return natural.astype(jnp.bfloat16), transposed.T.astype(jnp.bfloat16)


def _gate_row(g, p, branch):
    return (g * NSA_HPG + p) * 3 + branch


def _sigmoid(x):
    return 1.0 / (1.0 + jnp.exp(-x))


def _silu(x):
    return x * _sigmoid(x)


def _dot(a, b):
    return jnp.dot(a, b, preferred_element_type=jnp.float32)


def _proj_kernel(x_ref, g_ref, wn_ref, wt_ref,
                 ka_ref, kmean_ref, kcm_ref, vcm_ref, ksl_ref, kwi_ref,
                 qaT_ref, vaT_ref, qbT_ref, vslT_ref, vwiT_ref,
                 szaT_ref, szbT_ref, gateT_ref):
    x = x_ref[0]
    r = lax.rsqrt(jnp.mean(x * x, axis=-1, keepdims=True) + RMS_EPS)
    h = (x * r * g_ref[...]).astype(jnp.bfloat16)

    natural = _dot(h, wn_ref[...])

    def nat(name):
        a, b = _NAT[name]
        return natural[:, a:b]

    ka = nat("ka")
    for blk in range(x.shape[0] // MOBA_BLOCK):
        kmean_ref[0, blk] = jnp.mean(ka[blk * MOBA_BLOCK:(blk + 1) * MOBA_BLOCK],
                                     axis=0, keepdims=True)
    for p in range(N_PAIRS):
        ka_ref[0, p] = ka[:, p * LANES:(p + 1) * LANES].astype(jnp.bfloat16)
    kcm_ref[0] = nat("kcm").astype(jnp.bfloat16)
    vcm_ref[0] = nat("vcm").astype(jnp.bfloat16)
    ksl_ref[0] = nat("ksl").astype(jnp.bfloat16)
    kwi_ref[0] = nat("kwi").astype(jnp.bfloat16)

    def tr(name):
        a, b = _TR[name]
        return lax.dot_general(wt_ref[a:b, :], h, (((1,), (1,)), ((), ())),
                               preferred_element_type=jnp.float32)

    qa = tr("qa") * (SCALE * LOG2E)
    va = tr("va")
    qb = tr("qb") * (SCALE * LOG2E)
    sza = _silu(tr("za"))
    szb = _silu(tr("zb"))
    for p in range(N_PAIRS):
        sl = slice(p * LANES, (p + 1) * LANES)
        qaT_ref[0, p] = qa[sl].astype(jnp.bfloat16)
        vaT_ref[0, p] = va[sl].astype(jnp.bfloat16)
        qbT_ref[0, p] = qb[sl].astype(jnp.bfloat16)
        szaT_ref[0, p] = sza[sl]
        szbT_ref[0, p] = szb[sl]
    vslT_ref[0] = tr("vsl").astype(jnp.bfloat16)
    vwiT_ref[0] = tr("vwi").astype(jnp.bfloat16)
    gateT_ref[0] = _sigmoid(tr("gate"))


def _proj(x, g, wn, wt):
    B, S, _ = x.shape
    nq = S // TILE
    bf, f32 = jnp.bfloat16, jnp.float32
    slab_nat = lambda: pl.BlockSpec((1, N_PAIRS, ROWS, LANES), lambda b, i: (b, 0, i, 0))
    slab_tr = lambda: pl.BlockSpec((1, N_PAIRS, LANES, ROWS), lambda b, i: (b, 0, 0, i))
    rows = lambda w: pl.BlockSpec((1, ROWS, w), lambda b, i: (b, i, 0))
    chans = lambda c: pl.BlockSpec((1, c, ROWS), lambda b, i: (b, 0, i))
    out_shape = [
        jax.ShapeDtypeStruct((B, N_PAIRS, S, LANES), bf),
        jax.ShapeDtypeStruct((B, nq, 1, MOBA_W), f32),
        jax.ShapeDtypeStruct((B, S, KV_W), bf),
        jax.ShapeDtypeStruct((B, S, KV_W), bf),
        jax.ShapeDtypeStruct((B, S, KV_W), bf),
        jax.ShapeDtypeStruct((B, S, KV_W), bf),
        jax.ShapeDtypeStruct((B, N_PAIRS, LANES, S), bf),
        jax.ShapeDtypeStruct((B, N_PAIRS, LANES, S), bf),
        jax.ShapeDtypeStruct((B, N_PAIRS, LANES, S), bf),
        jax.ShapeDtypeStruct((B, KV_W, S), bf),
        jax.ShapeDtypeStruct((B, KV_W, S), bf),
        jax.ShapeDtypeStruct((B, N_PAIRS, LANES, S), f32),
        jax.ShapeDtypeStruct((B, N_PAIRS, LANES, S), f32),
        jax.ShapeDtypeStruct((B, GATE_ROWS, S), f32),
    ]
    out_specs = [
        slab_nat(),
        pl.BlockSpec((1, ROWS // MOBA_BLOCK, 1, MOBA_W), lambda b, i: (b, i, 0, 0)),
        rows(KV_W), rows(KV_W), rows(KV_W), rows(KV_W),
        slab_tr(), slab_tr(), slab_tr(),
        chans(KV_W), chans(KV_W),
        slab_tr(), slab_tr(),
        chans(GATE_ROWS),
    ]
    return pl.pallas_call(
        _proj_kernel,
        grid=(B, S // ROWS),
        in_specs=[
            pl.BlockSpec((1, ROWS, D_MODEL), lambda b, i: (b, i, 0)),
            pl.BlockSpec((1, D_MODEL), lambda b, i: (0, 0)),
            pl.BlockSpec((D_MODEL, N_NAT), lambda b, i: (0, 0)),
            pl.BlockSpec((N_TR, D_MODEL), lambda b, i: (0, 0)),
        ],
        out_specs=out_specs,
        out_shape=out_shape,
        compiler_params=pltpu.CompilerParams(
            dimension_semantics=("arbitrary", "arbitrary"),
            vmem_limit_bytes=VMEM_LIMIT),
        name="proj",
    )(x, g, wn, wt)


def _compress_kernel(n_cmp, xk_ref, xv_ref, wkt_ref, wkb_ref, wvt_ref, wvb_ref,
                     w2k_ref, w2v_ref, pek_ref, pev_ref, w1k_ref, w1v_ref,
                     kc_ref, vcT_ref):
    C = xk_ref.shape[1]
    row = lax.broadcasted_iota(jnp.int32, (C, KV_W), 0)

    def phi(x_ref, wt_ref, wb_ref, w2_ref, pe_ref, w1_ref):
        x = x_ref[0]
        top = _dot(x, wt_ref[...])
        bot = _dot(x, wb_ref[...])
        peb = _dot(jnp.broadcast_to(pe_ref[...], (8, pe_ref.shape[1])), w1_ref[...])[0:1]
        peb = jnp.concatenate([peb, peb], axis=1)
        hid = top + pltpu.roll(bot, C - 1, 0) + peb
        out = _dot(_silu(hid).astype(jnp.bfloat16), w2_ref[...])
        return jnp.where(row < n_cmp, out, 0.0)

    kc_ref[0] = phi(xk_ref, wkt_ref, wkb_ref, w2k_ref, pek_ref, w1k_ref).astype(jnp.bfloat16)
    vc = phi(xv_ref, wvt_ref, wvb_ref, w2v_ref, pev_ref, w1v_ref)
    vcT_ref[0] = vc.T.astype(jnp.bfloat16)


def _compress(xk, xv, wkt, wkb, wvt, wvb, w2k, w2v, pek, pev, w1k, w1v, n_cmp):
    B, C, F = xk.shape
    full = lambda a: pl.BlockSpec(a.shape, lambda b: (0,) * a.ndim)
    return pl.pallas_call(
        functools.partial(_compress_kernel, n_cmp),
        grid=(B,),
        in_specs=[pl.BlockSpec((1, C, F), lambda b: (b, 0, 0)),
                  pl.BlockSpec((1, C, F), lambda b: (b, 0, 0)),
                  full(wkt), full(wkb), full(wvt), full(wvb), full(w2k), full(w2v),
                  full(pek), full(pev), full(w1k), full(w1v)],
        out_specs=[pl.BlockSpec((1, C, KV_W), lambda b: (b, 0, 0)),
                   pl.BlockSpec((1, KV_W, C), lambda b: (b, 0, 0))],
        out_shape=[jax.ShapeDtypeStruct((B, C, KV_W), jnp.bfloat16),
                   jax.ShapeDtypeStruct((B, KV_W, C), jnp.bfloat16)],
        compiler_params=pltpu.CompilerParams(
            dimension_semantics=("arbitrary",), vmem_limit_bytes=VMEM_LIMIT),
        name="compress",
    )(xk, xv, wkt, wkb, wvt, wvb, w2k, w2v, pek, pev, w1k, w1v)


def _keep_half(qT, half):
    z = jnp.zeros((HALF, qT.shape[1]), qT.dtype)
    if half == 0:
        return jnp.concatenate([qT[0:HALF], z], axis=0)
    return jnp.concatenate([z, qT[HALF:2 * HALF]], axis=0)


def _rank_rows(vals, n_rows):
    R, T = vals.shape
    tiles = [vals[a:a + SUBLANES] for a in range(0, R, SUBLANES)]
    ranks = [jnp.zeros((SUBLANES, T), jnp.int32) for _ in tiles]
    j_in = lax.broadcasted_iota(jnp.int32, (SUBLANES, T), 0)
    for m in range(n_rows):
        vm = vals[m:m + 1, :]
        for a, tile in enumerate(tiles):
            lo = a * SUBLANES
            if lo > m:
                beats = vm >= tile
            elif lo + SUBLANES - 1 <= m:
                beats = vm > tile
            else:
                beats = (vm > tile) | ((vm == tile) & (j_in > m - lo))
            ranks[a] = ranks[a] + beats.astype(jnp.int32)
    return jnp.concatenate(ranks, axis=0)


def _query_tile(cols, tile0):
    return tile0 + lax.shift_right_logical(cols, int(np.log2(TILE)))


def _split_bf16(x):
    hi = x.astype(jnp.bfloat16).astype(jnp.float32)
    return hi, x - hi


def _slope_parts(slope):
    c = np.float32(slope * LOG2E)
    hi = np.asarray(c, jnp.bfloat16).astype(np.float32)
    lo = np.asarray(c - hi, jnp.bfloat16).astype(np.float32)
    return float(hi), float(lo)


AUG_ROWS = 16


def _pad_rows(x, n_rows):
    return jnp.concatenate([x, jnp.zeros((n_rows - x.shape[0], x.shape[1]), x.dtype)], axis=0)


def _alibi_rows(T, slope, tile0):
    r = lax.broadcasted_iota(jnp.int32, (AUG_ROWS, T), 0)
    c = lax.broadcasted_iota(jnp.int32, (AUG_ROWS, T), 1)
    t_abs = ((c & (TILE - 1)) + _query_tile(c, tile0) * TILE).astype(jnp.float32)
    c_hi, c_lo = _slope_parts(slope)
    w_hi, w_lo = _split_bf16(np.float32(slope * LOG2E) * t_abs)
    out = jnp.where(r == 0, -w_hi, 0.0)
    out = jnp.where(r == 1, -w_lo, out)
    out = jnp.where((r == 2) | (r == 4), c_hi, out)
    out = jnp.where((r == 3) | (r == 5), c_lo, out)
    return out


def _key_aug_table(nq, n_sel, blocks_per_tile):
    t = np.zeros((nq, TILE, LANES), np.float32)
    krel = np.arange(TILE)
    for j in range(nq):
        if n_sel:
            blk = j * blocks_per_tile + krel // (TILE // blocks_per_tile)
            t[j, krel, blk] = 1.0
        t[j, :, n_sel:n_sel + 2] = 1.0
        t[j, :, n_sel + 2:n_sel + 4] = krel[:, None]
        t[j, :, n_sel + 4:n_sel + 6] = TILE * j
    return jnp.asarray(t.reshape(nq * TILE // KEYS, KEYS, LANES), jnp.bfloat16)


ONES_ROWS = 16
ACC_ROWS = HALF + ONES_ROWS


WIN_TILES = WINDOW // TILE
WIN_KEYS = (WIN_TILES + QT // TILE) * TILE


def _win_key_aug_table():
    t = np.zeros((WIN_KEYS, LANES), np.float32)
    for slot in range(WIN_KEYS // TILE):
        rows = slice(slot * TILE, (slot + 1) * TILE)
        t[rows, 0:2] = 1.0
        t[rows, 2:4] = np.arange(TILE)[:, None]
        t[rows, 4:6] = TILE * (WIN_TILES - slot)
        if slot < WIN_TILES:
            t[rows, 6] = 1.0
    return jnp.asarray(t, jnp.bfloat16)


def _value_rows(vT_h):
    return jnp.concatenate([vT_h, jnp.ones((ONES_ROWS, vT_h.shape[1]), vT_h.dtype)], axis=0)


def _stage_scores(h, score_fn, blocks, cols, s_ref, mx_ref):
    c0, c1 = cols
    base = blocks[0][0]
    s_all = score_fn(h, base, blocks[-1][1], c0, c1)
    mx = None
    for r0, r1, keep in blocks:
        s = s_all[r0 - base:r1 - base]
        if keep is not None:
            s = jnp.where(keep, s, NEG_INF)
        s_ref[h, r0:r1, c0:c1] = s
        part = jnp.max(s, axis=0, keepdims=True)
        mx = part if mx is None else jnp.maximum(mx, part)
    mx_ref[h, :, c0:c1] = mx


def _consume_scores(h, value_fn, rows, cols, s_ref, mx_ref, m_ref, acc_ref):
    (r0, r1), (c0, c1) = rows, cols
    m_prev = m_ref[h, :, c0:c1]
    m_new = jnp.maximum(m_prev, mx_ref[h, :, c0:c1])
    alpha = jnp.exp2(m_prev - m_new)
    p = jnp.exp2(s_ref[h, r0:r1, c0:c1] - m_new).astype(jnp.bfloat16)
    acc_ref[h, :, c0:c1] = alpha * acc_ref[h, :, c0:c1] + _dot(value_fn(h, r0, r1), p)
    m_ref[h, :, c0:c1] = m_new


def _attend_heads(score_fn, value_fn, parts, s_ref, mx_ref, m_ref, acc_ref):
    for h in range(N_HEADS):
        for cols, blocks in parts:
            _stage_scores(h, score_fn, blocks, cols, s_ref, mx_ref)
    for h in range(N_HEADS):
        for cols, blocks in parts:
            rows = (min(b[0] for b in blocks), max(b[1] for b in blocks))
            _consume_scores(h, value_fn, rows, cols, s_ref, mx_ref, m_ref, acc_ref)


def _init_state(m_ref, acc_ref):
    m_ref[...] = jnp.full(m_ref.shape, NEG_INF, jnp.float32)
    acc_ref[...] = jnp.zeros(acc_ref.shape, jnp.float32)


def _tile_iotas(T=TILE):
    k = lax.broadcasted_iota(jnp.int32, (TILE, T), 0)
    t = lax.broadcasted_iota(jnp.int32, (TILE, T), 1)
    return k, t


def _key_step_parts(kind):
    if kind == "past":
        return [((0, QT), [(0, KEYS, None)])]
    k, t = _tile_iotas()
    tri = k <= t
    own = QT if kind == "reach" else 0
    first = [(0, own, None)] if own else []
    return [((0, TILE), first + [(own, own + TILE, tri)]),
            ((TILE, QT), [(0, own + TILE, None), (own + TILE, own + QT, tri)])]


def _store_pair(o_ref, p, o0T, o1T, szT, gates, rows=None):
    if gates is not None:
        o0T = o0T * gates[0]
        o1T = o1T * gates[1]
    oT = jnp.concatenate([o0T, o1T], axis=0) * szT
    r0, r1 = rows if rows is not None else (0, o_ref.shape[1])
    o_ref[0, r0:r1, p * LANES:(p + 1) * LANES] = oT.T.astype(o_ref.dtype)


def _finalize_pair(o_ref, acc_ref, p, szT, gates=None):
    a0 = acc_ref[2 * p]
    a1 = acc_ref[2 * p + 1]
    _store_pair(o_ref, p, a0[0:HALF] / a0[HALF:HALF + 1], a1[0:HALF] / a1[HALF:HALF + 1],
                szT, gates)


def _nsa_gates(gate_ref, p, branch, cols=None):
    c0, c1 = cols if cols is not None else (0, gate_ref.shape[2])
    return [gate_ref[0, _gate_row(g, p, branch):_gate_row(g, p, branch) + 1, c0:c1]
            for g in range(NSA_GROUPS)]


def _step_tables(n_q):
    it, jt = [], []
    for ii in range(n_q):
        for js in range(ii * QT // KEYS + 1):
            it.append(ii)
            jt.append(js)
    return jnp.asarray(it, jnp.int32), jnp.asarray(jt, jnp.int32)


def _key_steps(ii, js, score_fn, value_fn, finalize_fn, s_ref, mx_ref, m_ref, acc_ref):
    q0 = ii * QT
    k0 = js * KEYS
    for kind, when in (("past", k0 + KEYS <= q0), ("reach", k0 + QT == q0), ("own", k0 == q0)):
        @pl.when(when)
        def _():
            _attend_heads(score_fn, value_fn, _key_step_parts(kind), s_ref, mx_ref, m_ref, acc_ref)
            if kind != "past":
                finalize_fn()


def _state_scratch(n_keys, T=TILE):
    return [
        pltpu.VMEM((N_HEADS, 2 * LANES, T), jnp.bfloat16),
        pltpu.VMEM((N_HEADS, 1, T), jnp.float32),
        pltpu.VMEM((N_HEADS, ACC_ROWS, T), jnp.float32),
        pltpu.VMEM((N_HEADS, n_keys, T), jnp.float32),
        pltpu.VMEM((N_HEADS, 1, T), jnp.float32),
    ]


def _moba_kernel(nq, it_ref, jt_ref, qT_ref, k_ref, vT_ref, kmean_ref, kaug_ref, sz_ref, o_ref,
                 qaug_ref, m_ref, acc_ref, s_ref, mx_ref):
    s = pl.program_id(1)
    ii = it_ref[s]
    jj = jt_ref[s]
    n_sel = 16

    @pl.when(jj == 0)
    def _():
        _init_state(m_ref, acc_ref)
        n_iota = lax.broadcasted_iota(jnp.int32, (nq, QT), 0)
        own = _query_tile(lax.broadcasted_iota(jnp.int32, (nq, QT), 1), 2 * ii)
        for p in range(N_PAIRS):
            qT = qT_ref[0, p]
            km = kmean_ref[0, :, p * LANES:(p + 1) * LANES].astype(jnp.bfloat16)
            for hh in range(2):
                h = 2 * p + hh
                qm = _keep_half(qT, hh)
                gate = jnp.where(n_iota < own, _dot(km, qm), NEG_INF)
                rank = _rank_rows(gate, nq)
                sel = ((rank < MOBA_TOPK) & (n_iota < own)) | (n_iota == own)
                selneg = jnp.where(sel, 0.0, NEG_INF)
                if nq < n_sel:
                    selneg = jnp.concatenate(
                        [selneg, jnp.zeros((n_sel - nq, QT), jnp.float32)], axis=0)
                ali = _alibi_rows(QT, SLOPES[h], 2 * ii)
                aug = jnp.concatenate([selneg, ali], axis=0).astype(jnp.bfloat16)
                qaug_ref[h] = _pad_rows(jnp.concatenate([qm, aug], axis=0), 2 * LANES)

    def score(h, r0, r1, c0, c1):
        kfull = jnp.concatenate([k_ref[0, h // 2, r0:r1], kaug_ref[0, r0:r1]], axis=1)
        return _dot(kfull, qaug_ref[h, :, c0:c1])

    def value(h, r0, r1):
        return _value_rows(vT_ref[0, h // 2, (h % 2) * HALF:(h % 2 + 1) * HALF, r0:r1])

    def finalize():
        for p in range(N_PAIRS):
            _finalize_pair(o_ref, acc_ref, p, sz_ref[0, p])

    _key_steps(ii, jj, score, value, finalize, s_ref, mx_ref, m_ref, acc_ref)


def _moba(qT, k, vT, kmean, kaug, szT):
    B, _, _, S = qT.shape
    nq = S // TILE
    it, jt = _step_tables(S // QT)
    return pl.pallas_call(
        functools.partial(_moba_kernel, nq),
        grid_spec=pltpu.PrefetchScalarGridSpec(
            num_scalar_prefetch=2,
            grid=(B, int(it.shape[0])),
            in_specs=[
                pl.BlockSpec((1, N_PAIRS, LANES, QT), lambda b, s, it, jt: (b, 0, 0, it[s])),
                pl.BlockSpec((1, N_PAIRS, KEYS, LANES), lambda b, s, it, jt: (b, 0, jt[s], 0)),
                pl.BlockSpec((1, N_PAIRS, LANES, KEYS), lambda b, s, it, jt: (b, 0, 0, jt[s])),
                pl.BlockSpec((1, nq, MOBA_W), lambda b, s, it, jt: (b, 0, 0)),
                pl.BlockSpec((1, KEYS, LANES), lambda b, s, it, jt: (jt[s], 0, 0)),
                pl.BlockSpec((1, N_PAIRS, LANES, QT), lambda b, s, it, jt: (b, 0, 0, it[s])),
            ],
            out_specs=pl.BlockSpec((1, QT, MOBA_W), lambda b, s, it, jt: (b, it[s], 0)),
            scratch_shapes=_state_scratch(KEYS, QT),
        ),
        out_shape=jax.ShapeDtypeStruct((B, S, MOBA_W), BRANCH_DTYPE),
        compiler_params=pltpu.CompilerParams(
            dimension_semantics=("arbitrary", "arbitrary"), vmem_limit_bytes=VMEM_LIMIT),
        name="moba",
    )(it, jt, qT, k, vT, kmean, kaug, szT)


CMP_BAND = TILE // CMP_STRIDE + 1
RANK_ROWS_STEP = 16


def _cmp_query_aug_table():
    t = np.zeros((LANES, TILE), np.float32)
    trel = np.arange(TILE)
    for u in range(CMP_BAND):
        t[u, trel < CMP_STRIDE * (u - 1) + CMP_LEN - 1] = NEG_INF
    t[CMP_BAND, :] = NEG_INF
    return jnp.asarray(t, jnp.bfloat16)


def _cmp_tile(i, cols, slot, n_slc, qT_ref, kc_ref, vcT_ref, ovT_ref, qaug_ref, sz_ref, gate_ref,
              o_ref, sel_ref, s_ref, mx_ref):
    c0, c1 = cols
    C = kc_ref.shape[1]
    c_rel = (lax.broadcasted_iota(jnp.int32, (C, LANES), 0)
             - (TILE // CMP_STRIDE) * i + 1)
    u = lax.broadcasted_iota(jnp.int32, (C, LANES), 1)
    kaug = jnp.where((c_rel == u) & (u < CMP_BAND), 1.0, 0.0)
    kaug = jnp.where((u == CMP_BAND) & (c_rel >= CMP_BAND), 1.0, kaug)
    kfull = jnp.concatenate([kc_ref[0], kaug.astype(jnp.bfloat16)], axis=1)
    qaug = qaug_ref[...]

    for p in range(N_PAIRS):
        qT = qT_ref[0, p, :, c0:c1]
        for g in range(NSA_GROUPS):
            h = slot * N_HEADS + 2 * p + g
            s = _dot(kfull, jnp.concatenate([_keep_half(qT, g), qaug], axis=0))
            s_ref[h] = s
            mx_ref[h] = jnp.max(s, axis=0, keepdims=True)

    t_abs = i * TILE + lax.broadcasted_iota(jnp.int32, (1, TILE), 1)
    seen = t_abs >= CMP_LEN - 1
    ones = jnp.ones((ONES_ROWS, C), jnp.bfloat16)
    vals = [jnp.concatenate([vcT_ref[0, g * HALF:(g + 1) * HALF], ones, ovT_ref[...]], axis=0)
            for g in range(NSA_GROUPS)]
    R = sel_ref.shape[2]
    imp = [jnp.zeros((R, TILE), jnp.float32) for _ in range(NSA_GROUPS)]
    for p in range(N_PAIRS):
        outs = []
        for g in range(NSA_GROUPS):
            h = slot * N_HEADS + 2 * p + g
            pr = jnp.exp2(s_ref[h] - mx_ref[h]).astype(jnp.bfloat16)
            acc = _dot(vals[g], pr)
            inv = jnp.where(seen, 1.0 / acc[HALF:HALF + 1], 0.0)
            outs.append(acc[0:HALF] * inv)
            imp[g] = imp[g] + acc[ACC_ROWS:ACC_ROWS + R] * inv
        _store_pair(o_ref, p, outs[0], outs[1], sz_ref[0, p, :, c0:c1],
                    _nsa_gates(gate_ref, p, BRANCH_CMP, cols), rows=cols)

    def select_blocks():
        j_iota = lax.broadcasted_iota(jnp.int32, (R, TILE), 0)
        t_q = i * TILE + lax.broadcasted_iota(jnp.int32, (R, TILE), 1)
        own = lax.shift_right_logical(t_q, int(np.log2(SLC_BLOCK)))
        forced = (j_iota == 0) | (j_iota == own) | (j_iota == own - 1)
        causal = j_iota <= own

        def select(n_rows):
            for g in range(NSA_GROUPS):
                v = jnp.where(forced, FORCED_SCORE, jnp.where(causal, imp[g], NEG_INF))[0:n_rows]
                rank = _rank_rows(v, min(n_rows, n_slc))
                sel = (rank < min(SLC_TOPN, n_slc)) & causal[0:n_rows]
                selneg = jnp.where(sel, 0.0, NEG_INF)
                if n_rows < R:
                    selneg = jnp.concatenate(
                        [selneg, jnp.full((R - n_rows, TILE), NEG_INF, jnp.float32)], axis=0)
                sel_ref[0, g, :, c0:c1] = selneg.astype(jnp.bfloat16)

        blocks_per_tile = TILE // SLC_BLOCK
        variants = list(range(RANK_ROWS_STEP, R, RANK_ROWS_STEP)) + [R]
        for idx, n_rows in enumerate(variants):
            lo = 0 if idx == 0 else variants[idx - 1] // blocks_per_tile
            hi = n_rows // blocks_per_tile

            @pl.when((i >= lo) & (i < hi) if n_rows < R else i >= lo)
            def _():
                select(n_rows)

    return select_blocks


def _slc_kernel(it_ref, jt_ref, qT_ref, k_ref, vT_ref, sel_ref, kaug_ref, sz_ref, gate_ref, o_ref,
                qaug_ref, m_ref, acc_ref, s_ref, mx_ref):
    s = pl.program_id(1)
    ii = it_ref[s]
    jj = jt_ref[s]
    n_sel = sel_ref.shape[2]

    @pl.when(jj == 0)
    def _():
        _init_state(m_ref, acc_ref)
        for p in range(N_PAIRS):
            qT = qT_ref[0, p]
            for g in range(NSA_GROUPS):
                h = 2 * p + g
                qm = _keep_half(qT, g)
                ali = _alibi_rows(QT, SLOPES[g * NSA_HPG + p], 2 * ii)
                qaug_ref[h] = _pad_rows(
                    jnp.concatenate([qm, sel_ref[0, g], ali.astype(jnp.bfloat16)], axis=0),
                    2 * LANES)

    def score(h, r0, r1, c0, c1):
        kfull = jnp.concatenate([k_ref[0, r0:r1], kaug_ref[0, r0:r1]], axis=1)
        return _dot(kfull, qaug_ref[h, :, c0:c1])

    def value(h, r0, r1):
        return _value_rows(vT_ref[0, (h % 2) * HALF:(h % 2 + 1) * HALF, r0:r1])

    def finalize():
        for p in range(N_PAIRS):
            _finalize_pair(o_ref, acc_ref, p, sz_ref[0, p], _nsa_gates(gate_ref, p, BRANCH_SLC))

    _key_steps(ii, jj, score, value, finalize, s_ref, mx_ref, m_ref, acc_ref)


def _slc(qbT, ksl, vslT, selT, kaug, szT, gateT):
    B, _, _, S = qbT.shape
    R = selT.shape[2]
    it, jt = _step_tables(S // QT)
    return pl.pallas_call(
        _slc_kernel,
        grid_spec=pltpu.PrefetchScalarGridSpec(
            num_scalar_prefetch=2,
            grid=(B, int(it.shape[0])),
            in_specs=[
                pl.BlockSpec((1, N_PAIRS, LANES, QT), lambda b, s, it, jt: (b, 0, 0, it[s])),
                pl.BlockSpec((1, KEYS, KV_W), lambda b, s, it, jt: (b, jt[s], 0)),
                pl.BlockSpec((1, KV_W, KEYS), lambda b, s, it, jt: (b, 0, jt[s])),
                pl.BlockSpec((1, NSA_GROUPS, R, QT), lambda b, s, it, jt: (b, 0, 0, it[s])),
                pl.BlockSpec((1, KEYS, LANES), lambda b, s, it, jt: (jt[s], 0, 0)),
                pl.BlockSpec((1, N_PAIRS, LANES, QT), lambda b, s, it, jt: (b, 0, 0, it[s])),
                pl.BlockSpec((1, GATE_ROWS, QT), lambda b, s, it, jt: (b, 0, it[s])),
            ],
            out_specs=pl.BlockSpec((1, QT, NSA_W), lambda b, s, it, jt: (b, it[s], 0)),
            scratch_shapes=_state_scratch(KEYS, QT),
        ),
        out_shape=jax.ShapeDtypeStruct((B, S, NSA_W), BRANCH_DTYPE),
        compiler_params=pltpu.CompilerParams(
            dimension_semantics=("arbitrary", "arbitrary"), vmem_limit_bytes=VMEM_LIMIT),
        name="slc",
    )(it, jt, qbT, ksl, vslT, selT, kaug, szT, gateT)


def _local_kernel(n_slc, qT_ref, kb_ref, kq_ref, vb_ref, vq_ref, kaug_ref,
                  kc_ref, vcT_ref, ovT_ref, cq_ref, sz_ref, gate_ref,
                  o_ref, oc_ref, sel_ref,
                  qaug_ref, m_ref, acc_ref, s_ref, mx_ref, cs_ref, cmx_ref):
    i = pl.program_id(1)
    selects = [
        _cmp_tile(i * (QT // TILE) + q, (q * TILE, (q + 1) * TILE), q, n_slc, qT_ref, kc_ref,
                  vcT_ref, ovT_ref, cq_ref, sz_ref, gate_ref, oc_ref, sel_ref, cs_ref, cmx_ref)
        for q in range(QT // TILE)]
    _init_state(m_ref, acc_ref)
    r = lax.broadcasted_iota(jnp.int32, (AUG_ROWS, QT), 0)
    c = lax.broadcasted_iota(jnp.int32, (AUG_ROWS, QT), 1)
    trel = (c & (TILE - 1)).astype(jnp.float32)
    absent = jnp.where(i >= 1, 0.0, NEG_INF)
    for p in range(N_PAIRS):
        qT = qT_ref[0, p]
        for g in range(NSA_GROUPS):
            h = 2 * p + g
            slope = SLOPES[g * NSA_HPG + p]
            c_hi, c_lo = _slope_parts(slope)
            w_hi, w_lo = _split_bf16(np.float32(slope * LOG2E) * trel)
            aug = jnp.where(r == 0, -w_hi, 0.0)
            aug = jnp.where(r == 1, -w_lo, aug)
            aug = jnp.where(r == 2, c_hi, aug)
            aug = jnp.where(r == 3, c_lo, aug)
            aug = jnp.where(r == 4, -c_hi, aug)
            aug = jnp.where(r == 5, -c_lo, aug)
            aug = jnp.where(r == 6, absent, aug)
            qaug_ref[h] = _pad_rows(
                jnp.concatenate([_keep_half(qT, g), aug.astype(jnp.bfloat16)], axis=0), 2 * LANES)

    kfull = jnp.concatenate(
        [jnp.concatenate([kb_ref[0], kq_ref[0]], axis=0), kaug_ref[...]], axis=1)
    vT = jnp.concatenate([vb_ref[0], vq_ref[0]], axis=1)
    k, t = _tile_iotas()

    def score(h, r0, r1, c0, c1):
        return _dot(kfull[r0:r1], qaug_ref[h, :, c0:c1])

    def value(h, r0, r1):
        return _value_rows(vT[(h % 2) * HALF:(h % 2 + 1) * HALF, r0:r1])

    parts = []
    for q in range(QT // TILE):
        first, last = q * TILE, (q + WIN_TILES) * TILE
        parts.append(((q * TILE, (q + 1) * TILE),
                      [(first, first + TILE, k > t), (first + TILE, last, None),
                       (last, last + TILE, k <= t)]))
    _attend_heads(score, value, parts, s_ref, mx_ref, m_ref, acc_ref)
    for p in range(N_PAIRS):
        _finalize_pair(o_ref, acc_ref, p, sz_ref[0, p], _nsa_gates(gate_ref, p, BRANCH_WIN))
    for select_blocks in selects:
        select_blocks()


def _local(qbT, kwi, vwiT, kaug, kc, vcT, ovT, cmp_qaug, szT, gateT, n_slc):
    B, _, _, S = qbT.shape
    assert WINDOW == QT
    C = kc.shape[1]
    R = ovT.shape[0]
    before = lambda i: jnp.maximum(i - 1, 0)
    q_slabs = pl.BlockSpec((1, N_PAIRS, LANES, QT), lambda b, i: (b, 0, 0, i))
    whole = lambda a: pl.BlockSpec(a.shape, lambda b, i: (0,) * a.ndim)
    rows_out = pl.BlockSpec((1, QT, NSA_W), lambda b, i: (b, i, 0))
    return pl.pallas_call(
        functools.partial(_local_kernel, n_slc),
        grid=(B, S // QT),
        in_specs=[
            q_slabs,
            pl.BlockSpec((1, WINDOW, KV_W), lambda b, i: (b, before(i), 0)),
            pl.BlockSpec((1, QT, KV_W), lambda b, i: (b, i, 0)),
            pl.BlockSpec((1, KV_W, WINDOW), lambda b, i: (b, 0, before(i))),
            pl.BlockSpec((1, KV_W, QT), lambda b, i: (b, 0, i)),
            whole(kaug),
            pl.BlockSpec((1, C, KV_W), lambda b, i: (b, 0, 0)),
            pl.BlockSpec((1, KV_W, C), lambda b, i: (b, 0, 0)),
            whole(ovT), whole(cmp_qaug),
            q_slabs,
            pl.BlockSpec((1, GATE_ROWS, QT), lambda b, i: (b, 0, i)),
        ],
        out_specs=[rows_out, rows_out,
                   pl.BlockSpec((1, NSA_GROUPS, R, QT), lambda b, i: (b, 0, 0, i))],
        out_shape=[jax.ShapeDtypeStruct((B, S, NSA_W), BRANCH_DTYPE),
                   jax.ShapeDtypeStruct((B, S, NSA_W), BRANCH_DTYPE),
                   jax.ShapeDtypeStruct((B, NSA_GROUPS, R, S), jnp.bfloat16)],
        scratch_shapes=_state_scratch(WIN_KEYS, QT) + [
            pltpu.VMEM((QT // TILE * N_HEADS, C, TILE), jnp.float32),
            pltpu.VMEM((QT // TILE * N_HEADS, 1, TILE), jnp.float32)],
        compiler_params=pltpu.CompilerParams(
            dimension_semantics=("arbitrary", "arbitrary"), vmem_limit_bytes=VMEM_LIMIT),
        name="local",
    )(qbT, kwi, kwi, vwiT, vwiT, kaug, kc, vcT, ovT, cmp_qaug, szT, gateT)


def _out_kernel(x_ref, oa_ref, oc_ref, os_ref, ow_ref, w_ref, g_ref, y_ref):
    f32 = jnp.float32
    ob = oc_ref[0].astype(f32) + os_ref[0].astype(f32) + ow_ref[0].astype(f32)
    mix = jnp.concatenate([oa_ref[0].astype(jnp.bfloat16), ob.astype(jnp.bfloat16)], axis=1)
    y = _dot(mix, w_ref[...])
    r = lax.rsqrt(jnp.mean(y * y, axis=-1, keepdims=True) + RMS_EPS)
    y_ref[0] = x_ref[0] + y * r * g_ref[...]


def _out(x, oa, oc, os_, ow, w, g):
    B, S, _ = x.shape
    rows = lambda wd: pl.BlockSpec((1, ROWS, wd), lambda b, i: (b, i, 0))
    return pl.pallas_call(
        _out_kernel,
        grid=(B, S // ROWS),
        in_specs=[rows(D_MODEL), rows(MOBA_W), rows(NSA_W), rows(NSA_W), rows(NSA_W),
                  pl.BlockSpec(w.shape, lambda b, i: (0, 0)),
                  pl.BlockSpec((1, D_MODEL), lambda b, i: (0, 0))],
        out_specs=rows(D_MODEL),
        out_shape=jax.ShapeDtypeStruct((B, S, D_MODEL), jnp.float32),
        compiler_params=pltpu.CompilerParams(
            dimension_semantics=("arbitrary", "arbitrary"), vmem_limit_bytes=VMEM_LIMIT),
        name="out",
    )(x, oa, oc, os_, ow, w, g)


def _compress_weights(w1, w2):
    half = (CMP_LEN // 2) * HEAD_DIM
    w1r = w1.reshape(2, CMP_LEN // 2, HEAD_DIM, CMP_HIDDEN)
    z = jnp.zeros_like(w1r)
    g0 = jnp.concatenate([w1r, z], axis=-1)
    g1 = jnp.concatenate([z, w1r], axis=-1)
    both = jnp.stack([g0, g1], axis=2)
    both = both.reshape(2, half * NSA_GROUPS, NSA_GROUPS * CMP_HIDDEN).astype(jnp.bfloat16)
    zz = jnp.zeros_like(w2)
    w2bd = jnp.concatenate([jnp.concatenate([w2, zz], axis=1),
                            jnp.concatenate([zz, w2], axis=1)], axis=0).astype(jnp.bfloat16)
    return both[0], both[1], w2bd


def _overlap_T(n_cmp_pad, n_slc, rows):
    c = np.arange(n_cmp_pad)[None, :] * CMP_STRIDE
    j = np.arange(rows)[:, None] * SLC_BLOCK
    ov = (c < j + SLC_BLOCK) & (c + CMP_LEN > j) & (np.arange(rows)[:, None] < n_slc)
    return jnp.asarray(ov.astype(np.float32), jnp.bfloat16)


def _layer(x, pre_g, post_g, w_in, pos_k, pos_v, w_k1, w_k2, w_v1, w_v2, w_out):
    B, S, _ = x.shape
    nq = S // TILE
    n_cmp = (S - CMP_LEN) // CMP_STRIDE + 1
    n_slc = S // SLC_BLOCK
    C = S // CMP_STRIDE
    wn, wt = _projection_weights(w_in)
    (ka, kmean, kcm, vcm, ksl, kwi, qaT, vaT, qbT, vslT, vwiT,
     szaT, szbT, gateT) = _proj(x, pre_g.reshape(1, D_MODEL), wn, wt)

    wkt, wkb, w2k = _compress_weights(w_k1, w_k2)
    wvt, wvb, w2v = _compress_weights(w_v1, w_v2)
    chunk = CMP_STRIDE * KV_W
    kc, vcT = _compress(
        kcm.reshape(B, C, chunk), vcm.reshape(B, C, chunk), wkt, wkb, wvt, wvb, w2k, w2v,
        pos_k.reshape(1, CMP_LEN * HEAD_DIM).astype(jnp.bfloat16),
        pos_v.reshape(1, CMP_LEN * HEAD_DIM).astype(jnp.bfloat16),
        w_k1.astype(jnp.bfloat16), w_v1.astype(jnp.bfloat16), n_cmp)

    ow, oc, selT = _local(qbT, kwi, vwiT, _win_key_aug_table(), kc, vcT,
                          _overlap_T(C, n_slc, SLC_BLOCK), _cmp_query_aug_table(),
                          szbT, gateT, n_slc)
    oa = _moba(qaT, ka, vaT, kmean.reshape(B, nq, MOBA_W), _key_aug_table(nq, 16, 1), szaT)
    osl = _slc(qbT, ksl, vslT, selT, _key_aug_table(nq, selT.shape[2], TILE // SLC_BLOCK),
               szbT, gateT)

    w_o = jnp.concatenate([w_out[:MOBA_W], _to_pair_slabs(w_out[MOBA_W:], 0)],
                          axis=0).astype(jnp.bfloat16)
    return _out(x, oa, oc, osl, ow, w_o, post_g.reshape(1, D_MODEL))


def kernel(x, pre_norm_g, post_norm_g, w_in, cmp_pos_k, cmp_pos_v,
           w_cmp_k1, w_cmp_k2, w_cmp_v1, w_cmp_v2, w_out):
    for l in range(pre_norm_g.shape[0]):
        x = _layer(x, pre_norm_g[l], post_norm_g[l], w_in[l], cmp_pos_k[l], cmp_pos_v[l],
                   w_cmp_k1[l], w_cmp_k2[l], w_cmp_v1[l], w_cmp_v2[l], w_out[l])
    return x
```

```python
import functools

import numpy as np
import jax
import jax.numpy as jnp
from jax import lax
from jax.experimental import pallas as pl
from jax.experimental.pallas import tpu as pltpu

D_MODEL = 1024
HEAD_DIM = 64
N_HEADS = 8
N_PAIRS = N_HEADS // 2
NSA_GROUPS = 2
NSA_HPG = N_HEADS // NSA_GROUPS
MOBA_BLOCK = 256
MOBA_TOPK = 3
CMP_LEN = 32
CMP_STRIDE = 16
CMP_HIDDEN = 128
SLC_BLOCK = 64
SLC_TOPN = 16
WINDOW = 512
RMS_EPS = 1e-6
NEG_INF = -1e30
FORCED_SCORE = 1e9
SCALE = HEAD_DIM ** -0.5
LOG2E = float(np.log2(np.e))

TILE = 256
QT = 2 * TILE
KEYS = 4 * TILE
ROWS = 4 * TILE
LANES = 128
HALF = HEAD_DIM
SUBLANES = 8
MOBA_W = N_HEADS * HEAD_DIM
NSA_W = N_HEADS * HEAD_DIM
KV_W = NSA_GROUPS * HEAD_DIM
N_GATES = 3 * N_HEADS
GATE_ROWS = 32
SLOPES = tuple(2.0 ** (-(i + 1)) for i in range(N_HEADS))
BRANCH_CMP, BRANCH_SLC, BRANCH_WIN = 0, 1, 2
BRANCH_DTYPE = jnp.bfloat16

VMEM_LIMIT = 48 * 1024 * 1024

_OFF = dict(qa=0, ka=512, va=1024, za=1536, qb=2048, kcm=2560, vcm=2688,
            ksl=2816, vsl=2944, kwi=3072, vwi=3200, gate=3328, zb=3352)
_NAT = dict(ka=(0, 512), kcm=(512, 640), vcm=(640, 768), ksl=(768, 896), kwi=(896, 1024))
N_NAT = 1024
_TR = dict(qa=(0, 512), va=(512, 1024), qb=(1024, 1536), vsl=(1536, 1664),
           vwi=(1664, 1792), za=(1792, 2304), zb=(2304, 2816), gate=(2816, 2848))
N_TR = 2848


def _to_pair_slabs(w, axis):
    w = jnp.moveaxis(w, axis, -1)
    lead = w.shape[:-1]
    w = w.reshape(lead + (NSA_GROUPS, NSA_HPG, HEAD_DIM))
    w = jnp.swapaxes(w, -3, -2).reshape(lead + (NSA_W,))
    return jnp.moveaxis(w, -1, axis)


def _projection_weights(w_in):
    col = lambda name, width: w_in[:, _OFF[name]:_OFF[name] + width]
    natural = jnp.concatenate(
        [col("ka", MOBA_W), col("kcm", KV_W), col("vcm", KV_W), col("ksl", KV_W), col("kwi", KV_W)],
        axis=1)
    transposed = jnp.concatenate(
        [col("qa", MOBA_W), col("va", MOBA_W), _to_pair_slabs(col("qb", NSA_W), 1),
         col("vsl", KV_W), col("vwi", KV_W), col("za", MOBA_W), _to_pair_slabs(col("zb", NSA_W), 1),
         col("gate", N_GATES), jnp.zeros((D_MODEL, GATE_ROWS - N_GATES), w_in.dtype)], axis=1)
    return natural.astype(jnp.bfloat16), transposed.T.astype(jnp.bfloat16)


def _gate_row(g, p, branch):
    return (g * NSA_HPG + p) * 3 + branch


def _sigmoid(x):
    return 1.0 / (1.0 + jnp.exp(-x))


def _silu(x):
    return x * _sigmoid(x)


def _dot(a, b):
    return jnp.dot(a, b, preferred_element_type=jnp.float32)


def _proj_kernel(x_ref, g_ref, wn_ref, wt_ref,
                 ka_ref, kmean_ref, kcm_ref, vcm_ref, ksl_ref, kwi_ref,
                 qaT_ref, vaT_ref, qbT_ref, vslT_ref, vwiT_ref,
                 szaT_ref, szbT_ref, gateT_ref):
    bf = jnp.bfloat16
    for blk in range(x_ref.shape[1] // MOBA_BLOCK):
        r0, r1 = blk * MOBA_BLOCK, (blk + 1) * MOBA_BLOCK
        x = x_ref[0, r0:r1]
        r = lax.rsqrt(jnp.mean(x * x, axis=-1, keepdims=True) + RMS_EPS)
        h = (x * r * g_ref[...]).astype(bf)

        natural = _dot(h, wn_ref[...])

        def nat(name):
            a, b = _NAT[name]
            return natural[:, a:b]

        ka = nat("ka")
        kmean_ref[0, blk] = jnp.mean(ka, axis=0, keepdims=True)
        for p in range(N_PAIRS):
            ka_ref[0, p, r0:r1] = ka[:, p * LANES:(p + 1) * LANES].astype(bf)
        kcm_ref[0, r0:r1] = nat("kcm").astype(bf)
        vcm_ref[0, r0:r1] = nat("vcm").astype(bf)
        ksl_ref[0, r0:r1] = nat("ksl").astype(bf)
        kwi_ref[0, r0:r1] = nat("kwi").astype(bf)

        def tr(name):
            a, b = _TR[name]
            return lax.dot_general(wt_ref[a:b, :], h, (((1,), (1,)), ((), ())),
                                   preferred_element_type=jnp.float32)

        qa = tr("qa") * (SCALE * LOG2E)
        va = tr("va")
        qb = tr("qb") * (SCALE * LOG2E)
        sza = _silu(tr("za"))
        szb = _silu(tr("zb"))
        for p in range(N_PAIRS):
            sl = slice(p * LANES, (p + 1) * LANES)
            qaT_ref[0, p, :, r0:r1] = qa[sl].astype(bf)
            vaT_ref[0, p, :, r0:r1] = va[sl].astype(bf)
            qbT_ref[0, p, :, r0:r1] = qb[sl].astype(bf)
            szaT_ref[0, p, :, r0:r1] = sza[sl]
            szbT_ref[0, p, :, r0:r1] = szb[sl]
        vslT_ref[0, :, r0:r1] = tr("vsl").astype(bf)
        vwiT_ref[0, :, r0:r1] = tr("vwi").astype(bf)
        gateT_ref[0, :, r0:r1] = _sigmoid(tr("gate"))


def _proj(x, g, wn, wt):
    B, S, _ = x.shape
    nq = S // TILE
    bf, f32 = jnp.bfloat16, jnp.float32
    slab_nat = lambda: pl.BlockSpec((1, N_PAIRS, ROWS, LANES), lambda b, i: (b, 0, i, 0))
    slab_tr = lambda: pl.BlockSpec((1, N_PAIRS, LANES, ROWS), lambda b, i: (b, 0, 0, i))
    rows = lambda w: pl.BlockSpec((1, ROWS, w), lambda b, i: (b, i, 0))
    chans = lambda c: pl.BlockSpec((1, c, ROWS), lambda b, i: (b, 0, i))
    out_shape = [
        jax.ShapeDtypeStruct((B, N_PAIRS, S, LANES), bf),
        jax.ShapeDtypeStruct((B, nq, 1, MOBA_W), f32),
        jax.ShapeDtypeStruct((B, S, KV_W), bf),
        jax.ShapeDtypeStruct((B, S, KV_W), bf),
        jax.ShapeDtypeStruct((B, S, KV_W), bf),
        jax.ShapeDtypeStruct((B, S, KV_W), bf),
        jax.ShapeDtypeStruct((B, N_PAIRS, LANES, S), bf),
        jax.ShapeDtypeStruct((B, N_PAIRS, LANES, S), bf),
        jax.ShapeDtypeStruct((B, N_PAIRS, LANES, S), bf),
        jax.ShapeDtypeStruct((B, KV_W, S), bf),
        jax.ShapeDtypeStruct((B, KV_W, S), bf),
        jax.ShapeDtypeStruct((B, N_PAIRS, LANES, S), f32),
        jax.ShapeDtypeStruct((B, N_PAIRS, LANES, S), f32),
        jax.ShapeDtypeStruct((B, GATE_ROWS, S), f32),
    ]
    out_specs = [
        slab_nat(),
        pl.BlockSpec((1, ROWS // MOBA_BLOCK, 1, MOBA_W), lambda b, i: (b, i, 0, 0)),
        rows(KV_W), rows(KV_W), rows(KV_W), rows(KV_W),
        slab_tr(), slab_tr(), slab_tr(),
        chans(KV_W), chans(KV_W),
        slab_tr(), slab_tr(),
        chans(GATE_ROWS),
    ]
    return pl.pallas_call(
        _proj_kernel,
        grid=(B, S // ROWS),
        in_specs=[
            pl.BlockSpec((1, ROWS, D_MODEL), lambda b, i: (b, i, 0)),
            pl.BlockSpec((1, D_MODEL), lambda b, i: (0, 0)),
            pl.BlockSpec((D_MODEL, N_NAT), lambda b, i: (0, 0)),
            pl.BlockSpec((N_TR, D_MODEL), lambda b, i: (0, 0)),
        ],
        out_specs=out_specs,
        out_shape=out_shape,
        compiler_params=pltpu.CompilerParams(
            dimension_semantics=("arbitrary", "arbitrary"),
            vmem_limit_bytes=VMEM_LIMIT),
        name="proj",
    )(x, g, wn, wt)


def _compress_kernel(n_cmp, xk_ref, xv_ref, wkt_ref, wkb_ref, wvt_ref, wvb_ref,
                     w2k_ref, w2v_ref, pek_ref, pev_ref, w1k_ref, w1v_ref,
                     kc_ref, vcT_ref):
    C = xk_ref.shape[1]
    row = lax.broadcasted_iota(jnp.int32, (C, KV_W), 0)

    def phi(x_ref, wt_ref, wb_ref, w2_ref, pe_ref, w1_ref):
        x = x_ref[0]
        top = _dot(x, wt_ref[...])
        bot = _dot(x, wb_ref[...])
        peb = _dot(jnp.broadcast_to(pe_ref[...], (8, pe_ref.shape[1])), w1_ref[...])[0:1]
        peb = jnp.concatenate([peb, peb], axis=1)
        hid = top + pltpu.roll(bot, C - 1, 0) + peb
        out = _dot(_silu(hid).astype(jnp.bfloat16), w2_ref[...])
        return jnp.where(row < n_cmp, out, 0.0)

    kc_ref[0] = phi(xk_ref, wkt_ref, wkb_ref, w2k_ref, pek_ref, w1k_ref).astype(jnp.bfloat16)
    vc = phi(xv_ref, wvt_ref, wvb_ref, w2v_ref, pev_ref, w1v_ref)
    vcT_ref[0] = vc.T.astype(jnp.bfloat16)


def _compress(xk, xv, wkt, wkb, wvt, wvb, w2k, w2v, pek, pev, w1k, w1v, n_cmp):
    B, C, F = xk.shape
    full = lambda a: pl.BlockSpec(a.shape, lambda b: (0,) * a.ndim)
    return pl.pallas_call(
        functools.partial(_compress_kernel, n_cmp),
        grid=(B,),
        in_specs=[pl.BlockSpec((1, C, F), lambda b: (b, 0, 0)),
                  pl.BlockSpec((1, C, F), lambda b: (b, 0, 0)),
                  full(wkt), full(wkb), full(wvt), full(wvb), full(w2k), full(w2v),
                  full(pek), full(pev), full(w1k), full(w1v)],
        out_specs=[pl.BlockSpec((1, C, KV_W), lambda b: (b, 0, 0)),
                   pl.BlockSpec((1, KV_W, C), lambda b: (b, 0, 0))],
        out_shape=[jax.ShapeDtypeStruct((B, C, KV_W), jnp.bfloat16),
                   jax.ShapeDtypeStruct((B, KV_W, C), jnp.bfloat16)],
        compiler_params=pltpu.CompilerParams(
            dimension_semantics=("arbitrary",), vmem_limit_bytes=VMEM_LIMIT),
        name="compress",
    )(xk, xv, wkt, wkb, wvt, wvb, w2k, w2v, pek, pev, w1k, w1v)


def _keep_half(qT, half):
    z = jnp.zeros((HALF, qT.shape[1]), qT.dtype)
    if half == 0:
        return jnp.concatenate([qT[0:HALF], z], axis=0)
    return jnp.concatenate([z, qT[HALF:2 * HALF]], axis=0)


def _rank_rows(vals, n_rows):
    R, T = vals.shape
    tiles = [vals[a:a + SUBLANES] for a in range(0, R, SUBLANES)]
    ranks = [jnp.zeros((SUBLANES, T), jnp.int32) for _ in tiles]
    j_in = lax.broadcasted_iota(jnp.int32, (SUBLANES, T), 0)
    for m in range(n_rows):
        vm = vals[m:m + 1, :]
        for a, tile in enumerate(tiles):
            lo = a * SUBLANES
            if lo > m:
                beats = vm >= tile
            elif lo + SUBLANES - 1 <= m:
                beats = vm > tile
            else:
                beats = (vm > tile) | ((vm == tile) & (j_in > m - lo))
            ranks[a] = ranks[a] + beats.astype(jnp.int32)
    return jnp.concatenate(ranks, axis=0)


def _query_tile(cols, tile0):
    return tile0 + lax.shift_right_logical(cols, int(np.log2(TILE)))


def _split_bf16(x):
    hi = x.astype(jnp.bfloat16).astype(jnp.float32)
    return hi, x - hi


def _slope_parts(slope):
    c = np.float32(slope * LOG2E)
    hi = np.asarray(c, jnp.bfloat16).astype(np.float32)
    lo = np.asarray(c - hi, jnp.bfloat16).astype(np.float32)
    return float(hi), float(lo)


AUG_ROWS = 16


def _pad_rows(x, n_rows):
    return jnp.concatenate([x, jnp.zeros((n_rows - x.shape[0], x.shape[1]), x.dtype)], axis=0)


def _alibi_rows(T, slope, tile0):
    r = lax.broadcasted_iota(jnp.int32, (AUG_ROWS, T), 0)
    c = lax.broadcasted_iota(jnp.int32, (AUG_ROWS, T), 1)
    t_abs = ((c & (TILE - 1)) + _query_tile(c, tile0) * TILE).astype(jnp.float32)
    c_hi, c_lo = _slope_parts(slope)
    w_hi, w_lo = _split_bf16(np.float32(slope * LOG2E) * t_abs)
    out = jnp.where(r == 0, -w_hi, 0.0)
    out = jnp.where(r == 1, -w_lo, out)
    out = jnp.where((r == 2) | (r == 4), c_hi, out)
    out = jnp.where((r == 3) | (r == 5), c_lo, out)
    return out


def _key_aug_table(nq, n_sel, blocks_per_tile):
    t = np.zeros((nq, TILE, LANES), np.float32)
    krel = np.arange(TILE)
    for j in range(nq):
        if n_sel:
            blk = j * blocks_per_tile + krel // (TILE // blocks_per_tile)
            t[j, krel, blk] = 1.0
        t[j, :, n_sel:n_sel + 2] = 1.0
        t[j, :, n_sel + 2:n_sel + 4] = krel[:, None]
        t[j, :, n_sel + 4:n_sel + 6] = TILE * j
    return jnp.asarray(t.reshape(nq * TILE // KEYS, KEYS, LANES), jnp.bfloat16)


ONES_ROWS = 16
ACC_ROWS = HALF + ONES_ROWS


WIN_TILES = WINDOW // TILE
WIN_KEYS = (WIN_TILES + QT // TILE) * TILE


def _win_key_aug_table():
    t = np.zeros((WIN_KEYS, LANES), np.float32)
    for slot in range(WIN_KEYS // TILE):
        rows = slice(slot * TILE, (slot + 1) * TILE)
        t[rows, 0:2] = 1.0
        t[rows, 2:4] = np.arange(TILE)[:, None]
        t[rows, 4:6] = TILE * (WIN_TILES - slot)
        if slot < WIN_TILES:
            t[rows, 6] = 1.0
    return jnp.asarray(t, jnp.bfloat16)


def _value_rows(vT_h):
    return jnp.concatenate([vT_h, jnp.ones((ONES_ROWS, vT_h.shape[1]), vT_h.dtype)], axis=0)


def _stage_scores(h, score_fn, blocks, cols, s_ref, mx_ref):
    c0, c1 = cols
    base = blocks[0][0]
    s_all = score_fn(h, base, blocks[-1][1], c0, c1)
    mx = None
    for r0, r1, keep in blocks:
        s = s_all[r0 - base:r1 - base]
        if keep is not None:
            s = jnp.where(keep, s, NEG_INF)
        s_ref[h, r0:r1, c0:c1] = s
        part = jnp.max(s, axis=0, keepdims=True)
        mx = part if mx is None else jnp.maximum(mx, part)
    mx_ref[h, :, c0:c1] = mx


def _consume_scores(h, value_fn, rows, cols, s_ref, mx_ref, m_ref, acc_ref):
    (r0, r1), (c0, c1) = rows, cols
    m_prev = m_ref[h, :, c0:c1]
    m_new = jnp.maximum(m_prev, mx_ref[h, :, c0:c1])
    alpha = jnp.exp2(m_prev - m_new)
    p = jnp.exp2(s_ref[h, r0:r1, c0:c1] - m_new).astype(jnp.bfloat16)
    acc_ref[h, :, c0:c1] = alpha * acc_ref[h, :, c0:c1] + _dot(value_fn(h, r0, r1), p)
    m_ref[h, :, c0:c1] = m_new


def _attend_heads(score_fn, value_fn, parts, s_ref, mx_ref, m_ref, acc_ref):
    for h in range(N_HEADS):
        for cols, blocks in parts:
            _stage_scores(h, score_fn, blocks, cols, s_ref, mx_ref)
    for h in range(N_HEADS):
        for cols, blocks in parts:
            rows = (min(b[0] for b in blocks), max(b[1] for b in blocks))
            _consume_scores(h, value_fn, rows, cols, s_ref, mx_ref, m_ref, acc_ref)


def _init_state(m_ref, acc_ref):
    m_ref[...] = jnp.full(m_ref.shape, NEG_INF, jnp.float32)
    acc_ref[...] = jnp.zeros(acc_ref.shape, jnp.float32)


def _tile_iotas(T=TILE):
    k = lax.broadcasted_iota(jnp.int32, (TILE, T), 0)
    t = lax.broadcasted_iota(jnp.int32, (TILE, T), 1)
    return k, t


def _key_step_parts(kind):
    if kind == "past":
        return [((0, QT), [(0, KEYS, None)])]
    k, t = _tile_iotas()
    tri = k <= t
    own = QT if kind == "reach" else 0
    first = [(0, own, None)] if own else []
    return [((0, TILE), first + [(own, own + TILE, tri)]),
            ((TILE, QT), [(0, own + TILE, None), (own + TILE, own + QT, tri)])]


def _store_pair(o_ref, p, o0T, o1T, szT, gates, rows=None):
    if gates is not None:
        o0T = o0T * gates[0]
        o1T = o1T * gates[1]
    oT = jnp.concatenate([o0T, o1T], axis=0) * szT
    r0, r1 = rows if rows is not None else (0, o_ref.shape[1])
    o_ref[0, r0:r1, p * LANES:(p + 1) * LANES] = oT.T.astype(o_ref.dtype)


def _finalize_pair(o_ref, acc_ref, p, szT, gates=None):
    a0 = acc_ref[2 * p]
    a1 = acc_ref[2 * p + 1]
    _store_pair(o_ref, p, a0[0:HALF] / a0[HALF:HALF + 1], a1[0:HALF] / a1[HALF:HALF + 1],
                szT, gates)


def _nsa_gates(gate_ref, p, branch, cols=None):
    c0, c1 = cols if cols is not None else (0, gate_ref.shape[2])
    return [gate_ref[0, _gate_row(g, p, branch):_gate_row(g, p, branch) + 1, c0:c1]
            for g in range(NSA_GROUPS)]


def _step_tables(n_q):
    it, jt = [], []
    for ii in range(n_q):
        for js in range(ii * QT // KEYS + 1):
            it.append(ii)
            jt.append(js)
    return jnp.asarray(it, jnp.int32), jnp.asarray(jt, jnp.int32)


def _key_steps(ii, js, score_fn, value_fn, finalize_fn, s_ref, mx_ref, m_ref, acc_ref):
    q0 = ii * QT
    k0 = js * KEYS
    for kind, when in (("past", k0 + KEYS <= q0), ("reach", k0 + QT == q0), ("own", k0 == q0)):
        @pl.when(when)
        def _():
            _attend_heads(score_fn, value_fn, _key_step_parts(kind), s_ref, mx_ref, m_ref, acc_ref)
            if kind != "past":
                finalize_fn()


def _state_scratch(n_keys, T=TILE):
    return [
        pltpu.VMEM((N_HEADS, 2 * LANES, T), jnp.bfloat16),
        pltpu.VMEM((N_HEADS, 1, T), jnp.float32),
        pltpu.VMEM((N_HEADS, ACC_ROWS, T), jnp.float32),
        pltpu.VMEM((N_HEADS, n_keys, T), jnp.float32),
        pltpu.VMEM((N_HEADS, 1, T), jnp.float32),
    ]


def _moba_kernel(nq, it_ref, jt_ref, qT_ref, k_ref, vT_ref, kmean_ref, kaug_ref, sz_ref, o_ref,
                 qaug_ref, m_ref, acc_ref, s_ref, mx_ref):
    s = pl.program_id(1)
    ii = it_ref[s]
    jj = jt_ref[s]
    n_sel = 16

    @pl.when(jj == 0)
    def _():
        _init_state(m_ref, acc_ref)
        n_iota = lax.broadcasted_iota(jnp.int32, (nq, QT), 0)
        own = _query_tile(lax.broadcasted_iota(jnp.int32, (nq, QT), 1), 2 * ii)
        for p in range(N_PAIRS):
            qT = qT_ref[0, p]
            km = kmean_ref[0, :, p * LANES:(p + 1) * LANES].astype(jnp.bfloat16)
            for hh in range(2):
                h = 2 * p + hh
                qm = _keep_half(qT, hh)
                gate = jnp.where(n_iota < own, _dot(km, qm), NEG_INF)
                rank = _rank_rows(gate, nq)
                sel = ((rank < MOBA_TOPK) & (n_iota < own)) | (n_iota == own)
                selneg = jnp.where(sel, 0.0, NEG_INF)
                if nq < n_sel:
                    selneg = jnp.concatenate(
                        [selneg, jnp.zeros((n_sel - nq, QT), jnp.float32)], axis=0)
                ali = _alibi_rows(QT, SLOPES[h], 2 * ii)
                aug = jnp.concatenate([selneg, ali], axis=0).astype(jnp.bfloat16)
                qaug_ref[h] = _pad_rows(jnp.concatenate([qm, aug], axis=0), 2 * LANES)

    def score(h, r0, r1, c0, c1):
        kfull = jnp.concatenate([k_ref[0, h // 2, r0:r1], kaug_ref[0, r0:r1]], axis=1)
        return _dot(kfull, qaug_ref[h, :, c0:c1])

    def value(h, r0, r1):
        return _value_rows(vT_ref[0, h // 2, (h % 2) * HALF:(h % 2 + 1) * HALF, r0:r1])

    def finalize():
        for p in range(N_PAIRS):
            _finalize_pair(o_ref, acc_ref, p, sz_ref[0, p])

    _key_steps(ii, jj, score, value, finalize, s_ref, mx_ref, m_ref, acc_ref)


def _moba(qT, k, vT, kmean, kaug, szT):
    B, _, _, S = qT.shape
    nq = S // TILE
    it, jt = _step_tables(S // QT)
    return pl.pallas_call(
        functools.partial(_moba_kernel, nq),
        grid_spec=pltpu.PrefetchScalarGridSpec(
            num_scalar_prefetch=2,
            grid=(B, int(it.shape[0])),
            in_specs=[
                pl.BlockSpec((1, N_PAIRS, LANES, QT), lambda b, s, it, jt: (b, 0, 0, it[s])),
                pl.BlockSpec((1, N_PAIRS, KEYS, LANES), lambda b, s, it, jt: (b, 0, jt[s], 0)),
                pl.BlockSpec((1, N_PAIRS, LANES, KEYS), lambda b, s, it, jt: (b, 0, 0, jt[s])),
                pl.BlockSpec((1, nq, MOBA_W), lambda b, s, it, jt: (b, 0, 0)),
                pl.BlockSpec((1, KEYS, LANES), lambda b, s, it, jt: (jt[s], 0, 0)),
                pl.BlockSpec((1, N_PAIRS, LANES, QT), lambda b, s, it, jt: (b, 0, 0, it[s])),
            ],
            out_specs=pl.BlockSpec((1, QT, MOBA_W), lambda b, s, it, jt: (b, it[s], 0)),
            scratch_shapes=_state_scratch(KEYS, QT),
        ),
        out_shape=jax.ShapeDtypeStruct((B, S, MOBA_W), BRANCH_DTYPE),
        compiler_params=pltpu.CompilerParams(
            dimension_semantics=("arbitrary", "arbitrary"), vmem_limit_bytes=VMEM_LIMIT),
        name="moba",
    )(it, jt, qT, k, vT, kmean, kaug, szT)


CMP_BAND = TILE // CMP_STRIDE + 1
RANK_ROWS_STEP = 16


def _cmp_query_aug_table():
    t = np.zeros((LANES, TILE), np.float32)
    trel = np.arange(TILE)
    for u in range(CMP_BAND):
        t[u, trel < CMP_STRIDE * (u - 1) + CMP_LEN - 1] = NEG_INF
    t[CMP_BAND, :] = NEG_INF
    return jnp.asarray(t, jnp.bfloat16)


def _cmp_tile(i, cols, slot, n_slc, qT_ref, kc_ref, vcT_ref, ovT_ref, qaug_ref, sz_ref, gate_ref,
              o_ref, sel_ref, s_ref, mx_ref):
    c0, c1 = cols
    C = kc_ref.shape[1]
    c_rel = (lax.broadcasted_iota(jnp.int32, (C, LANES), 0)
             - (TILE // CMP_STRIDE) * i + 1)
    u = lax.broadcasted_iota(jnp.int32, (C, LANES), 1)
    kaug = jnp.where((c_rel == u) & (u < CMP_BAND), 1.0, 0.0)
    kaug = jnp.where((u == CMP_BAND) & (c_rel >= CMP_BAND), 1.0, kaug)
    kfull = jnp.concatenate([kc_ref[0], kaug.astype(jnp.bfloat16)], axis=1)
    qaug = qaug_ref[...]

    for p in range(N_PAIRS):
        qT = qT_ref[0, p, :, c0:c1]
        for g in range(NSA_GROUPS):
            h = slot * N_HEADS + 2 * p + g
            s = _dot(kfull, jnp.concatenate([_keep_half(qT, g), qaug], axis=0))
            s_ref[h] = s
            mx_ref[h] = jnp.max(s, axis=0, keepdims=True)

    t_abs = i * TILE + lax.broadcasted_iota(jnp.int32, (1, TILE), 1)
    seen = t_abs >= CMP_LEN - 1
    ones = jnp.ones((ONES_ROWS, C), jnp.bfloat16)
    vals = [jnp.concatenate([vcT_ref[0, g * HALF:(g + 1) * HALF], ones, ovT_ref[...]], axis=0)
            for g in range(NSA_GROUPS)]
    R = sel_ref.shape[2]
    imp = [jnp.zeros((R, TILE), jnp.float32) for _ in range(NSA_GROUPS)]
    for p in range(N_PAIRS):
        outs = []
        for g in range(NSA_GROUPS):
            h = slot * N_HEADS + 2 * p + g
            pr = jnp.exp2(s_ref[h] - mx_ref[h]).astype(jnp.bfloat16)
            acc = _dot(vals[g], pr)
            inv = jnp.where(seen, 1.0 / acc[HALF:HALF + 1], 0.0)
            outs.append(acc[0:HALF] * inv)
            imp[g] = imp[g] + acc[ACC_ROWS:ACC_ROWS + R] * inv
        _store_pair(o_ref, p, outs[0], outs[1], sz_ref[0, p, :, c0:c1],
                    _nsa_gates(gate_ref, p, BRANCH_CMP, cols), rows=cols)

    def select_blocks():
        j_iota = lax.broadcasted_iota(jnp.int32, (R, TILE), 0)
        t_q = i * TILE + lax.broadcasted_iota(jnp.int32, (R, TILE), 1)
        own = lax.shift_right_logical(t_q, int(np.log2(SLC_BLOCK)))
        forced = (j_iota == 0) | (j_iota == own) | (j_iota == own - 1)
        causal = j_iota <= own

        def select(n_rows):
            for g in range(NSA_GROUPS):
                v = jnp.where(forced, FORCED_SCORE, jnp.where(causal, imp[g], NEG_INF))[0:n_rows]
                rank = _rank_rows(v, min(n_rows, n_slc))
                sel = (rank < min(SLC_TOPN, n_slc)) & causal[0:n_rows]
                selneg = jnp.where(sel, 0.0, NEG_INF)
                if n_rows < R:
                    selneg = jnp.concatenate(
                        [selneg, jnp.full((R - n_rows, TILE), NEG_INF, jnp.float32)], axis=0)
                sel_ref[0, g, :, c0:c1] = selneg.astype(jnp.bfloat16)

        blocks_per_tile = TILE // SLC_BLOCK
        variants = list(range(RANK_ROWS_STEP, R, RANK_ROWS_STEP)) + [R]
        for idx, n_rows in enumerate(variants):
            lo = 0 if idx == 0 else variants[idx - 1] // blocks_per_tile
            hi = n_rows // blocks_per_tile

            @pl.when((i >= lo) & (i < hi) if n_rows < R else i >= lo)
            def _():
                select(n_rows)

    return select_blocks


def _slc_kernel(it_ref, jt_ref, qT_ref, k_ref, vT_ref, sel_ref, kaug_ref, sz_ref, gate_ref, o_ref,
                qaug_ref, m_ref, acc_ref, s_ref, mx_ref):
    s = pl.program_id(1)
    ii = it_ref[s]
    jj = jt_ref[s]
    n_sel = sel_ref.shape[2]

    @pl.when(jj == 0)
    def _():
        _init_state(m_ref, acc_ref)
        for p in range(N_PAIRS):
            qT = qT_ref[0, p]
            for g in range(NSA_GROUPS):
                h = 2 * p + g
                qm = _keep_half(qT, g)
                ali = _alibi_rows(QT, SLOPES[g * NSA_HPG + p], 2 * ii)
                qaug_ref[h] = _pad_rows(
                    jnp.concatenate([qm, sel_ref[0, g], ali.astype(jnp.bfloat16)], axis=0),
                    2 * LANES)

    def score(h, r0, r1, c0, c1):
        kfull = jnp.concatenate([k_ref[0, r0:r1], kaug_ref[0, r0:r1]], axis=1)
        return _dot(kfull, qaug_ref[h, :, c0:c1])

    def value(h, r0, r1):
        return _value_rows(vT_ref[0, (h % 2) * HALF:(h % 2 + 1) * HALF, r0:r1])

    def finalize():
        for p in range(N_PAIRS):
            _finalize_pair(o_ref, acc_ref, p, sz_ref[0, p], _nsa_gates(gate_ref, p, BRANCH_SLC))

    _key_steps(ii, jj, score, value, finalize, s_ref, mx_ref, m_ref, acc_ref)


def _slc(qbT, ksl, vslT, selT, kaug, szT, gateT):
    B, _, _, S = qbT.shape
    R = selT.shape[2]
    it, jt = _step_tables(S // QT)
    return pl.pallas_call(
        _slc_kernel,
        grid_spec=pltpu.PrefetchScalarGridSpec(
            num_scalar_prefetch=2,
            grid=(B, int(it.shape[0])),
            in_specs=[
                pl.BlockSpec((1, N_PAIRS, LANES, QT), lambda b, s, it, jt: (b, 0, 0, it[s])),
                pl.BlockSpec((1, KEYS, KV_W), lambda b, s, it, jt: (b, jt[s], 0)),
                pl.BlockSpec((1, KV_W, KEYS), lambda b, s, it, jt: (b, 0, jt[s])),
                pl.BlockSpec((1, NSA_GROUPS, R, QT), lambda b, s, it, jt: (b, 0, 0, it[s])),
                pl.BlockSpec((1, KEYS, LANES), lambda b, s, it, jt: (jt[s], 0, 0)),
                pl.BlockSpec((1, N_PAIRS, LANES, QT), lambda b, s, it, jt: (b, 0, 0, it[s])),
                pl.BlockSpec((1, GATE_ROWS, QT), lambda b, s, it, jt: (b, 0, it[s])),
            ],
            out_specs=pl.BlockSpec((1, QT, NSA_W), lambda b, s, it, jt: (b, it[s], 0)),
            scratch_shapes=_state_scratch(KEYS, QT),
        ),
        out_shape=jax.ShapeDtypeStruct((B, S, NSA_W), BRANCH_DTYPE),
        compiler_params=pltpu.CompilerParams(
            dimension_semantics=("arbitrary", "arbitrary"), vmem_limit_bytes=VMEM_LIMIT),
        name="slc",
    )(it, jt, qbT, ksl, vslT, selT, kaug, szT, gateT)


def _local_kernel(n_slc, qT_ref, kb_ref, kq_ref, vb_ref, vq_ref, kaug_ref,
                  kc_ref, vcT_ref, ovT_ref, cq_ref, sz_ref, gate_ref,
                  o_ref, oc_ref, sel_ref,
                  qaug_ref, m_ref, acc_ref, s_ref, mx_ref, cs_ref, cmx_ref):
    i = pl.program_id(1)
    selects = [
        _cmp_tile(i * (QT // TILE) + q, (q * TILE, (q + 1) * TILE), q, n_slc, qT_ref, kc_ref,
                  vcT_ref, ovT_ref, cq_ref, sz_ref, gate_ref, oc_ref, sel_ref, cs_ref, cmx_ref)
        for q in range(QT // TILE)]
    _init_state(m_ref, acc_ref)
    r = lax.broadcasted_iota(jnp.int32, (AUG_ROWS, QT), 0)
    c = lax.broadcasted_iota(jnp.int32, (AUG_ROWS, QT), 1)
    trel = (c & (TILE - 1)).astype(jnp.float32)
    absent = jnp.where(i >= 1, 0.0, NEG_INF)
    for p in range(N_PAIRS):
        qT = qT_ref[0, p]
        for g in range(NSA_GROUPS):
            h = 2 * p + g
            slope = SLOPES[g * NSA_HPG + p]
            c_hi, c_lo = _slope_parts(slope)
            w_hi, w_lo = _split_bf16(np.float32(slope * LOG2E) * trel)
            aug = jnp.where(r == 0, -w_hi, 0.0)
            aug = jnp.where(r == 1, -w_lo, aug)
            aug = jnp.where(r == 2, c_hi, aug)
            aug = jnp.where(r == 3, c_lo, aug)
            aug = jnp.where(r == 4, -c_hi, aug)
            aug = jnp.where(r == 5, -c_lo, aug)
            aug = jnp.where(r == 6, absent, aug)
            qaug_ref[h] = _pad_rows(
                jnp.concatenate([_keep_half(qT, g), aug.astype(jnp.bfloat16)], axis=0), 2 * LANES)

    kfull = jnp.concatenate(
        [jnp.concatenate([kb_ref[0], kq_ref[0]], axis=0), kaug_ref[...]], axis=1)
    vT = jnp.concatenate([vb_ref[0], vq_ref[0]], axis=1)
    k, t = _tile_iotas()

    def score(h, r0, r1, c0, c1):
        return _dot(kfull[r0:r1], qaug_ref[h, :, c0:c1])

    def value(h, r0, r1):
        return _value_rows(vT[(h % 2) * HALF:(h % 2 + 1) * HALF, r0:r1])

    parts = []
    for q in range(QT // TILE):
        first, last = q * TILE, (q + WIN_TILES) * TILE
        parts.append(((q * TILE, (q + 1) * TILE),
                      [(first, first + TILE, k > t), (first + TILE, last, None),
                       (last, last + TILE, k <= t)]))
    _attend_heads(score, value, parts, s_ref, mx_ref, m_ref, acc_ref)
    for p in range(N_PAIRS):
        _finalize_pair(o_ref, acc_ref, p, sz_ref[0, p], _nsa_gates(gate_ref, p, BRANCH_WIN))
    for select_blocks in selects:
        select_blocks()


def _local(qbT, kwi, vwiT, kaug, kc, vcT, ovT, cmp_qaug, szT, gateT, n_slc):
    B, _, _, S = qbT.shape
    assert WINDOW == QT
    C = kc.shape[1]
    R = ovT.shape[0]
    before = lambda i: jnp.maximum(i - 1, 0)
    q_slabs = pl.BlockSpec((1, N_PAIRS, LANES, QT), lambda b, i: (b, 0, 0, i))
    whole = lambda a: pl.BlockSpec(a.shape, lambda b, i: (0,) * a.ndim)
    rows_out = pl.BlockSpec((1, QT, NSA_W), lambda b, i: (b, i, 0))
    return pl.pallas_call(
        functools.partial(_local_kernel, n_slc),
        grid=(B, S // QT),
        in_specs=[
            q_slabs,
            pl.BlockSpec((1, WINDOW, KV_W), lambda b, i: (b, before(i), 0)),
            pl.BlockSpec((1, QT, KV_W), lambda b, i: (b, i, 0)),
            pl.BlockSpec((1, KV_W, WINDOW), lambda b, i: (b, 0, before(i))),
            pl.BlockSpec((1, KV_W, QT), lambda b, i: (b, 0, i)),
            whole(kaug),
            pl.BlockSpec((1, C, KV_W), lambda b, i: (b, 0, 0)),
            pl.BlockSpec((1, KV_W, C), lambda b, i: (b, 0, 0)),
            whole(ovT), whole(cmp_qaug),
            q_slabs,
            pl.BlockSpec((1, GATE_ROWS, QT), lambda b, i: (b, 0, i)),
        ],
        out_specs=[rows_out, rows_out,
                   pl.BlockSpec((1, NSA_GROUPS, R, QT), lambda b, i: (b, 0, 0, i))],
        out_shape=[jax.ShapeDtypeStruct((B, S, NSA_W), BRANCH_DTYPE),
                   jax.ShapeDtypeStruct((B, S, NSA_W), BRANCH_DTYPE),
                   jax.ShapeDtypeStruct((B, NSA_GROUPS, R, S), jnp.bfloat16)],
        scratch_shapes=_state_scratch(WIN_KEYS, QT) + [
            pltpu.VMEM((QT // TILE * N_HEADS, C, TILE), jnp.float32),
            pltpu.VMEM((QT // TILE * N_HEADS, 1, TILE), jnp.float32)],
        compiler_params=pltpu.CompilerParams(
            dimension_semantics=("arbitrary", "arbitrary"), vmem_limit_bytes=VMEM_LIMIT),
        name="local",
    )(qbT, kwi, kwi, vwiT, vwiT, kaug, kc, vcT, ovT, cmp_qaug, szT, gateT)


def _out_kernel(x_ref, oa_ref, oc_ref, os_ref, ow_ref, w_ref, g_ref, y_ref):
    f32 = jnp.float32
    ob = oc_ref[0].astype(f32) + os_ref[0].astype(f32) + ow_ref[0].astype(f32)
    mix = jnp.concatenate([oa_ref[0].astype(jnp.bfloat16), ob.astype(jnp.bfloat16)], axis=1)
    y = _dot(mix, w_ref[...])
    r = lax.rsqrt(jnp.mean(y * y, axis=-1, keepdims=True) + RMS_EPS)
    y_ref[0] = x_ref[0] + y * r * g_ref[...]


def _out(x, oa, oc, os_, ow, w, g):
    B, S, _ = x.shape
    rows = lambda wd: pl.BlockSpec((1, ROWS, wd), lambda b, i: (b, i, 0))
    return pl.pallas_call(
        _out_kernel,
        grid=(B, S // ROWS),
        in_specs=[rows(D_MODEL), rows(MOBA_W), rows(NSA_W), rows(NSA_W), rows(NSA_W),
                  pl.BlockSpec(w.shape, lambda b, i: (0, 0)),
                  pl.BlockSpec((1, D_MODEL), lambda b, i: (0, 0))],
        out_specs=rows(D_MODEL),
        out_shape=jax.ShapeDtypeStruct((B, S, D_MODEL), jnp.float32),
        compiler_params=pltpu.CompilerParams(
            dimension_semantics=("arbitrary", "arbitrary"), vmem_limit_bytes=VMEM_LIMIT),
        name="out",
    )(x, oa, oc, os_, ow, w, g)


def _compress_weights(w1, w2):
    half = (CMP_LEN // 2) * HEAD_DIM
    w1r = w1.reshape(2, CMP_LEN // 2, HEAD_DIM, CMP_HIDDEN)
    z = jnp.zeros_like(w1r)
    g0 = jnp.concatenate([w1r, z], axis=-1)
    g1 = jnp.concatenate([z, w1r], axis=-1)
    both = jnp.stack([g0, g1], axis=2)
    both = both.reshape(2, half * NSA_GROUPS, NSA_GROUPS * CMP_HIDDEN).astype(jnp.bfloat16)
    zz = jnp.zeros_like(w2)
    w2bd = jnp.concatenate([jnp.concatenate([w2, zz], axis=1),
                            jnp.concatenate([zz, w2], axis=1)], axis=0).astype(jnp.bfloat16)
    return both[0], both[1], w2bd


def _overlap_T(n_cmp_pad, n_slc, rows):
    c = np.arange(n_cmp_pad)[None, :] * CMP_STRIDE
    j = np.arange(rows)[:, None] * SLC_BLOCK
    ov = (c < j + SLC_BLOCK) & (c + CMP_LEN > j) & (np.arange(rows)[:, None] < n_slc)
    return jnp.asarray(ov.astype(np.float32), jnp.bfloat16)


def _layer(x, pre_g, post_g, w_in, pos_k, pos_v, w_k1, w_k2, w_v1, w_v2, w_out):
    B, S, _ = x.shape
    nq = S // TILE
    n_cmp = (S - CMP_LEN) // CMP_STRIDE + 1
    n_slc = S // SLC_BLOCK
    C = S // CMP_STRIDE
    wn, wt = _projection_weights(w_in)
    (ka, kmean, kcm, vcm, ksl, kwi, qaT, vaT, qbT, vslT, vwiT,
     szaT, szbT, gateT) = _proj(x, pre_g.reshape(1, D_MODEL), wn, wt)

    wkt, wkb, w2k = _compress_weights(w_k1, w_k2)
    wvt, wvb, w2v = _compress_weights(w_v1, w_v2)
    chunk = CMP_STRIDE * KV_W
    kc, vcT = _compress(
        kcm.reshape(B, C, chunk), vcm.reshape(B, C, chunk), wkt, wkb, wvt, wvb, w2k, w2v,
        pos_k.reshape(1, CMP_LEN * HEAD_DIM).astype(jnp.bfloat16),
        pos_v.reshape(1, CMP_LEN * HEAD_DIM).astype(jnp.bfloat16),
        w_k1.astype(jnp.bfloat16), w_v1.astype(jnp.bfloat16), n_cmp)

    ow, oc, selT = _local(qbT, kwi, vwiT, _win_key_aug_table(), kc, vcT,
                          _overlap_T(C, n_slc, SLC_BLOCK), _cmp_query_aug_table(),
                          szbT, gateT, n_slc)
    oa = _moba(qaT, ka, vaT, kmean.reshape(B, nq, MOBA_W), _key_aug_table(nq, 16, 1), szaT)
    osl = _slc(qbT, ksl, vslT, selT, _key_aug_table(nq, selT.shape[2], TILE // SLC_BLOCK),
               szbT, gateT)

    w_o = jnp.concatenate([w_out[:MOBA_W], _to_pair_slabs(w_out[MOBA_W:], 0)],
                          axis=0).astype(jnp.bfloat16)
    return _out(x, oa, oc, osl, ow, w_o, post_g.reshape(1, D_MODEL))


def kernel(x, pre_norm_g, post_norm_g, w_in, cmp_pos_k, cmp_pos_v,
           w_cmp_k1, w_cmp_k2, w_cmp_v1, w_cmp_v2, w_out):
    for l in range(pre_norm_g.shape[0]):
        x = _layer(x, pre_norm_g[l], post_norm_g[l], w_in[l], cmp_pos_k[l], cmp_pos_v[l],
                   w_cmp_k1[l], w_cmp_k2[l], w_cmp_v1[l], w_cmp_v2[l], w_out[l])
    return x
```

```python
import functools

import numpy as np
import jax
import jax.numpy as jnp
from jax import lax
from jax.experimental import pallas as pl
from jax.experimental.pallas import tpu as pltpu

D_MODEL = 1024
HEAD_DIM = 64
N_HEADS = 8
N_PAIRS = N_HEADS // 2
NSA_GROUPS = 2
NSA_HPG = N_HEADS // NSA_GROUPS
MOBA_BLOCK = 256
MOBA_TOPK = 3
CMP_LEN = 32
CMP_STRIDE = 16
CMP_HIDDEN = 128
SLC_BLOCK = 64
SLC_TOPN = 16
WINDOW = 512
RMS_EPS = 1e-6
NEG_INF = -1e30
FORCED_SCORE = 1e9
SCALE = HEAD_DIM ** -0.5
LOG2E = float(np.log2(np.e))

TILE = 256
QT = 2 * TILE
KEYS = 4 * TILE
ROWS = 4 * TILE
LANES = 128
HALF = HEAD_DIM
SUBLANES = 8
MOBA_W = N_HEADS * HEAD_DIM
NSA_W = N_HEADS * HEAD_DIM
KV_W = NSA_GROUPS * HEAD_DIM
N_GATES = 3 * N_HEADS
GATE_ROWS = 32
SLOPES = tuple(2.0 ** (-(i + 1)) for i in range(N_HEADS))
BRANCH_CMP, BRANCH_SLC, BRANCH_WIN = 0, 1, 2
BRANCH_DTYPE = jnp.bfloat16

VMEM_LIMIT = 48 * 1024 * 1024

_OFF = dict(qa=0, ka=512, va=1024, za=1536, qb=2048, kcm=2560, vcm=2688,
            ksl=2816, vsl=2944, kwi=3072, vwi=3200, gate=3328, zb=3352)
_NAT = dict(ka=(0, 512), kcm=(512, 640), vcm=(640, 768), ksl=(768, 896), kwi=(896, 1024))
N_NAT = 1024
_TR = dict(qa=(0, 512), va=(512, 1024), qb=(1024, 1536), vsl=(1536, 1664),
           vwi=(1664, 1792), za=(1792, 2304), zb=(2304, 2816), gate=(2816, 2848))
N_TR = 2848


def _to_pair_slabs(w, axis):
    w = jnp.moveaxis(w, axis, -1)
    lead = w.shape[:-1]
    w = w.reshape(lead + (NSA_GROUPS, NSA_HPG, HEAD_DIM))
    w = jnp.swapaxes(w, -3, -2).reshape(lead + (NSA_W,))
    return jnp.moveaxis(w, -1, axis)


def _projection_weights(w_in):
    col = lambda name, width: w_in[:, _OFF[name]:_OFF[name] + width]
    natural = jnp.concatenate(
        [col("ka", MOBA_W), col("kcm", KV_W), col("vcm", KV_W), col("ksl", KV_W), col("kwi", KV_W)],
        axis=1)
    transposed = jnp.concatenate(
        [col("qa", MOBA_W), col("va", MOBA_W), _to_pair_slabs(col("qb", NSA_W), 1),
         col("vsl", KV_W), col("vwi", KV_W), col("za", MOBA_W), _to_pair_slabs(col("zb", NSA_W), 1),
         col("gate", N_GATES), jnp.zeros((D_MODEL, GATE_ROWS - N_GATES), w_in.dtype)], axis=1)
    return natural.astype(jnp.bfloat16), transposed.T.astype(jnp.bfloat16)


def _gate_row(g, p, branch):
    return (g * NSA_HPG + p) * 3 + branch


def _sigmoid(x):
    return 1.0 / (1.0 + jnp.exp(-x))


def _silu(x):
    return x * _sigmoid(x)


def _dot(a, b):
    return jnp.dot(a, b, preferred_element_type=jnp.float32)


def _proj_kernel(x_ref, g_ref, wn_ref, wt_ref,
                 ka_ref, kmean_ref, kcm_ref, vcm_ref, ksl_ref, kwi_ref,
                 qaT_ref, vaT_ref, qbT_ref, vslT_ref, vwiT_ref,
                 szaT_ref, szbT_ref, gateT_ref):
    bf = jnp.bfloat16
    for blk in range(x_ref.shape[1] // MOBA_BLOCK):
        r0, r1 = blk * MOBA_BLOCK, (blk + 1) * MOBA_BLOCK
        x = x_ref[0, r0:r1]
        r = lax.rsqrt(jnp.mean(x * x, axis=-1, keepdims=True) + RMS_EPS)
        h = (x * r * g_ref[...]).astype(bf)

        natural = _dot(h, wn_ref[...])

        def nat(name):
            a, b = _NAT[name]
            return natural[:, a:b]

        ka = nat("ka")
        kmean_ref[0, blk] = jnp.mean(ka, axis=0, keepdims=True)
        ka_ref[0, r0:r1] = ka.astype(bf)
        kcm_ref[0, r0:r1] = nat("kcm").astype(bf)
        vcm_ref[0, r0:r1] = nat("vcm").astype(bf)
        ksl_ref[0, r0:r1] = nat("ksl").astype(bf)
        kwi_ref[0, r0:r1] = nat("kwi").astype(bf)

        def tr(name):
            a, b = _TR[name]
            return lax.dot_general(wt_ref[a:b, :], h, (((1,), (1,)), ((), ())),
                                   preferred_element_type=jnp.float32)

        qa = tr("qa") * (SCALE * LOG2E)
        va = tr("va")
        qb = tr("qb") * (SCALE * LOG2E)
        sza = _silu(tr("za"))
        szb = _silu(tr("zb"))
        for p in range(N_PAIRS):
            sl = slice(p * LANES, (p + 1) * LANES)
            qaT_ref[0, p, :, r0:r1] = qa[sl].astype(bf)
            vaT_ref[0, p, :, r0:r1] = va[sl].astype(bf)
            qbT_ref[0, p, :, r0:r1] = qb[sl].astype(bf)
            szaT_ref[0, p, :, r0:r1] = sza[sl]
            szbT_ref[0, p, :, r0:r1] = szb[sl]
        vslT_ref[0, :, r0:r1] = tr("vsl").astype(bf)
        vwiT_ref[0, :, r0:r1] = tr("vwi").astype(bf)
        gateT_ref[0, :, r0:r1] = _sigmoid(tr("gate"))


def _proj(x, g, wn, wt):
    B, S, _ = x.shape
    nq = S // TILE
    bf, f32 = jnp.bfloat16, jnp.float32
    slab_tr = lambda: pl.BlockSpec((1, N_PAIRS, LANES, ROWS), lambda b, i: (b, 0, 0, i))
    rows = lambda w: pl.BlockSpec((1, ROWS, w), lambda b, i: (b, i, 0))
    chans = lambda c: pl.BlockSpec((1, c, ROWS), lambda b, i: (b, 0, i))
    out_shape = [
        jax.ShapeDtypeStruct((B, S, MOBA_W), bf),
        jax.ShapeDtypeStruct((B, nq, 1, MOBA_W), f32),
        jax.ShapeDtypeStruct((B, S, KV_W), bf),
        jax.ShapeDtypeStruct((B, S, KV_W), bf),
        jax.ShapeDtypeStruct((B, S, KV_W), bf),
        jax.ShapeDtypeStruct((B, S, KV_W), bf),
        jax.ShapeDtypeStruct((B, N_PAIRS, LANES, S), bf),
        jax.ShapeDtypeStruct((B, N_PAIRS, LANES, S), bf),
        jax.ShapeDtypeStruct((B, N_PAIRS, LANES, S), bf),
        jax.ShapeDtypeStruct((B, KV_W, S), bf),
        jax.ShapeDtypeStruct((B, KV_W, S), bf),
        jax.ShapeDtypeStruct((B, N_PAIRS, LANES, S), f32),
        jax.ShapeDtypeStruct((B, N_PAIRS, LANES, S), f32),
        jax.ShapeDtypeStruct((B, GATE_ROWS, S), f32),
    ]
    out_specs = [
        rows(MOBA_W),
        pl.BlockSpec((1, ROWS // MOBA_BLOCK, 1, MOBA_W), lambda b, i: (b, i, 0, 0)),
        rows(KV_W), rows(KV_W), rows(KV_W), rows(KV_W),
        slab_tr(), slab_tr(), slab_tr(),
        chans(KV_W), chans(KV_W),
        slab_tr(), slab_tr(),
        chans(GATE_ROWS),
    ]
    return pl.pallas_call(
        _proj_kernel,
        grid=(B, S // ROWS),
        in_specs=[
            pl.BlockSpec((1, ROWS, D_MODEL), lambda b, i: (b, i, 0)),
            pl.BlockSpec((1, D_MODEL), lambda b, i: (0, 0)),
            pl.BlockSpec((D_MODEL, N_NAT), lambda b, i: (0, 0)),
            pl.BlockSpec((N_TR, D_MODEL), lambda b, i: (0, 0)),
        ],
        out_specs=out_specs,
        out_shape=out_shape,
        compiler_params=pltpu.CompilerParams(
            dimension_semantics=("arbitrary", "arbitrary"),
            vmem_limit_bytes=VMEM_LIMIT),
        name="proj",
    )(x, g, wn, wt)


def _compress_kernel(n_cmp, xk_ref, xv_ref, wkt_ref, wkb_ref, wvt_ref, wvb_ref,
                     w2k_ref, w2v_ref, pek_ref, pev_ref, w1k_ref, w1v_ref,
                     kc_ref, vcT_ref):
    C = xk_ref.shape[1]
    row = lax.broadcasted_iota(jnp.int32, (C, KV_W), 0)

    def phi(x_ref, wt_ref, wb_ref, w2_ref, pe_ref, w1_ref):
        x = x_ref[0]
        top = _dot(x, wt_ref[...])
        bot = _dot(x, wb_ref[...])
        peb = _dot(jnp.broadcast_to(pe_ref[...], (8, pe_ref.shape[1])), w1_ref[...])[0:1]
        peb = jnp.concatenate([peb, peb], axis=1)
        hid = top + pltpu.roll(bot, C - 1, 0) + peb
        out = _dot(_silu(hid).astype(jnp.bfloat16), w2_ref[...])
        return jnp.where(row < n_cmp, out, 0.0)

    kc_ref[0] = phi(xk_ref, wkt_ref, wkb_ref, w2k_ref, pek_ref, w1k_ref).astype(jnp.bfloat16)
    vc = phi(xv_ref, wvt_ref, wvb_ref, w2v_ref, pev_ref, w1v_ref)
    vcT_ref[0] = vc.T.astype(jnp.bfloat16)


def _compress(xk, xv, wkt, wkb, wvt, wvb, w2k, w2v, pek, pev, w1k, w1v, n_cmp):
    B, C, F = xk.shape
    full = lambda a: pl.BlockSpec(a.shape, lambda b: (0,) * a.ndim)
    return pl.pallas_call(
        functools.partial(_compress_kernel, n_cmp),
        grid=(B,),
        in_specs=[pl.BlockSpec((1, C, F), lambda b: (b, 0, 0)),
                  pl.BlockSpec((1, C, F), lambda b: (b, 0, 0)),
                  full(wkt), full(wkb), full(wvt), full(wvb), full(w2k), full(w2v),
                  full(pek), full(pev), full(w1k), full(w1v)],
        out_specs=[pl.BlockSpec((1, C, KV_W), lambda b: (b, 0, 0)),
                   pl.BlockSpec((1, KV_W, C), lambda b: (b, 0, 0))],
        out_shape=[jax.ShapeDtypeStruct((B, C, KV_W), jnp.bfloat16),
                   jax.ShapeDtypeStruct((B, KV_W, C), jnp.bfloat16)],
        compiler_params=pltpu.CompilerParams(
            dimension_semantics=("arbitrary",), vmem_limit_bytes=VMEM_LIMIT),
        name="compress",
    )(xk, xv, wkt, wkb, wvt, wvb, w2k, w2v, pek, pev, w1k, w1v)


def _keep_half(qT, half):
    z = jnp.zeros((HALF, qT.shape[1]), qT.dtype)
    if half == 0:
        return jnp.concatenate([qT[0:HALF], z], axis=0)
    return jnp.concatenate([z, qT[HALF:2 * HALF]], axis=0)


def _rank_rows(vals, n_rows):
    R, T = vals.shape
    tiles = [vals[a:a + SUBLANES] for a in range(0, R, SUBLANES)]
    ranks = [jnp.zeros((SUBLANES, T), jnp.int32) for _ in tiles]
    j_in = lax.broadcasted_iota(jnp.int32, (SUBLANES, T), 0)
    for m in range(n_rows):
        vm = vals[m:m + 1, :]
        for a, tile in enumerate(tiles):
            lo = a * SUBLANES
            if lo > m:
                beats = vm >= tile
            elif lo + SUBLANES - 1 <= m:
                beats = vm > tile
            else:
                beats = (vm > tile) | ((vm == tile) & (j_in > m - lo))
            ranks[a] = ranks[a] + beats.astype(jnp.int32)
    return jnp.concatenate(ranks, axis=0)


def _query_tile(cols, tile0):
    return tile0 + lax.shift_right_logical(cols, int(np.log2(TILE)))


def _split_bf16(x):
    hi = x.astype(jnp.bfloat16).astype(jnp.float32)
    return hi, x - hi


def _slope_parts(slope):
    c = np.float32(slope * LOG2E)
    hi = np.asarray(c, jnp.bfloat16).astype(np.float32)
    lo = np.asarray(c - hi, jnp.bfloat16).astype(np.float32)
    return float(hi), float(lo)


AUG_ROWS = 16


def _pad_rows(x, n_rows):
    return jnp.concatenate([x, jnp.zeros((n_rows - x.shape[0], x.shape[1]), x.dtype)], axis=0)


def _alibi_rows(T, slope, tile0):
    r = lax.broadcasted_iota(jnp.int32, (AUG_ROWS, T), 0)
    c = lax.broadcasted_iota(jnp.int32, (AUG_ROWS, T), 1)
    t_abs = ((c & (TILE - 1)) + _query_tile(c, tile0) * TILE).astype(jnp.float32)
    c_hi, c_lo = _slope_parts(slope)
    w_hi, w_lo = _split_bf16(np.float32(slope * LOG2E) * t_abs)
    out = jnp.where(r == 0, -w_hi, 0.0)
    out = jnp.where(r == 1, -w_lo, out)
    out = jnp.where((r == 2) | (r == 4), c_hi, out)
    out = jnp.where((r == 3) | (r == 5), c_lo, out)
    return out


def _key_aug_table(nq, n_sel, blocks_per_tile):
    t = np.zeros((nq, TILE, LANES), np.float32)
    krel = np.arange(TILE)
    for j in range(nq):
        if n_sel:
            blk = j * blocks_per_tile + krel // (TILE // blocks_per_tile)
            t[j, krel, blk] = 1.0
        t[j, :, n_sel:n_sel + 2] = 1.0
        t[j, :, n_sel + 2:n_sel + 4] = krel[:, None]
        t[j, :, n_sel + 4:n_sel + 6] = TILE * j
    return jnp.asarray(t.reshape(nq * TILE // KEYS, KEYS, LANES), jnp.bfloat16)


ONES_ROWS = 16
ACC_ROWS = HALF + ONES_ROWS


WIN_TILES = WINDOW // TILE
WIN_KEYS = (WIN_TILES + QT // TILE) * TILE


def _win_key_aug_table():
    t = np.zeros((WIN_KEYS, LANES), np.float32)
    for slot in range(WIN_KEYS // TILE):
        rows = slice(slot * TILE, (slot + 1) * TILE)
        t[rows, 0:2] = 1.0
        t[rows, 2:4] = np.arange(TILE)[:, None]
        t[rows, 4:6] = TILE * (WIN_TILES - slot)
        if slot < WIN_TILES:
            t[rows, 6] = 1.0
    return jnp.asarray(t, jnp.bfloat16)


def _value_rows(vT_h):
    return jnp.concatenate([vT_h, jnp.ones((ONES_ROWS, vT_h.shape[1]), vT_h.dtype)], axis=0)


def _stage_scores(h, score_fn, blocks, cols, s_ref, mx_ref):
    c0, c1 = cols
    base = blocks[0][0]
    s_all = score_fn(h, base, blocks[-1][1], c0, c1)
    mx = None
    for r0, r1, keep in blocks:
        s = s_all[r0 - base:r1 - base]
        if keep is not None:
            s = jnp.where(keep, s, NEG_INF)
        s_ref[h, r0:r1, c0:c1] = s
        part = jnp.max(s, axis=0, keepdims=True)
        mx = part if mx is None else jnp.maximum(mx, part)
    mx_ref[h, :, c0:c1] = mx


def _consume_scores(h, value_fn, rows, cols, s_ref, mx_ref, m_ref, acc_ref):
    (r0, r1), (c0, c1) = rows, cols
    m_prev = m_ref[h, :, c0:c1]
    m_new = jnp.maximum(m_prev, mx_ref[h, :, c0:c1])
    alpha = jnp.exp2(m_prev - m_new)
    p = jnp.exp2(s_ref[h, r0:r1, c0:c1] - m_new).astype(jnp.bfloat16)
    acc_ref[h, :, c0:c1] = alpha * acc_ref[h, :, c0:c1] + _dot(value_fn(h, r0, r1), p)
    m_ref[h, :, c0:c1] = m_new


def _attend_heads(score_fn, value_fn, parts, s_ref, mx_ref, m_ref, acc_ref):
    for h in range(N_HEADS):
        for cols, blocks in parts:
            _stage_scores(h, score_fn, blocks, cols, s_ref, mx_ref)
    for h in range(N_HEADS):
        for cols, blocks in parts:
            rows = (min(b[0] for b in blocks), max(b[1] for b in blocks))
            _consume_scores(h, value_fn, rows, cols, s_ref, mx_ref, m_ref, acc_ref)


def _init_state(m_ref, acc_ref):
    m_ref[...] = jnp.full(m_ref.shape, NEG_INF, jnp.float32)
    acc_ref[...] = jnp.zeros(acc_ref.shape, jnp.float32)


def _tile_iotas(T=TILE):
    k = lax.broadcasted_iota(jnp.int32, (TILE, T), 0)
    t = lax.broadcasted_iota(jnp.int32, (TILE, T), 1)
    return k, t


def _key_step_parts(kind):
    if kind == "past":
        return [((0, QT), [(0, KEYS, None)])]
    k, t = _tile_iotas()
    tri = k <= t
    own = QT if kind == "reach" else 0
    first = [(0, own, None)] if own else []
    return [((0, TILE), first + [(own, own + TILE, tri)]),
            ((TILE, QT), [(0, own + TILE, None), (own + TILE, own + QT, tri)])]


def _store_pair(o_ref, p, o0T, o1T, szT, gates, rows=None):
    if gates is not None:
        o0T = o0T * gates[0]
        o1T = o1T * gates[1]
    oT = jnp.concatenate([o0T, o1T], axis=0) * szT
    r0, r1 = rows if rows is not None else (0, o_ref.shape[1])
    o_ref[0, r0:r1, p * LANES:(p + 1) * LANES] = oT.T.astype(o_ref.dtype)


def _finalize_pair(o_ref, acc_ref, p, szT, gates=None):
    a0 = acc_ref[2 * p]
    a1 = acc_ref[2 * p + 1]
    _store_pair(o_ref, p, a0[0:HALF] / a0[HALF:HALF + 1], a1[0:HALF] / a1[HALF:HALF + 1],
                szT, gates)


def _nsa_gates(gate_ref, p, branch, cols=None):
    c0, c1 = cols if cols is not None else (0, gate_ref.shape[2])
    return [gate_ref[0, _gate_row(g, p, branch):_gate_row(g, p, branch) + 1, c0:c1]
            for g in range(NSA_GROUPS)]


def _step_tables(n_q):
    it, jt = [], []
    for ii in range(n_q):
        for js in range(ii * QT // KEYS + 1):
            it.append(ii)
            jt.append(js)
    return jnp.asarray(it, jnp.int32), jnp.asarray(jt, jnp.int32)


def _key_steps(ii, js, score_fn, value_fn, finalize_fn, s_ref, mx_ref, m_ref, acc_ref):
    q0 = ii * QT
    k0 = js * KEYS
    for kind, when in (("past", k0 + KEYS <= q0), ("reach", k0 + QT == q0), ("own", k0 == q0)):
        @pl.when(when)
        def _():
            _attend_heads(score_fn, value_fn, _key_step_parts(kind), s_ref, mx_ref, m_ref, acc_ref)
            if kind != "past":
                finalize_fn()


def _state_scratch(n_keys, T=TILE):
    return [
        pltpu.VMEM((N_HEADS, 2 * LANES, T), jnp.bfloat16),
        pltpu.VMEM((N_HEADS, 1, T), jnp.float32),
        pltpu.VMEM((N_HEADS, ACC_ROWS, T), jnp.float32),
        pltpu.VMEM((N_HEADS, n_keys, T), jnp.float32),
        pltpu.VMEM((N_HEADS, 1, T), jnp.float32),
    ]


def _moba_kernel(nq, it_ref, jt_ref, qT_ref, k_ref, vT_ref, kmean_ref, kaug_ref, sz_ref, o_ref,
                 qaug_ref, m_ref, acc_ref, s_ref, mx_ref):
    s = pl.program_id(1)
    ii = it_ref[s]
    jj = jt_ref[s]
    n_sel = 16

    @pl.when(jj == 0)
    def _():
        _init_state(m_ref, acc_ref)
        n_iota = lax.broadcasted_iota(jnp.int32, (nq, QT), 0)
        own = _query_tile(lax.broadcasted_iota(jnp.int32, (nq, QT), 1), 2 * ii)
        for p in range(N_PAIRS):
            qT = qT_ref[0, p]
            km = kmean_ref[0, :, p * LANES:(p + 1) * LANES].astype(jnp.bfloat16)
            for hh in range(2):
                h = 2 * p + hh
                qm = _keep_half(qT, hh)
                gate = jnp.where(n_iota < own, _dot(km, qm), NEG_INF)
                rank = _rank_rows(gate, nq)
                sel = ((rank < MOBA_TOPK) & (n_iota < own)) | (n_iota == own)
                selneg = jnp.where(sel, 0.0, NEG_INF)
                if nq < n_sel:
                    selneg = jnp.concatenate(
                        [selneg, jnp.zeros((n_sel - nq, QT), jnp.float32)], axis=0)
                ali = _alibi_rows(QT, SLOPES[h], 2 * ii)
                aug = jnp.concatenate([selneg, ali], axis=0).astype(jnp.bfloat16)
                qaug_ref[h] = _pad_rows(jnp.concatenate([qm, aug], axis=0), 2 * LANES)

    def score(h, r0, r1, c0, c1):
        p = h // 2
        kfull = jnp.concatenate(
            [k_ref[0, r0:r1, p * LANES:(p + 1) * LANES], kaug_ref[jj, r0:r1]], axis=1)
        return _dot(kfull, qaug_ref[h, :, c0:c1])

    def value(h, r0, r1):
        return _value_rows(vT_ref[0, h // 2, (h % 2) * HALF:(h % 2 + 1) * HALF, r0:r1])

    def finalize():
        for p in range(N_PAIRS):
            _finalize_pair(o_ref, acc_ref, p, sz_ref[0, p])

    _key_steps(ii, jj, score, value, finalize, s_ref, mx_ref, m_ref, acc_ref)


def _moba(qT, k, vT, kmean, kaug, szT):
    B, _, _, S = qT.shape
    nq = S // TILE
    it, jt = _step_tables(S // QT)
    return pl.pallas_call(
        functools.partial(_moba_kernel, nq),
        grid_spec=pltpu.PrefetchScalarGridSpec(
            num_scalar_prefetch=2,
            grid=(B, int(it.shape[0])),
            in_specs=[
                pl.BlockSpec((1, N_PAIRS, LANES, QT), lambda b, s, it, jt: (b, 0, 0, it[s])),
                pl.BlockSpec((1, KEYS, MOBA_W), lambda b, s, it, jt: (b, jt[s], 0)),
                pl.BlockSpec((1, N_PAIRS, LANES, KEYS), lambda b, s, it, jt: (b, 0, 0, jt[s])),
                pl.BlockSpec((1, nq, MOBA_W), lambda b, s, it, jt: (b, 0, 0)),
                pl.BlockSpec(kaug.shape, lambda b, s, it, jt: (0, 0, 0)),
                pl.BlockSpec((1, N_PAIRS, LANES, QT), lambda b, s, it, jt: (b, 0, 0, it[s])),
            ],
            out_specs=pl.BlockSpec((1, QT, MOBA_W), lambda b, s, it, jt: (b, it[s], 0)),
            scratch_shapes=_state_scratch(KEYS, QT),
        ),
        out_shape=jax.ShapeDtypeStruct((B, S, MOBA_W), BRANCH_DTYPE),
        compiler_params=pltpu.CompilerParams(
            dimension_semantics=("arbitrary", "arbitrary"), vmem_limit_bytes=VMEM_LIMIT),
        name="moba",
    )(it, jt, qT, k, vT, kmean, kaug, szT)


CMP_BAND = TILE // CMP_STRIDE + 1
RANK_ROWS_STEP = 16


def _cmp_query_aug_table():
    t = np.zeros((LANES, TILE), np.float32)
    trel = np.arange(TILE)
    for u in range(CMP_BAND):
        t[u, trel < CMP_STRIDE * (u - 1) + CMP_LEN - 1] = NEG_INF
    t[CMP_BAND, :] = NEG_INF
    return jnp.asarray(t, jnp.bfloat16)


def _cmp_tile(i, cols, slot, n_slc, qT_ref, kc_ref, vcT_ref, ovT_ref, qaug_ref, sz_ref, gate_ref,
              o_ref, sel_ref, s_ref, mx_ref):
    c0, c1 = cols
    C = kc_ref.shape[1]
    c_rel = (lax.broadcasted_iota(jnp.int32, (C, LANES), 0)
             - (TILE // CMP_STRIDE) * i + 1)
    u = lax.broadcasted_iota(jnp.int32, (C, LANES), 1)
    kaug = jnp.where((c_rel == u) & (u < CMP_BAND), 1.0, 0.0)
    kaug = jnp.where((u == CMP_BAND) & (c_rel >= CMP_BAND), 1.0, kaug)
    kfull = jnp.concatenate([kc_ref[0], kaug.astype(jnp.bfloat16)], axis=1)
    qaug = qaug_ref[...]

    for p in range(N_PAIRS):
        qT = qT_ref[0, p, :, c0:c1]
        for g in range(NSA_GROUPS):
            h = slot * N_HEADS + 2 * p + g
            s = _dot(kfull, jnp.concatenate([_keep_half(qT, g), qaug], axis=0))
            s_ref[h] = s
            mx_ref[h] = jnp.max(s, axis=0, keepdims=True)

    t_abs = i * TILE + lax.broadcasted_iota(jnp.int32, (1, TILE), 1)
    seen = t_abs >= CMP_LEN - 1
    ones = jnp.ones((ONES_ROWS, C), jnp.bfloat16)
    vals = [jnp.concatenate([vcT_ref[0, g * HALF:(g + 1) * HALF], ones, ovT_ref[...]], axis=0)
            for g in range(NSA_GROUPS)]
    R = sel_ref.shape[2]
    imp = [jnp.zeros((R, TILE), jnp.float32) for _ in range(NSA_GROUPS)]
    for p in range(N_PAIRS):
        outs = []
        for g in range(NSA_GROUPS):
            h = slot * N_HEADS + 2 * p + g
            pr = jnp.exp2(s_ref[h] - mx_ref[h]).astype(jnp.bfloat16)
            acc = _dot(vals[g], pr)
            inv = jnp.where(seen, 1.0 / acc[HALF:HALF + 1], 0.0)
            outs.append(acc[0:HALF] * inv)
            imp[g] = imp[g] + acc[ACC_ROWS:ACC_ROWS + R] * inv
        _store_pair(o_ref, p, outs[0], outs[1], sz_ref[0, p, :, c0:c1],
                    _nsa_gates(gate_ref, p, BRANCH_CMP, cols), rows=cols)

    def select_blocks():
        j_iota = lax.broadcasted_iota(jnp.int32, (R, TILE), 0)
        t_q = i * TILE + lax.broadcasted_iota(jnp.int32, (R, TILE), 1)
        own = lax.shift_right_logical(t_q, int(np.log2(SLC_BLOCK)))
        forced = (j_iota == 0) | (j_iota == own) | (j_iota == own - 1)
        causal = j_iota <= own

        def select(n_rows):
            for g in range(NSA_GROUPS):
                v = jnp.where(forced, FORCED_SCORE, jnp.where(causal, imp[g], NEG_INF))[0:n_rows]
                rank = _rank_rows(v, min(n_rows, n_slc))
                sel = (rank < min(SLC_TOPN, n_slc)) & causal[0:n_rows]
                selneg = jnp.where(sel, 0.0, NEG_INF)
                if n_rows < R:
                    selneg = jnp.concatenate(
                        [selneg, jnp.full((R - n_rows, TILE), NEG_INF, jnp.float32)], axis=0)
                sel_ref[0, g, :, c0:c1] = selneg.astype(jnp.bfloat16)

        blocks_per_tile = TILE // SLC_BLOCK
        variants = list(range(RANK_ROWS_STEP, R, RANK_ROWS_STEP)) + [R]
        for idx, n_rows in enumerate(variants):
            lo = 0 if idx == 0 else variants[idx - 1] // blocks_per_tile
            hi = n_rows // blocks_per_tile

            @pl.when((i >= lo) & (i < hi) if n_rows < R else i >= lo)
            def _():
                select(n_rows)

    return select_blocks


def _slc_kernel(it_ref, jt_ref, qT_ref, k_ref, vT_ref, sel_ref, kaug_ref, sz_ref, gate_ref, o_ref,
                qaug_ref, m_ref, acc_ref, s_ref, mx_ref):
    s = pl.program_id(1)
    ii = it_ref[s]
    jj = jt_ref[s]
    n_sel = sel_ref.shape[2]

    @pl.when(jj == 0)
    def _():
        _init_state(m_ref, acc_ref)
        for p in range(N_PAIRS):
            qT = qT_ref[0, p]
            for g in range(NSA_GROUPS):
                h = 2 * p + g
                qm = _keep_half(qT, g)
                ali = _alibi_rows(QT, SLOPES[g * NSA_HPG + p], 2 * ii)
                qaug_ref[h] = _pad_rows(
                    jnp.concatenate([qm, sel_ref[0, g], ali.astype(jnp.bfloat16)], axis=0),
                    2 * LANES)

    def score(h, r0, r1, c0, c1):
        kfull = jnp.concatenate([k_ref[0, r0:r1], kaug_ref[jj, r0:r1]], axis=1)
        return _dot(kfull, qaug_ref[h, :, c0:c1])

    def value(h, r0, r1):
        return _value_rows(vT_ref[0, (h % 2) * HALF:(h % 2 + 1) * HALF, r0:r1])

    def finalize():
        for p in range(N_PAIRS):
            _finalize_pair(o_ref, acc_ref, p, sz_ref[0, p], _nsa_gates(gate_ref, p, BRANCH_SLC))

    _key_steps(ii, jj, score, value, finalize, s_ref, mx_ref, m_ref, acc_ref)


def _slc(qbT, ksl, vslT, selT, kaug, szT, gateT):
    B, _, _, S = qbT.shape
    R = selT.shape[2]
    it, jt = _step_tables(S // QT)
    return pl.pallas_call(
        _slc_kernel,
        grid_spec=pltpu.PrefetchScalarGridSpec(
            num_scalar_prefetch=2,
            grid=(B, int(it.shape[0])),
            in_specs=[
                pl.BlockSpec((1, N_PAIRS, LANES, QT), lambda b, s, it, jt: (b, 0, 0, it[s])),
                pl.BlockSpec((1, KEYS, KV_W), lambda b, s, it, jt: (b, jt[s], 0)),
                pl.BlockSpec((1, KV_W, KEYS), lambda b, s, it, jt: (b, 0, jt[s])),
                pl.BlockSpec((1, NSA_GROUPS, R, QT), lambda b, s, it, jt: (b, 0, 0, it[s])),
                pl.BlockSpec(kaug.shape, lambda b, s, it, jt: (0, 0, 0)),
                pl.BlockSpec((1, N_PAIRS, LANES, QT), lambda b, s, it, jt: (b, 0, 0, it[s])),
                pl.BlockSpec((1, GATE_ROWS, QT), lambda b, s, it, jt: (b, 0, it[s])),
            ],
            out_specs=pl.BlockSpec((1, QT, NSA_W), lambda b, s, it, jt: (b, it[s], 0)),
            scratch_shapes=_state_scratch(KEYS, QT),
        ),
        out_shape=jax.ShapeDtypeStruct((B, S, NSA_W), BRANCH_DTYPE),
        compiler_params=pltpu.CompilerParams(
            dimension_semantics=("arbitrary", "arbitrary"), vmem_limit_bytes=VMEM_LIMIT),
        name="slc",
    )(it, jt, qbT, ksl, vslT, selT, kaug, szT, gateT)


def _local_kernel(n_slc, qT_ref, kb_ref, kq_ref, vb_ref, vq_ref, kaug_ref,
                  kc_ref, vcT_ref, ovT_ref, cq_ref, sz_ref, gate_ref,
                  o_ref, oc_ref, sel_ref,
                  qaug_ref, m_ref, acc_ref, s_ref, mx_ref, cs_ref, cmx_ref):
    i = pl.program_id(1)
    selects = [
        _cmp_tile(i * (QT // TILE) + q, (q * TILE, (q + 1) * TILE), q, n_slc, qT_ref, kc_ref,
                  vcT_ref, ovT_ref, cq_ref, sz_ref, gate_ref, oc_ref, sel_ref, cs_ref, cmx_ref)
        for q in range(QT // TILE)]
    _init_state(m_ref, acc_ref)
    r = lax.broadcasted_iota(jnp.int32, (AUG_ROWS, QT), 0)
    c = lax.broadcasted_iota(jnp.int32, (AUG_ROWS, QT), 1)
    trel = (c & (TILE - 1)).astype(jnp.float32)
    absent = jnp.where(i >= 1, 0.0, NEG_INF)
    for p in range(N_PAIRS):
        qT = qT_ref[0, p]
        for g in range(NSA_GROUPS):
            h = 2 * p + g
            slope = SLOPES[g * NSA_HPG + p]
            c_hi, c_lo = _slope_parts(slope)
            w_hi, w_lo = _split_bf16(np.float32(slope * LOG2E) * trel)
            aug = jnp.where(r == 0, -w_hi, 0.0)
            aug = jnp.where(r == 1, -w_lo, aug)
            aug = jnp.where(r == 2, c_hi, aug)
            aug = jnp.where(r == 3, c_lo, aug)
            aug = jnp.where(r == 4, -c_hi, aug)
            aug = jnp.where(r == 5, -c_lo, aug)
            aug = jnp.where(r == 6, absent, aug)
            qaug_ref[h] = _pad_rows(
                jnp.concatenate([_keep_half(qT, g), aug.astype(jnp.bfloat16)], axis=0), 2 * LANES)

    kfull = jnp.concatenate(
        [jnp.concatenate([kb_ref[0], kq_ref[0]], axis=0), kaug_ref[...]], axis=1)
    vT = jnp.concatenate([vb_ref[0], vq_ref[0]], axis=1)
    k, t = _tile_iotas()

    def score(h, r0, r1, c0, c1):
        return _dot(kfull[r0:r1], qaug_ref[h, :, c0:c1])

    def value(h, r0, r1):
        return _value_rows(vT[(h % 2) * HALF:(h % 2 + 1) * HALF, r0:r1])

    parts = []
    for q in range(QT // TILE):
        first, last = q * TILE, (q + WIN_TILES) * TILE
        parts.append(((q * TILE, (q + 1) * TILE),
                      [(first, first + TILE, k > t), (first + TILE, last, None),
                       (last, last + TILE, k <= t)]))
    _attend_heads(score, value, parts, s_ref, mx_ref, m_ref, acc_ref)
    for p in range(N_PAIRS):
        _finalize_pair(o_ref, acc_ref, p, sz_ref[0, p], _nsa_gates(gate_ref, p, BRANCH_WIN))
    for select_blocks in selects:
        select_blocks()


def _local(qbT, kwi, vwiT, kaug, kc, vcT, ovT, cmp_qaug, szT, gateT, n_slc):
    B, _, _, S = qbT.shape
    assert WINDOW == QT
    C = kc.shape[1]
    R = ovT.shape[0]
    before = lambda i: jnp.maximum(i - 1, 0)
    q_slabs = pl.BlockSpec((1, N_PAIRS, LANES, QT), lambda b, i: (b, 0, 0, i))
    whole = lambda a: pl.BlockSpec(a.shape, lambda b, i: (0,) * a.ndim)
    rows_out = pl.BlockSpec((1, QT, NSA_W), lambda b, i: (b, i, 0))
    return pl.pallas_call(
        functools.partial(_local_kernel, n_slc),
        grid=(B, S // QT),
        in_specs=[
            q_slabs,
            pl.BlockSpec((1, WINDOW, KV_W), lambda b, i: (b, before(i), 0)),
            pl.BlockSpec((1, QT, KV_W), lambda b, i: (b, i, 0)),
            pl.BlockSpec((1, KV_W, WINDOW), lambda b, i: (b, 0, before(i))),
            pl.BlockSpec((1, KV_W, QT), lambda b, i: (b, 0, i)),
            whole(kaug),
            pl.BlockSpec((1, C, KV_W), lambda b, i: (b, 0, 0)),
            pl.BlockSpec((1, KV_W, C), lambda b, i: (b, 0, 0)),
            whole(ovT), whole(cmp_qaug),
            q_slabs,
            pl.BlockSpec((1, GATE_ROWS, QT), lambda b, i: (b, 0, i)),
        ],
        out_specs=[rows_out, rows_out,
                   pl.BlockSpec((1, NSA_GROUPS, R, QT), lambda b, i: (b, 0, 0, i))],
        out_shape=[jax.ShapeDtypeStruct((B, S, NSA_W), BRANCH_DTYPE),
                   jax.ShapeDtypeStruct((B, S, NSA_W), BRANCH_DTYPE),
                   jax.ShapeDtypeStruct((B, NSA_GROUPS, R, S), jnp.bfloat16)],
        scratch_shapes=_state_scratch(WIN_KEYS, QT) + [
            pltpu.VMEM((QT // TILE * N_HEADS, C, TILE), jnp.float32),
            pltpu.VMEM((QT // TILE * N_HEADS, 1, TILE), jnp.float32)],
        compiler_params=pltpu.CompilerParams(
            dimension_semantics=("arbitrary", "arbitrary"), vmem_limit_bytes=VMEM_LIMIT),
        name="local",
    )(qbT, kwi, kwi, vwiT, vwiT, kaug, kc, vcT, ovT, cmp_qaug, szT, gateT)


def _out_kernel(x_ref, oa_ref, oc_ref, os_ref, ow_ref, w_ref, g_ref, y_ref):
    f32 = jnp.float32
    ob = oc_ref[0].astype(f32) + os_ref[0].astype(f32) + ow_ref[0].astype(f32)
    mix = jnp.concatenate([oa_ref[0].astype(jnp.bfloat16), ob.astype(jnp.bfloat16)], axis=1)
    y = _dot(mix, w_ref[...])
    r = lax.rsqrt(jnp.mean(y * y, axis=-1, keepdims=True) + RMS_EPS)
    y_ref[0] = x_ref[0] + y * r * g_ref[...]


def _out(x, oa, oc, os_, ow, w, g):
    B, S, _ = x.shape
    rows = lambda wd: pl.BlockSpec((1, ROWS, wd), lambda b, i: (b, i, 0))
    return pl.pallas_call(
        _out_kernel,
        grid=(B, S // ROWS),
        in_specs=[rows(D_MODEL), rows(MOBA_W), rows(NSA_W), rows(NSA_W), rows(NSA_W),
                  pl.BlockSpec(w.shape, lambda b, i: (0, 0)),
                  pl.BlockSpec((1, D_MODEL), lambda b, i: (0, 0))],
        out_specs=rows(D_MODEL),
        out_shape=jax.ShapeDtypeStruct((B, S, D_MODEL), jnp.float32),
        compiler_params=pltpu.CompilerParams(
            dimension_semantics=("arbitrary", "arbitrary"), vmem_limit_bytes=VMEM_LIMIT),
        name="out",
    )(x, oa, oc, os_, ow, w, g)


def _compress_weights(w1, w2):
    half = (CMP_LEN // 2) * HEAD_DIM
    w1r = w1.reshape(2, CMP_LEN // 2, HEAD_DIM, CMP_HIDDEN)
    z = jnp.zeros_like(w1r)
    g0 = jnp.concatenate([w1r, z], axis=-1)
    g1 = jnp.concatenate([z, w1r], axis=-1)
    both = jnp.stack([g0, g1], axis=2)
    both = both.reshape(2, half * NSA_GROUPS, NSA_GROUPS * CMP_HIDDEN).astype(jnp.bfloat16)
    zz = jnp.zeros_like(w2)
    w2bd = jnp.concatenate([jnp.concatenate([w2, zz], axis=1),
                            jnp.concatenate([zz, w2], axis=1)], axis=0).astype(jnp.bfloat16)
    return both[0], both[1], w2bd


def _overlap_T(n_cmp_pad, n_slc, rows):
    c = np.arange(n_cmp_pad)[None, :] * CMP_STRIDE
    j = np.arange(rows)[:, None] * SLC_BLOCK
    ov = (c < j + SLC_BLOCK) & (c + CMP_LEN > j) & (np.arange(rows)[:, None] < n_slc)
    return jnp.asarray(ov.astype(np.float32), jnp.bfloat16)


def _layer(x, pre_g, post_g, w_in, pos_k, pos_v, w_k1, w_k2, w_v1, w_v2, w_out):
    B, S, _ = x.shape
    nq = S // TILE
    n_cmp = (S - CMP_LEN) // CMP_STRIDE + 1
    n_slc = S // SLC_BLOCK
    C = S // CMP_STRIDE
    wn, wt = _projection_weights(w_in)
    (ka, kmean, kcm, vcm, ksl, kwi, qaT, vaT, qbT, vslT, vwiT,
     szaT, szbT, gateT) = _proj(x, pre_g.reshape(1, D_MODEL), wn, wt)

    wkt, wkb, w2k = _compress_weights(w_k1, w_k2)
    wvt, wvb, w2v = _compress_weights(w_v1, w_v2)
    chunk = CMP_STRIDE * KV_W
    kc, vcT = _compress(
        kcm.reshape(B, C, chunk), vcm.reshape(B, C, chunk), wkt, wkb, wvt, wvb, w2k, w2v,
        pos_k.reshape(1, CMP_LEN * HEAD_DIM).astype(jnp.bfloat16),
        pos_v.reshape(1, CMP_LEN * HEAD_DIM).astype(jnp.bfloat16),
        w_k1.astype(jnp.bfloat16), w_v1.astype(jnp.bfloat16), n_cmp)

    ow, oc, selT = _local(qbT, kwi, vwiT, _win_key_aug_table(), kc, vcT,
                          _overlap_T(C, n_slc, SLC_BLOCK), _cmp_query_aug_table(),
                          szbT, gateT, n_slc)
    oa = _moba(qaT, ka, vaT, kmean.reshape(B, nq, MOBA_W), _key_aug_table(nq, 16, 1), szaT)
    osl = _slc(qbT, ksl, vslT, selT, _key_aug_table(nq, selT.shape[2], TILE // SLC_BLOCK),
               szbT, gateT)

    w_o = jnp.concatenate([w_out[:MOBA_W], _to_pair_slabs(w_out[MOBA_W:], 0)],
                          axis=0).astype(jnp.bfloat16)
    return _out(x, oa, oc, osl, ow, w_o, post_g.reshape(1, D_MODEL))


def kernel(x, pre_norm_g, post_norm_g, w_in, cmp_pos_k, cmp_pos_v,
           w_cmp_k1, w_cmp_k2, w_cmp_v1, w_cmp_v2, w_out):
    for l in range(pre_norm_g.shape[0]):
        x = _layer(x, pre_norm_g[l], post_norm_g[l], w_in[l], cmp_pos_k[l], cmp_pos_v[l],
                   w_cmp_k1[l], w_cmp_k2[l], w_cmp_v1[l], w_cmp_v2[l], w_out[l])
    return x
```

```python
import functools

import numpy as np
import jax
import jax.numpy as jnp
from jax import lax
from jax.experimental import pallas as pl
from jax.experimental.pallas import tpu as pltpu

D_MODEL = 1024
HEAD_DIM = 64
N_HEADS = 8
N_PAIRS = N_HEADS // 2
NSA_GROUPS = 2
NSA_HPG = N_HEADS // NSA_GROUPS
MOBA_BLOCK = 256
MOBA_TOPK = 3
CMP_LEN = 32
CMP_STRIDE = 16
CMP_HIDDEN = 128
SLC_BLOCK = 64
SLC_TOPN = 16
WINDOW = 512
RMS_EPS = 1e-6
NEG_INF = -1e30
FORCED_SCORE = 1e9
SCALE = HEAD_DIM ** -0.5
LOG2E = float(np.log2(np.e))

TILE = 256
QT = 2 * TILE
KEYS = 4 * TILE
ROWS = 4 * TILE
LANES = 128
HALF = HEAD_DIM
SUBLANES = 8
MOBA_W = N_HEADS * HEAD_DIM
NSA_W = N_HEADS * HEAD_DIM
KV_W = NSA_GROUPS * HEAD_DIM
N_GATES = 3 * N_HEADS
GATE_ROWS = 32
SLOPES = tuple(2.0 ** (-(i + 1)) for i in range(N_HEADS))
BRANCH_CMP, BRANCH_SLC, BRANCH_WIN = 0, 1, 2
BRANCH_DTYPE = jnp.bfloat16

VMEM_LIMIT = 48 * 1024 * 1024

_OFF = dict(qa=0, ka=512, va=1024, za=1536, qb=2048, kcm=2560, vcm=2688,
            ksl=2816, vsl=2944, kwi=3072, vwi=3200, gate=3328, zb=3352)
_NAT = dict(ka=(0, 512), kcm=(512, 640), vcm=(640, 768), ksl=(768, 896), kwi=(896, 1024))
N_NAT = 1024
_TR = dict(qa=(0, 512), va=(512, 1024), qb=(1024, 1536), vsl=(1536, 1664),
           vwi=(1664, 1792), za=(1792, 2304), zb=(2304, 2816), gate=(2816, 2848))
N_TR = 2848


def _to_pair_slabs(w, axis):
    w = jnp.moveaxis(w, axis, -1)
    lead = w.shape[:-1]
    w = w.reshape(lead + (NSA_GROUPS, NSA_HPG, HEAD_DIM))
    w = jnp.swapaxes(w, -3, -2).reshape(lead + (NSA_W,))
    return jnp.moveaxis(w, -1, axis)


def _projection_weights(w_in):
    col = lambda name, width: w_in[:, _OFF[name]:_OFF[name] + width]
    natural = jnp.concatenate(
        [col("ka", MOBA_W), col("kcm", KV_W), col("vcm", KV_W), col("ksl", KV_W), col("kwi", KV_W)],
        axis=1)
    transposed = jnp.concatenate(
        [col("qa", MOBA_W), col("va", MOBA_W), _to_pair_slabs(col("qb", NSA_W), 1),
         col("vsl", KV_W), col("vwi", KV_W), col("za", MOBA_W), _to_pair_slabs(col("zb", NSA_W), 1),
         col("gate", N_GATES), jnp.zeros((D_MODEL, GATE_ROWS - N_GATES), w_in.dtype)], axis=1)
    return natural.astype(jnp.bfloat16), transposed.T.astype(jnp.bfloat16)


def _gate_row(g, p, branch):
    return (g * NSA_HPG + p) * 3 + branch


def _sigmoid(x):
    return 1.0 / (1.0 + jnp.exp(-x))


def _silu(x):
    return x * _sigmoid(x)


def _dot(a, b):
    return jnp.dot(a, b, preferred_element_type=jnp.float32)


def _proj_kernel(x_ref, g_ref, wn_ref, wt_ref,
                 ka_ref, kmean_ref, kcm_ref, vcm_ref, ksl_ref, kwi_ref,
                 qaT_ref, vaT_ref, qbT_ref, vslT_ref, vwiT_ref,
                 szaT_ref, szbT_ref, gateT_ref):
    bf = jnp.bfloat16
    for blk in range(x_ref.shape[1] // MOBA_BLOCK):
        r0, r1 = blk * MOBA_BLOCK, (blk + 1) * MOBA_BLOCK
        x = x_ref[0, r0:r1]
        r = lax.rsqrt(jnp.mean(x * x, axis=-1, keepdims=True) + RMS_EPS)
        h = (x * r * g_ref[...]).astype(bf)

        natural = _dot(h, wn_ref[...])

        def nat(name):
            a, b = _NAT[name]
            return natural[:, a:b]

        ka = nat("ka")
        kmean_ref[0, blk] = jnp.mean(ka, axis=0, keepdims=True)
        ka_ref[0, r0:r1] = ka.astype(bf)
        kcm_ref[0, r0:r1] = nat("kcm").astype(bf)
        vcm_ref[0, r0:r1] = nat("vcm").astype(bf)
        ksl_ref[0, r0:r1] = nat("ksl").astype(bf)
        kwi_ref[0, r0:r1] = nat("kwi").astype(bf)

        def tr(name):
            a, b = _TR[name]
            return lax.dot_general(wt_ref[a:b, :], h, (((1,), (1,)), ((), ())),
                                   preferred_element_type=jnp.float32)

        qa = tr("qa") * (SCALE * LOG2E)
        va = tr("va")
        qb = tr("qb") * (SCALE * LOG2E)
        sza = _silu(tr("za"))
        szb = _silu(tr("zb"))
        for p in range(N_PAIRS):
            sl = slice(p * LANES, (p + 1) * LANES)
            qaT_ref[0, p, :, r0:r1] = qa[sl].astype(bf)
            vaT_ref[0, p, :, r0:r1] = va[sl].astype(bf)
            qbT_ref[0, p, :, r0:r1] = qb[sl].astype(bf)
            szaT_ref[0, p, :, r0:r1] = sza[sl]
            szbT_ref[0, p, :, r0:r1] = szb[sl]
        vslT_ref[0, :, r0:r1] = tr("vsl").astype(bf)
        vwiT_ref[0, :, r0:r1] = tr("vwi").astype(bf)
        gateT_ref[0, :, r0:r1] = _sigmoid(tr("gate"))


def _proj(x, g, wn, wt):
    B, S, _ = x.shape
    nq = S // TILE
    bf, f32 = jnp.bfloat16, jnp.float32
    slab_tr = lambda: pl.BlockSpec((1, N_PAIRS, LANES, ROWS), lambda b, i: (b, 0, 0, i))
    rows = lambda w: pl.BlockSpec((1, ROWS, w), lambda b, i: (b, i, 0))
    chans = lambda c: pl.BlockSpec((1, c, ROWS), lambda b, i: (b, 0, i))
    out_shape = [
        jax.ShapeDtypeStruct((B, S, MOBA_W), bf),
        jax.ShapeDtypeStruct((B, nq, 1, MOBA_W), f32),
        jax.ShapeDtypeStruct((B, S, KV_W), bf),
        jax.ShapeDtypeStruct((B, S, KV_W), bf),
        jax.ShapeDtypeStruct((B, S, KV_W), bf),
        jax.ShapeDtypeStruct((B, S, KV_W), bf),
        jax.ShapeDtypeStruct((B, N_PAIRS, LANES, S), bf),
        jax.ShapeDtypeStruct((B, N_PAIRS, LANES, S), bf),
        jax.ShapeDtypeStruct((B, N_PAIRS, LANES, S), bf),
        jax.ShapeDtypeStruct((B, KV_W, S), bf),
        jax.ShapeDtypeStruct((B, KV_W, S), bf),
        jax.ShapeDtypeStruct((B, N_PAIRS, LANES, S), f32),
        jax.ShapeDtypeStruct((B, N_PAIRS, LANES, S), f32),
        jax.ShapeDtypeStruct((B, GATE_ROWS, S), f32),
    ]
    out_specs = [
        rows(MOBA_W),
        pl.BlockSpec((1, ROWS // MOBA_BLOCK, 1, MOBA_W), lambda b, i: (b, i, 0, 0)),
        rows(KV_W), rows(KV_W), rows(KV_W), rows(KV_W),
        slab_tr(), slab_tr(), slab_tr(),
        chans(KV_W), chans(KV_W),
        slab_tr(), slab_tr(),
        chans(GATE_ROWS),
    ]
    return pl.pallas_call(
        _proj_kernel,
        grid=(B, S // ROWS),
        in_specs=[
            pl.BlockSpec((1, ROWS, D_MODEL), lambda b, i: (b, i, 0)),
            pl.BlockSpec((1, D_MODEL), lambda b, i: (0, 0)),
            pl.BlockSpec((D_MODEL, N_NAT), lambda b, i: (0, 0)),
            pl.BlockSpec((N_TR, D_MODEL), lambda b, i: (0, 0)),
        ],
        out_specs=out_specs,
        out_shape=out_shape,
        compiler_params=pltpu.CompilerParams(
            dimension_semantics=("arbitrary", "arbitrary"),
            vmem_limit_bytes=VMEM_LIMIT),
        name="proj",
    )(x, g, wn, wt)


def _compress_kernel(n_cmp, xk_ref, xv_ref, wkt_ref, wkb_ref, wvt_ref, wvb_ref,
                     w2k_ref, w2v_ref, pek_ref, pev_ref, w1k_ref, w1v_ref,
                     kc_ref, vcT_ref):
    C = xk_ref.shape[1]
    row = lax.broadcasted_iota(jnp.int32, (C, KV_W), 0)

    def phi(x_ref, wt_ref, wb_ref, w2_ref, pe_ref, w1_ref):
        x = x_ref[0]
        top = _dot(x, wt_ref[...])
        bot = _dot(x, wb_ref[...])
        peb = _dot(jnp.broadcast_to(pe_ref[...], (8, pe_ref.shape[1])), w1_ref[...])[0:1]
        peb = jnp.concatenate([peb, peb], axis=1)
        hid = top + pltpu.roll(bot, C - 1, 0) + peb
        out = _dot(_silu(hid).astype(jnp.bfloat16), w2_ref[...])
        return jnp.where(row < n_cmp, out, 0.0)

    kc_ref[0] = phi(xk_ref, wkt_ref, wkb_ref, w2k_ref, pek_ref, w1k_ref).astype(jnp.bfloat16)
    vc = phi(xv_ref, wvt_ref, wvb_ref, w2v_ref, pev_ref, w1v_ref)
    vcT_ref[0] = vc.T.astype(jnp.bfloat16)


def _compress(xk, xv, wkt, wkb, wvt, wvb, w2k, w2v, pek, pev, w1k, w1v, n_cmp):
    B, C, F = xk.shape
    full = lambda a: pl.BlockSpec(a.shape, lambda b: (0,) * a.ndim)
    return pl.pallas_call(
        functools.partial(_compress_kernel, n_cmp),
        grid=(B,),
        in_specs=[pl.BlockSpec((1, C, F), lambda b: (b, 0, 0)),
                  pl.BlockSpec((1, C, F), lambda b: (b, 0, 0)),
                  full(wkt), full(wkb), full(wvt), full(wvb), full(w2k), full(w2v),
                  full(pek), full(pev), full(w1k), full(w1v)],
        out_specs=[pl.BlockSpec((1, C, KV_W), lambda b: (b, 0, 0)),
                   pl.BlockSpec((1, KV_W, C), lambda b: (b, 0, 0))],
        out_shape=[jax.ShapeDtypeStruct((B, C, KV_W), jnp.bfloat16),
                   jax.ShapeDtypeStruct((B, KV_W, C), jnp.bfloat16)],
        compiler_params=pltpu.CompilerParams(
            dimension_semantics=("arbitrary",), vmem_limit_bytes=VMEM_LIMIT),
        name="compress",
    )(xk, xv, wkt, wkb, wvt, wvb, w2k, w2v, pek, pev, w1k, w1v)


def _keep_half(qT, half):
    z = jnp.zeros((HALF, qT.shape[1]), qT.dtype)
    if half == 0:
        return jnp.concatenate([qT[0:HALF], z], axis=0)
    return jnp.concatenate([z, qT[HALF:2 * HALF]], axis=0)


def _rank_rows(vals, n_rows):
    R, T = vals.shape
    tiles = [vals[a:a + SUBLANES] for a in range(0, R, SUBLANES)]
    ranks = [jnp.zeros((SUBLANES, T), jnp.int32) for _ in tiles]
    j_in = lax.broadcasted_iota(jnp.int32, (SUBLANES, T), 0)
    for m in range(n_rows):
        vm = vals[m:m + 1, :]
        for a, tile in enumerate(tiles):
            lo = a * SUBLANES
            if lo > m:
                beats = vm >= tile
            elif lo + SUBLANES - 1 <= m:
                beats = vm > tile
            else:
                beats = (vm > tile) | ((vm == tile) & (j_in > m - lo))
            ranks[a] = ranks[a] + beats.astype(jnp.int32)
    return jnp.concatenate(ranks, axis=0)


def _top_rows(vals, k):
    R, T = vals.shape
    row = lax.broadcasted_iota(jnp.int32, (R, T), 0)
    picked = jnp.zeros((R, T), jnp.bool_)
    for _ in range(k):
        best = jnp.max(vals, axis=0, keepdims=True)
        first = jnp.min(jnp.where(vals == best, row, R), axis=0, keepdims=True)
        hit = row == first
        picked = picked | hit
        vals = jnp.where(hit, -jnp.inf, vals)
    return picked


def _query_tile(cols, tile0):
    return tile0 + lax.shift_right_logical(cols, int(np.log2(TILE)))


def _split_bf16(x):
    hi = x.astype(jnp.bfloat16).astype(jnp.float32)
    return hi, x - hi


def _slope_parts(slope):
    c = np.float32(slope * LOG2E)
    hi = np.asarray(c, jnp.bfloat16).astype(np.float32)
    lo = np.asarray(c - hi, jnp.bfloat16).astype(np.float32)
    return float(hi), float(lo)


AUG_ROWS = 16
MOBA_SEL_ROWS = 16


def _pad_rows(x, n_rows):
    return jnp.concatenate([x, jnp.zeros((n_rows - x.shape[0], x.shape[1]), x.dtype)], axis=0)


def _alibi_rows(T, slope, tile0):
    r = lax.broadcasted_iota(jnp.int32, (AUG_ROWS, T), 0)
    c = lax.broadcasted_iota(jnp.int32, (AUG_ROWS, T), 1)
    t_abs = ((c & (TILE - 1)) + _query_tile(c, tile0) * TILE).astype(jnp.float32)
    c_hi, c_lo = _slope_parts(slope)
    w_hi, w_lo = _split_bf16(np.float32(slope * LOG2E) * t_abs)
    out = jnp.where(r == 0, -w_hi, 0.0)
    out = jnp.where(r == 1, -w_lo, out)
    out = jnp.where((r == 2) | (r == 4), c_hi, out)
    out = jnp.where((r == 3) | (r == 5), c_lo, out)
    return out


def _key_aug_table(nq, n_sel, blocks_per_tile):
    t = np.zeros((nq, TILE, LANES), np.float32)
    krel = np.arange(TILE)
    for j in range(nq):
        if n_sel:
            blk = j * blocks_per_tile + krel // (TILE // blocks_per_tile)
            t[j, krel, blk] = 1.0
        t[j, :, n_sel:n_sel + 2] = 1.0
        t[j, :, n_sel + 2:n_sel + 4] = krel[:, None]
        t[j, :, n_sel + 4:n_sel + 6] = TILE * j
    return jnp.asarray(t.reshape(nq * TILE // KEYS, KEYS, LANES), jnp.bfloat16)


ONES_ROWS = 16
ACC_ROWS = HALF + ONES_ROWS


WIN_TILES = WINDOW // TILE
WIN_KEYS = (WIN_TILES + QT // TILE) * TILE


def _win_key_aug_table():
    t = np.zeros((WIN_KEYS, LANES), np.float32)
    for slot in range(WIN_KEYS // TILE):
        rows = slice(slot * TILE, (slot + 1) * TILE)
        t[rows, 0:2] = 1.0
        t[rows, 2:4] = np.arange(TILE)[:, None]
        t[rows, 4:6] = TILE * (WIN_TILES - slot)
        if slot < WIN_TILES:
            t[rows, 6] = 1.0
    return jnp.asarray(t, jnp.bfloat16)


def _value_rows(vT_h):
    return jnp.concatenate([vT_h, jnp.ones((ONES_ROWS, vT_h.shape[1]), vT_h.dtype)], axis=0)


def _stage_scores(h, score_fn, blocks, cols, s_ref, mx_ref):
    c0, c1 = cols
    base = blocks[0][0]
    s_all = score_fn(h, base, blocks[-1][1], c0, c1)
    mx = None
    for r0, r1, keep in blocks:
        s = s_all[r0 - base:r1 - base]
        if keep is not None:
            s = jnp.where(keep, s, NEG_INF)
        s_ref[h, r0:r1, c0:c1] = s
        part = jnp.max(s, axis=0, keepdims=True)
        mx = part if mx is None else jnp.maximum(mx, part)
    mx_ref[h, :, c0:c1] = mx


def _consume_scores(h, value_fn, rows, cols, s_ref, mx_ref, m_ref, acc_ref):
    (r0, r1), (c0, c1) = rows, cols
    m_prev = m_ref[h, :, c0:c1]
    m_new = jnp.maximum(m_prev, mx_ref[h, :, c0:c1])
    alpha = jnp.exp2(m_prev - m_new)
    p = jnp.exp2(s_ref[h, r0:r1, c0:c1] - m_new).astype(jnp.bfloat16)
    acc_ref[h, :, c0:c1] = alpha * acc_ref[h, :, c0:c1] + _dot(value_fn(h, r0, r1), p)
    m_ref[h, :, c0:c1] = m_new


def _attend_heads(score_fn, value_fn, parts, s_ref, mx_ref, m_ref, acc_ref):
    for h in range(N_HEADS):
        for cols, blocks in parts:
            _stage_scores(h, score_fn, blocks, cols, s_ref, mx_ref)
    for h in range(N_HEADS):
        for cols, blocks in parts:
            rows = (min(b[0] for b in blocks), max(b[1] for b in blocks))
            _consume_scores(h, value_fn, rows, cols, s_ref, mx_ref, m_ref, acc_ref)


def _init_state(m_ref, acc_ref):
    m_ref[...] = jnp.full(m_ref.shape, NEG_INF, jnp.float32)
    acc_ref[...] = jnp.zeros(acc_ref.shape, jnp.float32)


def _tile_iotas(T=TILE):
    k = lax.broadcasted_iota(jnp.int32, (TILE, T), 0)
    t = lax.broadcasted_iota(jnp.int32, (TILE, T), 1)
    return k, t


def _key_step_parts(kind):
    if kind == "past":
        return [((0, QT), [(0, KEYS, None)])]
    k, t = _tile_iotas()
    tri = k <= t
    own = QT if kind == "reach" else 0
    first = [(0, own, None)] if own else []
    return [((0, TILE), first + [(own, own + TILE, tri)]),
            ((TILE, QT), [(0, own + TILE, None), (own + TILE, own + QT, tri)])]


def _store_pair(o_ref, p, o0T, o1T, szT, gates, rows=None):
    if gates is not None:
        o0T = o0T * gates[0]
        o1T = o1T * gates[1]
    oT = jnp.concatenate([o0T, o1T], axis=0) * szT
    r0, r1 = rows if rows is not None else (0, o_ref.shape[1])
    o_ref[0, r0:r1, p * LANES:(p + 1) * LANES] = oT.T.astype(o_ref.dtype)


def _finalize_pair(o_ref, acc_ref, p, szT, gates=None):
    a0 = acc_ref[2 * p]
    a1 = acc_ref[2 * p + 1]
    _store_pair(o_ref, p, a0[0:HALF] / a0[HALF:HALF + 1], a1[0:HALF] / a1[HALF:HALF + 1],
                szT, gates)


def _nsa_gates(gate_ref, p, branch, cols=None):
    c0, c1 = cols if cols is not None else (0, gate_ref.shape[2])
    return [gate_ref[0, _gate_row(g, p, branch):_gate_row(g, p, branch) + 1, c0:c1]
            for g in range(NSA_GROUPS)]


def _step_tables(n_q):
    it, jt = [], []
    for ii in range(n_q):
        for js in range(ii * QT // KEYS + 1):
            it.append(ii)
            jt.append(js)
    return jnp.asarray(it, jnp.int32), jnp.asarray(jt, jnp.int32)


def _key_steps(ii, js, score_fn, value_fn, finalize_fn, s_ref, mx_ref, m_ref, acc_ref):
    q0 = ii * QT
    k0 = js * KEYS
    for kind, when in (("past", k0 + KEYS <= q0), ("reach", k0 + QT == q0), ("own", k0 == q0)):
        @pl.when(when)
        def _():
            _attend_heads(score_fn, value_fn, _key_step_parts(kind), s_ref, mx_ref, m_ref, acc_ref)
            if kind != "past":
                finalize_fn()


def _state_scratch(n_keys, T=TILE):
    return [
        pltpu.VMEM((N_HEADS, 2 * LANES, T), jnp.bfloat16),
        pltpu.VMEM((N_HEADS, 1, T), jnp.float32),
        pltpu.VMEM((N_HEADS, ACC_ROWS, T), jnp.float32),
        pltpu.VMEM((N_HEADS, n_keys, T), jnp.float32),
        pltpu.VMEM((N_HEADS, 1, T), jnp.float32),
    ]


def _moba_kernel(nq, it_ref, jt_ref, qT_ref, k_ref, vT_ref, kmean_ref, kaug_ref, sz_ref, o_ref,
                 qaug_ref, m_ref, acc_ref, s_ref, mx_ref):
    s = pl.program_id(1)
    ii = it_ref[s]
    jj = jt_ref[s]
    n_sel = MOBA_SEL_ROWS

    @pl.when(jj == 0)
    def _():
        _init_state(m_ref, acc_ref)
        n_iota = lax.broadcasted_iota(jnp.int32, (nq, QT), 0)
        own = _query_tile(lax.broadcasted_iota(jnp.int32, (nq, QT), 1), 2 * ii)
        for p in range(N_PAIRS):
            qT = qT_ref[0, p]
            km = kmean_ref[0, :, p * LANES:(p + 1) * LANES].astype(jnp.bfloat16)
            for hh in range(2):
                h = 2 * p + hh
                qm = _keep_half(qT, hh)
                gate = jnp.where(n_iota < own, _dot(km, qm), NEG_INF)
                top = _top_rows(gate, min(MOBA_TOPK, nq))
                sel = (top & (n_iota < own)) | (n_iota == own)
                selneg = jnp.where(sel, 0.0, NEG_INF)
                if nq < n_sel:
                    selneg = jnp.concatenate(
                        [selneg, jnp.zeros((n_sel - nq, QT), jnp.float32)], axis=0)
                ali = _alibi_rows(QT, SLOPES[h], 2 * ii)
                aug = jnp.concatenate([selneg, ali], axis=0).astype(jnp.bfloat16)
                qaug_ref[h] = _pad_rows(jnp.concatenate([qm, aug], axis=0), 2 * LANES)

    def score(h, r0, r1, c0, c1):
        p = h // 2
        kfull = jnp.concatenate(
            [k_ref[0, r0:r1, p * LANES:(p + 1) * LANES], kaug_ref[jj, r0:r1]], axis=1)
        return _dot(kfull, qaug_ref[h, :, c0:c1])

    def value(h, r0, r1):
        return _value_rows(vT_ref[0, h // 2, (h % 2) * HALF:(h % 2 + 1) * HALF, r0:r1])

    def finalize():
        for p in range(N_PAIRS):
            _finalize_pair(o_ref, acc_ref, p, sz_ref[0, p])

    _key_steps(ii, jj, score, value, finalize, s_ref, mx_ref, m_ref, acc_ref)


def _moba(qT, k, vT, kmean, kaug, szT):
    B, _, _, S = qT.shape
    nq = S // TILE
    it, jt = _step_tables(S // QT)
    return pl.pallas_call(
        functools.partial(_moba_kernel, nq),
        grid_spec=pltpu.PrefetchScalarGridSpec(
            num_scalar_prefetch=2,
            grid=(B, int(it.shape[0])),
            in_specs=[
                pl.BlockSpec((1, N_PAIRS, LANES, QT), lambda b, s, it, jt: (b, 0, 0, it[s])),
                pl.BlockSpec((1, KEYS, MOBA_W), lambda b, s, it, jt: (b, jt[s], 0)),
                pl.BlockSpec((1, N_PAIRS, LANES, KEYS), lambda b, s, it, jt: (b, 0, 0, jt[s])),
                pl.BlockSpec((1, nq, MOBA_W), lambda b, s, it, jt: (b, 0, 0)),
                pl.BlockSpec(kaug.shape, lambda b, s, it, jt: (0, 0, 0)),
                pl.BlockSpec((1, N_PAIRS, LANES, QT), lambda b, s, it, jt: (b, 0, 0, it[s])),
            ],
            out_specs=pl.BlockSpec((1, QT, MOBA_W), lambda b, s, it, jt: (b, it[s], 0)),
            scratch_shapes=_state_scratch(KEYS, QT),
        ),
        out_shape=jax.ShapeDtypeStruct((B, S, MOBA_W), BRANCH_DTYPE),
        compiler_params=pltpu.CompilerParams(
            dimension_semantics=("arbitrary", "arbitrary"), vmem_limit_bytes=VMEM_LIMIT),
        name="moba",
    )(it, jt, qT, k, vT, kmean, kaug, szT)


CMP_BAND = TILE // CMP_STRIDE + 1
RANK_ROWS_STEP = 16


def _cmp_query_aug_table():
    t = np.zeros((LANES, TILE), np.float32)
    trel = np.arange(TILE)
    for u in range(CMP_BAND):
        t[u, trel < CMP_STRIDE * (u - 1) + CMP_LEN - 1] = NEG_INF
    t[CMP_BAND, :] = NEG_INF
    return jnp.asarray(t, jnp.bfloat16)


def _cmp_tile(i, cols, slot, n_slc, qT_ref, kc_ref, vcT_ref, ovT_ref, qaug_ref, sz_ref, gate_ref,
              o_ref, sel_ref, s_ref, mx_ref):
    c0, c1 = cols
    C = kc_ref.shape[1]
    c_rel = (lax.broadcasted_iota(jnp.int32, (C, LANES), 0)
             - (TILE // CMP_STRIDE) * i + 1)
    u = lax.broadcasted_iota(jnp.int32, (C, LANES), 1)
    kaug = jnp.where((c_rel == u) & (u < CMP_BAND), 1.0, 0.0)
    kaug = jnp.where((u == CMP_BAND) & (c_rel >= CMP_BAND), 1.0, kaug)
    kfull = jnp.concatenate([kc_ref[0], kaug.astype(jnp.bfloat16)], axis=1)
    qaug = qaug_ref[...]

    for p in range(N_PAIRS):
        qT = qT_ref[0, p, :, c0:c1]
        for g in range(NSA_GROUPS):
            h = slot * N_HEADS + 2 * p + g
            s = _dot(kfull, jnp.concatenate([_keep_half(qT, g), qaug], axis=0))
            s_ref[h] = s
            mx_ref[h] = jnp.max(s, axis=0, keepdims=True)

    t_abs = i * TILE + lax.broadcasted_iota(jnp.int32, (1, TILE), 1)
    seen = t_abs >= CMP_LEN - 1
    ones = jnp.ones((ONES_ROWS, C), jnp.bfloat16)
    vals = [jnp.concatenate([vcT_ref[0, g * HALF:(g + 1) * HALF], ones, ovT_ref[...]], axis=0)
            for g in range(NSA_GROUPS)]
    R = sel_ref.shape[2]
    imp = [jnp.zeros((R, TILE), jnp.float32) for _ in range(NSA_GROUPS)]
    for p in range(N_PAIRS):
        outs = []
        for g in range(NSA_GROUPS):
            h = slot * N_HEADS + 2 * p + g
            pr = jnp.exp2(s_ref[h] - mx_ref[h]).astype(jnp.bfloat16)
            acc = _dot(vals[g], pr)
            inv = jnp.where(seen, 1.0 / acc[HALF:HALF + 1], 0.0)
            outs.append(acc[0:HALF] * inv)
            imp[g] = imp[g] + acc[ACC_ROWS:ACC_ROWS + R] * inv
        _store_pair(o_ref, p, outs[0], outs[1], sz_ref[0, p, :, c0:c1],
                    _nsa_gates(gate_ref, p, BRANCH_CMP, cols), rows=cols)

    def select_blocks():
        j_iota = lax.broadcasted_iota(jnp.int32, (R, TILE), 0)
        t_q = i * TILE + lax.broadcasted_iota(jnp.int32, (R, TILE), 1)
        own = lax.shift_right_logical(t_q, int(np.log2(SLC_BLOCK)))
        forced = (j_iota == 0) | (j_iota == own) | (j_iota == own - 1)
        causal = j_iota <= own

        def select(n_rows):
            for g in range(NSA_GROUPS):
                v = jnp.where(forced, FORCED_SCORE, jnp.where(causal, imp[g], NEG_INF))[0:n_rows]
                rank = _rank_rows(v, min(n_rows, n_slc))
                sel = (rank < min(SLC_TOPN, n_slc)) & causal[0:n_rows]
                selneg = jnp.where(sel, 0.0, NEG_INF)
                if n_rows < R:
                    selneg = jnp.concatenate(
                        [selneg, jnp.full((R - n_rows, TILE), NEG_INF, jnp.float32)], axis=0)
                sel_ref[0, g, :, c0:c1] = selneg.astype(jnp.bfloat16)

        blocks_per_tile = TILE // SLC_BLOCK
        variants = list(range(RANK_ROWS_STEP, R, RANK_ROWS_STEP)) + [R]
        for idx, n_rows in enumerate(variants):
            lo = 0 if idx == 0 else variants[idx - 1] // blocks_per_tile
            hi = n_rows // blocks_per_tile

            @pl.when((i >= lo) & (i < hi) if n_rows < R else i >= lo)
            def _():
                select(n_rows)

    return select_blocks


def _slc_kernel(it_ref, jt_ref, qT_ref, k_ref, vT_ref, sel_ref, kaug_ref, sz_ref, gate_ref, o_ref,
                qaug_ref, m_ref, acc_ref, s_ref, mx_ref):
    s = pl.program_id(1)
    ii = it_ref[s]
    jj = jt_ref[s]
    n_sel = sel_ref.shape[2]

    @pl.when(jj == 0)
    def _():
        _init_state(m_ref, acc_ref)
        for p in range(N_PAIRS):
            qT = qT_ref[0, p]
            for g in range(NSA_GROUPS):
                h = 2 * p + g
                qm = _keep_half(qT, g)
                ali = _alibi_rows(QT, SLOPES[g * NSA_HPG + p], 2 * ii)
                qaug_ref[h] = _pad_rows(
                    jnp.concatenate([qm, sel_ref[0, g], ali.astype(jnp.bfloat16)], axis=0),
                    2 * LANES)

    def score(h, r0, r1, c0, c1):
        kfull = jnp.concatenate([k_ref[0, r0:r1], kaug_ref[jj, r0:r1]], axis=1)
        return _dot(kfull, qaug_ref[h, :, c0:c1])

    def value(h, r0, r1):
        return _value_rows(vT_ref[0, (h % 2) * HALF:(h % 2 + 1) * HALF, r0:r1])

    def finalize():
        for p in range(N_PAIRS):
            _finalize_pair(o_ref, acc_ref, p, sz_ref[0, p], _nsa_gates(gate_ref, p, BRANCH_SLC))

    _key_steps(ii, jj, score, value, finalize, s_ref, mx_ref, m_ref, acc_ref)


def _slc(qbT, ksl, vslT, selT, kaug, szT, gateT):
    B, _, _, S = qbT.shape
    R = selT.shape[2]
    it, jt = _step_tables(S // QT)
    return pl.pallas_call(
        _slc_kernel,
        grid_spec=pltpu.PrefetchScalarGridSpec(
            num_scalar_prefetch=2,
            grid=(B, int(it.shape[0])),
            in_specs=[
                pl.BlockSpec((1, N_PAIRS, LANES, QT), lambda b, s, it, jt: (b, 0, 0, it[s])),
                pl.BlockSpec((1, KEYS, KV_W), lambda b, s, it, jt: (b, jt[s], 0)),
                pl.BlockSpec((1, KV_W, KEYS), lambda b, s, it, jt: (b, 0, jt[s])),
                pl.BlockSpec((1, NSA_GROUPS, R, QT), lambda b, s, it, jt: (b, 0, 0, it[s])),
                pl.BlockSpec(kaug.shape, lambda b, s, it, jt: (0, 0, 0)),
                pl.BlockSpec((1, N_PAIRS, LANES, QT), lambda b, s, it, jt: (b, 0, 0, it[s])),
                pl.BlockSpec((1, GATE_ROWS, QT), lambda b, s, it, jt: (b, 0, it[s])),
            ],
            out_specs=pl.BlockSpec((1, QT, NSA_W), lambda b, s, it, jt: (b, it[s], 0)),
            scratch_shapes=_state_scratch(KEYS, QT),
        ),
        out_shape=jax.ShapeDtypeStruct((B, S, NSA_W), BRANCH_DTYPE),
        compiler_params=pltpu.CompilerParams(
            dimension_semantics=("arbitrary", "arbitrary"), vmem_limit_bytes=VMEM_LIMIT),
        name="slc",
    )(it, jt, qbT, ksl, vslT, selT, kaug, szT, gateT)


def _local_kernel(n_slc, qT_ref, kb_ref, kq_ref, vb_ref, vq_ref, kaug_ref,
                  kc_ref, vcT_ref, ovT_ref, cq_ref, sz_ref, gate_ref,
                  o_ref, oc_ref, sel_ref,
                  qaug_ref, m_ref, acc_ref, s_ref, mx_ref, cs_ref, cmx_ref):
    i = pl.program_id(1)
    selects = [
        _cmp_tile(i * (QT // TILE) + q, (q * TILE, (q + 1) * TILE), q, n_slc, qT_ref, kc_ref,
                  vcT_ref, ovT_ref, cq_ref, sz_ref, gate_ref, oc_ref, sel_ref, cs_ref, cmx_ref)
        for q in range(QT // TILE)]
    _init_state(m_ref, acc_ref)
    r = lax.broadcasted_iota(jnp.int32, (AUG_ROWS, QT), 0)
    c = lax.broadcasted_iota(jnp.int32, (AUG_ROWS, QT), 1)
    trel = (c & (TILE - 1)).astype(jnp.float32)
    absent = jnp.where(i >= 1, 0.0, NEG_INF)
    for p in range(N_PAIRS):
        qT = qT_ref[0, p]
        for g in range(NSA_GROUPS):
            h = 2 * p + g
            slope = SLOPES[g * NSA_HPG + p]
            c_hi, c_lo = _slope_parts(slope)
            w_hi, w_lo = _split_bf16(np.float32(slope * LOG2E) * trel)
            aug = jnp.where(r == 0, -w_hi, 0.0)
            aug = jnp.where(r == 1, -w_lo, aug)
            aug = jnp.where(r == 2, c_hi, aug)
            aug = jnp.where(r == 3, c_lo, aug)
            aug = jnp.where(r == 4, -c_hi, aug)
            aug = jnp.where(r == 5, -c_lo, aug)
            aug = jnp.where(r == 6, absent, aug)
            qaug_ref[h] = _pad_rows(
                jnp.concatenate([_keep_half(qT, g), aug.astype(jnp.bfloat16)], axis=0), 2 * LANES)

    kfull = jnp.concatenate(
        [jnp.concatenate([kb_ref[0], kq_ref[0]], axis=0), kaug_ref[...]], axis=1)
    vT = jnp.concatenate([vb_ref[0], vq_ref[0]], axis=1)
    k, t = _tile_iotas()

    def score(h, r0, r1, c0, c1):
        return _dot(kfull[r0:r1], qaug_ref[h, :, c0:c1])

    def value(h, r0, r1):
        return _value_rows(vT[(h % 2) * HALF:(h % 2 + 1) * HALF, r0:r1])

    parts = []
    for q in range(QT // TILE):
        first, last = q * TILE, (q + WIN_TILES) * TILE
        parts.append(((q * TILE, (q + 1) * TILE),
                      [(first, first + TILE, k > t), (first + TILE, last, None),
                       (last, last + TILE, k <= t)]))
    _attend_heads(score, value, parts, s_ref, mx_ref, m_ref, acc_ref)
    for p in range(N_PAIRS):
        _finalize_pair(o_ref, acc_ref, p, sz_ref[0, p], _nsa_gates(gate_ref, p, BRANCH_WIN))
    for select_blocks in selects:
        select_blocks()


def _local(qbT, kwi, vwiT, kaug, kc, vcT, ovT, cmp_qaug, szT, gateT, n_slc):
    B, _, _, S = qbT.shape
    assert WINDOW == QT
    C = kc.shape[1]
    R = ovT.shape[0]
    before = lambda i: jnp.maximum(i - 1, 0)
    q_slabs = pl.BlockSpec((1, N_PAIRS, LANES, QT), lambda b, i: (b, 0, 0, i))
    whole = lambda a: pl.BlockSpec(a.shape, lambda b, i: (0,) * a.ndim)
    rows_out = pl.BlockSpec((1, QT, NSA_W), lambda b, i: (b, i, 0))
    return pl.pallas_call(
        functools.partial(_local_kernel, n_slc),
        grid=(B, S // QT),
        in_specs=[
            q_slabs,
            pl.BlockSpec((1, WINDOW, KV_W), lambda b, i: (b, before(i), 0)),
            pl.BlockSpec((1, QT, KV_W), lambda b, i: (b, i, 0)),
            pl.BlockSpec((1, KV_W, WINDOW), lambda b, i: (b, 0, before(i))),
            pl.BlockSpec((1, KV_W, QT), lambda b, i: (b, 0, i)),
            whole(kaug),
            pl.BlockSpec((1, C, KV_W), lambda b, i: (b, 0, 0)),
            pl.BlockSpec((1, KV_W, C), lambda b, i: (b, 0, 0)),
            whole(ovT), whole(cmp_qaug),
            q_slabs,
            pl.BlockSpec((1, GATE_ROWS, QT), lambda b, i: (b, 0, i)),
        ],
        out_specs=[rows_out, rows_out,
                   pl.BlockSpec((1, NSA_GROUPS, R, QT), lambda b, i: (b, 0, 0, i))],
        out_shape=[jax.ShapeDtypeStruct((B, S, NSA_W), BRANCH_DTYPE),
                   jax.ShapeDtypeStruct((B, S, NSA_W), BRANCH_DTYPE),
                   jax.ShapeDtypeStruct((B, NSA_GROUPS, R, S), jnp.bfloat16)],
        scratch_shapes=_state_scratch(WIN_KEYS, QT) + [
            pltpu.VMEM((QT // TILE * N_HEADS, C, TILE), jnp.float32),
            pltpu.VMEM((QT // TILE * N_HEADS, 1, TILE), jnp.float32)],
        compiler_params=pltpu.CompilerParams(
            dimension_semantics=("arbitrary", "arbitrary"), vmem_limit_bytes=VMEM_LIMIT),
        name="local",
    )(qbT, kwi, kwi, vwiT, vwiT, kaug, kc, vcT, ovT, cmp_qaug, szT, gateT)


def _out_kernel(x_ref, oa_ref, oc_ref, os_ref, ow_ref, w_ref, g_ref, y_ref):
    f32 = jnp.float32
    ob = oc_ref[0].astype(f32) + os_ref[0].astype(f32) + ow_ref[0].astype(f32)
    mix = jnp.concatenate([oa_ref[0].astype(jnp.bfloat16), ob.astype(jnp.bfloat16)], axis=1)
    y = _dot(mix, w_ref[...])
    r = lax.rsqrt(jnp.mean(y * y, axis=-1, keepdims=True) + RMS_EPS)
    y_ref[0] = x_ref[0] + y * r * g_ref[...]


def _out(x, oa, oc, os_, ow, w, g):
    B, S, _ = x.shape
    rows = lambda wd: pl.BlockSpec((1, ROWS, wd), lambda b, i: (b, i, 0))
    return pl.pallas_call(
        _out_kernel,
        grid=(B, S // ROWS),
        in_specs=[rows(D_MODEL), rows(MOBA_W), rows(NSA_W), rows(NSA_W), rows(NSA_W),
                  pl.BlockSpec(w.shape, lambda b, i: (0, 0)),
                  pl.BlockSpec((1, D_MODEL), lambda b, i: (0, 0))],
        out_specs=rows(D_MODEL),
        out_shape=jax.ShapeDtypeStruct((B, S, D_MODEL), jnp.float32),
        compiler_params=pltpu.CompilerParams(
            dimension_semantics=("arbitrary", "arbitrary"), vmem_limit_bytes=VMEM_LIMIT),
        name="out",
    )(x, oa, oc, os_, ow, w, g)


def _compress_weights(w1, w2):
    half = (CMP_LEN // 2) * HEAD_DIM
    w1r = w1.reshape(2, CMP_LEN // 2, HEAD_DIM, CMP_HIDDEN)
    z = jnp.zeros_like(w1r)
    g0 = jnp.concatenate([w1r, z], axis=-1)
    g1 = jnp.concatenate([z, w1r], axis=-1)
    both = jnp.stack([g0, g1], axis=2)
    both = both.reshape(2, half * NSA_GROUPS, NSA_GROUPS * CMP_HIDDEN).astype(jnp.bfloat16)
    zz = jnp.zeros_like(w2)
    w2bd = jnp.concatenate([jnp.concatenate([w2, zz], axis=1),
                            jnp.concatenate([zz, w2], axis=1)], axis=0).astype(jnp.bfloat16)
    return both[0], both[1], w2bd


def _overlap_T(n_cmp_pad, n_slc, rows):
    c = np.arange(n_cmp_pad)[None, :] * CMP_STRIDE
    j = np.arange(rows)[:, None] * SLC_BLOCK
    ov = (c < j + SLC_BLOCK) & (c + CMP_LEN > j) & (np.arange(rows)[:, None] < n_slc)
    return jnp.asarray(ov.astype(np.float32), jnp.bfloat16)


def _layer(x, pre_g, post_g, w_in, pos_k, pos_v, w_k1, w_k2, w_v1, w_v2, w_out):
    B, S, _ = x.shape
    nq = S // TILE
    n_cmp = (S - CMP_LEN) // CMP_STRIDE + 1
    n_slc = S // SLC_BLOCK
    C = S // CMP_STRIDE
    wn, wt = _projection_weights(w_in)
    (ka, kmean, kcm, vcm, ksl, kwi, qaT, vaT, qbT, vslT, vwiT,
     szaT, szbT, gateT) = _proj(x, pre_g.reshape(1, D_MODEL), wn, wt)

    wkt, wkb, w2k = _compress_weights(w_k1, w_k2)
    wvt, wvb, w2v = _compress_weights(w_v1, w_v2)
    chunk = CMP_STRIDE * KV_W
    kc, vcT = _compress(
        kcm.reshape(B, C, chunk), vcm.reshape(B, C, chunk), wkt, wkb, wvt, wvb, w2k, w2v,
        pos_k.reshape(1, CMP_LEN * HEAD_DIM).astype(jnp.bfloat16),
        pos_v.reshape(1, CMP_LEN * HEAD_DIM).astype(jnp.bfloat16),
        w_k1.astype(jnp.bfloat16), w_v1.astype(jnp.bfloat16), n_cmp)

    ow, oc, selT = _local(qbT, kwi, vwiT, _win_key_aug_table(), kc, vcT,
                          _overlap_T(C, n_slc, SLC_BLOCK), _cmp_query_aug_table(),
                          szbT, gateT, n_slc)
    oa = _moba(qaT, ka, vaT, kmean.reshape(B, nq, MOBA_W), _key_aug_table(nq, MOBA_SEL_ROWS, 1),
               szaT)
    osl = _slc(qbT, ksl, vslT, selT, _key_aug_table(nq, selT.shape[2], TILE // SLC_BLOCK),
               szbT, gateT)

    w_o = jnp.concatenate([w_out[:MOBA_W], _to_pair_slabs(w_out[MOBA_W:], 0)],
                          axis=0).astype(jnp.bfloat16)
    return _out(x, oa, oc, osl, ow, w_o, post_g.reshape(1, D_MODEL))


def kernel(x, pre_norm_g, post_norm_g, w_in, cmp_pos_k, cmp_pos_v,
           w_cmp_k1, w_cmp_k2, w_cmp_v1, w_cmp_v2, w_out):
    for l in range(pre_norm_g.shape[0]):
        x = _layer(x, pre_norm_g[l], post_norm_g[l], w_in[l], cmp_pos_k[l], cmp_pos_v[l],
                   w_cmp_k1[l], w_cmp_k2[l], w_cmp_v1[l], w_cmp_v2[l], w_out[l])
    return x
```

```python
import functools

import numpy as np
import jax
import jax.numpy as jnp
from jax import lax
from jax.experimental import pallas as pl
from jax.experimental.pallas import tpu as pltpu

D_MODEL = 1024
HEAD_DIM = 64
N_HEADS = 8
N_PAIRS = N_HEADS // 2
NSA_GROUPS = 2
NSA_HPG = N_HEADS // NSA_GROUPS
MOBA_BLOCK = 256
MOBA_TOPK = 3
CMP_LEN = 32
CMP_STRIDE = 16
CMP_HIDDEN = 128
SLC_BLOCK = 64
SLC_TOPN = 16
WINDOW = 512
RMS_EPS = 1e-6
NEG_INF = -1e30
FORCED_SCORE = 1e9
SCALE = HEAD_DIM ** -0.5
LOG2E = float(np.log2(np.e))

TILE = 256
QT = 2 * TILE
KEYS = 4 * TILE
ROWS = 4 * TILE
LANES = 128
HALF = HEAD_DIM
SUBLANES = 8
MOBA_W = N_HEADS * HEAD_DIM
NSA_W = N_HEADS * HEAD_DIM
KV_W = NSA_GROUPS * HEAD_DIM
N_GATES = 3 * N_HEADS
GATE_ROWS = 32
SLOPES = tuple(2.0 ** (-(i + 1)) for i in range(N_HEADS))
BRANCH_CMP, BRANCH_SLC, BRANCH_WIN = 0, 1, 2
BRANCH_DTYPE = jnp.bfloat16

VMEM_LIMIT = 48 * 1024 * 1024

_OFF = dict(qa=0, ka=512, va=1024, za=1536, qb=2048, kcm=2560, vcm=2688,
            ksl=2816, vsl=2944, kwi=3072, vwi=3200, gate=3328, zb=3352)
_NAT = dict(ka=(0, 512), kcm=(512, 640), vcm=(640, 768), ksl=(768, 896), kwi=(896, 1024))
N_NAT = 1024
_TR = dict(qa=(0, 512), va=(512, 1024), qb=(1024, 1536), vsl=(1536, 1664),
           vwi=(1664, 1792), za=(1792, 2304), zb=(2304, 2816), gate=(2816, 2848))
N_TR = 2848


def _to_pair_slabs(w, axis):
    w = jnp.moveaxis(w, axis, -1)
    lead = w.shape[:-1]
    w = w.reshape(lead + (NSA_GROUPS, NSA_HPG, HEAD_DIM))
    w = jnp.swapaxes(w, -3, -2).reshape(lead + (NSA_W,))
    return jnp.moveaxis(w, -1, axis)


def _projection_weights(w_in):
    col = lambda name, width: w_in[:, _OFF[name]:_OFF[name] + width]
    natural = jnp.concatenate(
        [col("ka", MOBA_W), col("kcm", KV_W), col("vcm", KV_W), col("ksl", KV_W), col("kwi", KV_W)],
        axis=1)
    transposed = jnp.concatenate(
        [col("qa", MOBA_W), col("va", MOBA_W), _to_pair_slabs(col("qb", NSA_W), 1),
         col("vsl", KV_W), col("vwi", KV_W), col("za", MOBA_W), _to_pair_slabs(col("zb", NSA_W), 1),
         col("gate", N_GATES), jnp.zeros((D_MODEL, GATE_ROWS - N_GATES), w_in.dtype)], axis=1)
    return natural.astype(jnp.bfloat16), transposed.T.astype(jnp.bfloat16)


def _gate_row(g, p, branch):
    return (g * NSA_HPG + p) * 3 + branch


def _sigmoid(x):
    return 1.0 / (1.0 + jnp.exp(-x))


def _silu(x):
    return x * _sigmoid(x)


def _dot(a, b):
    return jnp.dot(a, b, preferred_element_type=jnp.float32)


def _proj_kernel(x_ref, g_ref, wn_ref, wt_ref,
                 ka_ref, kmean_ref, kcm_ref, vcm_ref, ksl_ref, kwi_ref,
                 qaT_ref, vaT_ref, qbT_ref, vslT_ref, vwiT_ref,
                 szaT_ref, szbT_ref, gateT_ref):
    bf = jnp.bfloat16
    for blk in range(x_ref.shape[1] // MOBA_BLOCK):
        r0, r1 = blk * MOBA_BLOCK, (blk + 1) * MOBA_BLOCK
        x = x_ref[0, r0:r1]
        r = lax.rsqrt(jnp.mean(x * x, axis=-1, keepdims=True) + RMS_EPS)
        h = (x * r * g_ref[...]).astype(bf)

        natural = _dot(h, wn_ref[...])

        def nat(name):
            a, b = _NAT[name]
            return natural[:, a:b]

        ka = nat("ka")
        kmean_ref[0, blk] = jnp.mean(ka, axis=0, keepdims=True)
        ka_ref[0, r0:r1] = ka.astype(bf)
        kcm_ref[0, r0:r1] = nat("kcm").astype(bf)
        vcm_ref[0, r0:r1] = nat("vcm").astype(bf)
        ksl_ref[0, r0:r1] = nat("ksl").astype(bf)
        kwi_ref[0, r0:r1] = nat("kwi").astype(bf)

        def tr(name):
            a, b = _TR[name]
            return lax.dot_general(wt_ref[a:b, :], h, (((1,), (1,)), ((), ())),
                                   preferred_element_type=jnp.float32)

        qa = tr("qa") * (SCALE * LOG2E)
        va = tr("va")
        qb = tr("qb") * (SCALE * LOG2E)
        sza = _silu(tr("za"))
        szb = _silu(tr("zb"))
        for p in range(N_PAIRS):
            sl = slice(p * LANES, (p + 1) * LANES)
            qaT_ref[0, p, :, r0:r1] = qa[sl].astype(bf)
            vaT_ref[0, p, :, r0:r1] = va[sl].astype(bf)
            qbT_ref[0, p, :, r0:r1] = qb[sl].astype(bf)
            szaT_ref[0, p, :, r0:r1] = sza[sl]
            szbT_ref[0, p, :, r0:r1] = szb[sl]
        vslT_ref[0, :, r0:r1] = tr("vsl").astype(bf)
        vwiT_ref[0, :, r0:r1] = tr("vwi").astype(bf)
        gateT_ref[0, :, r0:r1] = _sigmoid(tr("gate"))


def _proj(x, g, wn, wt):
    B, S, _ = x.shape
    nq = S // TILE
    bf, f32 = jnp.bfloat16, jnp.float32
    slab_tr = lambda: pl.BlockSpec((1, N_PAIRS, LANES, ROWS), lambda b, i: (b, 0, 0, i))
    rows = lambda w: pl.BlockSpec((1, ROWS, w), lambda b, i: (b, i, 0))
    chans = lambda c: pl.BlockSpec((1, c, ROWS), lambda b, i: (b, 0, i))
    out_shape = [
        jax.ShapeDtypeStruct((B, S, MOBA_W), bf),
        jax.ShapeDtypeStruct((B, nq, 1, MOBA_W), f32),
        jax.ShapeDtypeStruct((B, S, KV_W), bf),
        jax.ShapeDtypeStruct((B, S, KV_W), bf),
        jax.ShapeDtypeStruct((B, S, KV_W), bf),
        jax.ShapeDtypeStruct((B, S, KV_W), bf),
        jax.ShapeDtypeStruct((B, N_PAIRS, LANES, S), bf),
        jax.ShapeDtypeStruct((B, N_PAIRS, LANES, S), bf),
        jax.ShapeDtypeStruct((B, N_PAIRS, LANES, S), bf),
        jax.ShapeDtypeStruct((B, KV_W, S), bf),
        jax.ShapeDtypeStruct((B, KV_W, S), bf),
        jax.ShapeDtypeStruct((B, N_PAIRS, LANES, S), f32),
        jax.ShapeDtypeStruct((B, N_PAIRS, LANES, S), f32),
        jax.ShapeDtypeStruct((B, GATE_ROWS, S), f32),
    ]
    out_specs = [
        rows(MOBA_W),
        pl.BlockSpec((1, ROWS // MOBA_BLOCK, 1, MOBA_W), lambda b, i: (b, i, 0, 0)),
        rows(KV_W), rows(KV_W), rows(KV_W), rows(KV_W),
        slab_tr(), slab_tr(), slab_tr(),
        chans(KV_W), chans(KV_W),
        slab_tr(), slab_tr(),
        chans(GATE_ROWS),
    ]
    return pl.pallas_call(
        _proj_kernel,
        grid=(B, S // ROWS),
        in_specs=[
            pl.BlockSpec((1, ROWS, D_MODEL), lambda b, i: (b, i, 0)),
            pl.BlockSpec((1, D_MODEL), lambda b, i: (0, 0)),
            pl.BlockSpec((D_MODEL, N_NAT), lambda b, i: (0, 0)),
            pl.BlockSpec((N_TR, D_MODEL), lambda b, i: (0, 0)),
        ],
        out_specs=out_specs,
        out_shape=out_shape,
        compiler_params=pltpu.CompilerParams(
            dimension_semantics=("arbitrary", "arbitrary"),
            vmem_limit_bytes=VMEM_LIMIT),
        name="proj",
    )(x, g, wn, wt)


def _compress_kernel(n_cmp, xk_ref, xv_ref, wkt_ref, wkb_ref, wvt_ref, wvb_ref,
                     w2k_ref, w2v_ref, pek_ref, pev_ref, w1k_ref, w1v_ref,
                     kc_ref, vcT_ref):
    C = xk_ref.shape[1]
    row = lax.broadcasted_iota(jnp.int32, (C, KV_W), 0)

    def phi(x_ref, wt_ref, wb_ref, w2_ref, pe_ref, w1_ref):
        x = x_ref[0]
        top = _dot(x, wt_ref[...])
        bot = _dot(x, wb_ref[...])
        peb = _dot(jnp.broadcast_to(pe_ref[...], (8, pe_ref.shape[1])), w1_ref[...])[0:1]
        peb = jnp.concatenate([peb, peb], axis=1)
        hid = top + pltpu.roll(bot, C - 1, 0) + peb
        out = _dot(_silu(hid).astype(jnp.bfloat16), w2_ref[...])
        return jnp.where(row < n_cmp, out, 0.0)

    kc_ref[0] = phi(xk_ref, wkt_ref, wkb_ref, w2k_ref, pek_ref, w1k_ref).astype(jnp.bfloat16)
    vc = phi(xv_ref, wvt_ref, wvb_ref, w2v_ref, pev_ref, w1v_ref)
    vcT_ref[0] = vc.T.astype(jnp.bfloat16)


def _compress(xk, xv, wkt, wkb, wvt, wvb, w2k, w2v, pek, pev, w1k, w1v, n_cmp):
    B, C, F = xk.shape
    full = lambda a: pl.BlockSpec(a.shape, lambda b: (0,) * a.ndim)
    return pl.pallas_call(
        functools.partial(_compress_kernel, n_cmp),
        grid=(B,),
        in_specs=[pl.BlockSpec((1, C, F), lambda b: (b, 0, 0)),
                  pl.BlockSpec((1, C, F), lambda b: (b, 0, 0)),
                  full(wkt), full(wkb), full(wvt), full(wvb), full(w2k), full(w2v),
                  full(pek), full(pev), full(w1k), full(w1v)],
        out_specs=[pl.BlockSpec((1, C, KV_W), lambda b: (b, 0, 0)),
                   pl.BlockSpec((1, KV_W, C), lambda b: (b, 0, 0))],
        out_shape=[jax.ShapeDtypeStruct((B, C, KV_W), jnp.bfloat16),
                   jax.ShapeDtypeStruct((B, KV_W, C), jnp.bfloat16)],
        compiler_params=pltpu.CompilerParams(
            dimension_semantics=("arbitrary",), vmem_limit_bytes=VMEM_LIMIT),
        name="compress",
    )(xk, xv, wkt, wkb, wvt, wvb, w2k, w2v, pek, pev, w1k, w1v)


def _keep_half(qT, half):
    z = jnp.zeros((HALF, qT.shape[1]), qT.dtype)
    if half == 0:
        return jnp.concatenate([qT[0:HALF], z], axis=0)
    return jnp.concatenate([z, qT[HALF:2 * HALF]], axis=0)


def _rank_rows(vals, n_rows):
    R, T = vals.shape
    tiles = [vals[a:a + SUBLANES] for a in range(0, R, SUBLANES)]
    ranks = [jnp.zeros((SUBLANES, T), jnp.int32) for _ in tiles]
    j_in = lax.broadcasted_iota(jnp.int32, (SUBLANES, T), 0)
    for m in range(n_rows):
        vm = vals[m:m + 1, :]
        for a, tile in enumerate(tiles):
            lo = a * SUBLANES
            if lo > m:
                beats = vm >= tile
            elif lo + SUBLANES - 1 <= m:
                beats = vm > tile
            else:
                beats = (vm > tile) | ((vm == tile) & (j_in > m - lo))
            ranks[a] = ranks[a] + beats.astype(jnp.int32)
    return jnp.concatenate(ranks, axis=0)


def _top_rows(vals, k):
    R, T = vals.shape
    row = lax.broadcasted_iota(jnp.int32, (R, T), 0)
    picked = jnp.zeros((R, T), jnp.bool_)
    for _ in range(k):
        best = jnp.max(vals, axis=0, keepdims=True)
        first = jnp.min(jnp.where(vals == best, row, R), axis=0, keepdims=True)
        hit = row == first
        picked = picked | hit
        vals = jnp.where(hit, -jnp.inf, vals)
    return picked


def _query_tile(cols, tile0):
    return tile0 + lax.shift_right_logical(cols, int(np.log2(TILE)))


def _split_bf16(x):
    hi = x.astype(jnp.bfloat16).astype(jnp.float32)
    return hi, x - hi


def _slope_parts(slope):
    c = np.float32(slope * LOG2E)
    hi = np.asarray(c, jnp.bfloat16).astype(np.float32)
    lo = np.asarray(c - hi, jnp.bfloat16).astype(np.float32)
    return float(hi), float(lo)


AUG_ROWS = 16
MOBA_SEL_ROWS = 16


def _pad_rows(x, n_rows):
    return jnp.concatenate([x, jnp.zeros((n_rows - x.shape[0], x.shape[1]), x.dtype)], axis=0)


def _alibi_rows(T, slope, tile0):
    r = lax.broadcasted_iota(jnp.int32, (AUG_ROWS, T), 0)
    c = lax.broadcasted_iota(jnp.int32, (AUG_ROWS, T), 1)
    t_abs = ((c & (TILE - 1)) + _query_tile(c, tile0) * TILE).astype(jnp.float32)
    c_hi, c_lo = _slope_parts(slope)
    w_hi, w_lo = _split_bf16(np.float32(slope * LOG2E) * t_abs)
    out = jnp.where(r == 0, -w_hi, 0.0)
    out = jnp.where(r == 1, -w_lo, out)
    out = jnp.where((r == 2) | (r == 4), c_hi, out)
    out = jnp.where((r == 3) | (r == 5), c_lo, out)
    return out


def _key_aug_table(nq, n_sel, blocks_per_tile):
    t = np.zeros((nq, TILE, LANES), np.float32)
    krel = np.arange(TILE)
    for j in range(nq):
        if n_sel:
            blk = j * blocks_per_tile + krel // (TILE // blocks_per_tile)
            t[j, krel, blk] = 1.0
        t[j, :, n_sel:n_sel + 2] = 1.0
        t[j, :, n_sel + 2:n_sel + 4] = krel[:, None]
        t[j, :, n_sel + 4:n_sel + 6] = TILE * j
    return jnp.asarray(t.reshape(nq * TILE // KEYS, KEYS, LANES), jnp.bfloat16)


ONES_ROWS = 16
ACC_ROWS = HALF + ONES_ROWS


WIN_TILES = WINDOW // TILE
WIN_KEYS = (WIN_TILES + QT // TILE) * TILE


def _win_key_aug_table():
    t = np.zeros((WIN_KEYS, LANES), np.float32)
    for slot in range(WIN_KEYS // TILE):
        rows = slice(slot * TILE, (slot + 1) * TILE)
        t[rows, 0:2] = 1.0
        t[rows, 2:4] = np.arange(TILE)[:, None]
        t[rows, 4:6] = TILE * (WIN_TILES - slot)
        if slot < WIN_TILES:
            t[rows, 6] = 1.0
    return jnp.asarray(t, jnp.bfloat16)


def _value_rows(vT_h):
    return jnp.concatenate([vT_h, jnp.ones((ONES_ROWS, vT_h.shape[1]), vT_h.dtype)], axis=0)


def _stage_scores(h, score_fn, blocks, cols, s_ref, mx_ref):
    c0, c1 = cols
    base = blocks[0][0]
    s_all = score_fn(h, base, blocks[-1][1], c0, c1)
    mx = None
    for r0, r1, keep in blocks:
        s = s_all[r0 - base:r1 - base]
        if keep is not None:
            s = jnp.where(keep, s, NEG_INF)
        s_ref[h, r0:r1, c0:c1] = s
        part = jnp.max(s, axis=0, keepdims=True)
        mx = part if mx is None else jnp.maximum(mx, part)
    mx_ref[h, :, c0:c1] = mx


def _consume_scores(h, value_fn, rows, cols, s_ref, mx_ref, m_ref, acc_ref):
    (r0, r1), (c0, c1) = rows, cols
    m_prev = m_ref[h, :, c0:c1]
    m_new = jnp.maximum(m_prev, mx_ref[h, :, c0:c1])
    alpha = jnp.exp2(m_prev - m_new)
    p = jnp.exp2(s_ref[h, r0:r1, c0:c1] - m_new).astype(jnp.bfloat16)
    acc_ref[h, :, c0:c1] = alpha * acc_ref[h, :, c0:c1] + _dot(value_fn(h, r0, r1), p)
    m_ref[h, :, c0:c1] = m_new


def _attend_heads(score_fn, value_fn, parts, s_ref, mx_ref, m_ref, acc_ref):
    for h in range(N_HEADS):
        for cols, blocks in parts:
            _stage_scores(h, score_fn, blocks, cols, s_ref, mx_ref)
    for h in range(N_HEADS):
        for cols, blocks in parts:
            rows = (min(b[0] for b in blocks), max(b[1] for b in blocks))
            _consume_scores(h, value_fn, rows, cols, s_ref, mx_ref, m_ref, acc_ref)


def _init_state(m_ref, acc_ref):
    m_ref[...] = jnp.full(m_ref.shape, NEG_INF, jnp.float32)
    acc_ref[...] = jnp.zeros(acc_ref.shape, jnp.float32)


def _tile_iotas(T=TILE):
    k = lax.broadcasted_iota(jnp.int32, (TILE, T), 0)
    t = lax.broadcasted_iota(jnp.int32, (TILE, T), 1)
    return k, t


def _key_step_parts(kind):
    if kind == "past":
        return [((0, QT), [(0, KEYS, None)])]
    k, t = _tile_iotas()
    tri = k <= t
    own = QT if kind == "reach" else 0
    first = [(0, own, None)] if own else []
    return [((0, TILE), first + [(own, own + TILE, tri)]),
            ((TILE, QT), [(0, own + TILE, None), (own + TILE, own + QT, tri)])]


def _store_pair(o_ref, p, o0T, o1T, szT, gates, rows=None):
    if gates is not None:
        o0T = o0T * gates[0]
        o1T = o1T * gates[1]
    oT = jnp.concatenate([o0T, o1T], axis=0) * szT
    r0, r1 = rows if rows is not None else (0, o_ref.shape[1])
    o_ref[0, r0:r1, p * LANES:(p + 1) * LANES] = oT.T.astype(o_ref.dtype)


def _finalize_pair(o_ref, acc_ref, p, szT, gates=None):
    a0 = acc_ref[2 * p]
    a1 = acc_ref[2 * p + 1]
    _store_pair(o_ref, p, a0[0:HALF] / a0[HALF:HALF + 1], a1[0:HALF] / a1[HALF:HALF + 1],
                szT, gates)


def _nsa_gates(gate_ref, p, branch, cols=None):
    c0, c1 = cols if cols is not None else (0, gate_ref.shape[2])
    return [gate_ref[0, _gate_row(g, p, branch):_gate_row(g, p, branch) + 1, c0:c1]
            for g in range(NSA_GROUPS)]


def _step_tables(n_q):
    it, jt = [], []
    for ii in range(n_q):
        for js in range(ii * QT // KEYS + 1):
            it.append(ii)
            jt.append(js)
    return jnp.asarray(it, jnp.int32), jnp.asarray(jt, jnp.int32)


def _key_steps(ii, js, score_fn, value_fn, finalize_fn, s_ref, mx_ref, m_ref, acc_ref):
    q0 = ii * QT
    k0 = js * KEYS
    for kind, when in (("past", k0 + KEYS <= q0), ("reach", k0 + QT == q0), ("own", k0 == q0)):
        @pl.when(when)
        def _():
            _attend_heads(score_fn, value_fn, _key_step_parts(kind), s_ref, mx_ref, m_ref, acc_ref)
            if kind != "past":
                finalize_fn()


def _state_scratch(n_keys, T=TILE):
    return [
        pltpu.VMEM((N_HEADS, 2 * LANES, T), jnp.bfloat16),
        pltpu.VMEM((N_HEADS, 1, T), jnp.float32),
        pltpu.VMEM((N_HEADS, ACC_ROWS, T), jnp.float32),
        pltpu.VMEM((N_HEADS, n_keys, T), jnp.float32),
        pltpu.VMEM((N_HEADS, 1, T), jnp.float32),
    ]


def _moba_kernel(nq, it_ref, jt_ref, qT_ref, k_ref, vT_ref, kmean_ref, kaug_ref, sz_ref, o_ref,
                 qaug_ref, m_ref, acc_ref, s_ref, mx_ref):
    s = pl.program_id(1)
    ii = it_ref[s]
    jj = jt_ref[s]
    n_sel = MOBA_SEL_ROWS

    @pl.when(jj == 0)
    def _():
        _init_state(m_ref, acc_ref)
        n_iota = lax.broadcasted_iota(jnp.int32, (nq, QT), 0)
        own = _query_tile(lax.broadcasted_iota(jnp.int32, (nq, QT), 1), 2 * ii)
        for p in range(N_PAIRS):
            qT = qT_ref[0, p]
            km = kmean_ref[0, :, p * LANES:(p + 1) * LANES].astype(jnp.bfloat16)
            for hh in range(2):
                h = 2 * p + hh
                qm = _keep_half(qT, hh)
                gate = jnp.where(n_iota < own, _dot(km, qm), NEG_INF)
                top = _top_rows(gate, min(MOBA_TOPK, nq))
                sel = (top & (n_iota < own)) | (n_iota == own)
                selneg = jnp.where(sel, 0.0, NEG_INF)
                if nq < n_sel:
                    selneg = jnp.concatenate(
                        [selneg, jnp.zeros((n_sel - nq, QT), jnp.float32)], axis=0)
                ali = _alibi_rows(QT, SLOPES[h], 2 * ii)
                aug = jnp.concatenate([selneg, ali], axis=0).astype(jnp.bfloat16)
                qaug_ref[h] = _pad_rows(jnp.concatenate([qm, aug], axis=0), 2 * LANES)

    def score(h, r0, r1, c0, c1):
        p = h // 2
        kfull = jnp.concatenate(
            [k_ref[0, r0:r1, p * LANES:(p + 1) * LANES], kaug_ref[jj, r0:r1]], axis=1)
        return _dot(kfull, qaug_ref[h, :, c0:c1])

    def value(h, r0, r1):
        return _value_rows(vT_ref[0, h // 2, (h % 2) * HALF:(h % 2 + 1) * HALF, r0:r1])

    def finalize():
        for p in range(N_PAIRS):
            _finalize_pair(o_ref, acc_ref, p, sz_ref[0, p])

    _key_steps(ii, jj, score, value, finalize, s_ref, mx_ref, m_ref, acc_ref)


def _moba(qT, k, vT, kmean, kaug, szT):
    B, _, _, S = qT.shape
    nq = S // TILE
    it, jt = _step_tables(S // QT)
    return pl.pallas_call(
        functools.partial(_moba_kernel, nq),
        grid_spec=pltpu.PrefetchScalarGridSpec(
            num_scalar_prefetch=2,
            grid=(B, int(it.shape[0])),
            in_specs=[
                pl.BlockSpec((1, N_PAIRS, LANES, QT), lambda b, s, it, jt: (b, 0, 0, it[s])),
                pl.BlockSpec((1, KEYS, MOBA_W), lambda b, s, it, jt: (b, jt[s], 0)),
                pl.BlockSpec((1, N_PAIRS, LANES, KEYS), lambda b, s, it, jt: (b, 0, 0, jt[s])),
                pl.BlockSpec((1, nq, MOBA_W), lambda b, s, it, jt: (b, 0, 0)),
                pl.BlockSpec(kaug.shape, lambda b, s, it, jt: (0, 0, 0)),
                pl.BlockSpec((1, N_PAIRS, LANES, QT), lambda b, s, it, jt: (b, 0, 0, it[s])),
            ],
            out_specs=pl.BlockSpec((1, QT, MOBA_W), lambda b, s, it, jt: (b, it[s], 0)),
            scratch_shapes=_state_scratch(KEYS, QT),
        ),
        out_shape=jax.ShapeDtypeStruct((B, S, MOBA_W), BRANCH_DTYPE),
        compiler_params=pltpu.CompilerParams(
            dimension_semantics=("arbitrary", "arbitrary"), vmem_limit_bytes=VMEM_LIMIT),
        name="moba",
    )(it, jt, qT, k, vT, kmean, kaug, szT)


CMP_BAND = TILE // CMP_STRIDE + 1
RANK_ROWS_STEP = 8


def _cmp_query_aug_table():
    t = np.zeros((LANES, TILE), np.float32)
    trel = np.arange(TILE)
    for u in range(CMP_BAND):
        t[u, trel < CMP_STRIDE * (u - 1) + CMP_LEN - 1] = NEG_INF
    t[CMP_BAND, :] = NEG_INF
    return jnp.asarray(t, jnp.bfloat16)


def _cmp_tile(i, cols, slot, n_slc, qT_ref, kc_ref, vcT_ref, ovT_ref, qaug_ref, sz_ref, gate_ref,
              o_ref, sel_ref, s_ref, mx_ref):
    c0, c1 = cols
    C = kc_ref.shape[1]
    c_rel = (lax.broadcasted_iota(jnp.int32, (C, LANES), 0)
             - (TILE // CMP_STRIDE) * i + 1)
    u = lax.broadcasted_iota(jnp.int32, (C, LANES), 1)
    kaug = jnp.where((c_rel == u) & (u < CMP_BAND), 1.0, 0.0)
    kaug = jnp.where((u == CMP_BAND) & (c_rel >= CMP_BAND), 1.0, kaug)
    kfull = jnp.concatenate([kc_ref[0], kaug.astype(jnp.bfloat16)], axis=1)
    qaug = qaug_ref[...]

    for p in range(N_PAIRS):
        qT = qT_ref[0, p, :, c0:c1]
        for g in range(NSA_GROUPS):
            h = slot * N_HEADS + 2 * p + g
            s = _dot(kfull, jnp.concatenate([_keep_half(qT, g), qaug], axis=0))
            s_ref[h] = s
            mx_ref[h] = jnp.max(s, axis=0, keepdims=True)

    t_abs = i * TILE + lax.broadcasted_iota(jnp.int32, (1, TILE), 1)
    seen = t_abs >= CMP_LEN - 1
    ones = jnp.ones((ONES_ROWS, C), jnp.bfloat16)
    vals = [jnp.concatenate([vcT_ref[0, g * HALF:(g + 1) * HALF], ones, ovT_ref[...]], axis=0)
            for g in range(NSA_GROUPS)]
    R = sel_ref.shape[2]
    imp = [jnp.zeros((R, TILE), jnp.float32) for _ in range(NSA_GROUPS)]
    for p in range(N_PAIRS):
        outs = []
        for g in range(NSA_GROUPS):
            h = slot * N_HEADS + 2 * p + g
            pr = jnp.exp2(s_ref[h] - mx_ref[h]).astype(jnp.bfloat16)
            acc = _dot(vals[g], pr)
            inv = jnp.where(seen, 1.0 / acc[HALF:HALF + 1], 0.0)
            outs.append(acc[0:HALF] * inv)
            imp[g] = imp[g] + acc[ACC_ROWS:ACC_ROWS + R] * inv
        _store_pair(o_ref, p, outs[0], outs[1], sz_ref[0, p, :, c0:c1],
                    _nsa_gates(gate_ref, p, BRANCH_CMP, cols), rows=cols)

    def select_blocks():
        j_iota = lax.broadcasted_iota(jnp.int32, (R, TILE), 0)
        t_q = i * TILE + lax.broadcasted_iota(jnp.int32, (R, TILE), 1)
        own = lax.shift_right_logical(t_q, int(np.log2(SLC_BLOCK)))
        forced = (j_iota == 0) | (j_iota == own) | (j_iota == own - 1)
        causal = j_iota <= own

        def select(n_rows):
            top_n = min(SLC_TOPN, n_slc)
            for g in range(NSA_GROUPS):
                if n_rows <= top_n:
                    sel = causal[0:n_rows]
                else:
                    v = jnp.where(forced, FORCED_SCORE,
                                  jnp.where(causal, imp[g], NEG_INF))[0:n_rows]
                    sel = (_rank_rows(v, min(n_rows, n_slc)) < top_n) & causal[0:n_rows]
                selneg = jnp.where(sel, 0.0, NEG_INF)
                if n_rows < R:
                    selneg = jnp.concatenate(
                        [selneg, jnp.full((R - n_rows, TILE), NEG_INF, jnp.float32)], axis=0)
                sel_ref[0, g, :, c0:c1] = selneg.astype(jnp.bfloat16)

        blocks_per_tile = TILE // SLC_BLOCK
        variants = list(range(RANK_ROWS_STEP, R, RANK_ROWS_STEP)) + [R]
        for idx, n_rows in enumerate(variants):
            lo = 0 if idx == 0 else variants[idx - 1] // blocks_per_tile
            hi = n_rows // blocks_per_tile

            @pl.when((i >= lo) & (i < hi) if n_rows < R else i >= lo)
            def _():
                select(n_rows)

    return select_blocks


def _slc_kernel(it_ref, jt_ref, qT_ref, k_ref, vT_ref, sel_ref, kaug_ref, sz_ref, gate_ref, o_ref,
                qaug_ref, m_ref, acc_ref, s_ref, mx_ref):
    s = pl.program_id(1)
    ii = it_ref[s]
    jj = jt_ref[s]
    n_sel = sel_ref.shape[2]

    @pl.when(jj == 0)
    def _():
        _init_state(m_ref, acc_ref)
        for p in range(N_PAIRS):
            qT = qT_ref[0, p]
            for g in range(NSA_GROUPS):
                h = 2 * p + g
                qm = _keep_half(qT, g)
                ali = _alibi_rows(QT, SLOPES[g * NSA_HPG + p], 2 * ii)
                qaug_ref[h] = _pad_rows(
                    jnp.concatenate([qm, sel_ref[0, g], ali.astype(jnp.bfloat16)], axis=0),
                    2 * LANES)

    def score(h, r0, r1, c0, c1):
        kfull = jnp.concatenate([k_ref[0, r0:r1], kaug_ref[jj, r0:r1]], axis=1)
        return _dot(kfull, qaug_ref[h, :, c0:c1])

    def value(h, r0, r1):
        return _value_rows(vT_ref[0, (h % 2) * HALF:(h % 2 + 1) * HALF, r0:r1])

    def finalize():
        for p in range(N_PAIRS):
            _finalize_pair(o_ref, acc_ref, p, sz_ref[0, p], _nsa_gates(gate_ref, p, BRANCH_SLC))

    _key_steps(ii, jj, score, value, finalize, s_ref, mx_ref, m_ref, acc_ref)


def _slc(qbT, ksl, vslT, selT, kaug, szT, gateT):
    B, _, _, S = qbT.shape
    R = selT.shape[2]
    it, jt = _step_tables(S // QT)
    return pl.pallas_call(
        _slc_kernel,
        grid_spec=pltpu.PrefetchScalarGridSpec(
            num_scalar_prefetch=2,
            grid=(B, int(it.shape[0])),
            in_specs=[
                pl.BlockSpec((1, N_PAIRS, LANES, QT), lambda b, s, it, jt: (b, 0, 0, it[s])),
                pl.BlockSpec((1, KEYS, KV_W), lambda b, s, it, jt: (b, jt[s], 0)),
                pl.BlockSpec((1, KV_W, KEYS), lambda b, s, it, jt: (b, 0, jt[s])),
                pl.BlockSpec((1, NSA_GROUPS, R, QT), lambda b, s, it, jt: (b, 0, 0, it[s])),
                pl.BlockSpec(kaug.shape, lambda b, s, it, jt: (0, 0, 0)),
                pl.BlockSpec((1, N_PAIRS, LANES, QT), lambda b, s, it, jt: (b, 0, 0, it[s])),
                pl.BlockSpec((1, GATE_ROWS, QT), lambda b, s, it, jt: (b, 0, it[s])),
            ],
            out_specs=pl.BlockSpec((1, QT, NSA_W), lambda b, s, it, jt: (b, it[s], 0)),
            scratch_shapes=_state_scratch(KEYS, QT),
        ),
        out_shape=jax.ShapeDtypeStruct((B, S, NSA_W), BRANCH_DTYPE),
        compiler_params=pltpu.CompilerParams(
            dimension_semantics=("arbitrary", "arbitrary"), vmem_limit_bytes=VMEM_LIMIT),
        name="slc",
    )(it, jt, qbT, ksl, vslT, selT, kaug, szT, gateT)


def _local_kernel(n_slc, qT_ref, kb_ref, kq_ref, vb_ref, vq_ref, kaug_ref,
                  kc_ref, vcT_ref, ovT_ref, cq_ref, sz_ref, gate_ref,
                  o_ref, oc_ref, sel_ref,
                  qaug_ref, m_ref, acc_ref, s_ref, mx_ref, cs_ref, cmx_ref):
    i = pl.program_id(1)
    selects = [
        _cmp_tile(i * (QT // TILE) + q, (q * TILE, (q + 1) * TILE), q, n_slc, qT_ref, kc_ref,
                  vcT_ref, ovT_ref, cq_ref, sz_ref, gate_ref, oc_ref, sel_ref, cs_ref, cmx_ref)
        for q in range(QT // TILE)]
    _init_state(m_ref, acc_ref)
    r = lax.broadcasted_iota(jnp.int32, (AUG_ROWS, QT), 0)
    c = lax.broadcasted_iota(jnp.int32, (AUG_ROWS, QT), 1)
    trel = (c & (TILE - 1)).astype(jnp.float32)
    absent = jnp.where(i >= 1, 0.0, NEG_INF)
    for p in range(N_PAIRS):
        qT = qT_ref[0, p]
        for g in range(NSA_GROUPS):
            h = 2 * p + g
            slope = SLOPES[g * NSA_HPG + p]
            c_hi, c_lo = _slope_parts(slope)
            w_hi, w_lo = _split_bf16(np.float32(slope * LOG2E) * trel)
            aug = jnp.where(r == 0, -w_hi, 0.0)
            aug = jnp.where(r == 1, -w_lo, aug)
            aug = jnp.where(r == 2, c_hi, aug)
            aug = jnp.where(r == 3, c_lo, aug)
            aug = jnp.where(r == 4, -c_hi, aug)
            aug = jnp.where(r == 5, -c_lo, aug)
            aug = jnp.where(r == 6, absent, aug)
            qaug_ref[h] = _pad_rows(
                jnp.concatenate([_keep_half(qT, g), aug.astype(jnp.bfloat16)], axis=0), 2 * LANES)

    kfull = jnp.concatenate(
        [jnp.concatenate([kb_ref[0], kq_ref[0]], axis=0), kaug_ref[...]], axis=1)
    vT = jnp.concatenate([vb_ref[0], vq_ref[0]], axis=1)
    k, t = _tile_iotas()

    def score(h, r0, r1, c0, c1):
        return _dot(kfull[r0:r1], qaug_ref[h, :, c0:c1])

    def value(h, r0, r1):
        return _value_rows(vT[(h % 2) * HALF:(h % 2 + 1) * HALF, r0:r1])

    parts = []
    for q in range(QT // TILE):
        first, last = q * TILE, (q + WIN_TILES) * TILE
        parts.append(((q * TILE, (q + 1) * TILE),
                      [(first, first + TILE, k > t), (first + TILE, last, None),
                       (last, last + TILE, k <= t)]))
    _attend_heads(score, value, parts, s_ref, mx_ref, m_ref, acc_ref)
    for p in range(N_PAIRS):
        _finalize_pair(o_ref, acc_ref, p, sz_ref[0, p], _nsa_gates(gate_ref, p, BRANCH_WIN))
    for select_blocks in selects:
        select_blocks()


def _local(qbT, kwi, vwiT, kaug, kc, vcT, ovT, cmp_qaug, szT, gateT, n_slc):
    B, _, _, S = qbT.shape
    assert WINDOW == QT
    C = kc.shape[1]
    R = ovT.shape[0]
    before = lambda i: jnp.maximum(i - 1, 0)
    q_slabs = pl.BlockSpec((1, N_PAIRS, LANES, QT), lambda b, i: (b, 0, 0, i))
    whole = lambda a: pl.BlockSpec(a.shape, lambda b, i: (0,) * a.ndim)
    rows_out = pl.BlockSpec((1, QT, NSA_W), lambda b, i: (b, i, 0))
    return pl.pallas_call(
        functools.partial(_local_kernel, n_slc),
        grid=(B, S // QT),
        in_specs=[
            q_slabs,
            pl.BlockSpec((1, WINDOW, KV_W), lambda b, i: (b, before(i), 0)),
            pl.BlockSpec((1, QT, KV_W), lambda b, i: (b, i, 0)),
            pl.BlockSpec((1, KV_W, WINDOW), lambda b, i: (b, 0, before(i))),
            pl.BlockSpec((1, KV_W, QT), lambda b, i: (b, 0, i)),
            whole(kaug),
            pl.BlockSpec((1, C, KV_W), lambda b, i: (b, 0, 0)),
            pl.BlockSpec((1, KV_W, C), lambda b, i: (b, 0, 0)),
            whole(ovT), whole(cmp_qaug),
            q_slabs,
            pl.BlockSpec((1, GATE_ROWS, QT), lambda b, i: (b, 0, i)),
        ],
        out_specs=[rows_out, rows_out,
                   pl.BlockSpec((1, NSA_GROUPS, R, QT), lambda b, i: (b, 0, 0, i))],
        out_shape=[jax.ShapeDtypeStruct((B, S, NSA_W), BRANCH_DTYPE),
                   jax.ShapeDtypeStruct((B, S, NSA_W), BRANCH_DTYPE),
                   jax.ShapeDtypeStruct((B, NSA_GROUPS, R, S), jnp.bfloat16)],
        scratch_shapes=_state_scratch(WIN_KEYS, QT) + [
            pltpu.VMEM((QT // TILE * N_HEADS, C, TILE), jnp.float32),
            pltpu.VMEM((QT // TILE * N_HEADS, 1, TILE), jnp.float32)],
        compiler_params=pltpu.CompilerParams(
            dimension_semantics=("arbitrary", "arbitrary"), vmem_limit_bytes=VMEM_LIMIT),
        name="local",
    )(qbT, kwi, kwi, vwiT, vwiT, kaug, kc, vcT, ovT, cmp_qaug, szT, gateT)


def _out_kernel(x_ref, oa_ref, oc_ref, os_ref, ow_ref, w_ref, g_ref, y_ref):
    f32 = jnp.float32
    ob = oc_ref[0].astype(f32) + os_ref[0].astype(f32) + ow_ref[0].astype(f32)
    mix = jnp.concatenate([oa_ref[0].astype(jnp.bfloat16), ob.astype(jnp.bfloat16)], axis=1)
    y = _dot(mix, w_ref[...])
    r = lax.rsqrt(jnp.mean(y * y, axis=-1, keepdims=True) + RMS_EPS)
    y_ref[0] = x_ref[0] + y * r * g_ref[...]


def _out(x, oa, oc, os_, ow, w, g):
    B, S, _ = x.shape
    rows = lambda wd: pl.BlockSpec((1, ROWS, wd), lambda b, i: (b, i, 0))
    return pl.pallas_call(
        _out_kernel,
        grid=(B, S // ROWS),
        in_specs=[rows(D_MODEL), rows(MOBA_W), rows(NSA_W), rows(NSA_W), rows(NSA_W),
                  pl.BlockSpec(w.shape, lambda b, i: (0, 0)),
                  pl.BlockSpec((1, D_MODEL), lambda b, i: (0, 0))],
        out_specs=rows(D_MODEL),
        out_shape=jax.ShapeDtypeStruct((B, S, D_MODEL), jnp.float32),
        compiler_params=pltpu.CompilerParams(
            dimension_semantics=("arbitrary", "arbitrary"), vmem_limit_bytes=VMEM_LIMIT),
        name="out",
    )(x, oa, oc, os_, ow, w, g)


def _compress_weights(w1, w2):
    half = (CMP_LEN // 2) * HEAD_DIM
    w1r = w1.reshape(2, CMP_LEN // 2, HEAD_DIM, CMP_HIDDEN)
    z = jnp.zeros_like(w1r)
    g0 = jnp.concatenate([w1r, z], axis=-1)
    g1 = jnp.concatenate([z, w1r], axis=-1)
    both = jnp.stack([g0, g1], axis=2)
    both = both.reshape(2, half * NSA_GROUPS, NSA_GROUPS * CMP_HIDDEN).astype(jnp.bfloat16)
    zz = jnp.zeros_like(w2)
    w2bd = jnp.concatenate([jnp.concatenate([w2, zz], axis=1),
                            jnp.concatenate([zz, w2], axis=1)], axis=0).astype(jnp.bfloat16)
    return both[0], both[1], w2bd


def _overlap_T(n_cmp_pad, n_slc, rows):
    c = np.arange(n_cmp_pad)[None, :] * CMP_STRIDE
    j = np.arange(rows)[:, None] * SLC_BLOCK
    ov = (c < j + SLC_BLOCK) & (c + CMP_LEN > j) & (np.arange(rows)[:, None] < n_slc)
    return jnp.asarray(ov.astype(np.float32), jnp.bfloat16)


def _layer(x, pre_g, post_g, w_in, pos_k, pos_v, w_k1, w_k2, w_v1, w_v2, w_out):
    B, S, _ = x.shape
    nq = S // TILE
    n_cmp = (S - CMP_LEN) // CMP_STRIDE + 1
    n_slc = S // SLC_BLOCK
    C = S // CMP_STRIDE
    wn, wt = _projection_weights(w_in)
    (ka, kmean, kcm, vcm, ksl, kwi, qaT, vaT, qbT, vslT, vwiT,
     szaT, szbT, gateT) = _proj(x, pre_g.reshape(1, D_MODEL), wn, wt)

    wkt, wkb, w2k = _compress_weights(w_k1, w_k2)
    wvt, wvb, w2v = _compress_weights(w_v1, w_v2)
    chunk = CMP_STRIDE * KV_W
    kc, vcT = _compress(
        kcm.reshape(B, C, chunk), vcm.reshape(B, C, chunk), wkt, wkb, wvt, wvb, w2k, w2v,
        pos_k.reshape(1, CMP_LEN * HEAD_DIM).astype(jnp.bfloat16),
        pos_v.reshape(1, CMP_LEN * HEAD_DIM).astype(jnp.bfloat16),
        w_k1.astype(jnp.bfloat16), w_v1.astype(jnp.bfloat16), n_cmp)

    ow, oc, selT = _local(qbT, kwi, vwiT, _win_key_aug_table(), kc, vcT,
                          _overlap_T(C, n_slc, SLC_BLOCK), _cmp_query_aug_table(),
                          szbT, gateT, n_slc)
    oa = _moba(qaT, ka, vaT, kmean.reshape(B, nq, MOBA_W), _key_aug_table(nq, MOBA_SEL_ROWS, 1),
               szaT)
    osl = _slc(qbT, ksl, vslT, selT, _key_aug_table(nq, selT.shape[2], TILE // SLC_BLOCK),
               szbT, gateT)

    w_o = jnp.concatenate([w_out[:MOBA_W], _to_pair_slabs(w_out[MOBA_W:], 0)],
                          axis=0).astype(jnp.bfloat16)
    return _out(x, oa, oc, osl, ow, w_o, post_g.reshape(1, D_MODEL))


def kernel(x, pre_norm_g, post_norm_g, w_in, cmp_pos_k, cmp_pos_v,
           w_cmp_k1, w_cmp_k2, w_cmp_v1, w_cmp_v2, w_out):
    for l in range(pre_norm_g.shape[0]):
        x = _layer(x, pre_norm_g[l], post_norm_g[l], w_in[l], cmp_pos_k[l], cmp_pos_v[l],
                   w_cmp_k1[l], w_cmp_k2[l], w_cmp_v1[l], w_cmp_v2[l], w_out[l])
    return x
```

```python
import functools

import numpy as np
import jax
import jax.numpy as jnp
from jax import lax
from jax.experimental import pallas as pl
from jax.experimental.pallas import tpu as pltpu

D_MODEL = 1024
HEAD_DIM = 64
N_HEADS = 8
N_PAIRS = N_HEADS // 2
NSA_GROUPS = 2
NSA_HPG = N_HEADS // NSA_GROUPS
MOBA_BLOCK = 256
MOBA_TOPK = 3
CMP_LEN = 32
CMP_STRIDE = 16
CMP_HIDDEN = 128
SLC_BLOCK = 64
SLC_TOPN = 16
WINDOW = 512
RMS_EPS = 1e-6
NEG_INF = -1e30
FORCED_SCORE = 1e9
SCALE = HEAD_DIM ** -0.5
LOG2E = float(np.log2(np.e))

TILE = 256
QT = 2 * TILE
KEYS = 4 * TILE
ROWS = 4 * TILE
LANES = 128
HALF = HEAD_DIM
SUBLANES = 8
MOBA_W = N_HEADS * HEAD_DIM
NSA_W = N_HEADS * HEAD_DIM
KV_W = NSA_GROUPS * HEAD_DIM
N_GATES = 3 * N_HEADS
GATE_ROWS = 32
SLOPES = tuple(2.0 ** (-(i + 1)) for i in range(N_HEADS))
BRANCH_CMP, BRANCH_SLC, BRANCH_WIN = 0, 1, 2
BRANCH_DTYPE = jnp.bfloat16

VMEM_LIMIT = 48 * 1024 * 1024
RESIDENT = pl.Buffered(1)

_OFF = dict(qa=0, ka=512, va=1024, za=1536, qb=2048, kcm=2560, vcm=2688,
            ksl=2816, vsl=2944, kwi=3072, vwi=3200, gate=3328, zb=3352)
_NAT = dict(ka=(0, 512), kcm=(512, 640), vcm=(640, 768), ksl=(768, 896), kwi=(896, 1024))
N_NAT = 1024
_TR = dict(qa=(0, 512), va=(512, 1024), qb=(1024, 1536), vsl=(1536, 1664),
           vwi=(1664, 1792), za=(1792, 2304), zb=(2304, 2816), gate=(2816, 2848))
N_TR = 2848


def _to_pair_slabs(w, axis):
    w = jnp.moveaxis(w, axis, -1)
    lead = w.shape[:-1]
    w = w.reshape(lead + (NSA_GROUPS, NSA_HPG, HEAD_DIM))
    w = jnp.swapaxes(w, -3, -2).reshape(lead + (NSA_W,))
    return jnp.moveaxis(w, -1, axis)


def _projection_weights(w_in):
    col = lambda name, width: w_in[:, _OFF[name]:_OFF[name] + width]
    natural = jnp.concatenate(
        [col("ka", MOBA_W), col("kcm", KV_W), col("vcm", KV_W), col("ksl", KV_W), col("kwi", KV_W)],
        axis=1)
    transposed = jnp.concatenate(
        [col("qa", MOBA_W), col("va", MOBA_W), _to_pair_slabs(col("qb", NSA_W), 1),
         col("vsl", KV_W), col("vwi", KV_W), col("za", MOBA_W), _to_pair_slabs(col("zb", NSA_W), 1),
         col("gate", N_GATES), jnp.zeros((D_MODEL, GATE_ROWS - N_GATES), w_in.dtype)], axis=1)
    return natural.astype(jnp.bfloat16), transposed.T.astype(jnp.bfloat16)


def _gate_row(g, p, branch):
    return (g * NSA_HPG + p) * 3 + branch


def _sigmoid(x):
    return 1.0 / (1.0 + jnp.exp(-x))


def _silu(x):
    return x * _sigmoid(x)


def _dot(a, b):
    return jnp.dot(a, b, preferred_element_type=jnp.float32)


def _proj_kernel(x_ref, g_ref, wn_ref, wt_ref,
                 ka_ref, kmean_ref, kcm_ref, vcm_ref, ksl_ref, kwi_ref,
                 qaT_ref, vaT_ref, qbT_ref, vslT_ref, vwiT_ref,
                 szaT_ref, szbT_ref, gateT_ref):
    bf = jnp.bfloat16
    for blk in range(x_ref.shape[1] // MOBA_BLOCK):
        r0, r1 = blk * MOBA_BLOCK, (blk + 1) * MOBA_BLOCK
        x = x_ref[0, r0:r1]
        r = lax.rsqrt(jnp.mean(x * x, axis=-1, keepdims=True) + RMS_EPS)
        h = (x * r * g_ref[...]).astype(bf)

        natural = _dot(h, wn_ref[...])

        def nat(name):
            a, b = _NAT[name]
            return natural[:, a:b]

        ka = nat("ka")
        kmean_ref[0, blk] = jnp.mean(ka, axis=0, keepdims=True)
        ka_ref[0, r0:r1] = ka.astype(bf)
        kcm_ref[0, r0:r1] = nat("kcm").astype(bf)
        vcm_ref[0, r0:r1] = nat("vcm").astype(bf)
        ksl_ref[0, r0:r1] = nat("ksl").astype(bf)
        kwi_ref[0, r0:r1] = nat("kwi").astype(bf)

        def tr(name):
            a, b = _TR[name]
            return lax.dot_general(wt_ref[a:b, :], h, (((1,), (1,)), ((), ())),
                                   preferred_element_type=jnp.float32)

        qa = tr("qa") * (SCALE * LOG2E)
        va = tr("va")
        qb = tr("qb") * (SCALE * LOG2E)
        sza = _silu(tr("za"))
        szb = _silu(tr("zb"))
        for p in range(N_PAIRS):
            sl = slice(p * LANES, (p + 1) * LANES)
            qaT_ref[0, p, :, r0:r1] = qa[sl].astype(bf)
            vaT_ref[0, p, :, r0:r1] = va[sl].astype(bf)
            qbT_ref[0, p, :, r0:r1] = qb[sl].astype(bf)
            szaT_ref[0, p, :, r0:r1] = sza[sl]
            szbT_ref[0, p, :, r0:r1] = szb[sl]
        vslT_ref[0, :, r0:r1] = tr("vsl").astype(bf)
        vwiT_ref[0, :, r0:r1] = tr("vwi").astype(bf)
        gateT_ref[0, :, r0:r1] = _sigmoid(tr("gate"))


def _proj(x, g, wn, wt):
    B, S, _ = x.shape
    nq = S // TILE
    bf, f32 = jnp.bfloat16, jnp.float32
    slab_tr = lambda: pl.BlockSpec((1, N_PAIRS, LANES, ROWS), lambda b, i: (b, 0, 0, i))
    rows = lambda w: pl.BlockSpec((1, ROWS, w), lambda b, i: (b, i, 0))
    chans = lambda c: pl.BlockSpec((1, c, ROWS), lambda b, i: (b, 0, i))
    out_shape = [
        jax.ShapeDtypeStruct((B, S, MOBA_W), bf),
        jax.ShapeDtypeStruct((B, nq, 1, MOBA_W), f32),
        jax.ShapeDtypeStruct((B, S, KV_W), bf),
        jax.ShapeDtypeStruct((B, S, KV_W), bf),
        jax.ShapeDtypeStruct((B, S, KV_W), bf),
        jax.ShapeDtypeStruct((B, S, KV_W), bf),
        jax.ShapeDtypeStruct((B, N_PAIRS, LANES, S), bf),
        jax.ShapeDtypeStruct((B, N_PAIRS, LANES, S), bf),
        jax.ShapeDtypeStruct((B, N_PAIRS, LANES, S), bf),
        jax.ShapeDtypeStruct((B, KV_W, S), bf),
        jax.ShapeDtypeStruct((B, KV_W, S), bf),
        jax.ShapeDtypeStruct((B, N_PAIRS, LANES, S), f32),
        jax.ShapeDtypeStruct((B, N_PAIRS, LANES, S), f32),
        jax.ShapeDtypeStruct((B, GATE_ROWS, S), f32),
    ]
    out_specs = [
        rows(MOBA_W),
        pl.BlockSpec((1, ROWS // MOBA_BLOCK, 1, MOBA_W), lambda b, i: (b, i, 0, 0)),
        rows(KV_W), rows(KV_W), rows(KV_W), rows(KV_W),
        slab_tr(), slab_tr(), slab_tr(),
        chans(KV_W), chans(KV_W),
        slab_tr(), slab_tr(),
        chans(GATE_ROWS),
    ]
    return pl.pallas_call(
        _proj_kernel,
        grid=(B, S // ROWS),
        in_specs=[
            pl.BlockSpec((1, ROWS, D_MODEL), lambda b, i: (b, i, 0)),
            pl.BlockSpec((1, D_MODEL), lambda b, i: (0, 0), pipeline_mode=RESIDENT),
            pl.BlockSpec((D_MODEL, N_NAT), lambda b, i: (0, 0), pipeline_mode=RESIDENT),
            pl.BlockSpec((N_TR, D_MODEL), lambda b, i: (0, 0), pipeline_mode=RESIDENT),
        ],
        out_specs=out_specs,
        out_shape=out_shape,
        compiler_params=pltpu.CompilerParams(
            dimension_semantics=("arbitrary", "arbitrary"),
            vmem_limit_bytes=VMEM_LIMIT),
        name="proj",
    )(x, g, wn, wt)


def _compress_kernel(n_cmp, xk_ref, xv_ref, wkt_ref, wkb_ref, wvt_ref, wvb_ref,
                     w2k_ref, w2v_ref, pek_ref, pev_ref, w1k_ref, w1v_ref,
                     kc_ref, vcT_ref):
    C = xk_ref.shape[1]
    row = lax.broadcasted_iota(jnp.int32, (C, KV_W), 0)

    def phi(x_ref, wt_ref, wb_ref, w2_ref, pe_ref, w1_ref):
        x = x_ref[0]
        top = _dot(x, wt_ref[...])
        bot = _dot(x, wb_ref[...])
        peb = _dot(jnp.broadcast_to(pe_ref[...], (8, pe_ref.shape[1])), w1_ref[...])[0:1]
        peb = jnp.concatenate([peb, peb], axis=1)
        hid = top + pltpu.roll(bot, C - 1, 0) + peb
        out = _dot(_silu(hid).astype(jnp.bfloat16), w2_ref[...])
        return jnp.where(row < n_cmp, out, 0.0)

    kc_ref[0] = phi(xk_ref, wkt_ref, wkb_ref, w2k_ref, pek_ref, w1k_ref).astype(jnp.bfloat16)
    vc = phi(xv_ref, wvt_ref, wvb_ref, w2v_ref, pev_ref, w1v_ref)
    vcT_ref[0] = vc.T.astype(jnp.bfloat16)


def _compress(xk, xv, wkt, wkb, wvt, wvb, w2k, w2v, pek, pev, w1k, w1v, n_cmp):
    B, C, F = xk.shape
    full = lambda a: pl.BlockSpec(a.shape, lambda b: (0,) * a.ndim, pipeline_mode=RESIDENT)
    return pl.pallas_call(
        functools.partial(_compress_kernel, n_cmp),
        grid=(B,),
        in_specs=[pl.BlockSpec((1, C, F), lambda b: (b, 0, 0)),
                  pl.BlockSpec((1, C, F), lambda b: (b, 0, 0)),
                  full(wkt), full(wkb), full(wvt), full(wvb), full(w2k), full(w2v),
                  full(pek), full(pev), full(w1k), full(w1v)],
        out_specs=[pl.BlockSpec((1, C, KV_W), lambda b: (b, 0, 0)),
                   pl.BlockSpec((1, KV_W, C), lambda b: (b, 0, 0))],
        out_shape=[jax.ShapeDtypeStruct((B, C, KV_W), jnp.bfloat16),
                   jax.ShapeDtypeStruct((B, KV_W, C), jnp.bfloat16)],
        compiler_params=pltpu.CompilerParams(
            dimension_semantics=("arbitrary",), vmem_limit_bytes=VMEM_LIMIT),
        name="compress",
    )(xk, xv, wkt, wkb, wvt, wvb, w2k, w2v, pek, pev, w1k, w1v)


def _keep_half(qT, half):
    z = jnp.zeros((HALF, qT.shape[1]), qT.dtype)
    if half == 0:
        return jnp.concatenate([qT[0:HALF], z], axis=0)
    return jnp.concatenate([z, qT[HALF:2 * HALF]], axis=0)


def _rank_rows(vals, n_rows):
    R, T = vals.shape
    tiles = [vals[a:a + SUBLANES] for a in range(0, R, SUBLANES)]
    ranks = [jnp.zeros((SUBLANES, T), jnp.int32) for _ in tiles]
    j_in = lax.broadcasted_iota(jnp.int32, (SUBLANES, T), 0)
    for m in range(n_rows):
        vm = vals[m:m + 1, :]
        for a, tile in enumerate(tiles):
            lo = a * SUBLANES
            if lo > m:
                beats = vm >= tile
            elif lo + SUBLANES - 1 <= m:
                beats = vm > tile
            else:
                beats = (vm > tile) | ((vm == tile) & (j_in > m - lo))
            ranks[a] = ranks[a] + beats.astype(jnp.int32)
    return jnp.concatenate(ranks, axis=0)


def _top_rows(vals, k):
    R, T = vals.shape
    row = lax.broadcasted_iota(jnp.int32, (R, T), 0)
    picked = jnp.zeros((R, T), jnp.bool_)
    for _ in range(k):
        best = jnp.max(vals, axis=0, keepdims=True)
        first = jnp.min(jnp.where(vals == best, row, R), axis=0, keepdims=True)
        hit = row == first
        picked = picked | hit
        vals = jnp.where(hit, -jnp.inf, vals)
    return picked


def _query_tile(cols, tile0):
    return tile0 + lax.shift_right_logical(cols, int(np.log2(TILE)))


def _split_bf16(x):
    hi = x.astype(jnp.bfloat16).astype(jnp.float32)
    return hi, x - hi


def _slope_parts(slope):
    c = np.float32(slope * LOG2E)
    hi = np.asarray(c, jnp.bfloat16).astype(np.float32)
    lo = np.asarray(c - hi, jnp.bfloat16).astype(np.float32)
    return float(hi), float(lo)


AUG_ROWS = 16
MOBA_SEL_ROWS = 16


def _pad_rows(x, n_rows):
    return jnp.concatenate([x, jnp.zeros((n_rows - x.shape[0], x.shape[1]), x.dtype)], axis=0)


def _alibi_rows(T, slope, tile0):
    r = lax.broadcasted_iota(jnp.int32, (AUG_ROWS, T), 0)
    c = lax.broadcasted_iota(jnp.int32, (AUG_ROWS, T), 1)
    t_abs = ((c & (TILE - 1)) + _query_tile(c, tile0) * TILE).astype(jnp.float32)
    c_hi, c_lo = _slope_parts(slope)
    w_hi, w_lo = _split_bf16(np.float32(slope * LOG2E) * t_abs)
    out = jnp.where(r == 0, -w_hi, 0.0)
    out = jnp.where(r == 1, -w_lo, out)
    out = jnp.where((r == 2) | (r == 4), c_hi, out)
    out = jnp.where((r == 3) | (r == 5), c_lo, out)
    return out


def _key_aug_table(nq, n_sel, blocks_per_tile):
    t = np.zeros((nq, TILE, LANES), np.float32)
    krel = np.arange(TILE)
    for j in range(nq):
        if n_sel:
            blk = j * blocks_per_tile + krel // (TILE // blocks_per_tile)
            t[j, krel, blk] = 1.0
        t[j, :, n_sel:n_sel + 2] = 1.0
        t[j, :, n_sel + 2:n_sel + 4] = krel[:, None]
        t[j, :, n_sel + 4:n_sel + 6] = TILE * j
    return jnp.asarray(t.reshape(nq * TILE // KEYS, KEYS, LANES), jnp.bfloat16)


ONES_ROWS = 16
ACC_ROWS = HALF + ONES_ROWS


WIN_TILES = WINDOW // TILE
WIN_KEYS = (WIN_TILES + QT // TILE) * TILE


def _win_key_aug_table():
    t = np.zeros((WIN_KEYS, LANES), np.float32)
    for slot in range(WIN_KEYS // TILE):
        rows = slice(slot * TILE, (slot + 1) * TILE)
        t[rows, 0:2] = 1.0
        t[rows, 2:4] = np.arange(TILE)[:, None]
        t[rows, 4:6] = TILE * (WIN_TILES - slot)
        if slot < WIN_TILES:
            t[rows, 6] = 1.0
    return jnp.asarray(t, jnp.bfloat16)


def _value_rows(vT_h):
    return jnp.concatenate([vT_h, jnp.ones((ONES_ROWS, vT_h.shape[1]), vT_h.dtype)], axis=0)


def _stage_scores(h, score_fn, blocks, cols, s_ref, mx_ref):
    c0, c1 = cols
    base = blocks[0][0]
    s_all = score_fn(h, base, blocks[-1][1], c0, c1)
    mx = None
    for r0, r1, keep in blocks:
        s = s_all[r0 - base:r1 - base]
        if keep is not None:
            s = jnp.where(keep, s, NEG_INF)
        s_ref[h, r0:r1, c0:c1] = s
        part = jnp.max(s, axis=0, keepdims=True)
        mx = part if mx is None else jnp.maximum(mx, part)
    mx_ref[h, :, c0:c1] = mx


def _consume_scores(h, value_fn, rows, cols, s_ref, mx_ref, m_ref, acc_ref):
    (r0, r1), (c0, c1) = rows, cols
    m_prev = m_ref[h, :, c0:c1]
    m_new = jnp.maximum(m_prev, mx_ref[h, :, c0:c1])
    alpha = jnp.exp2(m_prev - m_new)
    p = jnp.exp2(s_ref[h, r0:r1, c0:c1] - m_new).astype(jnp.bfloat16)
    acc_ref[h, :, c0:c1] = alpha * acc_ref[h, :, c0:c1] + _dot(value_fn(h, r0, r1), p)
    m_ref[h, :, c0:c1] = m_new


def _attend_heads(score_fn, value_fn, parts, s_ref, mx_ref, m_ref, acc_ref):
    for h in range(N_HEADS):
        for cols, blocks in parts:
            _stage_scores(h, score_fn, blocks, cols, s_ref, mx_ref)
    for h in range(N_HEADS):
        for cols, blocks in parts:
            rows = (min(b[0] for b in blocks), max(b[1] for b in blocks))
            _consume_scores(h, value_fn, rows, cols, s_ref, mx_ref, m_ref, acc_ref)


def _init_state(m_ref, acc_ref):
    m_ref[...] = jnp.full(m_ref.shape, NEG_INF, jnp.float32)
    acc_ref[...] = jnp.zeros(acc_ref.shape, jnp.float32)


def _tile_iotas(T=TILE):
    k = lax.broadcasted_iota(jnp.int32, (TILE, T), 0)
    t = lax.broadcasted_iota(jnp.int32, (TILE, T), 1)
    return k, t


def _key_step_parts(kind):
    if kind == "past":
        return [((0, QT), [(0, KEYS, None)])]
    k, t = _tile_iotas()
    tri = k <= t
    own = QT if kind == "reach" else 0
    first = [(0, own, None)] if own else []
    return [((0, TILE), first + [(own, own + TILE, tri)]),
            ((TILE, QT), [(0, own + TILE, None), (own + TILE, own + QT, tri)])]


def _store_pair(o_ref, p, o0T, o1T, szT, gates, rows=None):
    if gates is not None:
        o0T = o0T * gates[0]
        o1T = o1T * gates[1]
    oT = jnp.concatenate([o0T, o1T], axis=0) * szT
    r0, r1 = rows if rows is not None else (0, o_ref.shape[1])
    o_ref[0, r0:r1, p * LANES:(p + 1) * LANES] = oT.T.astype(o_ref.dtype)


def _finalize_pair(o_ref, acc_ref, p, szT, gates=None):
    a0 = acc_ref[2 * p]
    a1 = acc_ref[2 * p + 1]
    _store_pair(o_ref, p, a0[0:HALF] / a0[HALF:HALF + 1], a1[0:HALF] / a1[HALF:HALF + 1],
                szT, gates)


def _nsa_gates(gate_ref, p, branch, cols=None):
    c0, c1 = cols if cols is not None else (0, gate_ref.shape[2])
    return [gate_ref[0, _gate_row(g, p, branch):_gate_row(g, p, branch) + 1, c0:c1]
            for g in range(NSA_GROUPS)]


def _step_tables(n_q):
    it, jt = [], []
    for ii in range(n_q):
        for js in range(ii * QT // KEYS + 1):
            it.append(ii)
            jt.append(js)
    return jnp.asarray(it, jnp.int32), jnp.asarray(jt, jnp.int32)


def _key_steps(ii, js, score_fn, value_fn, finalize_fn, s_ref, mx_ref, m_ref, acc_ref):
    q0 = ii * QT
    k0 = js * KEYS
    for kind, when in (("past", k0 + KEYS <= q0), ("reach", k0 + QT == q0), ("own", k0 == q0)):
        @pl.when(when)
        def _():
            _attend_heads(score_fn, value_fn, _key_step_parts(kind), s_ref, mx_ref, m_ref, acc_ref)
            if kind != "past":
                finalize_fn()


def _state_scratch(n_keys, T=TILE):
    return [
        pltpu.VMEM((N_HEADS, 2 * LANES, T), jnp.bfloat16),
        pltpu.VMEM((N_HEADS, 1, T), jnp.float32),
        pltpu.VMEM((N_HEADS, ACC_ROWS, T), jnp.float32),
        pltpu.VMEM((N_HEADS, n_keys, T), jnp.float32),
        pltpu.VMEM((N_HEADS, 1, T), jnp.float32),
    ]


def _moba_kernel(nq, it_ref, jt_ref, qT_ref, k_ref, vT_ref, kmean_ref, kaug_ref, sz_ref, o_ref,
                 qaug_ref, m_ref, acc_ref, s_ref, mx_ref):
    s = pl.program_id(1)
    ii = it_ref[s]
    jj = jt_ref[s]
    n_sel = MOBA_SEL_ROWS

    @pl.when(jj == 0)
    def _():
        _init_state(m_ref, acc_ref)
        n_iota = lax.broadcasted_iota(jnp.int32, (nq, QT), 0)
        own = _query_tile(lax.broadcasted_iota(jnp.int32, (nq, QT), 1), 2 * ii)
        for p in range(N_PAIRS):
            qT = qT_ref[0, p]
            km = kmean_ref[0, :, p * LANES:(p + 1) * LANES].astype(jnp.bfloat16)
            for hh in range(2):
                h = 2 * p + hh
                qm = _keep_half(qT, hh)
                gate = jnp.where(n_iota < own, _dot(km, qm), NEG_INF)
                top = _top_rows(gate, min(MOBA_TOPK, nq))
                sel = (top & (n_iota < own)) | (n_iota == own)
                selneg = jnp.where(sel, 0.0, NEG_INF)
                if nq < n_sel:
                    selneg = jnp.concatenate(
                        [selneg, jnp.zeros((n_sel - nq, QT), jnp.float32)], axis=0)
                ali = _alibi_rows(QT, SLOPES[h], 2 * ii)
                aug = jnp.concatenate([selneg, ali], axis=0).astype(jnp.bfloat16)
                qaug_ref[h] = _pad_rows(jnp.concatenate([qm, aug], axis=0), 2 * LANES)

    def score(h, r0, r1, c0, c1):
        p = h // 2
        kfull = jnp.concatenate(
            [k_ref[0, r0:r1, p * LANES:(p + 1) * LANES], kaug_ref[jj, r0:r1]], axis=1)
        return _dot(kfull, qaug_ref[h, :, c0:c1])

    def value(h, r0, r1):
        return _value_rows(vT_ref[0, h // 2, (h % 2) * HALF:(h % 2 + 1) * HALF, r0:r1])

    def finalize():
        for p in range(N_PAIRS):
            _finalize_pair(o_ref, acc_ref, p, sz_ref[0, p])

    _key_steps(ii, jj, score, value, finalize, s_ref, mx_ref, m_ref, acc_ref)


def _moba(qT, k, vT, kmean, kaug, szT):
    B, _, _, S = qT.shape
    nq = S // TILE
    it, jt = _step_tables(S // QT)
    return pl.pallas_call(
        functools.partial(_moba_kernel, nq),
        grid_spec=pltpu.PrefetchScalarGridSpec(
            num_scalar_prefetch=2,
            grid=(B, int(it.shape[0])),
            in_specs=[
                pl.BlockSpec((1, N_PAIRS, LANES, QT), lambda b, s, it, jt: (b, 0, 0, it[s])),
                pl.BlockSpec((1, KEYS, MOBA_W), lambda b, s, it, jt: (b, jt[s], 0)),
                pl.BlockSpec((1, N_PAIRS, LANES, KEYS), lambda b, s, it, jt: (b, 0, 0, jt[s])),
                pl.BlockSpec((1, nq, MOBA_W), lambda b, s, it, jt: (b, 0, 0)),
                pl.BlockSpec(kaug.shape, lambda b, s, it, jt: (0, 0, 0), pipeline_mode=RESIDENT),
                pl.BlockSpec((1, N_PAIRS, LANES, QT), lambda b, s, it, jt: (b, 0, 0, it[s])),
            ],
            out_specs=pl.BlockSpec((1, QT, MOBA_W), lambda b, s, it, jt: (b, it[s], 0)),
            scratch_shapes=_state_scratch(KEYS, QT),
        ),
        out_shape=jax.ShapeDtypeStruct((B, S, MOBA_W), BRANCH_DTYPE),
        compiler_params=pltpu.CompilerParams(
            dimension_semantics=("arbitrary", "arbitrary"), vmem_limit_bytes=VMEM_LIMIT),
        name="moba",
    )(it, jt, qT, k, vT, kmean, kaug, szT)


CMP_BAND = TILE // CMP_STRIDE + 1
RANK_ROWS_STEP = 8


def _cmp_query_aug_table():
    t = np.zeros((LANES, TILE), np.float32)
    trel = np.arange(TILE)
    for u in range(CMP_BAND):
        t[u, trel < CMP_STRIDE * (u - 1) + CMP_LEN - 1] = NEG_INF
    t[CMP_BAND, :] = NEG_INF
    return jnp.asarray(t, jnp.bfloat16)


def _cmp_tile(i, cols, slot, n_slc, qT_ref, kc_ref, vcT_ref, ovT_ref, qaug_ref, sz_ref, gate_ref,
              o_ref, sel_ref, s_ref, mx_ref):
    c0, c1 = cols
    C = kc_ref.shape[1]
    c_rel = (lax.broadcasted_iota(jnp.int32, (C, LANES), 0)
             - (TILE // CMP_STRIDE) * i + 1)
    u = lax.broadcasted_iota(jnp.int32, (C, LANES), 1)
    kaug = jnp.where((c_rel == u) & (u < CMP_BAND), 1.0, 0.0)
    kaug = jnp.where((u == CMP_BAND) & (c_rel >= CMP_BAND), 1.0, kaug)
    kfull = jnp.concatenate([kc_ref[0], kaug.astype(jnp.bfloat16)], axis=1)
    qaug = qaug_ref[...]

    for p in range(N_PAIRS):
        qT = qT_ref[0, p, :, c0:c1]
        for g in range(NSA_GROUPS):
            h = slot * N_HEADS + 2 * p + g
            s = _dot(kfull, jnp.concatenate([_keep_half(qT, g), qaug], axis=0))
            s_ref[h] = s
            mx_ref[h] = jnp.max(s, axis=0, keepdims=True)

    t_abs = i * TILE + lax.broadcasted_iota(jnp.int32, (1, TILE), 1)
    seen = t_abs >= CMP_LEN - 1
    ones = jnp.ones((ONES_ROWS, C), jnp.bfloat16)
    vals = [jnp.concatenate([vcT_ref[0, g * HALF:(g + 1) * HALF], ones, ovT_ref[...]], axis=0)
            for g in range(NSA_GROUPS)]
    R = sel_ref.shape[2]
    imp = [jnp.zeros((R, TILE), jnp.float32) for _ in range(NSA_GROUPS)]
    for p in range(N_PAIRS):
        outs = []
        for g in range(NSA_GROUPS):
            h = slot * N_HEADS + 2 * p + g
            pr = jnp.exp2(s_ref[h] - mx_ref[h]).astype(jnp.bfloat16)
            acc = _dot(vals[g], pr)
            inv = jnp.where(seen, 1.0 / acc[HALF:HALF + 1], 0.0)
            outs.append(acc[0:HALF] * inv)
            imp[g] = imp[g] + acc[ACC_ROWS:ACC_ROWS + R] * inv
        _store_pair(o_ref, p, outs[0], outs[1], sz_ref[0, p, :, c0:c1],
                    _nsa_gates(gate_ref, p, BRANCH_CMP, cols), rows=cols)

    def select_blocks():
        j_iota = lax.broadcasted_iota(jnp.int32, (R, TILE), 0)
        t_q = i * TILE + lax.broadcasted_iota(jnp.int32, (R, TILE), 1)
        own = lax.shift_right_logical(t_q, int(np.log2(SLC_BLOCK)))
        forced = (j_iota == 0) | (j_iota == own) | (j_iota == own - 1)
        causal = j_iota <= own

        def select(n_rows):
            top_n = min(SLC_TOPN, n_slc)
            for g in range(NSA_GROUPS):
                if n_rows <= top_n:
                    sel = causal[0:n_rows]
                else:
                    v = jnp.where(forced, FORCED_SCORE,
                                  jnp.where(causal, imp[g], NEG_INF))[0:n_rows]
                    sel = (_rank_rows(v, min(n_rows, n_slc)) < top_n) & causal[0:n_rows]
                selneg = jnp.where(sel, 0.0, NEG_INF)
                if n_rows < R:
                    selneg = jnp.concatenate(
                        [selneg, jnp.full((R - n_rows, TILE), NEG_INF, jnp.float32)], axis=0)
                sel_ref[0, g, :, c0:c1] = selneg.astype(jnp.bfloat16)

        blocks_per_tile = TILE // SLC_BLOCK
        variants = list(range(RANK_ROWS_STEP, R, RANK_ROWS_STEP)) + [R]
        for idx, n_rows in enumerate(variants):
            lo = 0 if idx == 0 else variants[idx - 1] // blocks_per_tile
            hi = n_rows // blocks_per_tile

            @pl.when((i >= lo) & (i < hi) if n_rows < R else i >= lo)
            def _():
                select(n_rows)

    return select_blocks


def _slc_kernel(it_ref, jt_ref, qT_ref, k_ref, vT_ref, sel_ref, kaug_ref, sz_ref, gate_ref, o_ref,
                qaug_ref, m_ref, acc_ref, s_ref, mx_ref):
    s = pl.program_id(1)
    ii = it_ref[s]
    jj = jt_ref[s]
    n_sel = sel_ref.shape[2]

    @pl.when(jj == 0)
    def _():
        _init_state(m_ref, acc_ref)
        for p in range(N_PAIRS):
            qT = qT_ref[0, p]
            for g in range(NSA_GROUPS):
                h = 2 * p + g
                qm = _keep_half(qT, g)
                ali = _alibi_rows(QT, SLOPES[g * NSA_HPG + p], 2 * ii)
                qaug_ref[h] = _pad_rows(
                    jnp.concatenate([qm, sel_ref[0, g], ali.astype(jnp.bfloat16)], axis=0),
                    2 * LANES)

    def score(h, r0, r1, c0, c1):
        kfull = jnp.concatenate([k_ref[0, r0:r1], kaug_ref[jj, r0:r1]], axis=1)
        return _dot(kfull, qaug_ref[h, :, c0:c1])

    def value(h, r0, r1):
        return _value_rows(vT_ref[0, (h % 2) * HALF:(h % 2 + 1) * HALF, r0:r1])

    def finalize():
        for p in range(N_PAIRS):
            _finalize_pair(o_ref, acc_ref, p, sz_ref[0, p], _nsa_gates(gate_ref, p, BRANCH_SLC))

    _key_steps(ii, jj, score, value, finalize, s_ref, mx_ref, m_ref, acc_ref)


def _slc(qbT, ksl, vslT, selT, kaug, szT, gateT):
    B, _, _, S = qbT.shape
    R = selT.shape[2]
    it, jt = _step_tables(S // QT)
    return pl.pallas_call(
        _slc_kernel,
        grid_spec=pltpu.PrefetchScalarGridSpec(
            num_scalar_prefetch=2,
            grid=(B, int(it.shape[0])),
            in_specs=[
                pl.BlockSpec((1, N_PAIRS, LANES, QT), lambda b, s, it, jt: (b, 0, 0, it[s])),
                pl.BlockSpec((1, KEYS, KV_W), lambda b, s, it, jt: (b, jt[s], 0)),
                pl.BlockSpec((1, KV_W, KEYS), lambda b, s, it, jt: (b, 0, jt[s])),
                pl.BlockSpec((1, NSA_GROUPS, R, QT), lambda b, s, it, jt: (b, 0, 0, it[s])),
                pl.BlockSpec(kaug.shape, lambda b, s, it, jt: (0, 0, 0), pipeline_mode=RESIDENT),
                pl.BlockSpec((1, N_PAIRS, LANES, QT), lambda b, s, it, jt: (b, 0, 0, it[s])),
                pl.BlockSpec((1, GATE_ROWS, QT), lambda b, s, it, jt: (b, 0, it[s])),
            ],
            out_specs=pl.BlockSpec((1, QT, NSA_W), lambda b, s, it, jt: (b, it[s], 0)),
            scratch_shapes=_state_scratch(KEYS, QT),
        ),
        out_shape=jax.ShapeDtypeStruct((B, S, NSA_W), BRANCH_DTYPE),
        compiler_params=pltpu.CompilerParams(
            dimension_semantics=("arbitrary", "arbitrary"), vmem_limit_bytes=VMEM_LIMIT),
        name="slc",
    )(it, jt, qbT, ksl, vslT, selT, kaug, szT, gateT)


def _local_kernel(n_slc, qT_ref, kb_ref, kq_ref, vb_ref, vq_ref, kaug_ref,
                  kc_ref, vcT_ref, ovT_ref, cq_ref, sz_ref, gate_ref,
                  o_ref, oc_ref, sel_ref,
                  qaug_ref, m_ref, acc_ref, s_ref, mx_ref, cs_ref, cmx_ref):
    i = pl.program_id(1)
    selects = [
        _cmp_tile(i * (QT // TILE) + q, (q * TILE, (q + 1) * TILE), q, n_slc, qT_ref, kc_ref,
                  vcT_ref, ovT_ref, cq_ref, sz_ref, gate_ref, oc_ref, sel_ref, cs_ref, cmx_ref)
        for q in range(QT // TILE)]
    _init_state(m_ref, acc_ref)
    r = lax.broadcasted_iota(jnp.int32, (AUG_ROWS, QT), 0)
    c = lax.broadcasted_iota(jnp.int32, (AUG_ROWS, QT), 1)
    trel = (c & (TILE - 1)).astype(jnp.float32)
    absent = jnp.where(i >= 1, 0.0, NEG_INF)
    for p in range(N_PAIRS):
        qT = qT_ref[0, p]
        for g in range(NSA_GROUPS):
            h = 2 * p + g
            slope = SLOPES[g * NSA_HPG + p]
            c_hi, c_lo = _slope_parts(slope)
            w_hi, w_lo = _split_bf16(np.float32(slope * LOG2E) * trel)
            aug = jnp.where(r == 0, -w_hi, 0.0)
            aug = jnp.where(r == 1, -w_lo, aug)
            aug = jnp.where(r == 2, c_hi, aug)
            aug = jnp.where(r == 3, c_lo, aug)
            aug = jnp.where(r == 4, -c_hi, aug)
            aug = jnp.where(r == 5, -c_lo, aug)
            aug = jnp.where(r == 6, absent, aug)
            qaug_ref[h] = _pad_rows(
                jnp.concatenate([_keep_half(qT, g), aug.astype(jnp.bfloat16)], axis=0), 2 * LANES)

    kfull = jnp.concatenate(
        [jnp.concatenate([kb_ref[0], kq_ref[0]], axis=0), kaug_ref[...]], axis=1)
    vT = jnp.concatenate([vb_ref[0], vq_ref[0]], axis=1)
    k, t = _tile_iotas()

    def score(h, r0, r1, c0, c1):
        return _dot(kfull[r0:r1], qaug_ref[h, :, c0:c1])

    def value(h, r0, r1):
        return _value_rows(vT[(h % 2) * HALF:(h % 2 + 1) * HALF, r0:r1])

    parts = []
    for q in range(QT // TILE):
        first, last = q * TILE, (q + WIN_TILES) * TILE
        parts.append(((q * TILE, (q + 1) * TILE),
                      [(first, first + TILE, k > t), (first + TILE, last, None),
                       (last, last + TILE, k <= t)]))
    _attend_heads(score, value, parts, s_ref, mx_ref, m_ref, acc_ref)
    for p in range(N_PAIRS):
        _finalize_pair(o_ref, acc_ref, p, sz_ref[0, p], _nsa_gates(gate_ref, p, BRANCH_WIN))
    for select_blocks in selects:
        select_blocks()


def _local(qbT, kwi, vwiT, kaug, kc, vcT, ovT, cmp_qaug, szT, gateT, n_slc):
    B, _, _, S = qbT.shape
    assert WINDOW == QT
    C = kc.shape[1]
    R = ovT.shape[0]
    before = lambda i: jnp.maximum(i - 1, 0)
    q_slabs = pl.BlockSpec((1, N_PAIRS, LANES, QT), lambda b, i: (b, 0, 0, i))
    whole = lambda a: pl.BlockSpec(a.shape, lambda b, i: (0,) * a.ndim, pipeline_mode=RESIDENT)
    rows_out = pl.BlockSpec((1, QT, NSA_W), lambda b, i: (b, i, 0))
    return pl.pallas_call(
        functools.partial(_local_kernel, n_slc),
        grid=(B, S // QT),
        in_specs=[
            q_slabs,
            pl.BlockSpec((1, WINDOW, KV_W), lambda b, i: (b, before(i), 0)),
            pl.BlockSpec((1, QT, KV_W), lambda b, i: (b, i, 0)),
            pl.BlockSpec((1, KV_W, WINDOW), lambda b, i: (b, 0, before(i))),
            pl.BlockSpec((1, KV_W, QT), lambda b, i: (b, 0, i)),
            whole(kaug),
            pl.BlockSpec((1, C, KV_W), lambda b, i: (b, 0, 0)),
            pl.BlockSpec((1, KV_W, C), lambda b, i: (b, 0, 0)),
            whole(ovT), whole(cmp_qaug),
            q_slabs,
            pl.BlockSpec((1, GATE_ROWS, QT), lambda b, i: (b, 0, i)),
        ],
        out_specs=[rows_out, rows_out,
                   pl.BlockSpec((1, NSA_GROUPS, R, QT), lambda b, i: (b, 0, 0, i))],
        out_shape=[jax.ShapeDtypeStruct((B, S, NSA_W), BRANCH_DTYPE),
                   jax.ShapeDtypeStruct((B, S, NSA_W), BRANCH_DTYPE),
                   jax.ShapeDtypeStruct((B, NSA_GROUPS, R, S), jnp.bfloat16)],
        scratch_shapes=_state_scratch(WIN_KEYS, QT) + [
            pltpu.VMEM((QT // TILE * N_HEADS, C, TILE), jnp.float32),
            pltpu.VMEM((QT // TILE * N_HEADS, 1, TILE), jnp.float32)],
        compiler_params=pltpu.CompilerParams(
            dimension_semantics=("arbitrary", "arbitrary"), vmem_limit_bytes=VMEM_LIMIT),
        name="local",
    )(qbT, kwi, kwi, vwiT, vwiT, kaug, kc, vcT, ovT, cmp_qaug, szT, gateT)


def _out_kernel(x_ref, oa_ref, oc_ref, os_ref, ow_ref, w_ref, g_ref, y_ref):
    f32 = jnp.float32
    ob = oc_ref[0].astype(f32) + os_ref[0].astype(f32) + ow_ref[0].astype(f32)
    mix = jnp.concatenate([oa_ref[0].astype(jnp.bfloat16), ob.astype(jnp.bfloat16)], axis=1)
    y = _dot(mix, w_ref[...])
    r = lax.rsqrt(jnp.mean(y * y, axis=-1, keepdims=True) + RMS_EPS)
    y_ref[0] = x_ref[0] + y * r * g_ref[...]


def _out(x, oa, oc, os_, ow, w, g):
    B, S, _ = x.shape
    rows = lambda wd: pl.BlockSpec((1, ROWS, wd), lambda b, i: (b, i, 0))
    return pl.pallas_call(
        _out_kernel,
        grid=(B, S // ROWS),
        in_specs=[rows(D_MODEL), rows(MOBA_W), rows(NSA_W), rows(NSA_W), rows(NSA_W),
                  pl.BlockSpec(w.shape, lambda b, i: (0, 0), pipeline_mode=RESIDENT),
                  pl.BlockSpec((1, D_MODEL), lambda b, i: (0, 0), pipeline_mode=RESIDENT)],
        out_specs=rows(D_MODEL),
        out_shape=jax.ShapeDtypeStruct((B, S, D_MODEL), jnp.float32),
        compiler_params=pltpu.CompilerParams(
            dimension_semantics=("arbitrary", "arbitrary"), vmem_limit_bytes=VMEM_LIMIT),
        name="out",
    )(x, oa, oc, os_, ow, w, g)


def _compress_weights(w1, w2):
    half = (CMP_LEN // 2) * HEAD_DIM
    w1r = w1.reshape(2, CMP_LEN // 2, HEAD_DIM, CMP_HIDDEN)
    z = jnp.zeros_like(w1r)
    g0 = jnp.concatenate([w1r, z], axis=-1)
    g1 = jnp.concatenate([z, w1r], axis=-1)
    both = jnp.stack([g0, g1], axis=2)
    both = both.reshape(2, half * NSA_GROUPS, NSA_GROUPS * CMP_HIDDEN).astype(jnp.bfloat16)
    zz = jnp.zeros_like(w2)
    w2bd = jnp.concatenate([jnp.concatenate([w2, zz], axis=1),
                            jnp.concatenate([zz, w2], axis=1)], axis=0).astype(jnp.bfloat16)
    return both[0], both[1], w2bd


def _overlap_T(n_cmp_pad, n_slc, rows):
    c = np.arange(n_cmp_pad)[None, :] * CMP_STRIDE
    j = np.arange(rows)[:, None] * SLC_BLOCK
    ov = (c < j + SLC_BLOCK) & (c + CMP_LEN > j) & (np.arange(rows)[:, None] < n_slc)
    return jnp.asarray(ov.astype(np.float32), jnp.bfloat16)


def _layer(x, pre_g, post_g, w_in, pos_k, pos_v, w_k1, w_k2, w_v1, w_v2, w_out):
    B, S, _ = x.shape
    nq = S // TILE
    n_cmp = (S - CMP_LEN) // CMP_STRIDE + 1
    n_slc = S // SLC_BLOCK
    C = S // CMP_STRIDE
    wn, wt = _projection_weights(w_in)
    (ka, kmean, kcm, vcm, ksl, kwi, qaT, vaT, qbT, vslT, vwiT,
     szaT, szbT, gateT) = _proj(x, pre_g.reshape(1, D_MODEL), wn, wt)

    wkt, wkb, w2k = _compress_weights(w_k1, w_k2)
    wvt, wvb, w2v = _compress_weights(w_v1, w_v2)
    chunk = CMP_STRIDE * KV_W
    kc, vcT = _compress(
        kcm.reshape(B, C, chunk), vcm.reshape(B, C, chunk), wkt, wkb, wvt, wvb, w2k, w2v,
        pos_k.reshape(1, CMP_LEN * HEAD_DIM).astype(jnp.bfloat16),
        pos_v.reshape(1, CMP_LEN * HEAD_DIM).astype(jnp.bfloat16),
        w_k1.astype(jnp.bfloat16), w_v1.astype(jnp.bfloat16), n_cmp)

    ow, oc, selT = _local(qbT, kwi, vwiT, _win_key_aug_table(), kc, vcT,
                          _overlap_T(C, n_slc, SLC_BLOCK), _cmp_query_aug_table(),
                          szbT, gateT, n_slc)
    oa = _moba(qaT, ka, vaT, kmean.reshape(B, nq, MOBA_W), _key_aug_table(nq, MOBA_SEL_ROWS, 1),
               szaT)
    osl = _slc(qbT, ksl, vslT, selT, _key_aug_table(nq, selT.shape[2], TILE // SLC_BLOCK),
               szbT, gateT)

    w_o = jnp.concatenate([w_out[:MOBA_W], _to_pair_slabs(w_out[MOBA_W:], 0)],
                          axis=0).astype(jnp.bfloat16)
    return _out(x, oa, oc, osl, ow, w_o, post_g.reshape(1, D_MODEL))


def kernel(x, pre_norm_g, post_norm_g, w_in, cmp_pos_k, cmp_pos_v,
           w_cmp_k1, w_cmp_k2, w_cmp_v1, w_cmp_v2, w_out):
    for l in range(pre_norm_g.shape[0]):
        x = _layer(x, pre_norm_g[l], post_norm_g[l], w_in[l], cmp_pos_k[l], cmp_pos_v[l],
                   w_cmp_k1[l], w_cmp_k2[l], w_cmp_v1[l], w_cmp_v2[l], w_out[l])
    return x
```
